```python
import math
import jax
import jax.numpy as jnp
from jax import lax
import numpy as np

D_MODEL = 1024
BATCH = 16
SEQ = 4096
DEPTH = 2

CTX_LEN = 256
GRID_W = 64

RG_WIDTH = D_MODEL
RG_HEADS = 8
RG_HEAD_DIM = RG_WIDTH // RG_HEADS
RG_C = 8.0
CONV_WIDTH = 4
CONV_PAD_LO = 1

S5_WIDTH = D_MODEL
S5_GROUP = 16
S5_GROUPS = S5_WIDTH // S5_GROUP
S5_STATE = 64
DT_MIN = 1e-3
DT_MAX = 1e-1

MIX_WIDTH = RG_WIDTH + S5_WIDTH
IN_WIDTH = 2 * RG_WIDTH + S5_WIDTH
D_FF = 4 * D_MODEL
N_MOD = 6
DEEPNORM_ALPHA = (2.0 * DEPTH) ** 0.25
DEEPNORM_BETA = (8.0 * DEPTH) ** -0.25
LN_EPS = 1e-5

kernel_name = 'hybrid_rglru_s5_deepnorm_prefix_dit'


def layer_norm(x, g, b):
    xf = x.astype(jnp.float32)
    mu = jnp.mean(xf, axis=-1, keepdims=True)
    var = jnp.mean(jnp.square(xf - mu), axis=-1, keepdims=True)
    return ((xf - mu) * lax.rsqrt(var + LN_EPS) * g + b).astype(x.dtype)


def modulate(x, shift, scale):
    return x * (1.0 + scale) + shift


def to_col_major(t, rows):
    b, n, ch = t.shape
    return t.reshape(b, rows, GRID_W, ch).swapaxes(1, 2).reshape(b, n, ch)


def from_col_major(t, rows):
    b, n, ch = t.shape
    return t.reshape(b, GRID_W, rows, ch).swapaxes(1, 2).reshape(b, n, ch)


def centred_dwconv(x, w, b):
    length = x.shape[1]
    xp = jnp.pad(x, ((0, 0), (CONV_PAD_LO, CONV_WIDTH - 1 - CONV_PAD_LO), (0, 0)))
    out = b
    for k in range(CONV_WIDTH):
        out = out + xp[:, k:k + length] * w[k]
    return out


def block_diag_linear(x, w, b):
    bsz, length, _ = x.shape
    xh = x.reshape(bsz, length, RG_HEADS, RG_HEAD_DIM)
    y = jnp.einsum('blhi,hij->blhj', xh, w)
    return y.reshape(bsz, length, RG_WIDTH) + b


def _combine(left, right):
    a_l, b_l = left
    a_r, b_r = right
    return a_l * a_r, a_r * b_l + b_r


def linear_scan(a, b, h0, reverse):
    if reverse:
        a = jnp.flip(a, axis=1)
        b = jnp.flip(b, axis=1)
    a_cum, h = lax.associative_scan(_combine, (a, b), axis=1)
    h = h + a_cum * h0[:, None]
    h_final = h[:, -1]
    if reverse:
        h = jnp.flip(h, axis=1)
    return h, h_final


def rglru_direction(xc_ctx, xc_lat, lam, wa, ba, wi, bi, reverse):
    def coeffs(xc):
        xf = xc.astype(jnp.float32)
        r = jax.nn.sigmoid(block_diag_linear(xf, wa, ba))
        i = jax.nn.sigmoid(block_diag_linear(xf, wi, bi))
        log_a = -RG_C * jax.nn.softplus(-lam) * r
        return jnp.exp(log_a), jnp.sqrt(-jnp.expm1(2.0 * log_a)) * (i * xf)
    a_c, b_c = coeffs(xc_ctx)
    h0 = jnp.zeros((xc_ctx.shape[0], RG_WIDTH), jnp.float32)
    h_ctx, h_fin = linear_scan(a_c, b_c, h0, reverse)
    a_l, b_l = coeffs(xc_lat)
    h_lat, _ = linear_scan(a_l, b_l, h_fin, reverse)
    return h_ctx, h_lat


def s5_direction(u_ctx, u_lat, a_re, a_im, log_dt, b_re, b_im, c_re, c_im, reverse):
    f32 = jnp.float32
    lam = lax.complex(a_re.astype(f32), a_im.astype(f32))
    dt = jnp.exp(log_dt.astype(f32))[:, None]
    lam_bar = jnp.exp(lam * dt)
    b_bar = ((lam_bar - 1.0) / lam)[..., None] * lax.complex(b_re.astype(f32), b_im.astype(f32))
    c_mat = lax.complex(c_re.astype(f32), c_im.astype(f32))

    def run(u, h0):
        bu = jnp.einsum('blgh,gph->blgp', u.astype(jnp.complex64), b_bar)
        a = jnp.broadcast_to(lam_bar, (1, u.shape[1]) + lam_bar.shape)
        h, h_fin = linear_scan(a, bu, h0, reverse)
        return jnp.einsum('blgp,ghp->blgh', h, c_mat).real, h_fin

    h0 = jnp.zeros((u_ctx.shape[0], S5_GROUPS, S5_STATE), jnp.complex64)
    y_ctx, h_fin = run(u_ctx, h0)
    y_lat, _ = run(u_lat, h_fin)
    return y_ctx, y_lat


def hybrid_mixer(u_lat, u_ctx, rows, w_in, conv_w, conv_b, rg_lambda, rg_wa, rg_ba, rg_wi, rg_bi,
                 s5_a_re, s5_a_im, s5_log_dt, s5_b_re, s5_b_im, s5_c_re, s5_c_im, s5_d,
                 s5_glu_w, s5_glu_b, w_out, b_out, ctx_out):
    dtype = u_lat.dtype
    rgx_lat, gate_lat, s5u_lat = jnp.split(u_lat @ w_in, [RG_WIDTH, 2 * RG_WIDTH], axis=-1)
    rgx_ctx, gate_ctx, s5u_ctx = jnp.split(u_ctx @ w_in, [RG_WIDTH, 2 * RG_WIDTH], axis=-1)

    xc_lat = centred_dwconv(rgx_lat, conv_w, conv_b)
    xc_ctx = centred_dwconv(rgx_ctx, conv_w, conv_b)
    hc_f, hl_f = rglru_direction(xc_ctx, xc_lat, rg_lambda[0], rg_wa[0], rg_ba[0], rg_wi[0], rg_bi[0], False)
    hc_b, hl_b = rglru_direction(xc_ctx, xc_lat, rg_lambda[1], rg_wa[1], rg_ba[1], rg_wi[1], rg_bi[1], True)
    rg_lat = (hl_f + hl_b).astype(dtype) * jax.nn.gelu(gate_lat)

    def grouped(t):
        return t.astype(jnp.float32).reshape(t.shape[0], t.shape[1], S5_GROUPS, S5_GROUP)
    s_lat = grouped(to_col_major(s5u_lat, rows))
    s_ctx = grouped(s5u_ctx)
    yc_f, yl_f = s5_direction(s_ctx, s_lat, s5_a_re[0], s5_a_im[0], s5_log_dt[0], s5_b_re[0], s5_b_im[0],
                              s5_c_re[0], s5_c_im[0], False)
    yc_b, yl_b = s5_direction(s_ctx, s_lat, s5_a_re[1], s5_a_im[1], s5_log_dt[1], s5_b_re[1], s5_b_im[1],
                              s5_c_re[1], s5_c_im[1], True)
    d_skip = s5_d.astype(jnp.float32).reshape(S5_GROUPS, S5_GROUP)

    def s5_glu(y):
        g = jax.nn.gelu(y.reshape(y.shape[0], y.shape[1], S5_WIDTH).astype(dtype))
        return g * jax.nn.sigmoid(g @ s5_glu_w + s5_glu_b)

    s5_lat = from_col_major(s5_glu(yl_f + yl_b + d_skip * s_lat), rows)
    out_lat = jnp.concatenate([rg_lat, s5_lat], axis=-1) @ w_out + b_out
    if not ctx_out:
        return out_lat, None
    rg_ctx = (hc_f + hc_b).astype(dtype) * jax.nn.gelu(gate_ctx)
    s5_ctx = s5_glu(yc_f + yc_b + d_skip * s_ctx)
    out_ctx = jnp.concatenate([rg_ctx, s5_ctx], axis=-1) @ w_out + b_out
    return out_lat, out_ctx


def sq_relu_mlp(x, w1, b1, w2, b2):
    return jnp.square(jax.nn.relu(x @ w1 + b1)) @ w2 + b2


def _fwd_setup_inputs(seed: int = 0) -> dict:
    key = jax.random.key(seed)
    keys = iter(jax.random.split(key, 48))
    f32 = jnp.float32

    def normal(shape, scale):
        return scale * jax.random.normal(next(keys), shape, f32)

    def uniform(shape, lo, hi):
        return jax.random.uniform(next(keys), shape, f32, lo, hi)

    nl = DEPTH
    x = normal((BATCH, SEQ, D_MODEL), 1.0)
    c = normal((BATCH, D_MODEL), 1.0)
    ctx = normal((BATCH, CTX_LEN, D_MODEL), 1.0)
    c_ctx = normal((D_MODEL,), 1.0)
    ada_w = normal((nl, D_MODEL, N_MOD * D_MODEL), 0.5 * D_MODEL ** -0.5)
    ada_b = normal((nl, N_MOD * D_MODEL), 0.02)
    ln1_g = 1.0 + normal((nl, D_MODEL), 0.02)
    ln1_b = normal((nl, D_MODEL), 0.02)
    w_in = normal((nl, D_MODEL, IN_WIDTH), D_MODEL ** -0.5)
    conv_w = normal((nl, CONV_WIDTH, RG_WIDTH), CONV_WIDTH ** -0.5)
    conv_b = normal((nl, RG_WIDTH), 0.02)
    a_c = uniform((nl, 2, RG_WIDTH), 0.9, 0.999)
    a0 = a_c ** (1.0 / RG_C)
    rg_lambda = jnp.log(a0) - jnp.log1p(-a0)
    rg_wa = normal((nl, 2, RG_HEADS, RG_HEAD_DIM, RG_HEAD_DIM), RG_HEAD_DIM ** -0.5)
    rg_ba = normal((nl, 2, RG_WIDTH), 0.02)
    rg_wi = normal((nl, 2, RG_HEADS, RG_HEAD_DIM, RG_HEAD_DIM), RG_HEAD_DIM ** -0.5)
    rg_bi = normal((nl, 2, RG_WIDTH), 0.02)
    n_idx = jnp.arange(S5_STATE, dtype=f32)
    s5_a_re = -0.5 * jnp.exp(normal((nl, 2, S5_GROUPS, S5_STATE), 0.02))
    s5_a_im = np.pi * n_idx + normal((nl, 2, S5_GROUPS, S5_STATE), 0.02)
    s5_log_dt = uniform((nl, 2, S5_GROUPS), math.log(DT_MIN), math.log(DT_MAX))
    s5_b_re = normal((nl, 2, S5_GROUPS, S5_STATE, S5_GROUP), (2.0 * S5_GROUP) ** -0.5)
    s5_b_im = normal((nl, 2, S5_GROUPS, S5_STATE, S5_GROUP), (2.0 * S5_GROUP) ** -0.5)
    s5_c_re = normal((nl, 2, S5_GROUPS, S5_GROUP, S5_STATE), 0.5 ** 0.5)
    s5_c_im = normal((nl, 2, S5_GROUPS, S5_GROUP, S5_STATE), 0.5 ** 0.5)
    s5_d = normal((nl, S5_WIDTH), 1.0)
    s5_glu_w = normal((nl, S5_WIDTH, S5_WIDTH), S5_WIDTH ** -0.5)
    s5_glu_b = normal((nl, S5_WIDTH), 0.02)
    w_out = normal((nl, MIX_WIDTH, D_MODEL), DEEPNORM_BETA * MIX_WIDTH ** -0.5)
    b_out = normal((nl, D_MODEL), 0.02)
    ln2_g = 1.0 + normal((nl, D_MODEL), 0.02)
    ln2_b = normal((nl, D_MODEL), 0.02)
    mlp_w1 = normal((nl, D_MODEL, D_FF), D_MODEL ** -0.5)
    mlp_b1 = normal((nl, D_FF), 0.02)
    mlp_w2 = normal((nl, D_FF, D_MODEL), DEEPNORM_BETA * D_FF ** -0.5)
    mlp_b2 = normal((nl, D_MODEL), 0.02)
    return {'x': x, 'c': c, 'ctx': ctx, 'c_ctx': c_ctx, 'ada_w': ada_w, 'ada_b': ada_b,
            'ln1_g': ln1_g, 'ln1_b': ln1_b, 'w_in': w_in, 'conv_w': conv_w, 'conv_b': conv_b,
            'rg_lambda': rg_lambda, 'rg_wa': rg_wa, 'rg_ba': rg_ba, 'rg_wi': rg_wi, 'rg_bi': rg_bi,
            's5_a_re': s5_a_re, 's5_a_im': s5_a_im, 's5_log_dt': s5_log_dt, 's5_b_re': s5_b_re,
            's5_b_im': s5_b_im, 's5_c_re': s5_c_re, 's5_c_im': s5_c_im, 's5_d': s5_d,
            's5_glu_w': s5_glu_w, 's5_glu_b': s5_glu_b, 'w_out': w_out, 'b_out': b_out,
            'ln2_g': ln2_g, 'ln2_b': ln2_b, 'mlp_w1': mlp_w1, 'mlp_b1': mlp_b1,
            'mlp_w2': mlp_w2, 'mlp_b2': mlp_b2}


def _fwd_reference(x, c, ctx, c_ctx, ada_w, ada_b, ln1_g, ln1_b, w_in, conv_w, conv_b,
              rg_lambda, rg_wa, rg_ba, rg_wi, rg_bi, s5_a_re, s5_a_im, s5_log_dt, s5_b_re,
              s5_b_im, s5_c_re, s5_c_im, s5_d, s5_glu_w, s5_glu_b, w_out, b_out,
              ln2_g, ln2_b, mlp_w1, mlp_b1, mlp_w2, mlp_b2):
    rows = x.shape[1] // GRID_W
    for l in range(DEPTH):
        last = l == DEPTH - 1
        mod = jax.nn.silu(c) @ ada_w[l] + ada_b[l]
        mod_c = jax.nn.silu(c_ctx) @ ada_w[l] + ada_b[l]
        sh1, sc1, g1, sh2, sc2, g2 = jnp.split(mod[:, None, :], N_MOD, axis=-1)
        sh1c, sc1c, g1c, sh2c, sc2c, g2c = jnp.split(mod_c, N_MOD, axis=-1)

        m_lat, m_ctx = hybrid_mixer(
            modulate(x, sh1, sc1), modulate(ctx, sh1c, sc1c), rows,
            w_in[l], conv_w[l], conv_b[l], rg_lambda[l], rg_wa[l], rg_ba[l], rg_wi[l], rg_bi[l],
            s5_a_re[l], s5_a_im[l], s5_log_dt[l], s5_b_re[l], s5_b_im[l], s5_c_re[l], s5_c_im[l],
            s5_d[l], s5_glu_w[l], s5_glu_b[l], w_out[l], b_out[l], not last)

        x = layer_norm(DEEPNORM_ALPHA * x + g1 * m_lat, ln1_g[l], ln1_b[l])
        f_lat = sq_relu_mlp(modulate(x, sh2, sc2), mlp_w1[l], mlp_b1[l], mlp_w2[l], mlp_b2[l])
        x = layer_norm(DEEPNORM_ALPHA * x + g2 * f_lat, ln2_g[l], ln2_b[l])

        if not last:
            ctx = layer_norm(DEEPNORM_ALPHA * ctx + g1c * m_ctx, ln1_g[l], ln1_b[l])
            f_ctx = sq_relu_mlp(modulate(ctx, sh2c, sc2c), mlp_w1[l], mlp_b1[l], mlp_w2[l], mlp_b2[l])
            ctx = layer_norm(DEEPNORM_ALPHA * ctx + g2c * f_ctx, ln2_g[l], ln2_b[l])
    return x


import jax as _jax
import jax.numpy as _jnp

TWIN_FORMAT = 'train_step'
FWD_PARAMS = ['x', 'c', 'ctx', 'c_ctx', 'ada_w', 'ada_b', 'ln1_g', 'ln1_b', 'w_in', 'conv_w', 'conv_b', 'rg_lambda', 'rg_wa', 'rg_ba', 'rg_wi', 'rg_bi', 's5_a_re', 's5_a_im', 's5_log_dt', 's5_b_re', 's5_b_im', 's5_c_re', 's5_c_im', 's5_d', 's5_glu_w', 's5_glu_b', 'w_out', 'b_out', 'ln2_g', 'ln2_b', 'mlp_w1', 'mlp_b1', 'mlp_w2', 'mlp_b2']
TWIN_WEIGHTS = ['c_ctx', 'ada_w', 'ada_b', 'ln1_g', 'ln1_b', 'w_in', 'conv_w', 'conv_b', 'rg_lambda', 'rg_wa', 'rg_ba', 'rg_wi', 'rg_bi', 's5_a_re', 's5_a_im', 's5_log_dt', 's5_b_re', 's5_b_im', 's5_c_re', 's5_c_im', 's5_d', 's5_glu_w', 's5_glu_b', 'w_out', 'b_out', 'ln2_g', 'ln2_b', 'mlp_w1', 'mlp_b1', 'mlp_w2', 'mlp_b2']
TWIN_DIFF_INPUT = 'x'
TWIN_INPUTS = ['x', 'c', 'ctx', 'c_ctx', 'ada_w', 'ada_b', 'ln1_g', 'ln1_b', 'w_in', 'conv_w', 'conv_b', 'rg_lambda', 'rg_wa', 'rg_ba', 'rg_wi', 'rg_bi', 's5_a_re', 's5_a_im', 's5_log_dt', 's5_b_re', 's5_b_im', 's5_c_re', 's5_c_im', 's5_d', 's5_glu_w', 's5_glu_b', 'w_out', 'b_out', 'ln2_g', 'ln2_b', 'mlp_w1', 'mlp_b1', 'mlp_w2', 'mlp_b2', 'loss_target', 'm_c_ctx', 'm_ada_w', 'm_ada_b', 'm_ln1_g', 'm_ln1_b', 'm_w_in', 'm_conv_w', 'm_conv_b', 'm_rg_lambda', 'm_rg_wa', 'm_rg_ba', 'm_rg_wi', 'm_rg_bi', 'm_s5_a_re', 'm_s5_a_im', 'm_s5_log_dt', 'm_s5_b_re', 'm_s5_b_im', 'm_s5_c_re', 'm_s5_c_im', 'm_s5_d', 'm_s5_glu_w', 'm_s5_glu_b', 'm_w_out', 'm_b_out', 'm_ln2_g', 'm_ln2_b', 'm_mlp_w1', 'm_mlp_b1', 'm_mlp_w2', 'm_mlp_b2', 'v_c_ctx', 'v_ada_w', 'v_ada_b', 'v_ln1_g', 'v_ln1_b', 'v_w_in', 'v_conv_w', 'v_conv_b', 'v_rg_lambda', 'v_rg_wa', 'v_rg_ba', 'v_rg_wi', 'v_rg_bi', 'v_s5_a_re', 'v_s5_a_im', 'v_s5_log_dt', 'v_s5_b_re', 'v_s5_b_im', 'v_s5_c_re', 'v_s5_c_im', 'v_s5_d', 'v_s5_glu_w', 'v_s5_glu_b', 'v_w_out', 'v_b_out', 'v_ln2_g', 'v_ln2_b', 'v_mlp_w1', 'v_mlp_b1', 'v_mlp_w2', 'v_mlp_b2']
TWIN_OUTPUTS = ['loss', 'grad_x', 'grad_c_ctx', 'grad_ada_w', 'grad_ada_b', 'grad_ln1_g', 'grad_ln1_b', 'grad_w_in', 'grad_conv_w', 'grad_conv_b', 'grad_rg_lambda', 'grad_rg_wa', 'grad_rg_ba', 'grad_rg_wi', 'grad_rg_bi', 'grad_s5_a_re', 'grad_s5_a_im', 'grad_s5_log_dt', 'grad_s5_b_re', 'grad_s5_b_im', 'grad_s5_c_re', 'grad_s5_c_im', 'grad_s5_d', 'grad_s5_glu_w', 'grad_s5_glu_b', 'grad_w_out', 'grad_b_out', 'grad_ln2_g', 'grad_ln2_b', 'grad_mlp_w1', 'grad_mlp_b1', 'grad_mlp_w2', 'grad_mlp_b2', 'delta_c_ctx', 'delta_ada_w', 'delta_ada_b', 'delta_ln1_g', 'delta_ln1_b', 'delta_w_in', 'delta_conv_w', 'delta_conv_b', 'delta_rg_lambda', 'delta_rg_wa', 'delta_rg_ba', 'delta_rg_wi', 'delta_rg_bi', 'delta_s5_a_re', 'delta_s5_a_im', 'delta_s5_log_dt', 'delta_s5_b_re', 'delta_s5_b_im', 'delta_s5_c_re', 'delta_s5_c_im', 'delta_s5_d', 'delta_s5_glu_w', 'delta_s5_glu_b', 'delta_w_out', 'delta_b_out', 'delta_ln2_g', 'delta_ln2_b', 'delta_mlp_w1', 'delta_mlp_b1', 'delta_mlp_w2', 'delta_mlp_b2', 'new_m_c_ctx', 'new_m_ada_w', 'new_m_ada_b', 'new_m_ln1_g', 'new_m_ln1_b', 'new_m_w_in', 'new_m_conv_w', 'new_m_conv_b', 'new_m_rg_lambda', 'new_m_rg_wa', 'new_m_rg_ba', 'new_m_rg_wi', 'new_m_rg_bi', 'new_m_s5_a_re', 'new_m_s5_a_im', 'new_m_s5_log_dt', 'new_m_s5_b_re', 'new_m_s5_b_im', 'new_m_s5_c_re', 'new_m_s5_c_im', 'new_m_s5_d', 'new_m_s5_glu_w', 'new_m_s5_glu_b', 'new_m_w_out', 'new_m_b_out', 'new_m_ln2_g', 'new_m_ln2_b', 'new_m_mlp_w1', 'new_m_mlp_b1', 'new_m_mlp_w2', 'new_m_mlp_b2', 'new_v_c_ctx', 'new_v_ada_w', 'new_v_ada_b', 'new_v_ln1_g', 'new_v_ln1_b', 'new_v_w_in', 'new_v_conv_w', 'new_v_conv_b', 'new_v_rg_lambda', 'new_v_rg_wa', 'new_v_rg_ba', 'new_v_rg_wi', 'new_v_rg_bi', 'new_v_s5_a_re', 'new_v_s5_a_im', 'new_v_s5_log_dt', 'new_v_s5_b_re', 'new_v_s5_b_im', 'new_v_s5_c_re', 'new_v_s5_c_im', 'new_v_s5_d', 'new_v_s5_glu_w', 'new_v_s5_glu_b', 'new_v_w_out', 'new_v_b_out', 'new_v_ln2_g', 'new_v_ln2_b', 'new_v_mlp_w1', 'new_v_mlp_b1', 'new_v_mlp_w2', 'new_v_mlp_b2']
TWIN_LEAF_KINDS = {'loss': 'loss', 'grad_x': 'grad_x', 'grad_c_ctx': 'grad_w', 'grad_ada_w': 'grad_w', 'grad_ada_b': 'grad_w', 'grad_ln1_g': 'grad_w', 'grad_ln1_b': 'grad_w', 'grad_w_in': 'grad_w', 'grad_conv_w': 'grad_w', 'grad_conv_b': 'grad_w', 'grad_rg_lambda': 'grad_w', 'grad_rg_wa': 'grad_w', 'grad_rg_ba': 'grad_w', 'grad_rg_wi': 'grad_w', 'grad_rg_bi': 'grad_w', 'grad_s5_a_re': 'grad_w', 'grad_s5_a_im': 'grad_w', 'grad_s5_log_dt': 'grad_w', 'grad_s5_b_re': 'grad_w', 'grad_s5_b_im': 'grad_w', 'grad_s5_c_re': 'grad_w', 'grad_s5_c_im': 'grad_w', 'grad_s5_d': 'grad_w', 'grad_s5_glu_w': 'grad_w', 'grad_s5_glu_b': 'grad_w', 'grad_w_out': 'grad_w', 'grad_b_out': 'grad_w', 'grad_ln2_g': 'grad_w', 'grad_ln2_b': 'grad_w', 'grad_mlp_w1': 'grad_w', 'grad_mlp_b1': 'grad_w', 'grad_mlp_w2': 'grad_w', 'grad_mlp_b2': 'grad_w', 'delta_c_ctx': 'delta_w', 'delta_ada_w': 'delta_w', 'delta_ada_b': 'delta_w', 'delta_ln1_g': 'delta_w', 'delta_ln1_b': 'delta_w', 'delta_w_in': 'delta_w', 'delta_conv_w': 'delta_w', 'delta_conv_b': 'delta_w', 'delta_rg_lambda': 'delta_w', 'delta_rg_wa': 'delta_w', 'delta_rg_ba': 'delta_w', 'delta_rg_wi': 'delta_w', 'delta_rg_bi': 'delta_w', 'delta_s5_a_re': 'delta_w', 'delta_s5_a_im': 'delta_w', 'delta_s5_log_dt': 'delta_w', 'delta_s5_b_re': 'delta_w', 'delta_s5_b_im': 'delta_w', 'delta_s5_c_re': 'delta_w', 'delta_s5_c_im': 'delta_w', 'delta_s5_d': 'delta_w', 'delta_s5_glu_w': 'delta_w', 'delta_s5_glu_b': 'delta_w', 'delta_w_out': 'delta_w', 'delta_b_out': 'delta_w', 'delta_ln2_g': 'delta_w', 'delta_ln2_b': 'delta_w', 'delta_mlp_w1': 'delta_w', 'delta_mlp_b1': 'delta_w', 'delta_mlp_w2': 'delta_w', 'delta_mlp_b2': 'delta_w', 'new_m_c_ctx': 'new_m', 'new_m_ada_w': 'new_m', 'new_m_ada_b': 'new_m', 'new_m_ln1_g': 'new_m', 'new_m_ln1_b': 'new_m', 'new_m_w_in': 'new_m', 'new_m_conv_w': 'new_m', 'new_m_conv_b': 'new_m', 'new_m_rg_lambda': 'new_m', 'new_m_rg_wa': 'new_m', 'new_m_rg_ba': 'new_m', 'new_m_rg_wi': 'new_m', 'new_m_rg_bi': 'new_m', 'new_m_s5_a_re': 'new_m', 'new_m_s5_a_im': 'new_m', 'new_m_s5_log_dt': 'new_m', 'new_m_s5_b_re': 'new_m', 'new_m_s5_b_im': 'new_m', 'new_m_s5_c_re': 'new_m', 'new_m_s5_c_im': 'new_m', 'new_m_s5_d': 'new_m', 'new_m_s5_glu_w': 'new_m', 'new_m_s5_glu_b': 'new_m', 'new_m_w_out': 'new_m', 'new_m_b_out': 'new_m', 'new_m_ln2_g': 'new_m', 'new_m_ln2_b': 'new_m', 'new_m_mlp_w1': 'new_m', 'new_m_mlp_b1': 'new_m', 'new_m_mlp_w2': 'new_m', 'new_m_mlp_b2': 'new_m', 'new_v_c_ctx': 'new_v', 'new_v_ada_w': 'new_v', 'new_v_ada_b': 'new_v', 'new_v_ln1_g': 'new_v', 'new_v_ln1_b': 'new_v', 'new_v_w_in': 'new_v', 'new_v_conv_w': 'new_v', 'new_v_conv_b': 'new_v', 'new_v_rg_lambda': 'new_v', 'new_v_rg_wa': 'new_v', 'new_v_rg_ba': 'new_v', 'new_v_rg_wi': 'new_v', 'new_v_rg_bi': 'new_v', 'new_v_s5_a_re': 'new_v', 'new_v_s5_a_im': 'new_v', 'new_v_s5_log_dt': 'new_v', 'new_v_s5_b_re': 'new_v', 'new_v_s5_b_im': 'new_v', 'new_v_s5_c_re': 'new_v', 'new_v_s5_c_im': 'new_v', 'new_v_s5_d': 'new_v', 'new_v_s5_glu_w': 'new_v', 'new_v_s5_glu_b': 'new_v', 'new_v_w_out': 'new_v', 'new_v_b_out': 'new_v', 'new_v_ln2_g': 'new_v', 'new_v_ln2_b': 'new_v', 'new_v_mlp_w1': 'new_v', 'new_v_mlp_b1': 'new_v', 'new_v_mlp_w2': 'new_v', 'new_v_mlp_b2': 'new_v'}


def _forward(args):
    return _fwd_reference(*[args[k] for k in FWD_PARAMS])


def _output_shape():
    out = _jax.eval_shape(lambda: _forward(_fwd_setup_inputs(0)))
    return out.shape, out.dtype

N_MICROBATCH = 1
ADAM_LR = 0.001
ADAM_B1 = 0.9
ADAM_B2 = 0.999
ADAM_EPS = 1e-08
ADAM_WD = 0.01
ADAM_STEP = 10
PER_EXAMPLE_BATCH_AXIS = {'x': 0, 'c': 0, 'ctx': 0, 'loss_target': 0}
SHARED_INPUTS = []
_WEIGHT_DTYPES = {'c_ctx': _jnp.float32, 'ada_w': _jnp.float32, 'ada_b': _jnp.float32, 'ln1_g': _jnp.float32, 'ln1_b': _jnp.float32, 'w_in': _jnp.float32, 'conv_w': _jnp.float32, 'conv_b': _jnp.float32, 'rg_lambda': _jnp.float32, 'rg_wa': _jnp.float32, 'rg_ba': _jnp.float32, 'rg_wi': _jnp.float32, 'rg_bi': _jnp.float32, 's5_a_re': _jnp.float32, 's5_a_im': _jnp.float32, 's5_log_dt': _jnp.float32, 's5_b_re': _jnp.float32, 's5_b_im': _jnp.float32, 's5_c_re': _jnp.float32, 's5_c_im': _jnp.float32, 's5_d': _jnp.float32, 's5_glu_w': _jnp.float32, 's5_glu_b': _jnp.float32, 'w_out': _jnp.float32, 'b_out': _jnp.float32, 'ln2_g': _jnp.float32, 'ln2_b': _jnp.float32, 'mlp_w1': _jnp.float32, 'mlp_b1': _jnp.float32, 'mlp_w2': _jnp.float32, 'mlp_b2': _jnp.float32}
MOMENT_SCALE = {'c_ctx': 2.162104e-02, 'ada_w': 1.004394e-01, 'ada_b': 1.712682e-01, 'ln1_g': 1.844804e+00, 'ln1_b': 7.835340e-01, 'w_in': 6.102135e-02, 'conv_w': 7.750302e-02, 'conv_b': 2.062010e-01, 'rg_lambda': 1.908309e-02, 'rg_wa': 4.020462e-03, 'rg_ba': 7.179840e-03, 'rg_wi': 7.738890e-03, 'rg_bi': 1.507214e-02, 's5_a_re': 5.476268e-03, 's5_a_im': 5.626413e-03, 's5_log_dt': 2.383494e+00, 's5_b_re': 3.190554e-03, 's5_b_im': 3.237419e-03, 's5_c_re': 7.714381e-04, 's5_c_im': 8.311453e-04, 's5_d': 9.572481e-03, 's5_glu_w': 4.395857e-03, 's5_glu_b': 4.636156e-03, 'w_out': 1.625420e-01, 'b_out': 1.004027e-01, 'ln2_g': 4.528350e+01, 'ln2_b': 3.876294e+00, 'mlp_w1': 2.384027e-02, 'mlp_b1': 2.936979e-02, 'mlp_w2': 9.497632e-02, 'mlp_b2': 1.016130e-01}


def _to_microbatches(a, axis):
    t = _jnp.moveaxis(a, axis, 0)
    t = t.reshape((N_MICROBATCH, t.shape[0] // N_MICROBATCH) + t.shape[1:])
    return _jnp.moveaxis(t, 1, axis + 1)


def setup_inputs(seed: int = 0) -> dict:
    inp = _fwd_setup_inputs(seed)
    key = _jax.random.fold_in(_jax.random.key(seed), 7919)
    shape, _ = _output_shape()
    out = dict(inp)
    out["loss_target"] = _jax.random.normal(_jax.random.fold_in(key, 0), shape, _jnp.float32)
    for i, name in enumerate(TWIN_WEIGHTS):
        w = inp[name].astype(_jnp.float32)
        if MOMENT_SCALE is None:
            s = _jnp.sqrt(_jnp.mean(_jnp.square(w)) + 1e-30)
        else:
            s = MOMENT_SCALE[name]
        km, kv = _jax.random.split(_jax.random.fold_in(key, i + 1))
        out[name] = w
        out["m_" + name] = s * _jax.random.normal(km, w.shape, _jnp.float32)
        out["v_" + name] = (s * s) * _jax.random.uniform(kv, w.shape, _jnp.float32, 0.5, 1.5)
    if N_MICROBATCH > 1:
        for name, axis in PER_EXAMPLE_BATCH_AXIS.items():
            out[name] = _to_microbatches(out[name], axis)
    return {'x': out['x'], 'c': out['c'], 'ctx': out['ctx'], 'c_ctx': out['c_ctx'], 'ada_w': out['ada_w'], 'ada_b': out['ada_b'], 'ln1_g': out['ln1_g'], 'ln1_b': out['ln1_b'], 'w_in': out['w_in'], 'conv_w': out['conv_w'], 'conv_b': out['conv_b'], 'rg_lambda': out['rg_lambda'], 'rg_wa': out['rg_wa'], 'rg_ba': out['rg_ba'], 'rg_wi': out['rg_wi'], 'rg_bi': out['rg_bi'], 's5_a_re': out['s5_a_re'], 's5_a_im': out['s5_a_im'], 's5_log_dt': out['s5_log_dt'], 's5_b_re': out['s5_b_re'], 's5_b_im': out['s5_b_im'], 's5_c_re': out['s5_c_re'], 's5_c_im': out['s5_c_im'], 's5_d': out['s5_d'], 's5_glu_w': out['s5_glu_w'], 's5_glu_b': out['s5_glu_b'], 'w_out': out['w_out'], 'b_out': out['b_out'], 'ln2_g': out['ln2_g'], 'ln2_b': out['ln2_b'], 'mlp_w1': out['mlp_w1'], 'mlp_b1': out['mlp_b1'], 'mlp_w2': out['mlp_w2'], 'mlp_b2': out['mlp_b2'], 'loss_target': out['loss_target'], 'm_c_ctx': out['m_c_ctx'], 'm_ada_w': out['m_ada_w'], 'm_ada_b': out['m_ada_b'], 'm_ln1_g': out['m_ln1_g'], 'm_ln1_b': out['m_ln1_b'], 'm_w_in': out['m_w_in'], 'm_conv_w': out['m_conv_w'], 'm_conv_b': out['m_conv_b'], 'm_rg_lambda': out['m_rg_lambda'], 'm_rg_wa': out['m_rg_wa'], 'm_rg_ba': out['m_rg_ba'], 'm_rg_wi': out['m_rg_wi'], 'm_rg_bi': out['m_rg_bi'], 'm_s5_a_re': out['m_s5_a_re'], 'm_s5_a_im': out['m_s5_a_im'], 'm_s5_log_dt': out['m_s5_log_dt'], 'm_s5_b_re': out['m_s5_b_re'], 'm_s5_b_im': out['m_s5_b_im'], 'm_s5_c_re': out['m_s5_c_re'], 'm_s5_c_im': out['m_s5_c_im'], 'm_s5_d': out['m_s5_d'], 'm_s5_glu_w': out['m_s5_glu_w'], 'm_s5_glu_b': out['m_s5_glu_b'], 'm_w_out': out['m_w_out'], 'm_b_out': out['m_b_out'], 'm_ln2_g': out['m_ln2_g'], 'm_ln2_b': out['m_ln2_b'], 'm_mlp_w1': out['m_mlp_w1'], 'm_mlp_b1': out['m_mlp_b1'], 'm_mlp_w2': out['m_mlp_w2'], 'm_mlp_b2': out['m_mlp_b2'], 'v_c_ctx': out['v_c_ctx'], 'v_ada_w': out['v_ada_w'], 'v_ada_b': out['v_ada_b'], 'v_ln1_g': out['v_ln1_g'], 'v_ln1_b': out['v_ln1_b'], 'v_w_in': out['v_w_in'], 'v_conv_w': out['v_conv_w'], 'v_conv_b': out['v_conv_b'], 'v_rg_lambda': out['v_rg_lambda'], 'v_rg_wa': out['v_rg_wa'], 'v_rg_ba': out['v_rg_ba'], 'v_rg_wi': out['v_rg_wi'], 'v_rg_bi': out['v_rg_bi'], 'v_s5_a_re': out['v_s5_a_re'], 'v_s5_a_im': out['v_s5_a_im'], 'v_s5_log_dt': out['v_s5_log_dt'], 'v_s5_b_re': out['v_s5_b_re'], 'v_s5_b_im': out['v_s5_b_im'], 'v_s5_c_re': out['v_s5_c_re'], 'v_s5_c_im': out['v_s5_c_im'], 'v_s5_d': out['v_s5_d'], 'v_s5_glu_w': out['v_s5_glu_w'], 'v_s5_glu_b': out['v_s5_glu_b'], 'v_w_out': out['v_w_out'], 'v_b_out': out['v_b_out'], 'v_ln2_g': out['v_ln2_g'], 'v_ln2_b': out['v_ln2_b'], 'v_mlp_w1': out['v_mlp_w1'], 'v_mlp_b1': out['v_mlp_b1'], 'v_mlp_w2': out['v_mlp_w2'], 'v_mlp_b2': out['v_mlp_b2']}


def _loss(weights, diff, rest, loss_target):
    with _jax.named_scope("forward"):
        args = {**rest, TWIN_DIFF_INPUT: diff, **{k: w.astype(_WEIGHT_DTYPES[k]) for k, w in weights.items()}}
        y = _forward(args)
    with _jax.named_scope("loss_head"):
        err = _jnp.square(y.astype(_jnp.float32) - loss_target)
        return 0.5 * _jnp.sum(_jnp.mean(err, axis=-1)) if err.ndim else 0.5 * err


def _adamw(w, g, m, v):
    m = ADAM_B1 * m + (1.0 - ADAM_B1) * g
    v = ADAM_B2 * v + (1.0 - ADAM_B2) * _jnp.square(g)
    m_hat = m / (1.0 - ADAM_B1 ** ADAM_STEP)
    v_hat = v / (1.0 - ADAM_B2 ** ADAM_STEP)
    delta = -ADAM_LR * (m_hat / (_jnp.sqrt(v_hat) + ADAM_EPS) + ADAM_WD * w)
    return delta, m, v


def reference(x, c, ctx, c_ctx, ada_w, ada_b, ln1_g, ln1_b, w_in, conv_w, conv_b, rg_lambda, rg_wa, rg_ba, rg_wi, rg_bi, s5_a_re, s5_a_im, s5_log_dt, s5_b_re, s5_b_im, s5_c_re, s5_c_im, s5_d, s5_glu_w, s5_glu_b, w_out, b_out, ln2_g, ln2_b, mlp_w1, mlp_b1, mlp_w2, mlp_b2, loss_target, m_c_ctx, m_ada_w, m_ada_b, m_ln1_g, m_ln1_b, m_w_in, m_conv_w, m_conv_b, m_rg_lambda, m_rg_wa, m_rg_ba, m_rg_wi, m_rg_bi, m_s5_a_re, m_s5_a_im, m_s5_log_dt, m_s5_b_re, m_s5_b_im, m_s5_c_re, m_s5_c_im, m_s5_d, m_s5_glu_w, m_s5_glu_b, m_w_out, m_b_out, m_ln2_g, m_ln2_b, m_mlp_w1, m_mlp_b1, m_mlp_w2, m_mlp_b2, v_c_ctx, v_ada_w, v_ada_b, v_ln1_g, v_ln1_b, v_w_in, v_conv_w, v_conv_b, v_rg_lambda, v_rg_wa, v_rg_ba, v_rg_wi, v_rg_bi, v_s5_a_re, v_s5_a_im, v_s5_log_dt, v_s5_b_re, v_s5_b_im, v_s5_c_re, v_s5_c_im, v_s5_d, v_s5_glu_w, v_s5_glu_b, v_w_out, v_b_out, v_ln2_g, v_ln2_b, v_mlp_w1, v_mlp_b1, v_mlp_w2, v_mlp_b2):
    given = dict(x=x, c=c, ctx=ctx, c_ctx=c_ctx, ada_w=ada_w, ada_b=ada_b, ln1_g=ln1_g, ln1_b=ln1_b, w_in=w_in, conv_w=conv_w, conv_b=conv_b, rg_lambda=rg_lambda, rg_wa=rg_wa, rg_ba=rg_ba, rg_wi=rg_wi, rg_bi=rg_bi, s5_a_re=s5_a_re, s5_a_im=s5_a_im, s5_log_dt=s5_log_dt, s5_b_re=s5_b_re, s5_b_im=s5_b_im, s5_c_re=s5_c_re, s5_c_im=s5_c_im, s5_d=s5_d, s5_glu_w=s5_glu_w, s5_glu_b=s5_glu_b, w_out=w_out, b_out=b_out, ln2_g=ln2_g, ln2_b=ln2_b, mlp_w1=mlp_w1, mlp_b1=mlp_b1, mlp_w2=mlp_w2, mlp_b2=mlp_b2, loss_target=loss_target, m_c_ctx=m_c_ctx, m_ada_w=m_ada_w, m_ada_b=m_ada_b, m_ln1_g=m_ln1_g, m_ln1_b=m_ln1_b, m_w_in=m_w_in, m_conv_w=m_conv_w, m_conv_b=m_conv_b, m_rg_lambda=m_rg_lambda, m_rg_wa=m_rg_wa, m_rg_ba=m_rg_ba, m_rg_wi=m_rg_wi, m_rg_bi=m_rg_bi, m_s5_a_re=m_s5_a_re, m_s5_a_im=m_s5_a_im, m_s5_log_dt=m_s5_log_dt, m_s5_b_re=m_s5_b_re, m_s5_b_im=m_s5_b_im, m_s5_c_re=m_s5_c_re, m_s5_c_im=m_s5_c_im, m_s5_d=m_s5_d, m_s5_glu_w=m_s5_glu_w, m_s5_glu_b=m_s5_glu_b, m_w_out=m_w_out, m_b_out=m_b_out, m_ln2_g=m_ln2_g, m_ln2_b=m_ln2_b, m_mlp_w1=m_mlp_w1, m_mlp_b1=m_mlp_b1, m_mlp_w2=m_mlp_w2, m_mlp_b2=m_mlp_b2, v_c_ctx=v_c_ctx, v_ada_w=v_ada_w, v_ada_b=v_ada_b, v_ln1_g=v_ln1_g, v_ln1_b=v_ln1_b, v_w_in=v_w_in, v_conv_w=v_conv_w, v_conv_b=v_conv_b, v_rg_lambda=v_rg_lambda, v_rg_wa=v_rg_wa, v_rg_ba=v_rg_ba, v_rg_wi=v_rg_wi, v_rg_bi=v_rg_bi, v_s5_a_re=v_s5_a_re, v_s5_a_im=v_s5_a_im, v_s5_log_dt=v_s5_log_dt, v_s5_b_re=v_s5_b_re, v_s5_b_im=v_s5_b_im, v_s5_c_re=v_s5_c_re, v_s5_c_im=v_s5_c_im, v_s5_d=v_s5_d, v_s5_glu_w=v_s5_glu_w, v_s5_glu_b=v_s5_glu_b, v_w_out=v_w_out, v_b_out=v_b_out, v_ln2_g=v_ln2_g, v_ln2_b=v_ln2_b, v_mlp_w1=v_mlp_w1, v_mlp_b1=v_mlp_b1, v_mlp_w2=v_mlp_w2, v_mlp_b2=v_mlp_b2)
    weights = {n: given[n] for n in TWIN_WEIGHTS}
    shared = {n: given[n] for n in SHARED_INPUTS}
    per_example = {n: given[n] for n in ['x', 'c', 'ctx']}
    grad_fn = _jax.value_and_grad(_loss, argnums=(0, 1))

    def one_microbatch(ex, loss_target):
        ex = dict(ex)
        diff = ex.pop(TWIN_DIFF_INPUT)
        return grad_fn(weights, diff, {**shared, **ex}, loss_target)

    if N_MICROBATCH == 1:
        loss, (grad_w, grad_x) = one_microbatch(per_example, given["loss_target"])
    else:
        def body(carry, xs):
            loss_sum, grad_sum = carry
            l_k, (gw_k, gx_k) = one_microbatch(xs[0], xs[1])
            with _jax.named_scope("update"):
                return (loss_sum + l_k, _jax.tree.map(_jnp.add, grad_sum, gw_k)), gx_k

        init = (_jnp.zeros((), _jnp.float32), _jax.tree.map(_jnp.zeros_like, weights))
        (loss, grad_w), grad_x = _jax.lax.scan(body, init, (per_example, given["loss_target"]))
    with _jax.named_scope("update"):
        delta_w, new_m, new_v = {}, {}, {}
        for n in TWIN_WEIGHTS:
            delta_w[n], new_m[n], new_v[n] = _adamw(weights[n], grad_w[n], given["m_" + n], given["v_" + n])
    return (loss, grad_x, *[grad_w[n] for n in TWIN_WEIGHTS], *[delta_w[n] for n in TWIN_WEIGHTS],
            *[new_m[n] for n in TWIN_WEIGHTS], *[new_v[n] for n in TWIN_WEIGHTS])
```

```python
import functools
import math

import jax
import jax.numpy as jnp
from jax import lax
from jax.experimental import pallas as pl
from jax.experimental.pallas import tpu as pltpu

F32 = jnp.float32
BF16 = jnp.bfloat16
MESH = pl.DeviceIdType.MESH

D_MODEL = 1024
N_MOD = 6
GRID_W = 64
RG_HEADS = 8
HEAD = 128
RG_C = 8.0
S5_GROUPS = 64
S5_GROUP = 16
S5_STATE = 64
T_CH = 16
GB = 8
CW = T_CH * S5_GROUP
SW = 2 * S5_STATE
DEPTH = 2
ALPHA = (2.0 * DEPTH) ** 0.25
LN_EPS = 1e-5
TM = 256
LANES = 1024
ADAM_LR, ADAM_B1, ADAM_B2, ADAM_EPS, ADAM_WD, ADAM_STEP = 0.001, 0.9, 0.999, 1e-08, 0.01, 10
MIB = 2 ** 20

IN_NAMES = ['x', 'c', 'ctx', 'c_ctx', 'ada_w', 'ada_b', 'ln1_g', 'ln1_b', 'w_in', 'conv_w', 'conv_b', 'rg_lambda',
            'rg_wa', 'rg_ba', 'rg_wi', 'rg_bi', 's5_a_re', 's5_a_im', 's5_log_dt', 's5_b_re', 's5_b_im', 's5_c_re',
            's5_c_im', 's5_d', 's5_glu_w', 's5_glu_b', 'w_out', 'b_out', 'ln2_g', 'ln2_b', 'mlp_w1', 'mlp_b1',
            'mlp_w2', 'mlp_b2']
WEIGHTS = IN_NAMES[3:]
SHARD_AXIS = {'ada_w': 2, 'w_in': 2, 'conv_w': 2, 'rg_lambda': 2, 'rg_ba': 2, 'rg_bi': 2, 's5_glu_w': 1, 'w_out': 1,
              'mlp_w1': 2, 'mlp_w2': 1}
SHARDED = [n for n in WEIGHTS if n in SHARD_AXIS]
REPLICATED = [n for n in WEIGHTS if n not in SHARD_AXIS]
GATHER_BF16 = ['ada_w', 'w_in', 's5_glu_w', 'w_out', 'mlp_w1', 'mlp_w2']
GATHER_F32 = ['conv_w', 'rg_lambda', 'rg_ba', 'rg_bi']


def _params(n_axes, vmem_mb=40):
    return pltpu.CompilerParams(dimension_semantics=("arbitrary",) * n_axes, vmem_limit_bytes=vmem_mb * MIB)


def _tile(n, options):
    for t in options:
        if n % t == 0:
            return t
    return n


def _sigmoid(z):
    return 1.0 / (1.0 + jnp.exp(-z))


def _softplus(z):
    return jnp.maximum(z, 0.0) + jnp.log(1.0 + jnp.exp(-jnp.abs(z)))


def _neg_expm1(z):
    series = -z * (1.0 + 0.5 * z * (1.0 + (1.0 / 3.0) * z * (1.0 + 0.25 * z)))
    return jnp.where(jnp.abs(z) < 1e-2, series, 1.0 - jnp.exp(z))


_G0 = math.sqrt(2.0 / math.pi)
_G1 = 0.044715


def _gelu(v):
    return 0.5 * v * (1.0 + jnp.tanh(_G0 * (v + _G1 * v * v * v)))


def _gelu_and_grad(v):
    t = jnp.tanh(_G0 * (v + _G1 * v * v * v))
    g = 0.5 * v * (1.0 + t)
    dg = 0.5 * (1.0 + t) + 0.5 * v * (1.0 - t * t) * _G0 * (1.0 + 3.0 * _G1 * v * v)
    return g, dg


def _seq_of_tile(i, tps):
    return 2 * (i // tps) + jnp.minimum(i % tps, 1)


def _dot(a, b, dims=(((1,), (0,)), ((), ()))):
    return lax.dot_general(a, b, dims, preferred_element_type=F32)


_NT = (((1,), (1,)), ((), ()))
_TN = (((0,), (0,)), ((), ()))


def _mm_nn(a, b, bias=None, *, relu2=False, name):
    m, k = a.shape
    n = b.shape[1]
    tm, tn, tk = _tile(m, (512, 256)), _tile(n, (1024,)), _tile(k, (1024,))
    nk = k // tk
    has_bias = bias is not None

    def body(*refs):
        a_ref, b_ref = refs[0], refs[1]
        bias_ref = refs[2] if has_bias else None
        outs = refs[2 + has_bias:-1]
        acc = refs[-1]
        kk = pl.program_id(2)

        @pl.when(kk == 0)
        def _():
            acc[...] = jnp.zeros_like(acc)

        acc[...] += _dot(a_ref[...], b_ref[...])

        @pl.when(kk == nk - 1)
        def _():
            h = acc[...]
            if has_bias:
                h = h + bias_ref[...]
            if relu2:
                r = jnp.maximum(h, 0.0)
                outs[0][...] = (r * r).astype(BF16)
                outs[1][...] = h.astype(BF16)
            else:
                outs[0][...] = h

    in_specs = [pl.BlockSpec((tm, tk), lambda j, i, kk: (i, kk)), pl.BlockSpec((tk, tn), lambda j, i, kk: (kk, j))]
    args = [a, b]
    if has_bias:
        in_specs.append(pl.BlockSpec((1, tn), lambda j, i, kk: (0, j)))
        args.append(bias)
    o_spec = pl.BlockSpec((tm, tn), lambda j, i, kk: (i, j))
    if relu2:
        out_shape = (jax.ShapeDtypeStruct((m, n), BF16), jax.ShapeDtypeStruct((m, n), BF16))
        out_specs = (o_spec, o_spec)
    else:
        out_shape, out_specs = jax.ShapeDtypeStruct((m, n), F32), o_spec
    return pl.pallas_call(body, name=name, grid=(n // tn, m // tm, nk), in_specs=in_specs, out_specs=out_specs,
                          out_shape=out_shape, scratch_shapes=[pltpu.VMEM((tm, tn), F32)],
                          compiler_params=_params(3))(*args)


def _mm_nt(a, b, hb=None, *, name):
    m, n = a.shape
    k = b.shape[0]
    tm, tn, tk = _tile(m, (512, 256)), _tile(k, (1024,)), _tile(n, (1024,))
    nk = n // tk
    fused = hb is not None

    def body(*refs):
        a_ref, b_ref = refs[0], refs[1]
        hb_ref = refs[2] if fused else None
        o_ref, acc = refs[-2], refs[-1]
        kk = pl.program_id(2)

        @pl.when(kk == 0)
        def _():
            acc[...] = jnp.zeros_like(acc)

        acc[...] += _dot(a_ref[...], b_ref[...], _NT)

        @pl.when(kk == nk - 1)
        def _():
            if fused:
                o_ref[...] = (acc[...] * (2.0 * jnp.maximum(hb_ref[...].astype(F32), 0.0))).astype(BF16)
            else:
                o_ref[...] = acc[...]

    in_specs = [pl.BlockSpec((tm, tk), lambda j, i, kk: (i, kk)), pl.BlockSpec((tn, tk), lambda j, i, kk: (j, kk))]
    args = [a, b]
    if fused:
        in_specs.append(pl.BlockSpec((tm, tn), lambda j, i, kk: (i, j)))
        args.append(hb)
    return pl.pallas_call(body, name=name, grid=(k // tn, m // tm, nk), in_specs=in_specs,
                          out_specs=pl.BlockSpec((tm, tn), lambda j, i, kk: (i, j)),
                          out_shape=jax.ShapeDtypeStruct((m, k), BF16 if fused else F32),
                          scratch_shapes=[pltpu.VMEM((tm, tn), F32)], compiler_params=_params(3))(*args)


def _mm_tn(a, b, *, name):
    m, k = a.shape
    n = b.shape[1]
    tk, tn, tr = _tile(k, (1024,)), _tile(n, (1024,)), _tile(m, (512, 256))
    nr = m // tr

    def body(a_ref, b_ref, o_ref, acc):
        r = pl.program_id(2)

        @pl.when(r == 0)
        def _():
            acc[...] = jnp.zeros_like(acc)

        acc[...] += _dot(a_ref[...], b_ref[...], _TN)

        @pl.when(r == nr - 1)
        def _():
            o_ref[...] = acc[...]

    return pl.pallas_call(body, name=name, grid=(k // tk, n // tn, nr),
                          in_specs=[pl.BlockSpec((tr, tk), lambda i, j, r: (r, i)),
                                    pl.BlockSpec((tr, tn), lambda i, j, r: (r, j))],
                          out_specs=pl.BlockSpec((tk, tn), lambda i, j, r: (i, j)),
                          out_shape=jax.ShapeDtypeStruct((k, n), F32),
                          scratch_shapes=[pltpu.VMEM((tk, tn), F32)], compiler_params=_params(3))(a, b)


def _colsum(v, *, name):
    m, n = v.shape
    tn, tr = _tile(n, (1024,)), _tile(m, (512, 256))

    def body(v_ref, o_ref):
        @pl.when(pl.program_id(1) == 0)
        def _():
            o_ref[...] = jnp.zeros_like(o_ref)

        o_ref[...] += jnp.sum(v_ref[...].astype(F32), axis=0, keepdims=True)

    return pl.pallas_call(body, name=name, grid=(n // tn, m // tr),
                          in_specs=[pl.BlockSpec((tr, tn), lambda j, r: (r, j))],
                          out_specs=pl.BlockSpec((1, tn), lambda j, r: (0, j)),
                          out_shape=jax.ShapeDtypeStruct((1, n), F32), compiler_params=_params(2))(v)


def _tok_spec(d=D_MODEL, col=0):
    return pl.BlockSpec((TM, d), lambda i: (i, col))


def _mod_spec(tps):
    return pl.BlockSpec((1, N_MOD, D_MODEL), lambda i: (_seq_of_tile(i, tps), 0, 0))


def _row_spec(d=D_MODEL):
    return pl.BlockSpec((1, d), lambda i: (0, 0))


def _seq_acc_spec(tps):
    return pl.BlockSpec((1, 1, D_MODEL), lambda i: (_seq_of_tile(i, tps), 0, 0))


def _modulate(xs, modall, k_shift, k_scale, tps, *, name):
    n = xs.shape[0]

    def body(x_ref, m_ref, o_ref):
        sh = m_ref[0, k_shift:k_shift + 1, :]
        sc = m_ref[0, k_scale:k_scale + 1, :]
        o_ref[...] = (x_ref[...] * (1.0 + sc) + sh).astype(BF16)

    return pl.pallas_call(body, name=name, grid=(n // TM,), in_specs=[_tok_spec(), _mod_spec(tps)],
                          out_specs=_tok_spec(), out_shape=jax.ShapeDtypeStruct((n, D_MODEL), BF16),
                          compiler_params=_params(1))(xs, modall)


def _resid_ln(xs, ms, modall, k_gate, g, b, tps, *, name):
    n = xs.shape[0]

    def body(x_ref, m_ref, mod_ref, g_ref, b_ref, o_ref):
        z = ALPHA * x_ref[...] + mod_ref[0, k_gate:k_gate + 1, :] * m_ref[...]
        mu = jnp.mean(z, axis=-1, keepdims=True)
        zc = z - mu
        var = jnp.mean(zc * zc, axis=-1, keepdims=True)
        o_ref[...] = zc * lax.rsqrt(var + LN_EPS) * g_ref[...] + b_ref[...]

    return pl.pallas_call(body, name=name, grid=(n // TM,),
                          in_specs=[_tok_spec(), _tok_spec(), _mod_spec(tps), _row_spec(), _row_spec()],
                          out_specs=_tok_spec(), out_shape=jax.ShapeDtypeStruct((n, D_MODEL), F32),
                          compiler_params=_params(1))(xs, ms, modall, g, b)


def _resid_ln_bwd(xs, ms, modall, k_gate, g, dout, tps, *, name):
    n = xs.shape[0]

    def body(x_ref, m_ref, mod_ref, g_ref, d_ref, dxa_ref, dm_ref, dbias_ref, dg_ref, db_ref, dgate_ref):
        i = pl.program_id(0)
        gate = mod_ref[0, k_gate:k_gate + 1, :]
        m = m_ref[...]
        z = ALPHA * x_ref[...] + gate * m
        mu = jnp.mean(z, axis=-1, keepdims=True)
        zc = z - mu
        var = jnp.mean(zc * zc, axis=-1, keepdims=True)
        rstd = lax.rsqrt(var + LN_EPS)
        xhat = zc * rstd
        d = d_ref[...]
        dxh = d * g_ref[...]
        dz = rstd * (dxh - jnp.mean(dxh, axis=-1, keepdims=True)
                     - xhat * jnp.mean(dxh * xhat, axis=-1, keepdims=True))
        dxa_ref[...] = ALPHA * dz
        dm = gate * dz
        dm_ref[...] = dm.astype(BF16)

        @pl.when(i == 0)
        def _():
            dbias_ref[...] = jnp.zeros_like(dbias_ref)
            dg_ref[...] = jnp.zeros_like(dg_ref)
            db_ref[...] = jnp.zeros_like(db_ref)

        dbias_ref[...] += jnp.sum(dm, axis=0, keepdims=True)
        dg_ref[...] += jnp.sum(d * xhat, axis=0, keepdims=True)
        db_ref[...] += jnp.sum(d, axis=0, keepdims=True)
        part = jnp.sum(dz * m, axis=0, keepdims=True)

        @pl.when(i % tps <= 1)
        def _():
            dgate_ref[0] = part

        @pl.when(i % tps > 1)
        def _():
            dgate_ref[0] += part

    row = jax.ShapeDtypeStruct((1, D_MODEL), F32)
    return pl.pallas_call(
        body, name=name, grid=(n // TM,),
        in_specs=[_tok_spec(), _tok_spec(), _mod_spec(tps), _row_spec(), _tok_spec()],
        out_specs=(_tok_spec(), _tok_spec(), _row_spec(), _row_spec(), _row_spec(), _seq_acc_spec(tps)),
        out_shape=(jax.ShapeDtypeStruct((n, D_MODEL), F32), jax.ShapeDtypeStruct((n, D_MODEL), BF16), row, row, row,
                   jax.ShapeDtypeStruct((n // TM // tps * 2, 1, D_MODEL), F32)),
        compiler_params=_params(1))(xs, ms, modall, g, dout)


def _modulate_bwd(du, xs, modall, k_scale, dxa, tps, *, name):
    n = xs.shape[0]

    def body(du_ref, x_ref, mod_ref, dxa_ref, dx_ref, dsc_ref, dsh_ref):
        i = pl.program_id(0)
        du_t = du_ref[...]
        dx_ref[...] = dxa_ref[...] + du_t * (1.0 + mod_ref[0, k_scale:k_scale + 1, :])
        psc = jnp.sum(du_t * x_ref[...], axis=0, keepdims=True)
        psh = jnp.sum(du_t, axis=0, keepdims=True)

        @pl.when(i % tps <= 1)
        def _():
            dsc_ref[0] = psc
            dsh_ref[0] = psh

        @pl.when(i % tps > 1)
        def _():
            dsc_ref[0] += psc
            dsh_ref[0] += psh

    acc = jax.ShapeDtypeStruct((n // TM // tps * 2, 1, D_MODEL), F32)
    return pl.pallas_call(body, name=name, grid=(n // TM,),
                          in_specs=[_tok_spec(), _tok_spec(), _mod_spec(tps), _tok_spec()],
                          out_specs=(_tok_spec(), _seq_acc_spec(tps), _seq_acc_spec(tps)),
                          out_shape=(jax.ShapeDtypeStruct((n, D_MODEL), F32), acc, acc),
                          compiler_params=_params(1))(du, xs, modall, dxa)


def _loss_head(ys, target, tps, *, name):
    n = ys.shape[0]
    lat_tiles = tps - 1

    def body(y_ref, t_ref, acc_ref, dy_ref):
        i = pl.program_id(0)

        @pl.when(i == 0)
        def _():
            acc_ref[...] = jnp.zeros_like(acc_ref)

        @pl.when(i % tps == 0)
        def _():
            dy_ref[...] = jnp.zeros_like(dy_ref)

        @pl.when(i % tps > 0)
        def _():
            e = y_ref[...] - t_ref[...]
            dy_ref[...] = e * (1.0 / D_MODEL)
            acc_ref[...] += jnp.sum(e * e, axis=0, keepdims=True)

    t_spec = pl.BlockSpec((TM, D_MODEL), lambda i: ((i // tps) * lat_tiles + jnp.maximum(i % tps - 1, 0), 0))
    return pl.pallas_call(body, name=name, grid=(n // TM,), in_specs=[_tok_spec(), t_spec],
                          out_specs=(_row_spec(), _tok_spec()),
                          out_shape=(jax.ShapeDtypeStruct((1, D_MODEL), F32), jax.ShapeDtypeStruct((n, D_MODEL), F32)),
                          compiler_params=_params(1))(ys, target)


def _glu_fwd(y_nat, p, d_skip, w, b, *, name):
    n = y_nat.shape[0]

    def body(y_ref, u_ref, d_ref, w_ref, b_ref, o_ref):
        g = _gelu(y_ref[...] + d_ref[...] * u_ref[...])
        z = _dot(g.astype(BF16), w_ref[...]) + b_ref[...]
        o_ref[...] = (g * _sigmoid(z)).astype(BF16)

    return pl.pallas_call(body, name=name, grid=(n // TM,),
                          in_specs=[_tok_spec(), _tok_spec(col=2), _row_spec(),
                                    pl.BlockSpec((D_MODEL, D_MODEL), lambda i: (0, 0)), _row_spec()],
                          out_specs=_tok_spec(), out_shape=jax.ShapeDtypeStruct((n, D_MODEL), BF16),
                          compiler_params=_params(1))(y_nat, p, d_skip, w, b)


def _glu_bwd(dcat, y_nat, p, d_skip, w, b, *, name):
    n = y_nat.shape[0]

    def body(ds_ref, y_ref, u_ref, d_ref, w_ref, b_ref, dy_ref, dsk_ref, g_ref, dz_ref, dd_ref, dbz_ref):
        u = u_ref[...]
        g, gg = _gelu_and_grad(y_ref[...] + d_ref[...] * u)
        s = _sigmoid(_dot(g.astype(BF16), w_ref[...]) + b_ref[...])
        ds = ds_ref[...]
        dz = ds * g * s * (1.0 - s)
        dzb = dz.astype(BF16)
        dg = ds * s + _dot(dzb, w_ref[...], _NT)
        dyp = dg * gg
        dy_ref[...] = dyp.astype(BF16)
        dsk_ref[...] = dyp * d_ref[...]
        g_ref[...] = g.astype(BF16)
        dz_ref[...] = dzb

        @pl.when(pl.program_id(0) == 0)
        def _():
            dd_ref[...] = jnp.zeros_like(dd_ref)
            dbz_ref[...] = jnp.zeros_like(dbz_ref)

        dd_ref[...] += jnp.sum(dyp * u, axis=0, keepdims=True)
        dbz_ref[...] += jnp.sum(dz, axis=0, keepdims=True)

    tok_bf = jax.ShapeDtypeStruct((n, D_MODEL), BF16)
    row = jax.ShapeDtypeStruct((1, D_MODEL), F32)
    return pl.pallas_call(
        body, name=name, grid=(n // TM,),
        in_specs=[_tok_spec(col=1), _tok_spec(), _tok_spec(col=2), _row_spec(),
                  pl.BlockSpec((D_MODEL, D_MODEL), lambda i: (0, 0)), _row_spec()],
        out_specs=(_tok_spec(), _tok_spec(), _tok_spec(), _tok_spec(), _row_spec(), _row_spec()),
        out_shape=(tok_bf, jax.ShapeDtypeStruct((n, D_MODEL), F32), tok_bf, tok_bf, row, row),
        compiler_params=_params(1))(dcat, y_nat, p, d_skip, w, b)


def _add_cast(a, b, *, name):
    n = a.shape[0]

    def body(a_ref, b_ref, o_ref):
        o_ref[...] = (a_ref[...] + b_ref[...]).astype(BF16)

    return pl.pallas_call(body, name=name, grid=(n // TM,), in_specs=[_tok_spec(), _tok_spec()],
                          out_specs=_tok_spec(), out_shape=jax.ShapeDtypeStruct((n, D_MODEL), BF16),
                          compiler_params=_params(1))(a, b)


def _silu_rows(c16, *, name):
    def body(c_ref, s_ref, ds_ref):
        v = c_ref[...]
        sg = _sigmoid(v)
        s_ref[...] = v * sg
        ds_ref[...] = sg * (1.0 + v * (1.0 - sg))

    shp = jax.ShapeDtypeStruct(c16.shape, F32)
    return pl.pallas_call(body, name=name, out_shape=(shp, shp))(c16)


def _mul_rows(a, b, *, name):
    def body(a_ref, b_ref, o_ref):
        o_ref[...] = a_ref[...] * b_ref[...]

    return pl.pallas_call(body, name=name, out_shape=jax.ShapeDtypeStruct(a.shape, F32))(a, b)


def _pad_off(c):
    return pl.multiple_of(c * TM + 8 + 8 * jnp.minimum(c, 1), 8)


def _rows8(k):
    return pl.ds(pl.multiple_of(k * 8, 8), 8)


def _windows(buf, c, shifts):
    n = TM + 16
    win = buf[pl.ds(pl.multiple_of(_pad_off(c) - 8, 8), n), :]
    return [win[8:8 + TM] if k == 0 else pltpu.roll(win, (-k) % n, 0)[8:8 + TM] for k in shifts]


def _conv_window(xpad, c):
    return _windows(xpad, c, (-1, 0, 1, 2))


def _rg_coeffs(z, d, spl, xc):
    r = _sigmoid(z[:, 256 * d:256 * d + HEAD])
    i = _sigmoid(z[:, 256 * d + HEAD:256 * d + 2 * HEAD])
    la = -RG_C * spl[d:d + 1, :] * r
    a = jnp.exp(la)
    mult = jnp.sqrt(_neg_expm1(2.0 * la))
    return r, i, a, mult, a * a


def _zero_pads(buf, s, lc):
    z8 = jnp.zeros((8, HEAD), F32)
    buf[0:8, :] = z8
    buf[8 + lc:16 + lc, :] = z8
    buf[16 + s:24 + s, :] = z8


def _rg_fwd(p, conv_w, conv_b, lam, wcat, bcat, bl, s, lc, *, name):
    nch = s // TM

    def body(x_ref, gate_ref, cw_ref, cb_ref, lam_ref, w_ref, b_ref, rg_ref, hf_ref, hb_ref, xpad, af, bf, ab, bb):
        _zero_pads(xpad, s, lc)

        def copy_chunk(c, _):
            xpad[pl.ds(_pad_off(c), TM), :] = x_ref[pl.ds(pl.multiple_of(c * TM, TM), TM), :]
            return 0

        lax.fori_loop(0, nch, copy_chunk, 0)
        spl = _softplus(-lam_ref[...])
        cw = cw_ref[...]

        def coef_chunk(c, _):
            xm1, x0, xp1, xp2 = _conv_window(xpad, c)
            xc = cw[0:1] * xm1 + cw[1:2] * x0 + cw[2:3] * xp1 + cw[3:4] * xp2 + cb_ref[...]
            z = _dot(xc.astype(BF16), w_ref[0]) + b_ref[0]
            rows = pl.ds(pl.multiple_of(c * TM, TM), TM)
            for d, (a_s, b_s) in enumerate(((af, bf), (ab, bb))):
                _, i, a, mult, _ = _rg_coeffs(z, d, spl, xc)
                a_s[rows, :] = a
                b_s[rows, :] = mult * i * xc
            return 0

        lax.fori_loop(0, nch, coef_chunk, 0)

        rid = lax.broadcasted_iota(jnp.int32, (8, HEAD), 0)

        def step8(kf, kb, carry):
            hf, hb = carry
            rf, rb = _rows8(kf), _rows8(kb)
            a1, b1, a2, b2 = af[rf, :], bf[rf, :], ab[rb, :], bb[rb, :]
            of = ob = jnp.zeros((8, HEAD), F32)
            for i in range(8):
                k = 7 - i
                hf = a1[i:i + 1] * hf + b1[i:i + 1]
                of = jnp.where(rid == i, hf, of)
                hb = a2[k:k + 1] * hb + b2[k:k + 1]
                ob = jnp.where(rid == k, hb, ob)
            hf_ref[rf, :] = of
            hb_ref[rb, :] = ob
            return hf, hb

        zero = jnp.zeros((1, HEAD), F32)
        nbc, nb = lc // 8, s // 8
        carry = lax.fori_loop(0, nbc, lambda j, cr: step8(j, nbc - 1 - j, cr), (zero, zero))
        lax.fori_loop(nbc, nb, lambda j, cr: step8(j, nb + nbc - 1 - j, cr), carry)

        def out_chunk(c, _):
            rows = pl.ds(pl.multiple_of(c * TM, TM), TM)
            rg_ref[rows, :] = ((hf_ref[rows, :] + hb_ref[rows, :]) * _gelu(gate_ref[rows, :])).astype(BF16)
            return 0

        lax.fori_loop(0, nch, out_chunk, 0)

    seq = lambda col0: pl.BlockSpec((s, HEAD), lambda b, h: (b, col0 + h))
    par = lambda r: pl.BlockSpec((r, HEAD), lambda b, h: (0, h))
    n = bl * s
    return pl.pallas_call(
        body, name=name, grid=(bl, RG_HEADS),
        in_specs=[seq(0), seq(RG_HEADS), par(4), par(1), par(2),
                  pl.BlockSpec((1, HEAD, 4 * HEAD), lambda b, h: (h, 0, 0)),
                  pl.BlockSpec((1, 1, 4 * HEAD), lambda b, h: (h, 0, 0))],
        out_specs=(seq(0), seq(0), seq(0)),
        out_shape=(jax.ShapeDtypeStruct((n, D_MODEL), BF16), jax.ShapeDtypeStruct((n, D_MODEL), F32),
                   jax.ShapeDtypeStruct((n, D_MODEL), F32)),
        scratch_shapes=[pltpu.VMEM((s + 24, HEAD), F32)] + [pltpu.VMEM((s, HEAD), F32)] * 4,
        compiler_params=_params(2, 48))(p, p, conv_w, conv_b, lam, wcat, bcat)


def _rg_bwd(p, dcat, hf, hb, conv_w, conv_b, lam, wcat, bcat, bl, s, lc, *, name):
    nch = s // TM
    ll = s - lc

    def body(p_hbm, dcat_hbm, hf_hbm, hb_hbm, cw_ref, cb_ref, lam_ref, w_ref, b_ref,
             drgx_ref, dgate_ref, dcw_ref, dcb_ref, dlam_ref, dw_ref, db_ref,
             xpad, dxpad, hf_s, hb_s, gate_s, dhs, a_f, a_b, lam_f, lam_b, sems):
        h = pl.program_id(0)
        b = pl.program_id(1)
        row0 = b * s
        col = pl.multiple_of(h * HEAD, HEAD)

        def rows_of(ref, r0, nr, c0):
            return ref.at[pl.ds(row0 + r0, nr), pl.ds(c0, HEAD)]

        copies = [
            pltpu.make_async_copy(rows_of(p_hbm, 0, lc, col), xpad.at[pl.ds(8, lc), :], sems.at[0]),
            pltpu.make_async_copy(rows_of(p_hbm, lc, ll, col), xpad.at[pl.ds(16 + lc, ll), :], sems.at[1]),
            pltpu.make_async_copy(rows_of(p_hbm, 0, s, col + D_MODEL), gate_s, sems.at[2]),
            pltpu.make_async_copy(rows_of(dcat_hbm, 0, s, col), dhs, sems.at[3]),
            pltpu.make_async_copy(rows_of(hf_hbm, 0, s, col), hf_s.at[pl.ds(8, s), :], sems.at[4]),
            pltpu.make_async_copy(rows_of(hb_hbm, 0, s, col), hb_s.at[pl.ds(8, s), :], sems.at[5]),
        ]
        for cp in copies:
            cp.start()
        _zero_pads(xpad, s, lc)
        _zero_pads(dxpad, s, lc)
        for buf in (hf_s, hb_s):
            buf[0:8, :] = jnp.zeros((8, HEAD), F32)
            buf[8 + s:16 + s, :] = jnp.zeros((8, HEAD), F32)

        @pl.when(b == 0)
        def _():
            dcw_ref[...] = jnp.zeros_like(dcw_ref)
            dcb_ref[...] = jnp.zeros_like(dcb_ref)
            dlam_ref[...] = jnp.zeros_like(dlam_ref)
            dw_ref[...] = jnp.zeros_like(dw_ref)
            db_ref[...] = jnp.zeros_like(db_ref)

        for cp in copies:
            cp.wait()
        lam_v = lam_ref[...]
        spl = _softplus(-lam_v)
        cw = cw_ref[...]

        def conv(c):
            xm1, x0, xp1, xp2 = _conv_window(xpad, c)
            return cw[0:1] * xm1 + cw[1:2] * x0 + cw[2:3] * xp1 + cw[3:4] * xp2 + cb_ref[...]

        def pass_a(c, _):
            rows = pl.ds(pl.multiple_of(c * TM, TM), TM)
            xc = conv(c)
            z = _dot(xc.astype(BF16), w_ref[0]) + b_ref[0]
            for d, a_s in enumerate((a_f, a_b)):
                a_s[rows, :] = _rg_coeffs(z, d, spl, xc)[2]
            g, gg = _gelu_and_grad(gate_s[rows, :])
            drg = dhs[rows, :]
            hrows = pl.ds(pl.multiple_of(c * TM + 8, 8), TM)
            dgate_ref[rows, :] = (drg * (hf_s[hrows, :] + hb_s[hrows, :]) * gg).astype(BF16)
            dhs[rows, :] = drg * g
            return 0

        lax.fori_loop(0, nch, pass_a, 0)

        rid = lax.broadcasted_iota(jnp.int32, (8, HEAD), 0)

        def adj8(kf, kb, carry):
            cf, cb_ = carry
            rf, rb = _rows8(kf), _rows8(kb)
            d1, a1, d2, a2 = dhs[rf, :], a_f[rf, :], dhs[rb, :], a_b[rb, :]
            of = ob = jnp.zeros((8, HEAD), F32)
            for i in range(8):
                k = 7 - i
                lf = d1[k:k + 1] + cf
                of = jnp.where(rid == k, lf, of)
                cf = a1[k:k + 1] * lf
                lb = d2[i:i + 1] + cb_
                ob = jnp.where(rid == i, lb, ob)
                cb_ = a2[i:i + 1] * lb
            lam_f[rf, :] = of
            lam_b[rb, :] = ob
            return cf, cb_

        zero = jnp.zeros((1, HEAD), F32)
        nbc, nb = lc // 8, s // 8
        carry = lax.fori_loop(0, nb - nbc, lambda j, cr: adj8(nb - 1 - j, nbc + j, cr), (zero, zero))
        lax.fori_loop(0, nbc, lambda j, cr: adj8(nbc - 1 - j, j, cr), carry)

        sig_neg = _sigmoid(-lam_v)
        last_row = lax.broadcasted_iota(jnp.int32, (TM, HEAD), 0) == TM - 1
        hb_first = hb_s[8:9, :]

        def pass_b(c, _):
            rows = pl.ds(pl.multiple_of(c * TM, TM), TM)
            xc = conv(c)
            xcb = xc.astype(BF16)
            z = _dot(xcb, w_ref[0]) + b_ref[0]
            dxc = jnp.zeros((TM, HEAD), F32)
            dzs = []
            n = TM + 16
            hp_f = pltpu.roll(hf_s[pl.ds(pl.multiple_of(c * TM, TM), n), :], 1, 0)[8:8 + TM]
            hp_b = pltpu.roll(hb_s[pl.ds(pl.multiple_of(c * TM, TM), n), :], n - 1, 0)[8:8 + TM]
            hp_b = jnp.where(last_row & (c == 0), 0.0, hp_b)
            hp_b = jnp.where(last_row & (c == nch - 1), hb_first, hp_b)
            for d, (l_s, hp) in enumerate(((lam_f, hp_f), (lam_b, hp_b))):
                r, i, a, mult, e2 = _rg_coeffs(z, d, spl, xc)
                dbt = l_s[rows, :]
                dla = dbt * hp * a - dbt * i * xc * (e2 / mult)
                dlam_ref[0, d:d + 1, :] += jnp.sum(dla * r, axis=0, keepdims=True) * (RG_C * sig_neg[d:d + 1, :])
                dr = dla * (-RG_C * spl[d:d + 1, :])
                di = dbt * mult * xc
                dxc = dxc + dbt * mult * i
                dzs += [dr * r * (1.0 - r), di * i * (1.0 - i)]
            dz = jnp.concatenate(dzs, axis=1)
            dzb = dz.astype(BF16)
            dxc = dxc + _dot(dzb, w_ref[0], _NT)
            dw_ref[0] += _dot(xcb, dzb, _TN)
            db_ref[0] += jnp.sum(dz, axis=0, keepdims=True)
            dcb_ref[0] += jnp.sum(dxc, axis=0, keepdims=True)
            dxpad[pl.ds(_pad_off(c), TM), :] = dxc
            return 0

        lax.fori_loop(0, nch, pass_b, 0)

        def pass_c(c, _):
            rows = pl.ds(pl.multiple_of(c * TM, TM), TM)
            gp1, g0, gm1, gm2 = _windows(dxpad, c, (1, 0, -1, -2))
            drgx_ref[rows, :] = (cw[0:1] * gp1 + cw[1:2] * g0 + cw[2:3] * gm1 + cw[3:4] * gm2).astype(BF16)
            xm1, x0, xp1, xp2 = _conv_window(xpad, c)
            dcw_ref[0] += jnp.concatenate([jnp.sum(g0 * t, axis=0, keepdims=True) for t in (xm1, x0, xp1, xp2)],
                                          axis=0)
            return 0

        lax.fori_loop(0, nch, pass_c, 0)

    seq = pl.BlockSpec((s, HEAD), lambda h, b: (b, h))
    par = lambda r: pl.BlockSpec((r, HEAD), lambda h, b: (0, h))
    acc = lambda r, w: pl.BlockSpec((1, r, w), lambda h, b: (h, 0, 0))
    anyspec = pl.BlockSpec(memory_space=pl.ANY)
    n = bl * s
    big = pltpu.VMEM((s, HEAD), F32)
    return pl.pallas_call(
        body, name=name, grid=(RG_HEADS, bl),
        in_specs=[anyspec, anyspec, anyspec, anyspec, par(4), par(1), par(2), acc(HEAD, 4 * HEAD), acc(1, 4 * HEAD)],
        out_specs=(seq, seq, acc(4, HEAD), acc(1, HEAD), acc(2, HEAD), acc(HEAD, 4 * HEAD), acc(1, 4 * HEAD)),
        out_shape=(jax.ShapeDtypeStruct((n, D_MODEL), BF16), jax.ShapeDtypeStruct((n, D_MODEL), BF16),
                   jax.ShapeDtypeStruct((RG_HEADS, 4, HEAD), F32), jax.ShapeDtypeStruct((RG_HEADS, 1, HEAD), F32),
                   jax.ShapeDtypeStruct((RG_HEADS, 2, HEAD), F32),
                   jax.ShapeDtypeStruct((RG_HEADS, HEAD, 4 * HEAD), F32),
                   jax.ShapeDtypeStruct((RG_HEADS, 1, 4 * HEAD), F32)),
        scratch_shapes=[pltpu.VMEM((s + 24, HEAD), F32)] * 2 + [pltpu.VMEM((s + 16, HEAD), F32)] * 2 + [big] * 6
        + [pltpu.SemaphoreType.DMA((6,))],
        compiler_params=_params(2, 52))(p, dcat, hf, hb, conv_w, conv_b, lam, wcat, bcat)


def _s5_mats(a_re, a_im, log_dt, b_re, b_im, c_re, c_im):
    t = T_CH
    dt = jnp.exp(log_dt)[..., None]
    lr, li = a_re * dt, a_im * dt
    steps = jnp.arange(t + 1, dtype=F32)
    mag = jnp.exp(lr[..., None] * steps)
    pr, pi = mag * jnp.cos(li[..., None] * steps), mag * jnp.sin(li[..., None] * steps)
    xr, xi = pr[..., 1] - 1.0, pi[..., 1]
    den = a_re * a_re + a_im * a_im
    qr, qi = (xr * a_re + xi * a_im) / den, (xi * a_re - xr * a_im) / den
    bbr = qr[..., None] * b_re - qi[..., None] * b_im
    bbi = qr[..., None] * b_im + qi[..., None] * b_re
    cbr = jnp.einsum('dghp,dgpk->dgpkh', c_re, bbr) - jnp.einsum('dghp,dgpk->dgpkh', c_im, bbi)
    cbi = jnp.einsum('dghp,dgpk->dgpkh', c_re, bbi) + jnp.einsum('dghp,dgpk->dgpkh', c_im, bbr)
    hp = lax.Precision.HIGHEST
    kfull = (jnp.einsum('dgpn,dgpkh->dgnkh', pr[..., :t], cbr, precision=hp)
             - jnp.einsum('dgpn,dgpkh->dgnkh', pi[..., :t], cbi, precision=hp))
    s_idx = jnp.arange(t)[:, None]
    t_idx = jnp.arange(t)[None, :]
    kf = jnp.where((t_idx >= s_idx)[None, :, :, None, None], kfull[0][:, jnp.clip(t_idx - s_idx, 0, t - 1)], 0.0)
    kb = jnp.where((s_idx >= t_idx)[None, :, :, None, None], kfull[1][:, jnp.clip(s_idx - t_idx, 0, t - 1)], 0.0)
    g = a_re.shape[1]
    kcat = (kf + kb).transpose(0, 1, 3, 2, 4).reshape(g, CW, CW)

    def state_in(d, powers):
        wr, wi = pr[d][..., powers], pi[d][..., powers]
        re = wr[..., None] * bbr[d][:, :, None, :] - wi[..., None] * bbi[d][:, :, None, :]
        im = wr[..., None] * bbi[d][:, :, None, :] + wi[..., None] * bbr[d][:, :, None, :]
        return jnp.stack([re, im], axis=1).transpose(0, 3, 4, 1, 2).reshape(g, CW, SW)

    def state_out(d, powers):
        wr, wi = pr[d][..., powers], pi[d][..., powers]
        cr, ci = c_re[d].transpose(0, 2, 1), c_im[d].transpose(0, 2, 1)
        re = wr[..., None] * cr[:, :, None, :] - wi[..., None] * ci[:, :, None, :]
        im = wr[..., None] * ci[:, :, None, :] + wi[..., None] * cr[:, :, None, :]
        return jnp.stack([re, -im], axis=1).reshape(g, SW, CW)

    ar = jnp.arange(t)
    wcat = jnp.concatenate([kcat, state_in(0, t - 1 - ar), state_in(1, ar)], axis=2)
    mout = jnp.concatenate([state_out(0, ar + 1), state_out(1, t - ar)], axis=1)
    rows = []
    for d in range(2):
        art, ait = pr[d][..., t], pi[d][..., t]
        rows += [jnp.concatenate([art, art], axis=1).reshape(-1), jnp.concatenate([-ait, ait], axis=1).reshape(-1)]
    return wcat, mout, jnp.stack(rows)


def _lane_swap(v):
    return pltpu.roll(v, S5_STATE, 1)


def _grp(g, w):
    return slice(g * w, (g + 1) * w)


def _s5_fwd(u, wcat, mout, a2, ncc, *, name):
    bl, nc, _ = u.shape

    def body(u_ref, w_ref, mo_ref, a_ref, y_ref, sf_ref, sb_ref, vf, vb):
        for g in range(GB):
            zu = _dot(u_ref[:, _grp(g, CW)], w_ref[g])
            y_ref[:, _grp(g, CW)] = zu[:, :CW]
            vf[:, _grp(g, SW)] = zu[:, CW:CW + SW]
            vb[:, _grp(g, SW)] = zu[:, CW + SW:]
        co = [[a_ref[r:r + 1, _grp(g, SW)] for g in range(GB)] for r in range(4)]

        rid = lax.broadcasted_iota(jnp.int32, (8, SW), 0)

        def step8(kf, kb, carry):
            rf, rb = _rows8(kf), _rows8(kb)
            vfb, vbb = vf[rf, :], vb[rb, :]
            out = []
            for g in range(GB):
                sf, sb = carry[2 * g], carry[2 * g + 1]
                of = ob = jnp.zeros((8, SW), F32)
                for i in range(8):
                    k = 7 - i
                    of = jnp.where(rid == i, sf, of)
                    sf = co[0][g] * sf + co[1][g] * _lane_swap(sf) + vfb[i:i + 1, _grp(g, SW)]
                    ob = jnp.where(rid == k, sb, ob)
                    sb = co[2][g] * sb + co[3][g] * _lane_swap(sb) + vbb[k:k + 1, _grp(g, SW)]
                sf_ref[rf, _grp(g, SW)] = of
                sb_ref[rb, _grp(g, SW)] = ob
                out += [sf, sb]
            return tuple(out)

        zero = jnp.zeros((1, SW), F32)
        nbc, nb = ncc // 8, nc // 8
        carry = lax.fori_loop(0, nbc, lambda j, cr: step8(j, nbc - 1 - j, cr), (zero,) * (2 * GB))
        lax.fori_loop(nbc, nb, lambda j, cr: step8(j, nb + nbc - 1 - j, cr), carry)
        for g in range(GB):
            st = jnp.concatenate([sf_ref[:, _grp(g, SW)], sb_ref[:, _grp(g, SW)]], axis=1).astype(BF16)
            y_ref[:, _grp(g, CW)] += _dot(st, mo_ref[g])

    blk = lambda w: pl.BlockSpec((None, nc, GB * w), lambda b, gb: (b, 0, gb))
    return pl.pallas_call(
        body, name=name, grid=(bl, S5_GROUPS // GB),
        in_specs=[blk(CW), pl.BlockSpec((GB, CW, 2 * CW), lambda b, gb: (gb, 0, 0)),
                  pl.BlockSpec((GB, CW, CW), lambda b, gb: (gb, 0, 0)),
                  pl.BlockSpec((4, GB * SW), lambda b, gb: (0, gb))],
        out_specs=(blk(CW), blk(SW), blk(SW)),
        out_shape=(jax.ShapeDtypeStruct(u.shape, F32), jax.ShapeDtypeStruct((bl, nc, S5_GROUPS * SW), F32),
                   jax.ShapeDtypeStruct((bl, nc, S5_GROUPS * SW), F32)),
        scratch_shapes=[pltpu.VMEM((nc, GB * SW), F32)] * 2, compiler_params=_params(2))(u, wcat, mout, a2)


def _s5_bwd(dy, u, sf, sb, wcat, mout, a2, ncc, *, name):
    bl, nc, _ = u.shape

    def body(dy_ref, u_ref, sf_ref, sb_ref, w_ref, mo_ref, a_ref, du_ref, dw_ref, dmo_ref, dacc_ref, gsf, gsb, dvf, dvb):
        b = pl.program_id(1)

        @pl.when(b == 0)
        def _():
            dw_ref[...] = jnp.zeros_like(dw_ref)
            dmo_ref[...] = jnp.zeros_like(dmo_ref)
            dacc_ref[...] = jnp.zeros_like(dacc_ref)

        for g in range(GB):
            ds = _dot(dy_ref[:, _grp(g, CW)], mo_ref[g], _NT)
            gsf[:, _grp(g, SW)] = ds[:, :SW]
            gsb[:, _grp(g, SW)] = ds[:, SW:]
        co = [[a_ref[r:r + 1, _grp(g, SW)] for g in range(GB)] for r in range(4)]

        rid = lax.broadcasted_iota(jnp.int32, (8, SW), 0)

        def step8(kf, kb, carry):
            rf, rb = _rows8(kf), _rows8(kb)
            gfb, gbb = gsf[rf, :], gsb[rb, :]
            out = []
            for g in range(GB):
                gf, gb_ = carry[2 * g], carry[2 * g + 1]
                of = ob = jnp.zeros((8, SW), F32)
                for i in range(8):
                    k = 7 - i
                    of = jnp.where(rid == k, gf, of)
                    gf = gfb[k:k + 1, _grp(g, SW)] + co[0][g] * gf - co[1][g] * _lane_swap(gf)
                    ob = jnp.where(rid == i, gb_, ob)
                    gb_ = gbb[i:i + 1, _grp(g, SW)] + co[2][g] * gb_ - co[3][g] * _lane_swap(gb_)
                dvf[rf, _grp(g, SW)] = of
                dvb[rb, _grp(g, SW)] = ob
                out += [gf, gb_]
            return tuple(out)

        zero = jnp.zeros((1, SW), F32)
        nbc, nb = ncc // 8, nc // 8
        carry = lax.fori_loop(0, nb - nbc, lambda j, cr: step8(nb - 1 - j, nbc + j, cr), (zero,) * (2 * GB))
        lax.fori_loop(0, nbc, lambda j, cr: step8(nbc - 1 - j, j, cr), carry)
        for g in range(GB):
            dyg = dy_ref[:, _grp(g, CW)]
            dvf_g, dvb_g = dvf[:, _grp(g, SW)], dvb[:, _grp(g, SW)]
            sf_g, sb_g = sf_ref[:, _grp(g, SW)], sb_ref[:, _grp(g, SW)]
            dz = jnp.concatenate([dyg, dvf_g.astype(BF16), dvb_g.astype(BF16)], axis=1)
            du_ref[:, _grp(g, CW)] = _dot(dz, w_ref[g], _NT)
            dw_ref[g] += _dot(u_ref[:, _grp(g, CW)], dz, _TN)
            st = jnp.concatenate([sf_g, sb_g], axis=1).astype(BF16)
            dmo_ref[g] += _dot(st, dyg, _TN)
            dacc_ref[:, _grp(g, SW)] += jnp.concatenate(
                [jnp.sum(dvf_g * sf_g, axis=0, keepdims=True), jnp.sum(dvf_g * _lane_swap(sf_g), axis=0, keepdims=True),
                 jnp.sum(dvb_g * sb_g, axis=0, keepdims=True), jnp.sum(dvb_g * _lane_swap(sb_g), axis=0, keepdims=True)],
                axis=0)

    blk = lambda w: pl.BlockSpec((None, nc, GB * w), lambda gb, b: (b, 0, gb))
    wspec = pl.BlockSpec((GB, CW, 2 * CW), lambda gb, b: (gb, 0, 0))
    mspec = pl.BlockSpec((GB, CW, CW), lambda gb, b: (gb, 0, 0))
    aspec = pl.BlockSpec((4, GB * SW), lambda gb, b: (0, gb))
    return pl.pallas_call(
        body, name=name, grid=(S5_GROUPS // GB, bl),
        in_specs=[blk(CW), blk(CW), blk(SW), blk(SW), wspec, mspec, aspec],
        out_specs=(blk(CW), wspec, mspec, aspec),
        out_shape=(jax.ShapeDtypeStruct(u.shape, F32), jax.ShapeDtypeStruct(wcat.shape, F32),
                   jax.ShapeDtypeStruct(mout.shape, F32), jax.ShapeDtypeStruct(a2.shape, F32)),
        scratch_shapes=[pltpu.VMEM((nc, GB * SW), F32)] * 4,
        compiler_params=_params(2, 48))(dy, u, sf, sb, wcat, mout, a2)


def _to_chunks(v, bl, lc, ll):
    rows = ll // GRID_W
    v = v.reshape(bl, lc + ll, S5_GROUPS, S5_GROUP)
    ctx = v[:, :lc].reshape(bl, lc // T_CH, T_CH, S5_GROUPS, S5_GROUP).transpose(0, 1, 3, 2, 4)
    lat = v[:, lc:].reshape(bl, rows // T_CH, T_CH, GRID_W, S5_GROUPS, S5_GROUP).transpose(0, 3, 1, 4, 2, 5)
    return jnp.concatenate([ctx.reshape(bl, lc // T_CH, -1), lat.reshape(bl, ll // T_CH, -1)], axis=1)


def _from_chunks(v, bl, lc, ll):
    rows = ll // GRID_W
    ncc = lc // T_CH
    ctx = v[:, :ncc].reshape(bl, ncc, S5_GROUPS, T_CH, S5_GROUP).transpose(0, 1, 3, 2, 4).reshape(bl, lc, D_MODEL)
    lat = v[:, ncc:].reshape(bl, GRID_W, rows // T_CH, S5_GROUPS, T_CH, S5_GROUP).transpose(0, 2, 4, 1, 3, 5)
    return jnp.concatenate([ctx, lat.reshape(bl, ll, D_MODEL)], axis=1).reshape(bl * (lc + ll), D_MODEL)


_ANY = pl.BlockSpec(memory_space=pl.ANY)


def _xy_peers():
    x, y, c = lax.axis_index("x"), lax.axis_index("y"), lax.axis_index("c")
    return x, y, c, [(1 - x, y), (x, 1 - y), (1 - x, 1 - y)]


def _all_gather_xy(shard, *, name):
    def body(x_ref, out_ref, send_sems, recv_sems, local_sem):
        x, y, c, peers = _xy_peers()
        me = 2 * x + y
        mine = pltpu.make_async_copy(x_ref, out_ref.at[me], local_sem)
        mine.start()

        def copy(k, px, py, slot):
            return pltpu.make_async_remote_copy(src_ref=x_ref, dst_ref=out_ref.at[slot], send_sem=send_sems.at[k],
                                                recv_sem=recv_sems.at[k], device_id=(px, py, c), device_id_type=MESH)

        sends = [copy(k, px, py, me) for k, (px, py) in enumerate(peers)]
        for cp in sends:
            cp.start()
        for k, (px, py) in enumerate(peers):
            copy(k, px, py, 2 * px + py).wait_recv()
        for cp in sends:
            cp.wait_send()
        mine.wait()

    return pl.pallas_call(body, name=name, in_specs=[_ANY], out_specs=_ANY,
                          out_shape=jax.ShapeDtypeStruct((4,) + shard.shape, shard.dtype),
                          scratch_shapes=[pltpu.SemaphoreType.DMA((3,)), pltpu.SemaphoreType.DMA((3,)),
                                          pltpu.SemaphoreType.DMA])(shard)


def _scatter_xy(parts, *, name):
    def body(p_ref, out_ref, send_sems, recv_sems, local_sem):
        x, y, c, peers = _xy_peers()
        mine = pltpu.make_async_copy(p_ref.at[2 * x + y], out_ref.at[0], local_sem)
        mine.start()

        def copy(k, px, py):
            return pltpu.make_async_remote_copy(src_ref=p_ref.at[2 * px + py], dst_ref=out_ref.at[1 + k],
                                                send_sem=send_sems.at[k], recv_sem=recv_sems.at[k],
                                                device_id=(px, py, c), device_id_type=MESH)

        sends = [copy(k, px, py) for k, (px, py) in enumerate(peers)]
        for cp in sends:
            cp.start()
        for cp in sends:
            cp.wait_recv()
        for cp in sends:
            cp.wait_send()
        mine.wait()

    return pl.pallas_call(body, name=name, in_specs=[_ANY], out_specs=_ANY,
                          out_shape=jax.ShapeDtypeStruct(parts.shape, parts.dtype),
                          scratch_shapes=[pltpu.SemaphoreType.DMA((3,)), pltpu.SemaphoreType.DMA((3,)),
                                          pltpu.SemaphoreType.DMA])(parts)


def _swap_sibling(v, *, name):
    def body(v_ref, out_ref, send_sem, recv_sem):
        x, y, c = lax.axis_index("x"), lax.axis_index("y"), lax.axis_index("c")
        cp = pltpu.make_async_remote_copy(src_ref=v_ref, dst_ref=out_ref, send_sem=send_sem, recv_sem=recv_sem,
                                          device_id=(x, y, 1 - c), device_id_type=MESH)
        cp.start()
        cp.wait()

    return pl.pallas_call(body, name=name, in_specs=[_ANY], out_specs=_ANY,
                          out_shape=jax.ShapeDtypeStruct(v.shape, v.dtype),
                          scratch_shapes=[pltpu.SemaphoreType.DMA, pltpu.SemaphoreType.DMA])(v)


def _flat_tile(r):
    return _tile(r, (512, 256, 128, 64, 32, 16, 8))


def _sum4(parts, *, name):
    r = parts.shape[1]
    tr = _flat_tile(r)

    def body(p_ref, o_ref):
        o_ref[...] = ((p_ref[0] + p_ref[1]) + p_ref[2]) + p_ref[3]

    return pl.pallas_call(body, name=name, grid=(r // tr,),
                          in_specs=[pl.BlockSpec((4, tr, LANES), lambda i: (0, i, 0))],
                          out_specs=pl.BlockSpec((tr, LANES), lambda i: (i, 0)),
                          out_shape=jax.ShapeDtypeStruct((r, LANES), F32), compiler_params=_params(1))(parts)


def _add2(a, b, *, name):
    r = a.shape[0]
    tr = _flat_tile(r)
    spec = pl.BlockSpec((tr, LANES), lambda i: (i, 0))

    def body(a_ref, b_ref, o_ref):
        o_ref[...] = a_ref[...] + b_ref[...]

    return pl.pallas_call(body, name=name, grid=(r // tr,), in_specs=[spec, spec], out_specs=spec,
                          out_shape=jax.ShapeDtypeStruct((r, LANES), F32), compiler_params=_params(1))(a, b)


def _adamw(w, ga, gb, m, v, *, name):
    r = w.shape[0]
    tr = _flat_tile(r)
    spec = pl.BlockSpec((tr, LANES), lambda i: (i, 0))
    two = gb is not None
    c1 = 1.0 / (1.0 - ADAM_B1 ** ADAM_STEP)
    c2 = 1.0 / (1.0 - ADAM_B2 ** ADAM_STEP)

    def body(*refs):
        w_ref, ga_ref = refs[0], refs[1]
        m_ref, v_ref, g_ref, d_ref, nm_ref, nv_ref = refs[2 + two:]
        g = ga_ref[...] + refs[2][...] if two else ga_ref[...]
        nm = ADAM_B1 * m_ref[...] + (1.0 - ADAM_B1) * g
        nv = ADAM_B2 * v_ref[...] + (1.0 - ADAM_B2) * (g * g)
        g_ref[...] = g
        nm_ref[...] = nm
        nv_ref[...] = nv
        d_ref[...] = -ADAM_LR * ((nm * c1) / (jnp.sqrt(nv * c2) + ADAM_EPS) + ADAM_WD * w_ref[...])

    args = [w, ga] + ([gb] if two else []) + [m, v]
    shp = jax.ShapeDtypeStruct((r, LANES), F32)
    return pl.pallas_call(body, name=name, grid=(r // tr,), in_specs=[spec] * len(args), out_specs=(spec,) * 4,
                          out_shape=(shp,) * 4, compiler_params=_params(1))(*args)


def _pack(arrs, dtype=F32):
    flat = jnp.concatenate([a.astype(dtype).reshape(-1) for a in arrs])
    pad = (-flat.shape[0]) % (32 * LANES)
    return jnp.pad(flat, (0, pad)).reshape(-1, LANES)


def _unpack(buf, shapes):
    flat = buf.reshape(-1)
    out, off = [], 0
    for shp in shapes:
        sz = math.prod(shp)
        out.append(flat[off:off + sz].reshape(shp))
        off += sz
    return out


def _stack_shards(full, axis):
    shp = full.shape
    return jnp.moveaxis(full.reshape(shp[:axis] + (4, shp[axis] // 4) + shp[axis + 1:]), axis, 0)


def _unstack_shards(st, axis):
    v = jnp.moveaxis(st, 0, axis)
    shp = v.shape
    return v.reshape(shp[:axis] + (shp[axis] * shp[axis + 1],) + shp[axis + 2:])


def _layer_weights(w, l):
    lw = {n: w[n][l] for n in w}
    lw['wcat'] = jnp.concatenate([lw['rg_wa'][0], lw['rg_wi'][0], lw['rg_wa'][1], lw['rg_wi'][1]],
                                 axis=-1).astype(BF16)
    ba, bi = lw['rg_ba'].reshape(2, RG_HEADS, HEAD), lw['rg_bi'].reshape(2, RG_HEADS, HEAD)
    lw['bcat'] = jnp.concatenate([ba[0], bi[0], ba[1], bi[1]], axis=-1)[:, None, :]
    s5_names = ['s5_a_re', 's5_a_im', 's5_log_dt', 's5_b_re', 's5_b_im', 's5_c_re', 's5_c_im']
    (wcat, mout, a2), lw['s5_vjp'] = jax.vjp(_s5_mats, *[lw[n] for n in s5_names])
    lw['s5_wcat'], lw['s5_mout'], lw['s5_a2'] = wcat.astype(BF16), mout.astype(BF16), a2
    for n in ('conv_b', 's5_d', 's5_glu_b', 'b_out', 'mlp_b1', 'mlp_b2', 'ln1_g', 'ln1_b', 'ln2_g', 'ln2_b'):
        lw[n] = lw[n][None, :]
    return lw


def _layer_fwd(l, x0, modall, lw, dims):
    bl, s, lc, tps = dims
    ll = s - lc
    tag = f"l{l}_"
    sv = {'x0': x0}
    sv['u1'] = _modulate(x0, modall, 0, 1, tps, name=tag + "mod1")
    sv['p'] = p = _mm_nn(sv['u1'], lw['w_in'], name=tag + "w_in")
    rg, sv['hf'], sv['hb'] = _rg_fwd(p, lw['conv_w'], lw['conv_b'], lw['rg_lambda'], lw['wcat'], lw['bcat'],
                                     bl, s, lc, name=tag + "rg_fwd")
    sv['u_ch'] = _to_chunks(p[:, 2 * D_MODEL:].astype(BF16), bl, lc, ll)
    y_ch, sv['sf'], sv['sb'] = _s5_fwd(sv['u_ch'], lw['s5_wcat'], lw['s5_mout'], lw['s5_a2'], lc // T_CH,
                                       name=tag + "s5_fwd")
    sv['y'] = _from_chunks(y_ch, bl, lc, ll)
    s5 = _glu_fwd(sv['y'], p, lw['s5_d'], lw['s5_glu_w'], lw['s5_glu_b'], name=tag + "glu_fwd")
    sv['cat'] = jnp.concatenate([rg, s5], axis=1)
    sv['m'] = _mm_nn(sv['cat'], lw['w_out'], lw['b_out'], name=tag + "w_out")
    sv['x1'] = _resid_ln(x0, sv['m'], modall, 2, lw['ln1_g'], lw['ln1_b'], tps, name=tag + "ln1")
    sv['u2'] = _modulate(sv['x1'], modall, 3, 4, tps, name=tag + "mod2")
    sv['a'], sv['h'] = _mm_nn(sv['u2'], lw['mlp_w1'], lw['mlp_b1'], relu2=True, name=tag + "mlp1")
    sv['f'] = _mm_nn(sv['a'], lw['mlp_w2'], lw['mlp_b2'], name=tag + "mlp2")
    x2 = _resid_ln(sv['x1'], sv['f'], modall, 5, lw['ln2_g'], lw['ln2_b'], tps, name=tag + "ln2")
    return x2, sv


def _layer_bwd(l, dx2, modall, lw, sv, dims):
    bl, s, lc, tps = dims
    ll = s - lc
    tag = f"l{l}_"
    g = {}
    dx1a, df, db2, g['ln2_g'], g['ln2_b'], dg2 = _resid_ln_bwd(sv['x1'], sv['f'], modall, 5, lw['ln2_g'], dx2, tps,
                                                              name=tag + "ln2_bwd")
    g['mlp_b2'] = db2
    g['mlp_w2'] = _mm_tn(sv['a'], df, name=tag + "mlp2_dw")
    dh = _mm_nt(df, lw['mlp_w2'], sv['h'], name=tag + "mlp2_dx")
    g['mlp_b1'] = _colsum(dh, name=tag + "mlp1_db")
    g['mlp_w1'] = _mm_tn(sv['u2'], dh, name=tag + "mlp1_dw")
    du2 = _mm_nt(dh, lw['mlp_w1'], name=tag + "mlp1_dx")
    dx1, dsc2, dsh2 = _modulate_bwd(du2, sv['x1'], modall, 4, dx1a, tps, name=tag + "mod2_bwd")
    dx0a, dm, g['b_out'], g['ln1_g'], g['ln1_b'], dg1 = _resid_ln_bwd(sv['x0'], sv['m'], modall, 2, lw['ln1_g'], dx1,
                                                                     tps, name=tag + "ln1_bwd")
    g['w_out'] = _mm_tn(sv['cat'], dm, name=tag + "w_out_dw")
    dcat = _mm_nt(dm, lw['w_out'], name=tag + "w_out_dx")
    dy, dskip, g_bf, dz_bf, g['s5_d'], g['s5_glu_b'] = _glu_bwd(dcat, sv['y'], sv['p'], lw['s5_d'], lw['s5_glu_w'],
                                                                lw['s5_glu_b'], name=tag + "glu_bwd")
    g['s5_glu_w'] = _mm_tn(g_bf, dz_bf, name=tag + "glu_dw")
    du_ch, dwcat, dmout, dacc = _s5_bwd(_to_chunks(dy, bl, lc, ll), sv['u_ch'], sv['sf'], sv['sb'], lw['s5_wcat'],
                                        lw['s5_mout'], lw['s5_a2'], lc // T_CH, name=tag + "s5_bwd")
    s5g = lw['s5_vjp']((dwcat, dmout, dacc))
    for n, v in zip(['s5_a_re', 's5_a_im', 's5_log_dt', 's5_b_re', 's5_b_im', 's5_c_re', 's5_c_im'], s5g):
        g[n] = v
    ds5u = _add_cast(_from_chunks(du_ch, bl, lc, ll), dskip, name=tag + "ds5u")
    drgx, dgate, dcw, dcb, dlam, dwc, dbc = _rg_bwd(sv['p'], dcat, sv['hf'], sv['hb'], lw['conv_w'], lw['conv_b'],
                                                    lw['rg_lambda'], lw['wcat'], lw['bcat'], bl, s, lc,
                                                    name=tag + "rg_bwd")
    g['conv_w'] = dcw.transpose(1, 0, 2).reshape(4, D_MODEL)
    g['conv_b'] = dcb.reshape(D_MODEL)
    g['rg_lambda'] = dlam.transpose(1, 0, 2).reshape(2, D_MODEL)
    g['rg_wa'] = jnp.stack([dwc[:, :, 0:HEAD], dwc[:, :, 2 * HEAD:3 * HEAD]])
    g['rg_wi'] = jnp.stack([dwc[:, :, HEAD:2 * HEAD], dwc[:, :, 3 * HEAD:]])
    dbc = dbc.reshape(RG_HEADS, 4, HEAD)
    g['rg_ba'] = jnp.stack([dbc[:, 0], dbc[:, 2]]).reshape(2, D_MODEL)
    g['rg_bi'] = jnp.stack([dbc[:, 1], dbc[:, 3]]).reshape(2, D_MODEL)
    dp = jnp.concatenate([drgx, dgate, ds5u], axis=1)
    g['w_in'] = _mm_tn(sv['u1'], dp, name=tag + "w_in_dw")
    du1 = _mm_nt(dp, lw['w_in'], name=tag + "w_in_dx")
    dx0, dsc1, dsh1 = _modulate_bwd(du1, sv['x0'], modall, 1, dx0a, tps, name=tag + "mod1_bwd")
    dmod = jnp.concatenate([dsh1, dsc1, dg1, dsh2, dsc2, dg2], axis=1)
    return dx0, g, dmod


def _kernel_impl(*args):
    nin = len(IN_NAMES)
    a = dict(zip(IN_NAMES, args[:nin]))
    target = args[nin]
    nw = len(WEIGHTS)
    mom = dict(zip(WEIGHTS, args[nin + 1:nin + 1 + nw]))
    var = dict(zip(WEIGHTS, args[nin + 1 + nw:nin + 1 + 2 * nw]))
    bl, ll, d = a['x'].shape
    lc = a['ctx'].shape[1]
    assert d == D_MODEL and lc == TM and bl == 2 and ll % (GRID_W * T_CH) == 0
    s = lc + ll
    tps = s // TM
    dims = (bl, s, lc, tps)

    def gather(names, dtype, tag):
        shards = [a[n] for n in names]
        got = _all_gather_xy(_pack(shards, dtype), name="gather_" + tag)
        per = [_unpack(got[j], [w.shape for w in shards]) for j in range(4)]
        return {n: _unstack_shards(jnp.stack([per[j][i] for j in range(4)]), SHARD_AXIS[n])
                for i, n in enumerate(names)}

    w = dict(gather(GATHER_BF16, BF16, "bf16"))
    w.update(gather(GATHER_F32, F32, "f32"))
    for n in REPLICATED:
        w[n] = a[n]

    xs = jnp.concatenate([a['ctx'], a['x']], axis=1).reshape(bl * s, D_MODEL)
    c16 = jnp.zeros((16, D_MODEL), F32).at[0:2].set(a['c']).at[2].set(a['c_ctx'])
    s16, ds16 = _silu_rows(c16, name="silu")
    s16b = s16.astype(BF16)
    layers, saved, mods = [], [], []
    for l in range(DEPTH):
        lw = _layer_weights({n: w[n] for n in WEIGHTS if n not in ('c_ctx',)}, l)
        mod16 = _mm_nn(s16b, lw['ada_w'], lw['ada_b'][None, :], name=f"l{l}_ada").reshape(16, N_MOD, D_MODEL)
        modall = jnp.stack([mod16[2], mod16[0], mod16[2], mod16[1]])
        xs, sv = _layer_fwd(l, xs, modall, lw, dims)
        layers.append(lw)
        saved.append(sv)
        mods.append(modall)
    lossrow, dx = _loss_head(xs, target.reshape(bl * ll, D_MODEL), tps, name="loss_head")
    loss = lax.psum(0.5 / D_MODEL * jnp.sum(lossrow), ("x", "y", "c"))

    grads = {n: [None] * DEPTH for n in WEIGHTS if n != 'c_ctx'}
    ds_rows = jnp.zeros((16, D_MODEL), F32)
    for l in reversed(range(DEPTH)):
        dx, g, dmod = _layer_bwd(l, dx, mods[l], layers[l], saved[l], dims)
        dmod16 = jnp.zeros((16, N_MOD * D_MODEL), F32).at[0].set(dmod[1].reshape(-1)).at[1].set(
            dmod[3].reshape(-1)).at[2].set((dmod[0] + dmod[2]).reshape(-1))
        dmod16b = dmod16.astype(BF16)
        g['ada_w'] = _mm_tn(s16b, dmod16b, name=f"l{l}_ada_dw")
        g['ada_b'] = _colsum(dmod16, name=f"l{l}_ada_db")
        ds_rows = ds_rows + _mm_nt(dmod16b, layers[l]['ada_w'], name=f"l{l}_ada_dx")
        for n, v in g.items():
            grads[n][l] = v.reshape(a[n].shape[1:] if n in REPLICATED else w[n].shape[1:])
    full = {n: jnp.stack(v) for n, v in grads.items()}
    full['c_ctx'] = _mul_rows(ds_rows, ds16, name="silu_bwd")[2]
    grad_x = dx.reshape(bl, s, D_MODEL)[:, lc:]

    rep_flat = _pack([full[n] for n in REPLICATED])
    rr = rep_flat.shape[0]
    sh_stacked = [_stack_shards(full[n], SHARD_AXIS[n] + 0).reshape(4, -1) for n in SHARDED]
    parts = jnp.concatenate(sh_stacked + [rep_flat.reshape(4, -1)], axis=1)
    pad = (-parts.shape[1]) % (32 * LANES)
    parts = jnp.pad(parts, ((0, 0), (0, pad))).reshape(4, -1, LANES)
    mine = _sum4(_scatter_xy(parts, name="grad_scatter"), name="grad_sum4")
    other = _swap_sibling(mine, name="grad_swap")
    n_sh = sum(math.prod(a[n].shape) for n in SHARDED)
    r_sh = n_sh // LANES
    assert n_sh % LANES == 0
    rq = rr // 4

    sh_shapes = [a[n].shape for n in SHARDED]
    pk = lambda dct: _pack([dct[n] for n in SHARDED])
    r_pk = pk(a).shape[0]
    take = lambda buf: jnp.pad(buf[:r_sh], ((0, r_pk - r_sh), (0, 0)))
    outs_sh = _adamw(pk(a), take(mine), take(other), pk(mom), pk(var), name="adamw_sharded")
    res_sh = [dict(zip(SHARDED, _unpack(o, sh_shapes))) for o in outs_sh]

    quarter = _add2(mine[r_sh:r_sh + rq], other[r_sh:r_sh + rq], name="grad_rep_sum")
    rep_g = _all_gather_xy(quarter, name="grad_rep_gather").reshape(rr, LANES)
    rep_shapes = [a[n].shape for n in REPLICATED]
    pr = lambda dct: _pack([dct[n] for n in REPLICATED])
    outs_rep = _adamw(pr(a), rep_g, None, pr(mom), pr(var), name="adamw_replicated")
    res_rep = [dict(zip(REPLICATED, _unpack(o, rep_shapes))) for o in outs_rep]

    out = [loss, grad_x]
    for k in range(4):
        out += [res_sh[k][n] if n in SHARD_AXIS else res_rep[k][n] for n in WEIGHTS]
    return tuple(out)


def kernel(x, c, ctx, c_ctx, ada_w, ada_b, ln1_g, ln1_b, w_in, conv_w, conv_b, rg_lambda, rg_wa, rg_ba, rg_wi, rg_bi, s5_a_re, s5_a_im, s5_log_dt, s5_b_re, s5_b_im, s5_c_re, s5_c_im, s5_d, s5_glu_w, s5_glu_b, w_out, b_out, ln2_g, ln2_b, mlp_w1, mlp_b1, mlp_w2, mlp_b2, loss_target, m_c_ctx, m_ada_w, m_ada_b, m_ln1_g, m_ln1_b, m_w_in, m_conv_w, m_conv_b, m_rg_lambda, m_rg_wa, m_rg_ba, m_rg_wi, m_rg_bi, m_s5_a_re, m_s5_a_im, m_s5_log_dt, m_s5_b_re, m_s5_b_im, m_s5_c_re, m_s5_c_im, m_s5_d, m_s5_glu_w, m_s5_glu_b, m_w_out, m_b_out, m_ln2_g, m_ln2_b, m_mlp_w1, m_mlp_b1, m_mlp_w2, m_mlp_b2, v_c_ctx, v_ada_w, v_ada_b, v_ln1_g, v_ln1_b, v_w_in, v_conv_w, v_conv_b, v_rg_lambda, v_rg_wa, v_rg_ba, v_rg_wi, v_rg_bi, v_s5_a_re, v_s5_a_im, v_s5_log_dt, v_s5_b_re, v_s5_b_im, v_s5_c_re, v_s5_c_im, v_s5_d, v_s5_glu_w, v_s5_glu_b, v_w_out, v_b_out, v_ln2_g, v_ln2_b, v_mlp_w1, v_mlp_b1, v_mlp_w2, v_mlp_b2):
    return _kernel_impl(x, c, ctx, c_ctx, ada_w, ada_b, ln1_g, ln1_b, w_in, conv_w, conv_b, rg_lambda, rg_wa, rg_ba, rg_wi, rg_bi, s5_a_re, s5_a_im, s5_log_dt, s5_b_re, s5_b_im, s5_c_re, s5_c_im, s5_d, s5_glu_w, s5_glu_b, w_out, b_out, ln2_g, ln2_b, mlp_w1, mlp_b1, mlp_w2, mlp_b2, loss_target, m_c_ctx, m_ada_w, m_ada_b, m_ln1_g, m_ln1_b, m_w_in, m_conv_w, m_conv_b, m_rg_lambda, m_rg_wa, m_rg_ba, m_rg_wi, m_rg_bi, m_s5_a_re, m_s5_a_im, m_s5_log_dt, m_s5_b_re, m_s5_b_im, m_s5_c_re, m_s5_c_im, m_s5_d, m_s5_glu_w, m_s5_glu_b, m_w_out, m_b_out, m_ln2_g, m_ln2_b, m_mlp_w1, m_mlp_b1, m_mlp_w2, m_mlp_b2, v_c_ctx, v_ada_w, v_ada_b, v_ln1_g, v_ln1_b, v_w_in, v_conv_w, v_conv_b, v_rg_lambda, v_rg_wa, v_rg_ba, v_rg_wi, v_rg_bi, v_s5_a_re, v_s5_a_im, v_s5_log_dt, v_s5_b_re, v_s5_b_im, v_s5_c_re, v_s5_c_im, v_s5_d, v_s5_glu_w, v_s5_glu_b, v_w_out, v_b_out, v_ln2_g, v_ln2_b, v_mlp_w1, v_mlp_b1, v_mlp_w2, v_mlp_b2)
```

```python
import functools
import math

import jax
import jax.numpy as jnp
from jax import lax
from jax.experimental import pallas as pl
from jax.experimental.pallas import tpu as pltpu

F32 = jnp.float32
BF16 = jnp.bfloat16
MESH = pl.DeviceIdType.MESH

D_MODEL = 1024
N_MOD = 6
GRID_W = 64
RG_HEADS = 8
HEAD = 128
RG_C = 8.0
S5_GROUPS = 64
S5_GROUP = 16
S5_STATE = 64
T_CH = 16
GB = 8
CW = T_CH * S5_GROUP
SW = 2 * S5_STATE
DEPTH = 2
ALPHA = (2.0 * DEPTH) ** 0.25
LN_EPS = 1e-5
TM = 256
LANES = 1024
ADAM_LR, ADAM_B1, ADAM_B2, ADAM_EPS, ADAM_WD, ADAM_STEP = 0.001, 0.9, 0.999, 1e-08, 0.01, 10
MIB = 2 ** 20

IN_NAMES = ['x', 'c', 'ctx', 'c_ctx', 'ada_w', 'ada_b', 'ln1_g', 'ln1_b', 'w_in', 'conv_w', 'conv_b', 'rg_lambda',
            'rg_wa', 'rg_ba', 'rg_wi', 'rg_bi', 's5_a_re', 's5_a_im', 's5_log_dt', 's5_b_re', 's5_b_im', 's5_c_re',
            's5_c_im', 's5_d', 's5_glu_w', 's5_glu_b', 'w_out', 'b_out', 'ln2_g', 'ln2_b', 'mlp_w1', 'mlp_b1',
            'mlp_w2', 'mlp_b2']
WEIGHTS = IN_NAMES[3:]
SHARD_AXIS = {'ada_w': 2, 'w_in': 2, 'conv_w': 2, 'rg_lambda': 2, 'rg_ba': 2, 'rg_bi': 2, 's5_glu_w': 1, 'w_out': 1,
              'mlp_w1': 2, 'mlp_w2': 1}
SHARDED = ['conv_w', 'rg_lambda', 'rg_ba', 'rg_bi']
REPLICATED = [n for n in WEIGHTS if n not in SHARD_AXIS]
GATHER_BF16 = ['ada_w', 'w_in', 's5_glu_w', 'w_out', 'mlp_w1', 'mlp_w2']
GATHER_F32 = ['conv_w', 'rg_lambda', 'rg_ba', 'rg_bi']


def _params(n_axes, vmem_mb=40):
    return pltpu.CompilerParams(dimension_semantics=("arbitrary",) * n_axes, vmem_limit_bytes=vmem_mb * MIB)


def _tile(n, options):
    for t in options:
        if n % t == 0:
            return t
    return n


def _sigmoid(z):
    return 1.0 / (1.0 + jnp.exp(-z))


def _softplus(z):
    return jnp.maximum(z, 0.0) + jnp.log(1.0 + jnp.exp(-jnp.abs(z)))


def _neg_expm1(z):
    series = -z * (1.0 + 0.5 * z * (1.0 + (1.0 / 3.0) * z * (1.0 + 0.25 * z)))
    return jnp.where(jnp.abs(z) < 1e-2, series, 1.0 - jnp.exp(z))


_G0 = math.sqrt(2.0 / math.pi)
_G1 = 0.044715


def _gelu(v):
    return 0.5 * v * (1.0 + jnp.tanh(_G0 * (v + _G1 * v * v * v)))


def _gelu_and_grad(v):
    t = jnp.tanh(_G0 * (v + _G1 * v * v * v))
    g = 0.5 * v * (1.0 + t)
    dg = 0.5 * (1.0 + t) + 0.5 * v * (1.0 - t * t) * _G0 * (1.0 + 3.0 * _G1 * v * v)
    return g, dg


def _seq_of_tile(i, tps):
    return 2 * (i // tps) + jnp.minimum(i % tps, 1)


def _dot(a, b, dims=(((1,), (0,)), ((), ()))):
    return lax.dot_general(a, b, dims, preferred_element_type=F32)


_NT = (((1,), (1,)), ((), ()))
_TN = (((0,), (0,)), ((), ()))


def _mm_nn(a, b, bias=None, *, relu2=False, name):
    m, k = a.shape
    n = b.shape[1]
    tm, tn, tk = _tile(m, (512, 256)), _tile(n, (1024,)), _tile(k, (1024,))
    nk = k // tk
    has_bias = bias is not None

    def body(*refs):
        a_ref, b_ref = refs[0], refs[1]
        bias_ref = refs[2] if has_bias else None
        outs = refs[2 + has_bias:-1]
        acc = refs[-1]
        kk = pl.program_id(2)

        @pl.when(kk == 0)
        def _():
            acc[...] = jnp.zeros_like(acc)

        acc[...] += _dot(a_ref[...], b_ref[...])

        @pl.when(kk == nk - 1)
        def _():
            h = acc[...]
            if has_bias:
                h = h + bias_ref[...]
            if relu2:
                r = jnp.maximum(h, 0.0)
                outs[0][...] = (r * r).astype(BF16)
                outs[1][...] = h.astype(BF16)
            else:
                outs[0][...] = h

    in_specs = [pl.BlockSpec((tm, tk), lambda j, i, kk: (i, kk)), pl.BlockSpec((tk, tn), lambda j, i, kk: (kk, j))]
    args = [a, b]
    if has_bias:
        in_specs.append(pl.BlockSpec((1, tn), lambda j, i, kk: (0, j)))
        args.append(bias)
    o_spec = pl.BlockSpec((tm, tn), lambda j, i, kk: (i, j))
    if relu2:
        out_shape = (jax.ShapeDtypeStruct((m, n), BF16), jax.ShapeDtypeStruct((m, n), BF16))
        out_specs = (o_spec, o_spec)
    else:
        out_shape, out_specs = jax.ShapeDtypeStruct((m, n), F32), o_spec
    return pl.pallas_call(body, name=name, grid=(n // tn, m // tm, nk), in_specs=in_specs, out_specs=out_specs,
                          out_shape=out_shape, scratch_shapes=[pltpu.VMEM((tm, tn), F32)],
                          compiler_params=_params(3))(*args)


def _mm_nt(a, b, hb=None, *, name):
    m, n = a.shape
    k = b.shape[0]
    tm, tn, tk = _tile(m, (512, 256)), _tile(k, (1024,)), _tile(n, (1024,))
    nk = n // tk
    fused = hb is not None

    def body(*refs):
        a_ref, b_ref = refs[0], refs[1]
        hb_ref = refs[2] if fused else None
        o_ref, acc = refs[-2], refs[-1]
        kk = pl.program_id(2)

        @pl.when(kk == 0)
        def _():
            acc[...] = jnp.zeros_like(acc)

        acc[...] += _dot(a_ref[...], b_ref[...], _NT)

        @pl.when(kk == nk - 1)
        def _():
            if fused:
                o_ref[...] = (acc[...] * (2.0 * jnp.maximum(hb_ref[...].astype(F32), 0.0))).astype(BF16)
            else:
                o_ref[...] = acc[...]

    in_specs = [pl.BlockSpec((tm, tk), lambda j, i, kk: (i, kk)), pl.BlockSpec((tn, tk), lambda j, i, kk: (j, kk))]
    args = [a, b]
    if fused:
        in_specs.append(pl.BlockSpec((tm, tn), lambda j, i, kk: (i, j)))
        args.append(hb)
    return pl.pallas_call(body, name=name, grid=(k // tn, m // tm, nk), in_specs=in_specs,
                          out_specs=pl.BlockSpec((tm, tn), lambda j, i, kk: (i, j)),
                          out_shape=jax.ShapeDtypeStruct((m, k), BF16 if fused else F32),
                          scratch_shapes=[pltpu.VMEM((tm, tn), F32)], compiler_params=_params(3))(*args)


def _mm_tn(a, b, *, name, layer=None, into=None):
    m, k = a.shape
    n = b.shape[1]
    tk, tn, tr = _tile(k, (1024,)), _tile(n, (1024,)), _tile(m, (512, 256))
    nr = m // tr

    def body(a_ref, b_ref, *rest):
        o_ref, acc = rest[-2], rest[-1]
        r = pl.program_id(2)

        @pl.when(r == 0)
        def _():
            acc[...] = jnp.zeros_like(acc)

        acc[...] += _dot(a_ref[...], b_ref[...], _TN)

        @pl.when(r == nr - 1)
        def _():
            o_ref[...] = acc[...]

    in_specs = [pl.BlockSpec((tr, tk), lambda i, j, r: (r, i)), pl.BlockSpec((tr, tn), lambda i, j, r: (r, j))]
    args, aliases = [a, b], {}
    if layer is None:
        out_spec, out_shape = pl.BlockSpec((tk, tn), lambda i, j, r: (i, j)), jax.ShapeDtypeStruct((k, n), F32)
    else:
        out_spec = pl.BlockSpec((None, tk, tn), lambda i, j, r: (layer, i, j))
        out_shape = jax.ShapeDtypeStruct((DEPTH, k, n), F32)
        if into is not None:
            in_specs.append(_ANY)
            args.append(into)
            aliases = {2: 0}
    return pl.pallas_call(body, name=name, grid=(k // tk, n // tn, nr), in_specs=in_specs, out_specs=out_spec,
                          out_shape=out_shape, input_output_aliases=aliases,
                          scratch_shapes=[pltpu.VMEM((tk, tn), F32)], compiler_params=_params(3))(*args)


def _colsum(v, *, name):
    m, n = v.shape
    tn, tr = _tile(n, (1024,)), _tile(m, (512, 256))

    def body(v_ref, o_ref):
        @pl.when(pl.program_id(1) == 0)
        def _():
            o_ref[...] = jnp.zeros_like(o_ref)

        o_ref[...] += jnp.sum(v_ref[...].astype(F32), axis=0, keepdims=True)

    return pl.pallas_call(body, name=name, grid=(n // tn, m // tr),
                          in_specs=[pl.BlockSpec((tr, tn), lambda j, r: (r, j))],
                          out_specs=pl.BlockSpec((1, tn), lambda j, r: (0, j)),
                          out_shape=jax.ShapeDtypeStruct((1, n), F32), compiler_params=_params(2))(v)


def _tok_spec(d=D_MODEL, col=0):
    return pl.BlockSpec((TM, d), lambda i: (i, col))


def _mod_spec(tps):
    return pl.BlockSpec((1, N_MOD, D_MODEL), lambda i: (_seq_of_tile(i, tps), 0, 0))


def _row_spec(d=D_MODEL):
    return pl.BlockSpec((1, d), lambda i: (0, 0))


def _seq_acc_spec(tps):
    return pl.BlockSpec((1, 1, D_MODEL), lambda i: (_seq_of_tile(i, tps), 0, 0))


def _modulate(xs, modall, k_shift, k_scale, tps, *, name):
    n = xs.shape[0]

    def body(x_ref, m_ref, o_ref):
        sh = m_ref[0, k_shift:k_shift + 1, :]
        sc = m_ref[0, k_scale:k_scale + 1, :]
        o_ref[...] = (x_ref[...] * (1.0 + sc) + sh).astype(BF16)

    return pl.pallas_call(body, name=name, grid=(n // TM,), in_specs=[_tok_spec(), _mod_spec(tps)],
                          out_specs=_tok_spec(), out_shape=jax.ShapeDtypeStruct((n, D_MODEL), BF16),
                          compiler_params=_params(1))(xs, modall)


def _resid_ln(xs, ms, modall, k_gate, g, b, tps, *, name):
    n = xs.shape[0]

    def body(x_ref, m_ref, mod_ref, g_ref, b_ref, o_ref):
        z = ALPHA * x_ref[...] + mod_ref[0, k_gate:k_gate + 1, :] * m_ref[...]
        mu = jnp.mean(z, axis=-1, keepdims=True)
        zc = z - mu
        var = jnp.mean(zc * zc, axis=-1, keepdims=True)
        o_ref[...] = zc * lax.rsqrt(var + LN_EPS) * g_ref[...] + b_ref[...]

    return pl.pallas_call(body, name=name, grid=(n // TM,),
                          in_specs=[_tok_spec(), _tok_spec(), _mod_spec(tps), _row_spec(), _row_spec()],
                          out_specs=_tok_spec(), out_shape=jax.ShapeDtypeStruct((n, D_MODEL), F32),
                          compiler_params=_params(1))(xs, ms, modall, g, b)


def _resid_ln_bwd(xs, ms, modall, k_gate, g, dout, tps, *, name):
    n = xs.shape[0]

    def body(x_ref, m_ref, mod_ref, g_ref, d_ref, dxa_ref, dm_ref, dbias_ref, dg_ref, db_ref, dgate_ref):
        i = pl.program_id(0)
        gate = mod_ref[0, k_gate:k_gate + 1, :]
        m = m_ref[...]
        z = ALPHA * x_ref[...] + gate * m
        mu = jnp.mean(z, axis=-1, keepdims=True)
        zc = z - mu
        var = jnp.mean(zc * zc, axis=-1, keepdims=True)
        rstd = lax.rsqrt(var + LN_EPS)
        xhat = zc * rstd
        d = d_ref[...]
        dxh = d * g_ref[...]
        dz = rstd * (dxh - jnp.mean(dxh, axis=-1, keepdims=True)
                     - xhat * jnp.mean(dxh * xhat, axis=-1, keepdims=True))
        dxa_ref[...] = ALPHA * dz
        dm = gate * dz
        dm_ref[...] = dm.astype(BF16)

        @pl.when(i == 0)
        def _():
            dbias_ref[...] = jnp.zeros_like(dbias_ref)
            dg_ref[...] = jnp.zeros_like(dg_ref)
            db_ref[...] = jnp.zeros_like(db_ref)

        dbias_ref[...] += jnp.sum(dm, axis=0, keepdims=True)
        dg_ref[...] += jnp.sum(d * xhat, axis=0, keepdims=True)
        db_ref[...] += jnp.sum(d, axis=0, keepdims=True)
        part = jnp.sum(dz * m, axis=0, keepdims=True)

        @pl.when(i % tps <= 1)
        def _():
            dgate_ref[0] = part

        @pl.when(i % tps > 1)
        def _():
            dgate_ref[0] += part

    row = jax.ShapeDtypeStruct((1, D_MODEL), F32)
    return pl.pallas_call(
        body, name=name, grid=(n // TM,),
        in_specs=[_tok_spec(), _tok_spec(), _mod_spec(tps), _row_spec(), _tok_spec()],
        out_specs=(_tok_spec(), _tok_spec(), _row_spec(), _row_spec(), _row_spec(), _seq_acc_spec(tps)),
        out_shape=(jax.ShapeDtypeStruct((n, D_MODEL), F32), jax.ShapeDtypeStruct((n, D_MODEL), BF16), row, row, row,
                   jax.ShapeDtypeStruct((n // TM // tps * 2, 1, D_MODEL), F32)),
        compiler_params=_params(1))(xs, ms, modall, g, dout)


def _modulate_bwd(du, xs, modall, k_scale, dxa, tps, *, name):
    n = xs.shape[0]

    def body(du_ref, x_ref, mod_ref, dxa_ref, dx_ref, dsc_ref, dsh_ref):
        i = pl.program_id(0)
        du_t = du_ref[...]
        dx_ref[...] = dxa_ref[...] + du_t * (1.0 + mod_ref[0, k_scale:k_scale + 1, :])
        psc = jnp.sum(du_t * x_ref[...], axis=0, keepdims=True)
        psh = jnp.sum(du_t, axis=0, keepdims=True)

        @pl.when(i % tps <= 1)
        def _():
            dsc_ref[0] = psc
            dsh_ref[0] = psh

        @pl.when(i % tps > 1)
        def _():
            dsc_ref[0] += psc
            dsh_ref[0] += psh

    acc = jax.ShapeDtypeStruct((n // TM // tps * 2, 1, D_MODEL), F32)
    return pl.pallas_call(body, name=name, grid=(n // TM,),
                          in_specs=[_tok_spec(), _tok_spec(), _mod_spec(tps), _tok_spec()],
                          out_specs=(_tok_spec(), _seq_acc_spec(tps), _seq_acc_spec(tps)),
                          out_shape=(jax.ShapeDtypeStruct((n, D_MODEL), F32), acc, acc),
                          compiler_params=_params(1))(du, xs, modall, dxa)


def _loss_head(ys, target, tps, *, name):
    n = ys.shape[0]
    lat_tiles = tps - 1

    def body(y_ref, t_ref, acc_ref, dy_ref):
        i = pl.program_id(0)

        @pl.when(i == 0)
        def _():
            acc_ref[...] = jnp.zeros_like(acc_ref)

        @pl.when(i % tps == 0)
        def _():
            dy_ref[...] = jnp.zeros_like(dy_ref)

        @pl.when(i % tps > 0)
        def _():
            e = y_ref[...] - t_ref[...]
            dy_ref[...] = e * (1.0 / D_MODEL)
            acc_ref[...] += jnp.sum(e * e, axis=0, keepdims=True)

    t_spec = pl.BlockSpec((TM, D_MODEL), lambda i: ((i // tps) * lat_tiles + jnp.maximum(i % tps - 1, 0), 0))
    return pl.pallas_call(body, name=name, grid=(n // TM,), in_specs=[_tok_spec(), t_spec],
                          out_specs=(_row_spec(), _tok_spec()),
                          out_shape=(jax.ShapeDtypeStruct((1, D_MODEL), F32), jax.ShapeDtypeStruct((n, D_MODEL), F32)),
                          compiler_params=_params(1))(ys, target)


def _glu_fwd(y_nat, p, d_skip, w, b, *, name):
    n = y_nat.shape[0]

    def body(y_ref, u_ref, d_ref, w_ref, b_ref, o_ref):
        g = _gelu(y_ref[...] + d_ref[...] * u_ref[...])
        z = _dot(g.astype(BF16), w_ref[...]) + b_ref[...]
        o_ref[...] = (g * _sigmoid(z)).astype(BF16)

    return pl.pallas_call(body, name=name, grid=(n // TM,),
                          in_specs=[_tok_spec(), _tok_spec(col=2), _row_spec(),
                                    pl.BlockSpec((D_MODEL, D_MODEL), lambda i: (0, 0)), _row_spec()],
                          out_specs=_tok_spec(), out_shape=jax.ShapeDtypeStruct((n, D_MODEL), BF16),
                          compiler_params=_params(1))(y_nat, p, d_skip, w, b)


def _glu_bwd(dcat, y_nat, p, d_skip, w, b, *, name):
    n = y_nat.shape[0]

    def body(ds_ref, y_ref, u_ref, d_ref, w_ref, b_ref, dy_ref, dsk_ref, g_ref, dz_ref, dd_ref, dbz_ref):
        u = u_ref[...]
        g, gg = _gelu_and_grad(y_ref[...] + d_ref[...] * u)
        s = _sigmoid(_dot(g.astype(BF16), w_ref[...]) + b_ref[...])
        ds = ds_ref[...]
        dz = ds * g * s * (1.0 - s)
        dzb = dz.astype(BF16)
        dg = ds * s + _dot(dzb, w_ref[...], _NT)
        dyp = dg * gg
        dy_ref[...] = dyp.astype(BF16)
        dsk_ref[...] = dyp * d_ref[...]
        g_ref[...] = g.astype(BF16)
        dz_ref[...] = dzb

        @pl.when(pl.program_id(0) == 0)
        def _():
            dd_ref[...] = jnp.zeros_like(dd_ref)
            dbz_ref[...] = jnp.zeros_like(dbz_ref)

        dd_ref[...] += jnp.sum(dyp * u, axis=0, keepdims=True)
        dbz_ref[...] += jnp.sum(dz, axis=0, keepdims=True)

    tok_bf = jax.ShapeDtypeStruct((n, D_MODEL), BF16)
    row = jax.ShapeDtypeStruct((1, D_MODEL), F32)
    return pl.pallas_call(
        body, name=name, grid=(n // TM,),
        in_specs=[_tok_spec(col=1), _tok_spec(), _tok_spec(col=2), _row_spec(),
                  pl.BlockSpec((D_MODEL, D_MODEL), lambda i: (0, 0)), _row_spec()],
        out_specs=(_tok_spec(), _tok_spec(), _tok_spec(), _tok_spec(), _row_spec(), _row_spec()),
        out_shape=(tok_bf, jax.ShapeDtypeStruct((n, D_MODEL), F32), tok_bf, tok_bf, row, row),
        compiler_params=_params(1))(dcat, y_nat, p, d_skip, w, b)


def _add_cast(a, b, *, name):
    n = a.shape[0]

    def body(a_ref, b_ref, o_ref):
        o_ref[...] = (a_ref[...] + b_ref[...]).astype(BF16)

    return pl.pallas_call(body, name=name, grid=(n // TM,), in_specs=[_tok_spec(), _tok_spec()],
                          out_specs=_tok_spec(), out_shape=jax.ShapeDtypeStruct((n, D_MODEL), BF16),
                          compiler_params=_params(1))(a, b)


def _silu_rows(c16, *, name):
    def body(c_ref, s_ref, ds_ref):
        v = c_ref[...]
        sg = _sigmoid(v)
        s_ref[...] = v * sg
        ds_ref[...] = sg * (1.0 + v * (1.0 - sg))

    shp = jax.ShapeDtypeStruct(c16.shape, F32)
    return pl.pallas_call(body, name=name, out_shape=(shp, shp))(c16)


def _mul_rows(a, b, *, name):
    def body(a_ref, b_ref, o_ref):
        o_ref[...] = a_ref[...] * b_ref[...]

    return pl.pallas_call(body, name=name, out_shape=jax.ShapeDtypeStruct(a.shape, F32))(a, b)


def _pad_off(c):
    return pl.multiple_of(c * TM + 8 + 8 * jnp.minimum(c, 1), 8)


def _rows8(k):
    return pl.ds(pl.multiple_of(k * 8, 8), 8)


def _windows(buf, c, shifts):
    n = TM + 16
    win = buf[pl.ds(pl.multiple_of(_pad_off(c) - 8, 8), n), :]
    return [win[8:8 + TM] if k == 0 else pltpu.roll(win, (-k) % n, 0)[8:8 + TM] for k in shifts]


def _conv_window(xpad, c):
    return _windows(xpad, c, (-1, 0, 1, 2))


def _rg_coeffs(z, d, spl, xc):
    r = _sigmoid(z[:, 256 * d:256 * d + HEAD])
    i = _sigmoid(z[:, 256 * d + HEAD:256 * d + 2 * HEAD])
    la = -RG_C * spl[d:d + 1, :] * r
    a = jnp.exp(la)
    mult = jnp.sqrt(_neg_expm1(2.0 * la))
    return r, i, a, mult, a * a


def _zero_pads(buf, s, lc):
    z8 = jnp.zeros((8, HEAD), F32)
    buf[0:8, :] = z8
    buf[8 + lc:16 + lc, :] = z8
    buf[16 + s:24 + s, :] = z8


def _rg_fwd(p, conv_w, conv_b, lam, wcat, bcat, bl, s, lc, *, name):
    nch = s // TM

    def body(x_ref, gate_ref, cw_ref, cb_ref, lam_ref, w_ref, b_ref, rg_ref, hf_ref, hb_ref, xpad, af, bf, ab, bb):
        _zero_pads(xpad, s, lc)

        def copy_chunk(c, _):
            xpad[pl.ds(_pad_off(c), TM), :] = x_ref[pl.ds(pl.multiple_of(c * TM, TM), TM), :]
            return 0

        lax.fori_loop(0, nch, copy_chunk, 0)
        spl = _softplus(-lam_ref[...])
        cw = cw_ref[...]

        def coef_chunk(c, _):
            xm1, x0, xp1, xp2 = _conv_window(xpad, c)
            xc = cw[0:1] * xm1 + cw[1:2] * x0 + cw[2:3] * xp1 + cw[3:4] * xp2 + cb_ref[...]
            z = _dot(xc.astype(BF16), w_ref[0]) + b_ref[0]
            rows = pl.ds(pl.multiple_of(c * TM, TM), TM)
            for d, (a_s, b_s) in enumerate(((af, bf), (ab, bb))):
                _, i, a, mult, _ = _rg_coeffs(z, d, spl, xc)
                a_s[rows, :] = a
                b_s[rows, :] = mult * i * xc
            return 0

        lax.fori_loop(0, nch, coef_chunk, 0)

        rid = lax.broadcasted_iota(jnp.int32, (8, HEAD), 0)

        def step8(kf, kb, carry):
            hf, hb = carry
            rf, rb = _rows8(kf), _rows8(kb)
            a1, b1, a2, b2 = af[rf, :], bf[rf, :], ab[rb, :], bb[rb, :]
            of = ob = jnp.zeros((8, HEAD), F32)
            for i in range(8):
                k = 7 - i
                hf = a1[i:i + 1] * hf + b1[i:i + 1]
                of = jnp.where(rid == i, hf, of)
                hb = a2[k:k + 1] * hb + b2[k:k + 1]
                ob = jnp.where(rid == k, hb, ob)
            hf_ref[rf, :] = of
            hb_ref[rb, :] = ob
            return hf, hb

        zero = jnp.zeros((1, HEAD), F32)
        nbc, nb = lc // 8, s // 8
        carry = lax.fori_loop(0, nbc, lambda j, cr: step8(j, nbc - 1 - j, cr), (zero, zero))
        lax.fori_loop(nbc, nb, lambda j, cr: step8(j, nb + nbc - 1 - j, cr), carry)

        def out_chunk(c, _):
            rows = pl.ds(pl.multiple_of(c * TM, TM), TM)
            rg_ref[rows, :] = ((hf_ref[rows, :] + hb_ref[rows, :]) * _gelu(gate_ref[rows, :])).astype(BF16)
            return 0

        lax.fori_loop(0, nch, out_chunk, 0)

    seq = lambda col0: pl.BlockSpec((s, HEAD), lambda b, h: (b, col0 + h))
    par = lambda r: pl.BlockSpec((r, HEAD), lambda b, h: (0, h))
    n = bl * s
    return pl.pallas_call(
        body, name=name, grid=(bl, RG_HEADS),
        in_specs=[seq(0), seq(RG_HEADS), par(4), par(1), par(2),
                  pl.BlockSpec((1, HEAD, 4 * HEAD), lambda b, h: (h, 0, 0)),
                  pl.BlockSpec((1, 1, 4 * HEAD), lambda b, h: (h, 0, 0))],
        out_specs=(seq(0), seq(0), seq(0)),
        out_shape=(jax.ShapeDtypeStruct((n, D_MODEL), BF16), jax.ShapeDtypeStruct((n, D_MODEL), F32),
                   jax.ShapeDtypeStruct((n, D_MODEL), F32)),
        scratch_shapes=[pltpu.VMEM((s + 24, HEAD), F32)] + [pltpu.VMEM((s, HEAD), F32)] * 4,
        compiler_params=_params(2, 48))(p, p, conv_w, conv_b, lam, wcat, bcat)


def _rg_bwd(p, dcat, hf, hb, conv_w, conv_b, lam, wcat, bcat, bl, s, lc, *, name):
    nch = s // TM
    ll = s - lc

    def body(p_hbm, dcat_hbm, hf_hbm, hb_hbm, cw_ref, cb_ref, lam_ref, w_ref, b_ref,
             drgx_ref, dgate_ref, dcw_ref, dcb_ref, dlam_ref, dw_ref, db_ref,
             xpad, dxpad, hf_s, hb_s, gate_s, dhs, a_f, a_b, lam_f, lam_b, sems):
        h = pl.program_id(0)
        b = pl.program_id(1)
        row0 = b * s
        col = pl.multiple_of(h * HEAD, HEAD)

        def rows_of(ref, r0, nr, c0):
            return ref.at[pl.ds(row0 + r0, nr), pl.ds(c0, HEAD)]

        copies = [
            pltpu.make_async_copy(rows_of(p_hbm, 0, lc, col), xpad.at[pl.ds(8, lc), :], sems.at[0]),
            pltpu.make_async_copy(rows_of(p_hbm, lc, ll, col), xpad.at[pl.ds(16 + lc, ll), :], sems.at[1]),
            pltpu.make_async_copy(rows_of(p_hbm, 0, s, col + D_MODEL), gate_s, sems.at[2]),
            pltpu.make_async_copy(rows_of(dcat_hbm, 0, s, col), dhs, sems.at[3]),
            pltpu.make_async_copy(rows_of(hf_hbm, 0, s, col), hf_s.at[pl.ds(8, s), :], sems.at[4]),
            pltpu.make_async_copy(rows_of(hb_hbm, 0, s, col), hb_s.at[pl.ds(8, s), :], sems.at[5]),
        ]
        for cp in copies:
            cp.start()
        _zero_pads(xpad, s, lc)
        _zero_pads(dxpad, s, lc)
        for buf in (hf_s, hb_s):
            buf[0:8, :] = jnp.zeros((8, HEAD), F32)
            buf[8 + s:16 + s, :] = jnp.zeros((8, HEAD), F32)

        @pl.when(b == 0)
        def _():
            dcw_ref[...] = jnp.zeros_like(dcw_ref)
            dcb_ref[...] = jnp.zeros_like(dcb_ref)
            dlam_ref[...] = jnp.zeros_like(dlam_ref)
            dw_ref[...] = jnp.zeros_like(dw_ref)
            db_ref[...] = jnp.zeros_like(db_ref)

        for cp in copies:
            cp.wait()
        lam_v = lam_ref[...]
        spl = _softplus(-lam_v)
        cw = cw_ref[...]

        def conv(c):
            xm1, x0, xp1, xp2 = _conv_window(xpad, c)
            return cw[0:1] * xm1 + cw[1:2] * x0 + cw[2:3] * xp1 + cw[3:4] * xp2 + cb_ref[...]

        def pass_a(c, _):
            rows = pl.ds(pl.multiple_of(c * TM, TM), TM)
            xc = conv(c)
            z = _dot(xc.astype(BF16), w_ref[0]) + b_ref[0]
            for d, a_s in enumerate((a_f, a_b)):
                a_s[rows, :] = _rg_coeffs(z, d, spl, xc)[2]
            g, gg = _gelu_and_grad(gate_s[rows, :])
            drg = dhs[rows, :]
            hrows = pl.ds(pl.multiple_of(c * TM + 8, 8), TM)
            dgate_ref[rows, :] = (drg * (hf_s[hrows, :] + hb_s[hrows, :]) * gg).astype(BF16)
            dhs[rows, :] = drg * g
            return 0

        lax.fori_loop(0, nch, pass_a, 0)

        rid = lax.broadcasted_iota(jnp.int32, (8, HEAD), 0)

        def adj8(kf, kb, carry):
            cf, cb_ = carry
            rf, rb = _rows8(kf), _rows8(kb)
            d1, a1, d2, a2 = dhs[rf, :], a_f[rf, :], dhs[rb, :], a_b[rb, :]
            of = ob = jnp.zeros((8, HEAD), F32)
            for i in range(8):
                k = 7 - i
                lf = d1[k:k + 1] + cf
                of = jnp.where(rid == k, lf, of)
                cf = a1[k:k + 1] * lf
                lb = d2[i:i + 1] + cb_
                ob = jnp.where(rid == i, lb, ob)
                cb_ = a2[i:i + 1] * lb
            lam_f[rf, :] = of
            lam_b[rb, :] = ob
            return cf, cb_

        zero = jnp.zeros((1, HEAD), F32)
        nbc, nb = lc // 8, s // 8
        carry = lax.fori_loop(0, nb - nbc, lambda j, cr: adj8(nb - 1 - j, nbc + j, cr), (zero, zero))
        lax.fori_loop(0, nbc, lambda j, cr: adj8(nbc - 1 - j, j, cr), carry)

        sig_neg = _sigmoid(-lam_v)
        last_row = lax.broadcasted_iota(jnp.int32, (TM, HEAD), 0) == TM - 1
        hb_first = hb_s[8:9, :]

        def pass_b(c, _):
            rows = pl.ds(pl.multiple_of(c * TM, TM), TM)
            xc = conv(c)
            xcb = xc.astype(BF16)
            z = _dot(xcb, w_ref[0]) + b_ref[0]
            dxc = jnp.zeros((TM, HEAD), F32)
            dzs = []
            n = TM + 16
            hp_f = pltpu.roll(hf_s[pl.ds(pl.multiple_of(c * TM, TM), n), :], 1, 0)[8:8 + TM]
            hp_b = pltpu.roll(hb_s[pl.ds(pl.multiple_of(c * TM, TM), n), :], n - 1, 0)[8:8 + TM]
            hp_b = jnp.where(last_row & (c == 0), 0.0, hp_b)
            hp_b = jnp.where(last_row & (c == nch - 1), hb_first, hp_b)
            for d, (l_s, hp) in enumerate(((lam_f, hp_f), (lam_b, hp_b))):
                r, i, a, mult, e2 = _rg_coeffs(z, d, spl, xc)
                dbt = l_s[rows, :]
                dla = dbt * hp * a - dbt * i * xc * (e2 / mult)
                dlam_ref[0, d:d + 1, :] += jnp.sum(dla * r, axis=0, keepdims=True) * (RG_C * sig_neg[d:d + 1, :])
                dr = dla * (-RG_C * spl[d:d + 1, :])
                di = dbt * mult * xc
                dxc = dxc + dbt * mult * i
                dzs += [dr * r * (1.0 - r), di * i * (1.0 - i)]
            dz = jnp.concatenate(dzs, axis=1)
            dzb = dz.astype(BF16)
            dxc = dxc + _dot(dzb, w_ref[0], _NT)
            dw_ref[0] += _dot(xcb, dzb, _TN)
            db_ref[0] += jnp.sum(dz, axis=0, keepdims=True)
            dcb_ref[0] += jnp.sum(dxc, axis=0, keepdims=True)
            dxpad[pl.ds(_pad_off(c), TM), :] = dxc
            return 0

        lax.fori_loop(0, nch, pass_b, 0)

        def pass_c(c, _):
            rows = pl.ds(pl.multiple_of(c * TM, TM), TM)
            gp1, g0, gm1, gm2 = _windows(dxpad, c, (1, 0, -1, -2))
            drgx_ref[rows, :] = (cw[0:1] * gp1 + cw[1:2] * g0 + cw[2:3] * gm1 + cw[3:4] * gm2).astype(BF16)
            xm1, x0, xp1, xp2 = _conv_window(xpad, c)
            dcw_ref[0] += jnp.concatenate([jnp.sum(g0 * t, axis=0, keepdims=True) for t in (xm1, x0, xp1, xp2)],
                                          axis=0)
            return 0

        lax.fori_loop(0, nch, pass_c, 0)

    seq = pl.BlockSpec((s, HEAD), lambda h, b: (b, h))
    par = lambda r: pl.BlockSpec((r, HEAD), lambda h, b: (0, h))
    acc = lambda r, w: pl.BlockSpec((1, r, w), lambda h, b: (h, 0, 0))
    anyspec = pl.BlockSpec(memory_space=pl.ANY)
    n = bl * s
    big = pltpu.VMEM((s, HEAD), F32)
    return pl.pallas_call(
        body, name=name, grid=(RG_HEADS, bl),
        in_specs=[anyspec, anyspec, anyspec, anyspec, par(4), par(1), par(2), acc(HEAD, 4 * HEAD), acc(1, 4 * HEAD)],
        out_specs=(seq, seq, acc(4, HEAD), acc(1, HEAD), acc(2, HEAD), acc(HEAD, 4 * HEAD), acc(1, 4 * HEAD)),
        out_shape=(jax.ShapeDtypeStruct((n, D_MODEL), BF16), jax.ShapeDtypeStruct((n, D_MODEL), BF16),
                   jax.ShapeDtypeStruct((RG_HEADS, 4, HEAD), F32), jax.ShapeDtypeStruct((RG_HEADS, 1, HEAD), F32),
                   jax.ShapeDtypeStruct((RG_HEADS, 2, HEAD), F32),
                   jax.ShapeDtypeStruct((RG_HEADS, HEAD, 4 * HEAD), F32),
                   jax.ShapeDtypeStruct((RG_HEADS, 1, 4 * HEAD), F32)),
        scratch_shapes=[pltpu.VMEM((s + 24, HEAD), F32)] * 2 + [pltpu.VMEM((s + 16, HEAD), F32)] * 2 + [big] * 6
        + [pltpu.SemaphoreType.DMA((6,))],
        compiler_params=_params(2, 52))(p, dcat, hf, hb, conv_w, conv_b, lam, wcat, bcat)


def _s5_mats(a_re, a_im, log_dt, b_re, b_im, c_re, c_im):
    t = T_CH
    g = a_re.shape[1]
    dt = jnp.exp(log_dt)[..., None]
    lr, li = a_re * dt, a_im * dt
    steps = jnp.arange(t + 1, dtype=F32)[:, None]
    mag = jnp.exp(lr[:, :, None, :] * steps)
    ang = li[:, :, None, :] * steps
    pr, pi = mag * jnp.cos(ang), mag * jnp.sin(ang)
    xr, xi = pr[:, :, 1] - 1.0, pi[:, :, 1]
    den = a_re * a_re + a_im * a_im
    qr, qi = (xr * a_re + xi * a_im) / den, (xi * a_re - xr * a_im) / den
    btr, bti = b_re.transpose(0, 1, 3, 2), b_im.transpose(0, 1, 3, 2)
    bbr = qr[:, :, None, :] * btr - qi[:, :, None, :] * bti
    bbi = qr[:, :, None, :] * bti + qi[:, :, None, :] * btr
    up, down = slice(0, t), slice(t - 1, None, -1)

    def pow_c(d, sl):
        wr, wi = pr[d][:, sl, None, :], pi[d][:, sl, None, :]
        cr, ci = c_re[d][:, None], c_im[d][:, None]
        return (wr * cr - wi * ci).reshape(g, CW, S5_STATE), (wr * ci + wi * cr).reshape(g, CW, S5_STATE)

    hp = lax.Precision.HIGHEST

    def lag_map(d, sl):
        re, im = pow_c(d, sl)
        return (jnp.einsum('gkp,gmp->gkm', bbr[d], re, precision=hp)
                - jnp.einsum('gkp,gmp->gkm', bbi[d], im, precision=hp))

    z_f, z_b = lag_map(0, up), lag_map(1, down)
    kf = jnp.stack([jnp.pad(z_f, ((0, 0), (0, 0), (S5_GROUP * s, 0)))[:, :, :CW] for s in range(t)], axis=1)
    kb = jnp.stack([jnp.pad(z_b, ((0, 0), (0, 0), (0, S5_GROUP * (t - 1 - s))))[:, :, S5_GROUP * (t - 1 - s):]
                    for s in range(t)], axis=1)
    kcat = (kf + kb).reshape(g, CW, CW)

    def state_in(d, sl):
        wr, wi = pr[d][:, sl, None, :], pi[d][:, sl, None, :]
        br, bi = bbr[d][:, None], bbi[d][:, None]
        return jnp.concatenate([wr * br - wi * bi, wr * bi + wi * br], axis=-1).reshape(g, CW, SW)

    wcat = jnp.concatenate([kcat, state_in(0, down), state_in(1, up)], axis=2)
    of_r, of_i = pow_c(0, slice(1, t + 1))
    ob_r, ob_i = pow_c(1, slice(t, 0, -1))
    mout_t = jnp.concatenate([of_r, -of_i, ob_r, -ob_i], axis=2)
    rows = []
    for d in range(2):
        art, ait = pr[d][:, t], pi[d][:, t]
        rows += [jnp.concatenate([art, art], axis=1).reshape(-1), jnp.concatenate([-ait, ait], axis=1).reshape(-1)]
    return wcat, mout_t, jnp.stack(rows)


def _lane_swap(v):
    return pltpu.roll(v, S5_STATE, 1)


def _grp(g, w):
    return slice(g * w, (g + 1) * w)


def _s5_fwd(u, wcat, mout, a2, ncc, *, name):
    bl, nc, _ = u.shape

    def body(u_ref, w_ref, mo_ref, a_ref, y_ref, sf_ref, sb_ref, vf, vb):
        for g in range(GB):
            zu = _dot(u_ref[:, _grp(g, CW)], w_ref[g])
            y_ref[:, _grp(g, CW)] = zu[:, :CW]
            vf[:, _grp(g, SW)] = zu[:, CW:CW + SW]
            vb[:, _grp(g, SW)] = zu[:, CW + SW:]
        co = [[a_ref[r:r + 1, _grp(g, SW)] for g in range(GB)] for r in range(4)]

        rid = lax.broadcasted_iota(jnp.int32, (8, SW), 0)

        def step8(kf, kb, carry):
            rf, rb = _rows8(kf), _rows8(kb)
            vfb, vbb = vf[rf, :], vb[rb, :]
            out = []
            for g in range(GB):
                sf, sb = carry[2 * g], carry[2 * g + 1]
                of = ob = jnp.zeros((8, SW), F32)
                for i in range(8):
                    k = 7 - i
                    of = jnp.where(rid == i, sf, of)
                    sf = co[0][g] * sf + co[1][g] * _lane_swap(sf) + vfb[i:i + 1, _grp(g, SW)]
                    ob = jnp.where(rid == k, sb, ob)
                    sb = co[2][g] * sb + co[3][g] * _lane_swap(sb) + vbb[k:k + 1, _grp(g, SW)]
                sf_ref[rf, _grp(g, SW)] = of
                sb_ref[rb, _grp(g, SW)] = ob
                out += [sf, sb]
            return tuple(out)

        zero = jnp.zeros((1, SW), F32)
        nbc, nb = ncc // 8, nc // 8
        carry = lax.fori_loop(0, nbc, lambda j, cr: step8(j, nbc - 1 - j, cr), (zero,) * (2 * GB))
        lax.fori_loop(nbc, nb, lambda j, cr: step8(j, nb + nbc - 1 - j, cr), carry)
        for g in range(GB):
            st = jnp.concatenate([sf_ref[:, _grp(g, SW)], sb_ref[:, _grp(g, SW)]], axis=1).astype(BF16)
            y_ref[:, _grp(g, CW)] += _dot(st, mo_ref[g], _NT)

    blk = lambda w: pl.BlockSpec((None, nc, GB * w), lambda b, gb: (b, 0, gb))
    return pl.pallas_call(
        body, name=name, grid=(bl, S5_GROUPS // GB),
        in_specs=[blk(CW), pl.BlockSpec((GB, CW, 2 * CW), lambda b, gb: (gb, 0, 0)),
                  pl.BlockSpec((GB, CW, CW), lambda b, gb: (gb, 0, 0)),
                  pl.BlockSpec((4, GB * SW), lambda b, gb: (0, gb))],
        out_specs=(blk(CW), blk(SW), blk(SW)),
        out_shape=(jax.ShapeDtypeStruct(u.shape, F32), jax.ShapeDtypeStruct((bl, nc, S5_GROUPS * SW), F32),
                   jax.ShapeDtypeStruct((bl, nc, S5_GROUPS * SW), F32)),
        scratch_shapes=[pltpu.VMEM((nc, GB * SW), F32)] * 2, compiler_params=_params(2))(u, wcat, mout, a2)


def _s5_bwd(dy, u, sf, sb, wcat, mout, a2, ncc, *, name):
    bl, nc, _ = u.shape

    def body(dy_ref, u_ref, sf_ref, sb_ref, w_ref, mo_ref, a_ref, du_ref, dw_ref, dmo_ref, dacc_ref, gsf, gsb, dvf, dvb):
        b = pl.program_id(1)

        @pl.when(b == 0)
        def _():
            dw_ref[...] = jnp.zeros_like(dw_ref)
            dmo_ref[...] = jnp.zeros_like(dmo_ref)
            dacc_ref[...] = jnp.zeros_like(dacc_ref)

        for g in range(GB):
            ds = _dot(dy_ref[:, _grp(g, CW)], mo_ref[g])
            gsf[:, _grp(g, SW)] = ds[:, :SW]
            gsb[:, _grp(g, SW)] = ds[:, SW:]
        co = [[a_ref[r:r + 1, _grp(g, SW)] for g in range(GB)] for r in range(4)]

        rid = lax.broadcasted_iota(jnp.int32, (8, SW), 0)

        def step8(kf, kb, carry):
            rf, rb = _rows8(kf), _rows8(kb)
            gfb, gbb = gsf[rf, :], gsb[rb, :]
            out = []
            for g in range(GB):
                gf, gb_ = carry[2 * g], carry[2 * g + 1]
                of = ob = jnp.zeros((8, SW), F32)
                for i in range(8):
                    k = 7 - i
                    of = jnp.where(rid == k, gf, of)
                    gf = gfb[k:k + 1, _grp(g, SW)] + co[0][g] * gf - co[1][g] * _lane_swap(gf)
                    ob = jnp.where(rid == i, gb_, ob)
                    gb_ = gbb[i:i + 1, _grp(g, SW)] + co[2][g] * gb_ - co[3][g] * _lane_swap(gb_)
                dvf[rf, _grp(g, SW)] = of
                dvb[rb, _grp(g, SW)] = ob
                out += [gf, gb_]
            return tuple(out)

        zero = jnp.zeros((1, SW), F32)
        nbc, nb = ncc // 8, nc // 8
        carry = lax.fori_loop(0, nb - nbc, lambda j, cr: step8(nb - 1 - j, nbc + j, cr), (zero,) * (2 * GB))
        lax.fori_loop(0, nbc, lambda j, cr: step8(nbc - 1 - j, j, cr), carry)
        for g in range(GB):
            dyg = dy_ref[:, _grp(g, CW)]
            dvf_g, dvb_g = dvf[:, _grp(g, SW)], dvb[:, _grp(g, SW)]
            sf_g, sb_g = sf_ref[:, _grp(g, SW)], sb_ref[:, _grp(g, SW)]
            dz = jnp.concatenate([dyg, dvf_g.astype(BF16), dvb_g.astype(BF16)], axis=1)
            du_ref[:, _grp(g, CW)] = _dot(dz, w_ref[g], _NT)
            dw_ref[g] += _dot(u_ref[:, _grp(g, CW)], dz, _TN)
            st = jnp.concatenate([sf_g, sb_g], axis=1).astype(BF16)
            dmo_ref[g] += _dot(dyg, st, _TN)
            dacc_ref[:, _grp(g, SW)] += jnp.concatenate(
                [jnp.sum(dvf_g * sf_g, axis=0, keepdims=True), jnp.sum(dvf_g * _lane_swap(sf_g), axis=0, keepdims=True),
                 jnp.sum(dvb_g * sb_g, axis=0, keepdims=True), jnp.sum(dvb_g * _lane_swap(sb_g), axis=0, keepdims=True)],
                axis=0)

    blk = lambda w: pl.BlockSpec((None, nc, GB * w), lambda gb, b: (b, 0, gb))
    wspec = pl.BlockSpec((GB, CW, 2 * CW), lambda gb, b: (gb, 0, 0))
    mspec = pl.BlockSpec((GB, CW, CW), lambda gb, b: (gb, 0, 0))
    aspec = pl.BlockSpec((4, GB * SW), lambda gb, b: (0, gb))
    return pl.pallas_call(
        body, name=name, grid=(S5_GROUPS // GB, bl),
        in_specs=[blk(CW), blk(CW), blk(SW), blk(SW), wspec, mspec, aspec],
        out_specs=(blk(CW), wspec, mspec, aspec),
        out_shape=(jax.ShapeDtypeStruct(u.shape, F32), jax.ShapeDtypeStruct(wcat.shape, F32),
                   jax.ShapeDtypeStruct(mout.shape, F32), jax.ShapeDtypeStruct(a2.shape, F32)),
        scratch_shapes=[pltpu.VMEM((nc, GB * SW), F32)] * 4,
        compiler_params=_params(2, 48))(dy, u, sf, sb, wcat, mout, a2)


def _to_chunks(v, bl, lc, ll):
    rows = ll // GRID_W
    v = v.reshape(bl, lc + ll, S5_GROUPS, S5_GROUP)
    ctx = v[:, :lc].reshape(bl, lc // T_CH, T_CH, S5_GROUPS, S5_GROUP).transpose(0, 1, 3, 2, 4)
    lat = v[:, lc:].reshape(bl, rows // T_CH, T_CH, GRID_W, S5_GROUPS, S5_GROUP).transpose(0, 3, 1, 4, 2, 5)
    return jnp.concatenate([ctx.reshape(bl, lc // T_CH, -1), lat.reshape(bl, ll // T_CH, -1)], axis=1)


def _from_chunks(v, bl, lc, ll):
    rows = ll // GRID_W
    ncc = lc // T_CH
    ctx = v[:, :ncc].reshape(bl, ncc, S5_GROUPS, T_CH, S5_GROUP).transpose(0, 1, 3, 2, 4).reshape(bl, lc, D_MODEL)
    lat = v[:, ncc:].reshape(bl, GRID_W, rows // T_CH, S5_GROUPS, T_CH, S5_GROUP).transpose(0, 2, 4, 1, 3, 5)
    return jnp.concatenate([ctx, lat.reshape(bl, ll, D_MODEL)], axis=1).reshape(bl * (lc + ll), D_MODEL)


_ANY = pl.BlockSpec(memory_space=pl.ANY)


def _xy_peers():
    x, y, c = lax.axis_index("x"), lax.axis_index("y"), lax.axis_index("c")
    return x, y, c, [(1 - x, y), (x, 1 - y), (1 - x, 1 - y)]


def _all_gather_xy(shard, *, name):
    def body(x_ref, out_ref, send_sems, recv_sems, local_sem):
        x, y, c, peers = _xy_peers()
        me = 2 * x + y
        mine = pltpu.make_async_copy(x_ref, out_ref.at[me], local_sem)
        mine.start()

        def copy(k, px, py, slot):
            return pltpu.make_async_remote_copy(src_ref=x_ref, dst_ref=out_ref.at[slot], send_sem=send_sems.at[k],
                                                recv_sem=recv_sems.at[k], device_id=(px, py, c), device_id_type=MESH)

        sends = [copy(k, px, py, me) for k, (px, py) in enumerate(peers)]
        for cp in sends:
            cp.start()
        for k, (px, py) in enumerate(peers):
            copy(k, px, py, 2 * px + py).wait_recv()
        for cp in sends:
            cp.wait_send()
        mine.wait()

    return pl.pallas_call(body, name=name, in_specs=[_ANY], out_specs=_ANY,
                          out_shape=jax.ShapeDtypeStruct((4,) + shard.shape, shard.dtype),
                          scratch_shapes=[pltpu.SemaphoreType.DMA((3,)), pltpu.SemaphoreType.DMA((3,)),
                                          pltpu.SemaphoreType.DMA])(shard)


def _scatter_xy(parts, *, name):
    def body(p_ref, out_ref, send_sems, recv_sems, local_sem):
        x, y, c, peers = _xy_peers()
        mine = pltpu.make_async_copy(p_ref.at[2 * x + y], out_ref.at[0], local_sem)
        mine.start()

        def copy(k, px, py):
            return pltpu.make_async_remote_copy(src_ref=p_ref.at[2 * px + py], dst_ref=out_ref.at[1 + k],
                                                send_sem=send_sems.at[k], recv_sem=recv_sems.at[k],
                                                device_id=(px, py, c), device_id_type=MESH)

        sends = [copy(k, px, py) for k, (px, py) in enumerate(peers)]
        for cp in sends:
            cp.start()
        for cp in sends:
            cp.wait_recv()
        for cp in sends:
            cp.wait_send()
        mine.wait()

    return pl.pallas_call(body, name=name, in_specs=[_ANY], out_specs=_ANY,
                          out_shape=jax.ShapeDtypeStruct(parts.shape, parts.dtype),
                          scratch_shapes=[pltpu.SemaphoreType.DMA((3,)), pltpu.SemaphoreType.DMA((3,)),
                                          pltpu.SemaphoreType.DMA])(parts)


def _swap_sibling(v, *, name):
    def body(v_ref, out_ref, send_sem, recv_sem):
        x, y, c = lax.axis_index("x"), lax.axis_index("y"), lax.axis_index("c")
        cp = pltpu.make_async_remote_copy(src_ref=v_ref, dst_ref=out_ref, send_sem=send_sem, recv_sem=recv_sem,
                                          device_id=(x, y, 1 - c), device_id_type=MESH)
        cp.start()
        cp.wait()

    return pl.pallas_call(body, name=name, in_specs=[_ANY], out_specs=_ANY,
                          out_shape=jax.ShapeDtypeStruct(v.shape, v.dtype),
                          scratch_shapes=[pltpu.SemaphoreType.DMA, pltpu.SemaphoreType.DMA])(v)


BIG_COLS = {'ada_w': True, 'w_in': True, 'mlp_w1': True, 's5_glu_w': False, 'w_out': False, 'mlp_w2': False}
BIG = list(BIG_COLS)


def _block(ref2d, j, cols, size):
    if cols:
        return ref2d.at[:, pl.ds(pl.multiple_of(j * size, 128), size)]
    return ref2d.at[pl.ds(pl.multiple_of(j * size, 8), size), :]


def _shard_size(shape, cols):
    return shape[-1] if cols else shape[-2]


def _gather_big(shards, cols, *, name):
    n = len(shards)

    def body(*refs):
        ins, outs = refs[:n], refs[n:2 * n]
        ici_send, ici_recv, d2d_send, d2d_recv, loc = refs[2 * n:]
        x, y, c, peers = _xy_peers()
        me = 2 * x + y

        def blk(w, layer, j):
            return _block(outs[w].at[layer], j, cols[w], _shard_size(ins[w].shape, cols[w]))

        def ici(w, k, px, py, j):
            return pltpu.make_async_remote_copy(src_ref=ins[w].at[c], dst_ref=blk(w, c, j),
                                                send_sem=ici_send.at[3 * w + k], recv_sem=ici_recv.at[3 * w + k],
                                                device_id=(px, py, c), device_id_type=MESH)

        def d2d(w, k, j, layer):
            return pltpu.make_async_remote_copy(src_ref=blk(w, layer, j), dst_ref=blk(w, layer, j),
                                                send_sem=d2d_send.at[3 * w + k], recv_sem=d2d_recv.at[3 * w + k],
                                                device_id=(x, y, 1 - c), device_id_type=MESH)

        started = []
        for w in range(n):
            for layer in range(DEPTH):
                started.append(pltpu.make_async_copy(ins[w].at[layer], blk(w, layer, me), loc.at[DEPTH * w + layer]))
            started += [ici(w, k, px, py, me) for k, (px, py) in enumerate(peers)]
        for cp in started:
            cp.start()
        passed = []
        for w in range(n):
            for k, (px, py) in enumerate(peers):
                ici(w, k, px, py, 2 * px + py).wait_recv()
                passed.append(d2d(w, k, 2 * px + py, c))
                passed[-1].start()
        for w in range(n):
            for k, (px, py) in enumerate(peers):
                d2d(w, k, 2 * px + py, 1 - c).wait_recv()
        for i, cp in enumerate(started):
            if i % (DEPTH + 3) < DEPTH:
                cp.wait()
            else:
                cp.wait_send()
        for cp in passed:
            cp.wait_send()

    def full(s, cf):
        return (DEPTH, s.shape[1], 4 * s.shape[2]) if cf else (DEPTH, 4 * s.shape[1], s.shape[2])

    return pl.pallas_call(
        body, name=name, in_specs=[_ANY] * n, out_specs=[_ANY] * n,
        out_shape=[jax.ShapeDtypeStruct(full(s, cf), s.dtype) for s, cf in zip(shards, cols)],
        scratch_shapes=[pltpu.SemaphoreType.DMA((3 * n,))] * 4 + [pltpu.SemaphoreType.DMA((DEPTH * n,))])(*shards)


def _sibling_partials(gbufs, *, name):
    n = len(gbufs)

    def body(*refs):
        ins, outs, send, recv = refs[:n], refs[n:2 * n], refs[2 * n], refs[2 * n + 1]
        x, y, c = lax.axis_index("x"), lax.axis_index("y"), lax.axis_index("c")
        cps = [pltpu.make_async_remote_copy(src_ref=ins[w].at[1 - c], dst_ref=outs[w], send_sem=send.at[w],
                                            recv_sem=recv.at[w], device_id=(x, y, 1 - c), device_id_type=MESH)
               for w in range(n)]
        for cp in cps:
            cp.start()
        for cp in cps:
            cp.wait()

    return pl.pallas_call(body, name=name, in_specs=[_ANY] * n, out_specs=[_ANY] * n,
                          out_shape=[jax.ShapeDtypeStruct(g.shape[1:], g.dtype) for g in gbufs],
                          scratch_shapes=[pltpu.SemaphoreType.DMA((n,))] * 2)(*gbufs)


def _chip_sum(gbuf, other, my_c, *, name):
    _, k, n = gbuf.shape
    tr, tc = _tile(k, (512,)), _tile(n, (1024,))

    def body(c_ref, a_ref, b_ref, o_ref):
        o_ref[...] = (a_ref[...] + b_ref[...]).astype(BF16)

    spec = pl.BlockSpec((tr, tc), lambda i, j, c_ref: (i, j))
    return pl.pallas_call(
        body, name=name,
        grid_spec=pltpu.PrefetchScalarGridSpec(
            num_scalar_prefetch=1, grid=(k // tr, n // tc),
            in_specs=[pl.BlockSpec((None, tr, tc), lambda i, j, c_ref: (c_ref[0], i, j)), spec], out_specs=spec),
        out_shape=jax.ShapeDtypeStruct((k, n), BF16), compiler_params=_params(2))(my_c, gbuf, other)


def _scatter_big(sums, cols, *, name):
    n = len(sums)

    def shard(s, cf):
        return (s.shape[0], s.shape[1] // 4) if cf else (s.shape[0] // 4, s.shape[1])

    def body(*refs):
        ins, outs, send, recv = refs[:n], refs[n:2 * n], refs[2 * n], refs[2 * n + 1]
        x, y, c, peers = _xy_peers()
        cps = []
        for w in range(n):
            size = _shard_size(shard(ins[w], cols[w]), cols[w])
            for k, (px, py) in enumerate(peers):
                cps.append(pltpu.make_async_remote_copy(
                    src_ref=_block(ins[w], 2 * px + py, cols[w], size), dst_ref=outs[w].at[k],
                    send_sem=send.at[3 * w + k], recv_sem=recv.at[3 * w + k], device_id=(px, py, c),
                    device_id_type=MESH))
        for cp in cps:
            cp.start()
        for cp in cps:
            cp.wait()

    return pl.pallas_call(body, name=name, in_specs=[_ANY] * n, out_specs=[_ANY] * n,
                          out_shape=[jax.ShapeDtypeStruct((3,) + shard(s, cf), s.dtype) for s, cf in zip(sums, cols)],
                          scratch_shapes=[pltpu.SemaphoreType.DMA((3 * n,))] * 2)(*sums)


def _block_sum(own, got, cols, my_j, *, name):
    _, r, c = got.shape
    tr, tc = _tile(r, (256,)), _tile(c, (1024, 768, 512))

    def body(j_ref, a_ref, g_ref, o_ref):
        o_ref[...] = ((a_ref[...].astype(F32) + g_ref[0].astype(F32)) + g_ref[1].astype(F32)) + g_ref[2].astype(F32)

    if cols:
        own_spec = pl.BlockSpec((tr, tc), lambda i, j, j_ref: (i, j_ref[0] * (c // tc) + j))
    else:
        own_spec = pl.BlockSpec((tr, tc), lambda i, j, j_ref: (j_ref[0] * (r // tr) + i, j))
    return pl.pallas_call(
        body, name=name,
        grid_spec=pltpu.PrefetchScalarGridSpec(
            num_scalar_prefetch=1, grid=(r // tr, c // tc),
            in_specs=[own_spec, pl.BlockSpec((3, tr, tc), lambda i, j, j_ref: (0, i, j))],
            out_specs=pl.BlockSpec((tr, tc), lambda i, j, j_ref: (i, j))),
        out_shape=jax.ShapeDtypeStruct((r, c), F32), compiler_params=_params(2))(my_j, own, got)


def _share_final(finals, *, name):
    n = len(finals)

    def body(*refs):
        ins, outs, send, recv, loc = refs[:n], refs[n:2 * n], refs[2 * n], refs[2 * n + 1], refs[2 * n + 2]
        x, y, c = lax.axis_index("x"), lax.axis_index("y"), lax.axis_index("c")
        mine = [pltpu.make_async_copy(ins[w], outs[w].at[c], loc.at[w]) for w in range(n)]
        away = [pltpu.make_async_remote_copy(src_ref=ins[w], dst_ref=outs[w].at[c], send_sem=send.at[w],
                                             recv_sem=recv.at[w], device_id=(x, y, 1 - c), device_id_type=MESH)
                for w in range(n)]
        for cp in mine + away:
            cp.start()
        for w in range(n):
            pltpu.make_async_remote_copy(src_ref=ins[w], dst_ref=outs[w].at[1 - c], send_sem=send.at[w],
                                         recv_sem=recv.at[w], device_id=(x, y, 1 - c),
                                         device_id_type=MESH).wait_recv()
        for cp in away:
            cp.wait_send()
        for cp in mine:
            cp.wait()

    return pl.pallas_call(body, name=name, in_specs=[_ANY] * n, out_specs=[_ANY] * n,
                          out_shape=[jax.ShapeDtypeStruct((DEPTH,) + f.shape, f.dtype) for f in finals],
                          scratch_shapes=[pltpu.SemaphoreType.DMA((n,))] * 3)(*finals)


def _adamw_native(w, g, m, v, *, name):
    r, c = w.shape
    tr = _tile(r, (256, 128, 64, 32, 16, 8))
    spec = pl.BlockSpec((tr, c), lambda i: (i, 0))
    c1 = 1.0 / (1.0 - ADAM_B1 ** ADAM_STEP)
    c2 = 1.0 / (1.0 - ADAM_B2 ** ADAM_STEP)

    def body(w_ref, g_ref, m_ref, v_ref, d_ref, nm_ref, nv_ref):
        g_t = g_ref[...]
        nm = ADAM_B1 * m_ref[...] + (1.0 - ADAM_B1) * g_t
        nv = ADAM_B2 * v_ref[...] + (1.0 - ADAM_B2) * (g_t * g_t)
        nm_ref[...] = nm
        nv_ref[...] = nv
        d_ref[...] = -ADAM_LR * ((nm * c1) / (jnp.sqrt(nv * c2) + ADAM_EPS) + ADAM_WD * w_ref[...])

    shp = jax.ShapeDtypeStruct((r, c), F32)
    return pl.pallas_call(body, name=name, grid=(r // tr,), in_specs=[spec] * 4, out_specs=(spec,) * 3,
                          out_shape=(shp,) * 3, compiler_params=_params(1))(w, g, m, v)


def _flat_tile(r):
    return _tile(r, (512, 256, 128, 64, 32, 16, 8))


def _sum4(parts, *, name):
    r = parts.shape[1]
    tr = _flat_tile(r)

    def body(p_ref, o_ref):
        o_ref[...] = ((p_ref[0] + p_ref[1]) + p_ref[2]) + p_ref[3]

    return pl.pallas_call(body, name=name, grid=(r // tr,),
                          in_specs=[pl.BlockSpec((4, tr, LANES), lambda i: (0, i, 0))],
                          out_specs=pl.BlockSpec((tr, LANES), lambda i: (i, 0)),
                          out_shape=jax.ShapeDtypeStruct((r, LANES), F32), compiler_params=_params(1))(parts)


def _add2(a, b, *, name):
    r = a.shape[0]
    tr = _flat_tile(r)
    spec = pl.BlockSpec((tr, LANES), lambda i: (i, 0))

    def body(a_ref, b_ref, o_ref):
        o_ref[...] = a_ref[...] + b_ref[...]

    return pl.pallas_call(body, name=name, grid=(r // tr,), in_specs=[spec, spec], out_specs=spec,
                          out_shape=jax.ShapeDtypeStruct((r, LANES), F32), compiler_params=_params(1))(a, b)


def _adamw(w, ga, gb, m, v, *, name):
    r = w.shape[0]
    tr = _flat_tile(r)
    spec = pl.BlockSpec((tr, LANES), lambda i: (i, 0))
    two = gb is not None
    c1 = 1.0 / (1.0 - ADAM_B1 ** ADAM_STEP)
    c2 = 1.0 / (1.0 - ADAM_B2 ** ADAM_STEP)

    def body(*refs):
        w_ref, ga_ref = refs[0], refs[1]
        m_ref, v_ref, g_ref, d_ref, nm_ref, nv_ref = refs[2 + two:]
        g = ga_ref[...] + refs[2][...] if two else ga_ref[...]
        nm = ADAM_B1 * m_ref[...] + (1.0 - ADAM_B1) * g
        nv = ADAM_B2 * v_ref[...] + (1.0 - ADAM_B2) * (g * g)
        g_ref[...] = g
        nm_ref[...] = nm
        nv_ref[...] = nv
        d_ref[...] = -ADAM_LR * ((nm * c1) / (jnp.sqrt(nv * c2) + ADAM_EPS) + ADAM_WD * w_ref[...])

    args = [w, ga] + ([gb] if two else []) + [m, v]
    shp = jax.ShapeDtypeStruct((r, LANES), F32)
    return pl.pallas_call(body, name=name, grid=(r // tr,), in_specs=[spec] * len(args), out_specs=(spec,) * 4,
                          out_shape=(shp,) * 4, compiler_params=_params(1))(*args)


def _pack(arrs, dtype=F32):
    flat = jnp.concatenate([a.astype(dtype).reshape(-1) for a in arrs])
    pad = (-flat.shape[0]) % (32 * LANES)
    return jnp.pad(flat, (0, pad)).reshape(-1, LANES)


def _unpack(buf, shapes):
    flat = buf.reshape(-1)
    out, off = [], 0
    for shp in shapes:
        sz = math.prod(shp)
        out.append(flat[off:off + sz].reshape(shp))
        off += sz
    return out


def _stack_shards(full, axis):
    shp = full.shape
    return jnp.moveaxis(full.reshape(shp[:axis] + (4, shp[axis] // 4) + shp[axis + 1:]), axis, 0)


def _unstack_shards(st, axis):
    v = jnp.moveaxis(st, 0, axis)
    shp = v.shape
    return v.reshape(shp[:axis] + (shp[axis] * shp[axis + 1],) + shp[axis + 2:])


def _layer_weights(w, l):
    lw = {n: w[n][l] for n in w}
    lw['wcat'] = jnp.concatenate([lw['rg_wa'][0], lw['rg_wi'][0], lw['rg_wa'][1], lw['rg_wi'][1]],
                                 axis=-1).astype(BF16)
    ba, bi = lw['rg_ba'].reshape(2, RG_HEADS, HEAD), lw['rg_bi'].reshape(2, RG_HEADS, HEAD)
    lw['bcat'] = jnp.concatenate([ba[0], bi[0], ba[1], bi[1]], axis=-1)[:, None, :]
    s5_names = ['s5_a_re', 's5_a_im', 's5_log_dt', 's5_b_re', 's5_b_im', 's5_c_re', 's5_c_im']
    (wcat, mout, a2), lw['s5_vjp'] = jax.vjp(_s5_mats, *[lw[n] for n in s5_names])
    lw['s5_wcat'], lw['s5_mout'], lw['s5_a2'] = wcat.astype(BF16), mout.astype(BF16), a2
    for n in ('conv_b', 's5_d', 's5_glu_b', 'b_out', 'mlp_b1', 'mlp_b2', 'ln1_g', 'ln1_b', 'ln2_g', 'ln2_b'):
        lw[n] = lw[n][None, :]
    return lw


def _layer_fwd(l, x0, modall, lw, dims):
    bl, s, lc, tps = dims
    ll = s - lc
    tag = f"l{l}_"
    sv = {'x0': x0}
    sv['u1'] = _modulate(x0, modall, 0, 1, tps, name=tag + "mod1")
    sv['p'] = p = _mm_nn(sv['u1'], lw['w_in'], name=tag + "w_in")
    rg, sv['hf'], sv['hb'] = _rg_fwd(p, lw['conv_w'], lw['conv_b'], lw['rg_lambda'], lw['wcat'], lw['bcat'],
                                     bl, s, lc, name=tag + "rg_fwd")
    sv['u_ch'] = _to_chunks(p[:, 2 * D_MODEL:].astype(BF16), bl, lc, ll)
    y_ch, sv['sf'], sv['sb'] = _s5_fwd(sv['u_ch'], lw['s5_wcat'], lw['s5_mout'], lw['s5_a2'], lc // T_CH,
                                       name=tag + "s5_fwd")
    sv['y'] = _from_chunks(y_ch, bl, lc, ll)
    s5 = _glu_fwd(sv['y'], p, lw['s5_d'], lw['s5_glu_w'], lw['s5_glu_b'], name=tag + "glu_fwd")
    sv['cat'] = jnp.concatenate([rg, s5], axis=1)
    sv['m'] = _mm_nn(sv['cat'], lw['w_out'], lw['b_out'], name=tag + "w_out")
    sv['x1'] = _resid_ln(x0, sv['m'], modall, 2, lw['ln1_g'], lw['ln1_b'], tps, name=tag + "ln1")
    sv['u2'] = _modulate(sv['x1'], modall, 3, 4, tps, name=tag + "mod2")
    sv['a'], sv['h'] = _mm_nn(sv['u2'], lw['mlp_w1'], lw['mlp_b1'], relu2=True, name=tag + "mlp1")
    sv['f'] = _mm_nn(sv['a'], lw['mlp_w2'], lw['mlp_b2'], name=tag + "mlp2")
    x2 = _resid_ln(sv['x1'], sv['f'], modall, 5, lw['ln2_g'], lw['ln2_b'], tps, name=tag + "ln2")
    return x2, sv


def _layer_bwd(l, dx2, modall, lw, sv, dims, gbufs):
    bl, s, lc, tps = dims
    ll = s - lc
    tag = f"l{l}_"
    g = {}

    def big_grad(n, a_mat, b_mat, label):
        gbufs[n] = _mm_tn(a_mat, b_mat, name=tag + label, layer=l, into=gbufs.get(n))
    dx1a, df, db2, g['ln2_g'], g['ln2_b'], dg2 = _resid_ln_bwd(sv['x1'], sv['f'], modall, 5, lw['ln2_g'], dx2, tps,
                                                              name=tag + "ln2_bwd")
    g['mlp_b2'] = db2
    big_grad('mlp_w2', sv['a'], df, "mlp2_dw")
    dh = _mm_nt(df, lw['mlp_w2'], sv['h'], name=tag + "mlp2_dx")
    g['mlp_b1'] = _colsum(dh, name=tag + "mlp1_db")
    big_grad('mlp_w1', sv['u2'], dh, "mlp1_dw")
    du2 = _mm_nt(dh, lw['mlp_w1'], name=tag + "mlp1_dx")
    dx1, dsc2, dsh2 = _modulate_bwd(du2, sv['x1'], modall, 4, dx1a, tps, name=tag + "mod2_bwd")
    dx0a, dm, g['b_out'], g['ln1_g'], g['ln1_b'], dg1 = _resid_ln_bwd(sv['x0'], sv['m'], modall, 2, lw['ln1_g'], dx1,
                                                                     tps, name=tag + "ln1_bwd")
    big_grad('w_out', sv['cat'], dm, "w_out_dw")
    dcat = _mm_nt(dm, lw['w_out'], name=tag + "w_out_dx")
    dy, dskip, g_bf, dz_bf, g['s5_d'], g['s5_glu_b'] = _glu_bwd(dcat, sv['y'], sv['p'], lw['s5_d'], lw['s5_glu_w'],
                                                                lw['s5_glu_b'], name=tag + "glu_bwd")
    big_grad('s5_glu_w', g_bf, dz_bf, "glu_dw")
    du_ch, dwcat, dmout, dacc = _s5_bwd(_to_chunks(dy, bl, lc, ll), sv['u_ch'], sv['sf'], sv['sb'], lw['s5_wcat'],
                                        lw['s5_mout'], lw['s5_a2'], lc // T_CH, name=tag + "s5_bwd")
    s5g = lw['s5_vjp']((dwcat, dmout, dacc))
    for n, v in zip(['s5_a_re', 's5_a_im', 's5_log_dt', 's5_b_re', 's5_b_im', 's5_c_re', 's5_c_im'], s5g):
        g[n] = v
    ds5u = _add_cast(_from_chunks(du_ch, bl, lc, ll), dskip, name=tag + "ds5u")
    drgx, dgate, dcw, dcb, dlam, dwc, dbc = _rg_bwd(sv['p'], dcat, sv['hf'], sv['hb'], lw['conv_w'], lw['conv_b'],
                                                    lw['rg_lambda'], lw['wcat'], lw['bcat'], bl, s, lc,
                                                    name=tag + "rg_bwd")
    g['conv_w'] = dcw.transpose(1, 0, 2).reshape(4, D_MODEL)
    g['conv_b'] = dcb.reshape(D_MODEL)
    g['rg_lambda'] = dlam.transpose(1, 0, 2).reshape(2, D_MODEL)
    g['rg_wa'] = jnp.stack([dwc[:, :, 0:HEAD], dwc[:, :, 2 * HEAD:3 * HEAD]])
    g['rg_wi'] = jnp.stack([dwc[:, :, HEAD:2 * HEAD], dwc[:, :, 3 * HEAD:]])
    dbc = dbc.reshape(RG_HEADS, 4, HEAD)
    g['rg_ba'] = jnp.stack([dbc[:, 0], dbc[:, 2]]).reshape(2, D_MODEL)
    g['rg_bi'] = jnp.stack([dbc[:, 1], dbc[:, 3]]).reshape(2, D_MODEL)
    dp = jnp.concatenate([drgx, dgate, ds5u], axis=1)
    big_grad('w_in', sv['u1'], dp, "w_in_dw")
    du1 = _mm_nt(dp, lw['w_in'], name=tag + "w_in_dx")
    dx0, dsc1, dsh1 = _modulate_bwd(du1, sv['x0'], modall, 1, dx0a, tps, name=tag + "mod1_bwd")
    dmod = jnp.concatenate([dsh1, dsc1, dg1, dsh2, dsc2, dg2], axis=1)
    return dx0, g, dmod


def _kernel_impl(*args):
    nin = len(IN_NAMES)
    a = dict(zip(IN_NAMES, args[:nin]))
    target = args[nin]
    nw = len(WEIGHTS)
    mom = dict(zip(WEIGHTS, args[nin + 1:nin + 1 + nw]))
    var = dict(zip(WEIGHTS, args[nin + 1 + nw:nin + 1 + 2 * nw]))
    bl, ll, d = a['x'].shape
    lc = a['ctx'].shape[1]
    assert d == D_MODEL and lc == TM and bl == 2 and ll % (GRID_W * T_CH) == 0
    s = lc + ll
    tps = s // TM
    dims = (bl, s, lc, tps)

    def gather(names, dtype, tag):
        shards = [a[n] for n in names]
        got = _all_gather_xy(_pack(shards, dtype), name="gather_" + tag)
        per = [_unpack(got[j], [w.shape for w in shards]) for j in range(4)]
        return {n: _unstack_shards(jnp.stack([per[j][i] for j in range(4)]), SHARD_AXIS[n])
                for i, n in enumerate(names)}

    big_cols = [BIG_COLS[n] for n in BIG]
    w = dict(zip(BIG, _gather_big([a[n].astype(BF16) for n in BIG], big_cols, name="gather_big")))
    w.update(gather(GATHER_F32, F32, "f32"))
    for n in REPLICATED:
        w[n] = a[n]
    my_c = lax.axis_index("c").astype(jnp.int32).reshape(1)
    my_j = (2 * lax.axis_index("x") + lax.axis_index("y")).astype(jnp.int32).reshape(1)

    xs = jnp.concatenate([a['ctx'], a['x']], axis=1).reshape(bl * s, D_MODEL)
    c16 = jnp.zeros((16, D_MODEL), F32).at[0:2].set(a['c']).at[2].set(a['c_ctx'])
    s16, ds16 = _silu_rows(c16, name="silu")
    s16b = s16.astype(BF16)
    layers, saved, mods = [], [], []
    for l in range(DEPTH):
        lw = _layer_weights({n: w[n] for n in WEIGHTS if n not in ('c_ctx',)}, l)
        mod16 = _mm_nn(s16b, lw['ada_w'], lw['ada_b'][None, :], name=f"l{l}_ada").reshape(16, N_MOD, D_MODEL)
        modall = jnp.stack([mod16[2], mod16[0], mod16[2], mod16[1]])
        xs, sv = _layer_fwd(l, xs, modall, lw, dims)
        layers.append(lw)
        saved.append(sv)
        mods.append(modall)
    lossrow, dx = _loss_head(xs, target.reshape(bl * ll, D_MODEL), tps, name="loss_head")
    loss = lax.psum(0.5 / D_MODEL * jnp.sum(lossrow), ("x", "y", "c"))

    small = [n for n in WEIGHTS if n != 'c_ctx' and n not in BIG_COLS]
    grads = {n: [None] * DEPTH for n in small}
    gbufs = {}
    ds_rows = jnp.zeros((16, D_MODEL), F32)
    for l in reversed(range(DEPTH)):
        dx, g, dmod = _layer_bwd(l, dx, mods[l], layers[l], saved[l], dims, gbufs)
        dmod16 = jnp.zeros((16, N_MOD * D_MODEL), F32).at[0].set(dmod[1].reshape(-1)).at[1].set(
            dmod[3].reshape(-1)).at[2].set((dmod[0] + dmod[2]).reshape(-1))
        dmod16b = dmod16.astype(BF16)
        gbufs['ada_w'] = _mm_tn(s16b, dmod16b, name=f"l{l}_ada_dw", layer=l, into=gbufs.get('ada_w'))
        g['ada_b'] = _colsum(dmod16, name=f"l{l}_ada_db")
        ds_rows = ds_rows + _mm_nt(dmod16b, layers[l]['ada_w'], name=f"l{l}_ada_dx")
        for n, v in g.items():
            grads[n][l] = v.reshape(a[n].shape[1:] if n in REPLICATED else w[n].shape[1:])
    full = {n: jnp.stack(v) for n, v in grads.items()}
    full['c_ctx'] = _mul_rows(ds_rows, ds16, name="silu_bwd")[2]
    grad_x = dx.reshape(bl, s, D_MODEL)[:, lc:]

    from_sib = _sibling_partials([gbufs[n] for n in BIG], name="grad_big_sibling")
    sums = [_chip_sum(gbufs[n], o, my_c, name=f"grad_chip_sum_{n}") for n, o in zip(BIG, from_sib)]
    got = _scatter_big(sums, big_cols, name="grad_big_scatter")
    finals = [_block_sum(sm, gt, cf, my_j, name=f"grad_block_sum_{n}")
              for n, sm, gt, cf in zip(BIG, sums, got, big_cols)]
    res_big = {}
    for n, gfull in zip(BIG, _share_final(finals, name="grad_big_share")):
        flat = lambda t: t.reshape(-1, t.shape[-1])
        d_w, n_m, n_v = _adamw_native(flat(a[n]), flat(gfull), flat(mom[n]), flat(var[n]), name=f"adamw_{n}")
        res_big[n] = [gfull] + [t.reshape(a[n].shape) for t in (d_w, n_m, n_v)]

    rep_flat = _pack([full[n] for n in REPLICATED])
    rr = rep_flat.shape[0]
    sh_stacked = [_stack_shards(full[n], SHARD_AXIS[n] + 0).reshape(4, -1) for n in SHARDED]
    parts = jnp.concatenate(sh_stacked + [rep_flat.reshape(4, -1)], axis=1)
    pad = (-parts.shape[1]) % (32 * LANES)
    parts = jnp.pad(parts, ((0, 0), (0, pad))).reshape(4, -1, LANES)
    mine = _sum4(_scatter_xy(parts, name="grad_scatter"), name="grad_sum4")
    other = _swap_sibling(mine, name="grad_swap")
    n_sh = sum(math.prod(a[n].shape) for n in SHARDED)
    r_sh = n_sh // LANES
    assert n_sh % LANES == 0
    rq = rr // 4

    sh_shapes = [a[n].shape for n in SHARDED]
    pk = lambda dct: _pack([dct[n] for n in SHARDED])
    r_pk = pk(a).shape[0]
    take = lambda buf: jnp.pad(buf[:r_sh], ((0, r_pk - r_sh), (0, 0)))
    outs_sh = _adamw(pk(a), take(mine), take(other), pk(mom), pk(var), name="adamw_sharded")
    res_sh = [dict(zip(SHARDED, _unpack(o, sh_shapes))) for o in outs_sh]

    quarter = _add2(mine[r_sh:r_sh + rq], other[r_sh:r_sh + rq], name="grad_rep_sum")
    rep_g = _all_gather_xy(quarter, name="grad_rep_gather").reshape(rr, LANES)
    rep_shapes = [a[n].shape for n in REPLICATED]
    pr = lambda dct: _pack([dct[n] for n in REPLICATED])
    outs_rep = _adamw(pr(a), rep_g, None, pr(mom), pr(var), name="adamw_replicated")
    res_rep = [dict(zip(REPLICATED, _unpack(o, rep_shapes))) for o in outs_rep]

    out = [loss, grad_x]
    for k in range(4):
        out += [res_big[n][k] if n in BIG_COLS else res_sh[k][n] if n in SHARDED else res_rep[k][n] for n in WEIGHTS]
    return tuple(out)


def kernel(x, c, ctx, c_ctx, ada_w, ada_b, ln1_g, ln1_b, w_in, conv_w, conv_b, rg_lambda, rg_wa, rg_ba, rg_wi, rg_bi, s5_a_re, s5_a_im, s5_log_dt, s5_b_re, s5_b_im, s5_c_re, s5_c_im, s5_d, s5_glu_w, s5_glu_b, w_out, b_out, ln2_g, ln2_b, mlp_w1, mlp_b1, mlp_w2, mlp_b2, loss_target, m_c_ctx, m_ada_w, m_ada_b, m_ln1_g, m_ln1_b, m_w_in, m_conv_w, m_conv_b, m_rg_lambda, m_rg_wa, m_rg_ba, m_rg_wi, m_rg_bi, m_s5_a_re, m_s5_a_im, m_s5_log_dt, m_s5_b_re, m_s5_b_im, m_s5_c_re, m_s5_c_im, m_s5_d, m_s5_glu_w, m_s5_glu_b, m_w_out, m_b_out, m_ln2_g, m_ln2_b, m_mlp_w1, m_mlp_b1, m_mlp_w2, m_mlp_b2, v_c_ctx, v_ada_w, v_ada_b, v_ln1_g, v_ln1_b, v_w_in, v_conv_w, v_conv_b, v_rg_lambda, v_rg_wa, v_rg_ba, v_rg_wi, v_rg_bi, v_s5_a_re, v_s5_a_im, v_s5_log_dt, v_s5_b_re, v_s5_b_im, v_s5_c_re, v_s5_c_im, v_s5_d, v_s5_glu_w, v_s5_glu_b, v_w_out, v_b_out, v_ln2_g, v_ln2_b, v_mlp_w1, v_mlp_b1, v_mlp_w2, v_mlp_b2):
    return _kernel_impl(x, c, ctx, c_ctx, ada_w, ada_b, ln1_g, ln1_b, w_in, conv_w, conv_b, rg_lambda, rg_wa, rg_ba, rg_wi, rg_bi, s5_a_re, s5_a_im, s5_log_dt, s5_b_re, s5_b_im, s5_c_re, s5_c_im, s5_d, s5_glu_w, s5_glu_b, w_out, b_out, ln2_g, ln2_b, mlp_w1, mlp_b1, mlp_w2, mlp_b2, loss_target, m_c_ctx, m_ada_w, m_ada_b, m_ln1_g, m_ln1_b, m_w_in, m_conv_w, m_conv_b, m_rg_lambda, m_rg_wa, m_rg_ba, m_rg_wi, m_rg_bi, m_s5_a_re, m_s5_a_im, m_s5_log_dt, m_s5_b_re, m_s5_b_im, m_s5_c_re, m_s5_c_im, m_s5_d, m_s5_glu_w, m_s5_glu_b, m_w_out, m_b_out, m_ln2_g, m_ln2_b, m_mlp_w1, m_mlp_b1, m_mlp_w2, m_mlp_b2, v_c_ctx, v_ada_w, v_ada_b, v_ln1_g, v_ln1_b, v_w_in, v_conv_w, v_conv_b, v_rg_lambda, v_rg_wa, v_rg_ba, v_rg_wi, v_rg_bi, v_s5_a_re, v_s5_a_im, v_s5_log_dt, v_s5_b_re, v_s5_b_im, v_s5_c_re, v_s5_c_im, v_s5_d, v_s5_glu_w, v_s5_glu_b, v_w_out, v_b_out, v_ln2_g, v_ln2_b, v_mlp_w1, v_mlp_b1, v_mlp_w2, v_mlp_b2)
```

```python
import functools
import math

import jax
import jax.numpy as jnp
from jax import lax
from jax.experimental import pallas as pl
from jax.experimental.pallas import tpu as pltpu

F32 = jnp.float32
BF16 = jnp.bfloat16
MESH = pl.DeviceIdType.MESH

D_MODEL = 1024
N_MOD = 6
GRID_W = 64
RG_HEADS = 8
HEAD = 128
RG_C = 8.0
S5_GROUPS = 64
S5_GROUP = 16
S5_STATE = 64
T_CH = 16
GB = 8
CW = T_CH * S5_GROUP
SW = 2 * S5_STATE
DEPTH = 2
ALPHA = (2.0 * DEPTH) ** 0.25
LN_EPS = 1e-5
TM = 256
LANES = 1024
ADAM_LR, ADAM_B1, ADAM_B2, ADAM_EPS, ADAM_WD, ADAM_STEP = 0.001, 0.9, 0.999, 1e-08, 0.01, 10
MIB = 2 ** 20

IN_NAMES = ['x', 'c', 'ctx', 'c_ctx', 'ada_w', 'ada_b', 'ln1_g', 'ln1_b', 'w_in', 'conv_w', 'conv_b', 'rg_lambda',
            'rg_wa', 'rg_ba', 'rg_wi', 'rg_bi', 's5_a_re', 's5_a_im', 's5_log_dt', 's5_b_re', 's5_b_im', 's5_c_re',
            's5_c_im', 's5_d', 's5_glu_w', 's5_glu_b', 'w_out', 'b_out', 'ln2_g', 'ln2_b', 'mlp_w1', 'mlp_b1',
            'mlp_w2', 'mlp_b2']
WEIGHTS = IN_NAMES[3:]
SHARD_AXIS = {'ada_w': 2, 'w_in': 2, 'conv_w': 2, 'rg_lambda': 2, 'rg_ba': 2, 'rg_bi': 2, 's5_glu_w': 1, 'w_out': 1,
              'mlp_w1': 2, 'mlp_w2': 1}
SHARDED = ['conv_w', 'rg_lambda', 'rg_ba', 'rg_bi']
REPLICATED = [n for n in WEIGHTS if n not in SHARD_AXIS]
GATHER_BF16 = ['ada_w', 'w_in', 's5_glu_w', 'w_out', 'mlp_w1', 'mlp_w2']
GATHER_F32 = ['conv_w', 'rg_lambda', 'rg_ba', 'rg_bi']


def _params(n_axes, vmem_mb=40):
    return pltpu.CompilerParams(dimension_semantics=("arbitrary",) * n_axes, vmem_limit_bytes=vmem_mb * MIB)


def _tile(n, options):
    for t in options:
        if n % t == 0:
            return t
    return n


def _sigmoid(z):
    return 1.0 / (1.0 + jnp.exp(-z))


def _softplus(z):
    return jnp.maximum(z, 0.0) + jnp.log(1.0 + jnp.exp(-jnp.abs(z)))


def _neg_expm1(z):
    series = -z * (1.0 + 0.5 * z * (1.0 + (1.0 / 3.0) * z * (1.0 + 0.25 * z)))
    return jnp.where(jnp.abs(z) < 1e-2, series, 1.0 - jnp.exp(z))


_G0 = math.sqrt(2.0 / math.pi)
_G1 = 0.044715


def _gelu(v):
    return 0.5 * v * (1.0 + jnp.tanh(_G0 * (v + _G1 * v * v * v)))


def _gelu_and_grad(v):
    t = jnp.tanh(_G0 * (v + _G1 * v * v * v))
    g = 0.5 * v * (1.0 + t)
    dg = 0.5 * (1.0 + t) + 0.5 * v * (1.0 - t * t) * _G0 * (1.0 + 3.0 * _G1 * v * v)
    return g, dg


def _seq_of_tile(i, tps):
    return 2 * (i // tps) + jnp.minimum(i % tps, 1)


def _dot(a, b, dims=(((1,), (0,)), ((), ()))):
    return lax.dot_general(a, b, dims, preferred_element_type=F32)


_NT = (((1,), (1,)), ((), ()))
_TN = (((0,), (0,)), ((), ()))


def _mm_nn(a, b, bias=None, *, relu2=False, name):
    m, k = a.shape
    n = b.shape[1]
    tm, tn, tk = _tile(m, (512, 256)), _tile(n, (1024,)), _tile(k, (1024,))
    nk = k // tk
    has_bias = bias is not None

    def body(*refs):
        a_ref, b_ref = refs[0], refs[1]
        bias_ref = refs[2] if has_bias else None
        outs = refs[2 + has_bias:-1]
        acc = refs[-1]
        kk = pl.program_id(2)

        @pl.when(kk == 0)
        def _():
            acc[...] = jnp.zeros_like(acc)

        acc[...] += _dot(a_ref[...], b_ref[...])

        @pl.when(kk == nk - 1)
        def _():
            h = acc[...]
            if has_bias:
                h = h + bias_ref[...]
            if relu2:
                r = jnp.maximum(h, 0.0)
                outs[0][...] = (r * r).astype(BF16)
                outs[1][...] = h.astype(BF16)
            else:
                outs[0][...] = h

    in_specs = [pl.BlockSpec((tm, tk), lambda j, i, kk: (i, kk)), pl.BlockSpec((tk, tn), lambda j, i, kk: (kk, j))]
    args = [a, b]
    if has_bias:
        in_specs.append(pl.BlockSpec((1, tn), lambda j, i, kk: (0, j)))
        args.append(bias)
    o_spec = pl.BlockSpec((tm, tn), lambda j, i, kk: (i, j))
    if relu2:
        out_shape = (jax.ShapeDtypeStruct((m, n), BF16), jax.ShapeDtypeStruct((m, n), BF16))
        out_specs = (o_spec, o_spec)
    else:
        out_shape, out_specs = jax.ShapeDtypeStruct((m, n), F32), o_spec
    return pl.pallas_call(body, name=name, grid=(n // tn, m // tm, nk), in_specs=in_specs, out_specs=out_specs,
                          out_shape=out_shape, scratch_shapes=[pltpu.VMEM((tm, tn), F32)],
                          compiler_params=_params(3))(*args)


def _mm_nt(a, b, hb=None, *, name):
    m, n = a.shape
    k = b.shape[0]
    tm, tn, tk = _tile(m, (512, 256)), _tile(k, (1024,)), _tile(n, (1024,))
    nk = n // tk
    fused = hb is not None

    def body(*refs):
        a_ref, b_ref = refs[0], refs[1]
        hb_ref = refs[2] if fused else None
        o_ref, acc = refs[-2], refs[-1]
        kk = pl.program_id(2)

        @pl.when(kk == 0)
        def _():
            acc[...] = jnp.zeros_like(acc)

        acc[...] += _dot(a_ref[...], b_ref[...], _NT)

        @pl.when(kk == nk - 1)
        def _():
            if fused:
                o_ref[...] = (acc[...] * (2.0 * jnp.maximum(hb_ref[...].astype(F32), 0.0))).astype(BF16)
            else:
                o_ref[...] = acc[...]

    in_specs = [pl.BlockSpec((tm, tk), lambda j, i, kk: (i, kk)), pl.BlockSpec((tn, tk), lambda j, i, kk: (j, kk))]
    args = [a, b]
    if fused:
        in_specs.append(pl.BlockSpec((tm, tn), lambda j, i, kk: (i, j)))
        args.append(hb)
    return pl.pallas_call(body, name=name, grid=(k // tn, m // tm, nk), in_specs=in_specs,
                          out_specs=pl.BlockSpec((tm, tn), lambda j, i, kk: (i, j)),
                          out_shape=jax.ShapeDtypeStruct((m, k), BF16 if fused else F32),
                          scratch_shapes=[pltpu.VMEM((tm, tn), F32)], compiler_params=_params(3))(*args)


def _mm_tn(a, b, *, name, layer=None, into=None):
    m, k = a.shape
    n = b.shape[1]
    tk, tn, tr = _tile(k, (1024,)), _tile(n, (1024,)), _tile(m, (512, 256))
    nr = m // tr

    def body(a_ref, b_ref, *rest):
        o_ref, acc = rest[-2], rest[-1]
        r = pl.program_id(2)

        @pl.when(r == 0)
        def _():
            acc[...] = jnp.zeros_like(acc)

        acc[...] += _dot(a_ref[...], b_ref[...], _TN)

        @pl.when(r == nr - 1)
        def _():
            o_ref[...] = acc[...]

    in_specs = [pl.BlockSpec((tr, tk), lambda i, j, r: (r, i)), pl.BlockSpec((tr, tn), lambda i, j, r: (r, j))]
    args, aliases = [a, b], {}
    if layer is None:
        out_spec, out_shape = pl.BlockSpec((tk, tn), lambda i, j, r: (i, j)), jax.ShapeDtypeStruct((k, n), F32)
    else:
        out_spec = pl.BlockSpec((None, tk, tn), lambda i, j, r: (layer, i, j))
        out_shape = jax.ShapeDtypeStruct((DEPTH, k, n), F32)
        if into is not None:
            in_specs.append(_ANY)
            args.append(into)
            aliases = {2: 0}
    return pl.pallas_call(body, name=name, grid=(k // tk, n // tn, nr), in_specs=in_specs, out_specs=out_spec,
                          out_shape=out_shape, input_output_aliases=aliases,
                          scratch_shapes=[pltpu.VMEM((tk, tn), F32)], compiler_params=_params(3))(*args)


def _colsum(v, *, name):
    m, n = v.shape
    tn, tr = _tile(n, (1024,)), _tile(m, (512, 256))

    def body(v_ref, o_ref):
        @pl.when(pl.program_id(1) == 0)
        def _():
            o_ref[...] = jnp.zeros_like(o_ref)

        o_ref[...] += jnp.sum(v_ref[...].astype(F32), axis=0, keepdims=True)

    return pl.pallas_call(body, name=name, grid=(n // tn, m // tr),
                          in_specs=[pl.BlockSpec((tr, tn), lambda j, r: (r, j))],
                          out_specs=pl.BlockSpec((1, tn), lambda j, r: (0, j)),
                          out_shape=jax.ShapeDtypeStruct((1, n), F32), compiler_params=_params(2))(v)


def _tok_spec(d=D_MODEL, col=0):
    return pl.BlockSpec((TM, d), lambda i: (i, col))


def _mod_spec(tps):
    return pl.BlockSpec((1, N_MOD, D_MODEL), lambda i: (_seq_of_tile(i, tps), 0, 0))


def _row_spec(d=D_MODEL):
    return pl.BlockSpec((1, d), lambda i: (0, 0))


def _seq_acc_spec(tps):
    return pl.BlockSpec((1, 1, D_MODEL), lambda i: (_seq_of_tile(i, tps), 0, 0))


def _modulate(xs, modall, k_shift, k_scale, tps, *, name):
    n = xs.shape[0]

    def body(x_ref, m_ref, o_ref):
        sh = m_ref[0, k_shift:k_shift + 1, :]
        sc = m_ref[0, k_scale:k_scale + 1, :]
        o_ref[...] = (x_ref[...] * (1.0 + sc) + sh).astype(BF16)

    return pl.pallas_call(body, name=name, grid=(n // TM,), in_specs=[_tok_spec(), _mod_spec(tps)],
                          out_specs=_tok_spec(), out_shape=jax.ShapeDtypeStruct((n, D_MODEL), BF16),
                          compiler_params=_params(1))(xs, modall)


def _resid_ln(xs, ms, modall, k_gate, g, b, tps, *, name):
    n = xs.shape[0]

    def body(x_ref, m_ref, mod_ref, g_ref, b_ref, o_ref):
        z = ALPHA * x_ref[...] + mod_ref[0, k_gate:k_gate + 1, :] * m_ref[...]
        mu = jnp.mean(z, axis=-1, keepdims=True)
        zc = z - mu
        var = jnp.mean(zc * zc, axis=-1, keepdims=True)
        o_ref[...] = zc * lax.rsqrt(var + LN_EPS) * g_ref[...] + b_ref[...]

    return pl.pallas_call(body, name=name, grid=(n // TM,),
                          in_specs=[_tok_spec(), _tok_spec(), _mod_spec(tps), _row_spec(), _row_spec()],
                          out_specs=_tok_spec(), out_shape=jax.ShapeDtypeStruct((n, D_MODEL), F32),
                          compiler_params=_params(1))(xs, ms, modall, g, b)


def _resid_ln_bwd(xs, ms, modall, k_gate, g, dout, tps, *, name):
    n = xs.shape[0]

    def body(x_ref, m_ref, mod_ref, g_ref, d_ref, dxa_ref, dm_ref, dbias_ref, dg_ref, db_ref, dgate_ref):
        i = pl.program_id(0)
        gate = mod_ref[0, k_gate:k_gate + 1, :]
        m = m_ref[...]
        z = ALPHA * x_ref[...] + gate * m
        mu = jnp.mean(z, axis=-1, keepdims=True)
        zc = z - mu
        var = jnp.mean(zc * zc, axis=-1, keepdims=True)
        rstd = lax.rsqrt(var + LN_EPS)
        xhat = zc * rstd
        d = d_ref[...]
        dxh = d * g_ref[...]
        dz = rstd * (dxh - jnp.mean(dxh, axis=-1, keepdims=True)
                     - xhat * jnp.mean(dxh * xhat, axis=-1, keepdims=True))
        dxa_ref[...] = ALPHA * dz
        dm = gate * dz
        dm_ref[...] = dm.astype(BF16)

        @pl.when(i == 0)
        def _():
            dbias_ref[...] = jnp.zeros_like(dbias_ref)
            dg_ref[...] = jnp.zeros_like(dg_ref)
            db_ref[...] = jnp.zeros_like(db_ref)

        dbias_ref[...] += jnp.sum(dm, axis=0, keepdims=True)
        dg_ref[...] += jnp.sum(d * xhat, axis=0, keepdims=True)
        db_ref[...] += jnp.sum(d, axis=0, keepdims=True)
        part = jnp.sum(dz * m, axis=0, keepdims=True)

        @pl.when(i % tps <= 1)
        def _():
            dgate_ref[0] = part

        @pl.when(i % tps > 1)
        def _():
            dgate_ref[0] += part

    row = jax.ShapeDtypeStruct((1, D_MODEL), F32)
    return pl.pallas_call(
        body, name=name, grid=(n // TM,),
        in_specs=[_tok_spec(), _tok_spec(), _mod_spec(tps), _row_spec(), _tok_spec()],
        out_specs=(_tok_spec(), _tok_spec(), _row_spec(), _row_spec(), _row_spec(), _seq_acc_spec(tps)),
        out_shape=(jax.ShapeDtypeStruct((n, D_MODEL), F32), jax.ShapeDtypeStruct((n, D_MODEL), BF16), row, row, row,
                   jax.ShapeDtypeStruct((n // TM // tps * 2, 1, D_MODEL), F32)),
        compiler_params=_params(1))(xs, ms, modall, g, dout)


def _modulate_bwd(du, xs, modall, k_scale, dxa, tps, *, name):
    n = xs.shape[0]

    def body(du_ref, x_ref, mod_ref, dxa_ref, dx_ref, dsc_ref, dsh_ref):
        i = pl.program_id(0)
        du_t = du_ref[...]
        dx_ref[...] = dxa_ref[...] + du_t * (1.0 + mod_ref[0, k_scale:k_scale + 1, :])
        psc = jnp.sum(du_t * x_ref[...], axis=0, keepdims=True)
        psh = jnp.sum(du_t, axis=0, keepdims=True)

        @pl.when(i % tps <= 1)
        def _():
            dsc_ref[0] = psc
            dsh_ref[0] = psh

        @pl.when(i % tps > 1)
        def _():
            dsc_ref[0] += psc
            dsh_ref[0] += psh

    acc = jax.ShapeDtypeStruct((n // TM // tps * 2, 1, D_MODEL), F32)
    return pl.pallas_call(body, name=name, grid=(n // TM,),
                          in_specs=[_tok_spec(), _tok_spec(), _mod_spec(tps), _tok_spec()],
                          out_specs=(_tok_spec(), _seq_acc_spec(tps), _seq_acc_spec(tps)),
                          out_shape=(jax.ShapeDtypeStruct((n, D_MODEL), F32), acc, acc),
                          compiler_params=_params(1))(du, xs, modall, dxa)


def _loss_head(ys, target, tps, *, name):
    n = ys.shape[0]
    lat_tiles = tps - 1

    def body(y_ref, t_ref, acc_ref, dy_ref):
        i = pl.program_id(0)

        @pl.when(i == 0)
        def _():
            acc_ref[...] = jnp.zeros_like(acc_ref)

        @pl.when(i % tps == 0)
        def _():
            dy_ref[...] = jnp.zeros_like(dy_ref)

        @pl.when(i % tps > 0)
        def _():
            e = y_ref[...] - t_ref[...]
            dy_ref[...] = e * (1.0 / D_MODEL)
            acc_ref[...] += jnp.sum(e * e, axis=0, keepdims=True)

    t_spec = pl.BlockSpec((TM, D_MODEL), lambda i: ((i // tps) * lat_tiles + jnp.maximum(i % tps - 1, 0), 0))
    return pl.pallas_call(body, name=name, grid=(n // TM,), in_specs=[_tok_spec(), t_spec],
                          out_specs=(_row_spec(), _tok_spec()),
                          out_shape=(jax.ShapeDtypeStruct((1, D_MODEL), F32), jax.ShapeDtypeStruct((n, D_MODEL), F32)),
                          compiler_params=_params(1))(ys, target)


def _glu_fwd(y_nat, p, d_skip, w, b, *, name):
    n = y_nat.shape[0]

    def body(y_ref, u_ref, d_ref, w_ref, b_ref, o_ref):
        g = _gelu(y_ref[...] + d_ref[...] * u_ref[...])
        z = _dot(g.astype(BF16), w_ref[...]) + b_ref[...]
        o_ref[...] = (g * _sigmoid(z)).astype(BF16)

    return pl.pallas_call(body, name=name, grid=(n // TM,),
                          in_specs=[_tok_spec(), _tok_spec(col=2), _row_spec(),
                                    pl.BlockSpec((D_MODEL, D_MODEL), lambda i: (0, 0)), _row_spec()],
                          out_specs=_tok_spec(), out_shape=jax.ShapeDtypeStruct((n, D_MODEL), BF16),
                          compiler_params=_params(1))(y_nat, p, d_skip, w, b)


def _glu_bwd(dcat, y_nat, p, d_skip, w, b, *, name):
    n = y_nat.shape[0]

    def body(ds_ref, y_ref, u_ref, d_ref, w_ref, b_ref, dy_ref, dsk_ref, g_ref, dz_ref, dd_ref, dbz_ref):
        u = u_ref[...]
        g, gg = _gelu_and_grad(y_ref[...] + d_ref[...] * u)
        s = _sigmoid(_dot(g.astype(BF16), w_ref[...]) + b_ref[...])
        ds = ds_ref[...]
        dz = ds * g * s * (1.0 - s)
        dzb = dz.astype(BF16)
        dg = ds * s + _dot(dzb, w_ref[...], _NT)
        dyp = dg * gg
        dy_ref[...] = dyp
        dsk_ref[...] = dyp * d_ref[...]
        g_ref[...] = g.astype(BF16)
        dz_ref[...] = dzb

        @pl.when(pl.program_id(0) == 0)
        def _():
            dd_ref[...] = jnp.zeros_like(dd_ref)
            dbz_ref[...] = jnp.zeros_like(dbz_ref)

        dd_ref[...] += jnp.sum(dyp * u, axis=0, keepdims=True)
        dbz_ref[...] += jnp.sum(dz, axis=0, keepdims=True)

    tok_bf = jax.ShapeDtypeStruct((n, D_MODEL), BF16)
    tok_f32 = jax.ShapeDtypeStruct((n, D_MODEL), F32)
    row = jax.ShapeDtypeStruct((1, D_MODEL), F32)
    return pl.pallas_call(
        body, name=name, grid=(n // TM,),
        in_specs=[_tok_spec(col=1), _tok_spec(), _tok_spec(col=2), _row_spec(),
                  pl.BlockSpec((D_MODEL, D_MODEL), lambda i: (0, 0)), _row_spec()],
        out_specs=(_tok_spec(), _tok_spec(), _tok_spec(), _tok_spec(), _row_spec(), _row_spec()),
        out_shape=(tok_f32, tok_f32, tok_bf, tok_bf, row, row),
        compiler_params=_params(1))(dcat, y_nat, p, d_skip, w, b)


def _add_cast(a, b, *, name):
    n = a.shape[0]

    def body(a_ref, b_ref, o_ref):
        o_ref[...] = (a_ref[...] + b_ref[...]).astype(BF16)

    return pl.pallas_call(body, name=name, grid=(n // TM,), in_specs=[_tok_spec(), _tok_spec()],
                          out_specs=_tok_spec(), out_shape=jax.ShapeDtypeStruct((n, D_MODEL), BF16),
                          compiler_params=_params(1))(a, b)


def _silu_rows(c16, *, name):
    def body(c_ref, s_ref, ds_ref):
        v = c_ref[...]
        sg = _sigmoid(v)
        s_ref[...] = v * sg
        ds_ref[...] = sg * (1.0 + v * (1.0 - sg))

    shp = jax.ShapeDtypeStruct(c16.shape, F32)
    return pl.pallas_call(body, name=name, out_shape=(shp, shp))(c16)


def _mul_rows(a, b, *, name):
    def body(a_ref, b_ref, o_ref):
        o_ref[...] = a_ref[...] * b_ref[...]

    return pl.pallas_call(body, name=name, out_shape=jax.ShapeDtypeStruct(a.shape, F32))(a, b)


def _pad_off(c):
    return pl.multiple_of(c * TM + 8 + 8 * jnp.minimum(c, 1), 8)


def _rows8(k):
    return pl.ds(pl.multiple_of(k * 8, 8), 8)


def _windows(buf, c, shifts):
    n = TM + 16
    win = buf[pl.ds(pl.multiple_of(_pad_off(c) - 8, 8), n), :]
    return [win[8:8 + TM] if k == 0 else pltpu.roll(win, (-k) % n, 0)[8:8 + TM] for k in shifts]


def _conv_window(xpad, c):
    return _windows(xpad, c, (-1, 0, 1, 2))


def _rg_coeffs(z, d, spl, xc):
    r = _sigmoid(z[:, 256 * d:256 * d + HEAD])
    i = _sigmoid(z[:, 256 * d + HEAD:256 * d + 2 * HEAD])
    la = -RG_C * spl[d:d + 1, :] * r
    a = jnp.exp(la)
    mult = jnp.sqrt(_neg_expm1(2.0 * la))
    return r, i, a, mult, a * a


def _chunk_scan(a, b, reverse):
    row = lax.broadcasted_iota(jnp.int32, (TM, HEAD), 0)
    sft = 1
    while sft < TM:
        keep = (row < TM - sft) if reverse else (row >= sft)
        amt = TM - sft if reverse else sft
        a_prev = jnp.where(keep, pltpu.roll(a, amt, 0), 1.0)
        b_prev = jnp.where(keep, pltpu.roll(b, amt, 0), 0.0)
        b = a * b_prev + b
        a = a * a_prev
        sft *= 2
    return a, b


def _zero_pads(buf, s, lc):
    z8 = jnp.zeros((8, HEAD), F32)
    buf[0:8, :] = z8
    buf[8 + lc:16 + lc, :] = z8
    buf[16 + s:24 + s, :] = z8


def _rg_fwd(p, conv_w, conv_b, lam, wcat, bcat, bl, s, lc, *, name):
    nch = s // TM

    def body(x_ref, gate_ref, cw_ref, cb_ref, lam_ref, w_ref, b_ref, rg_ref, hf_ref, hb_ref, xpad, af, bf, ab, bb):
        _zero_pads(xpad, s, lc)

        def copy_chunk(c, _):
            xpad[pl.ds(_pad_off(c), TM), :] = x_ref[pl.ds(pl.multiple_of(c * TM, TM), TM), :]
            return 0

        lax.fori_loop(0, nch, copy_chunk, 0)
        spl = _softplus(-lam_ref[...])
        cw = cw_ref[...]

        def coef_chunk(c, _):
            xm1, x0, xp1, xp2 = _conv_window(xpad, c)
            xc = cw[0:1] * xm1 + cw[1:2] * x0 + cw[2:3] * xp1 + cw[3:4] * xp2 + cb_ref[...]
            z = _dot(xc.astype(BF16), w_ref[0]) + b_ref[0]
            rows = pl.ds(pl.multiple_of(c * TM, TM), TM)
            for d, (a_s, b_s) in enumerate(((af, bf), (ab, bb))):
                _, i, a, mult, _ = _rg_coeffs(z, d, spl, xc)
                a_s[rows, :] = a
                b_s[rows, :] = mult * i * xc
            return 0

        lax.fori_loop(0, nch, coef_chunk, 0)

        def scan_pair(j, carry):
            cf, cb_ = carry
            rf = pl.ds(pl.multiple_of(j * TM, TM), TM)
            rb = pl.ds(pl.multiple_of(jnp.where(j == 0, 0, nch - j) * TM, TM), TM)
            a1, h1 = _chunk_scan(af[rf, :], bf[rf, :], False)
            h1 = h1 + a1 * cf
            hf_ref[rf, :] = h1
            a2, h2 = _chunk_scan(ab[rb, :], bb[rb, :], True)
            h2 = h2 + a2 * cb_
            hb_ref[rb, :] = h2
            return h1[TM - 1:TM], h2[0:1]

        zero = jnp.zeros((1, HEAD), F32)
        lax.fori_loop(0, nch, scan_pair, (zero, zero))

        def out_chunk(c, _):
            rows = pl.ds(pl.multiple_of(c * TM, TM), TM)
            rg_ref[rows, :] = ((hf_ref[rows, :] + hb_ref[rows, :]) * _gelu(gate_ref[rows, :])).astype(BF16)
            return 0

        lax.fori_loop(0, nch, out_chunk, 0)

    seq = lambda col0: pl.BlockSpec((s, HEAD), lambda b, h: (b, col0 + h))
    par = lambda r: pl.BlockSpec((r, HEAD), lambda b, h: (0, h))
    n = bl * s
    return pl.pallas_call(
        body, name=name, grid=(bl, RG_HEADS),
        in_specs=[seq(0), seq(RG_HEADS), par(4), par(1), par(2),
                  pl.BlockSpec((1, HEAD, 4 * HEAD), lambda b, h: (h, 0, 0)),
                  pl.BlockSpec((1, 1, 4 * HEAD), lambda b, h: (h, 0, 0))],
        out_specs=(seq(0), seq(0), seq(0)),
        out_shape=(jax.ShapeDtypeStruct((n, D_MODEL), BF16), jax.ShapeDtypeStruct((n, D_MODEL), F32),
                   jax.ShapeDtypeStruct((n, D_MODEL), F32)),
        scratch_shapes=[pltpu.VMEM((s + 24, HEAD), F32)] + [pltpu.VMEM((s, HEAD), F32)] * 4,
        compiler_params=_params(2, 48))(p, p, conv_w, conv_b, lam, wcat, bcat)


def _rg_bwd(p, dcat, hf, hb, conv_w, conv_b, lam, wcat, bcat, bl, s, lc, *, name):
    nch = s // TM
    ll = s - lc

    def body(p_hbm, dcat_hbm, hf_hbm, hb_hbm, cw_ref, cb_ref, lam_ref, w_ref, b_ref,
             drgx_ref, dgate_ref, dcw_ref, dcb_ref, dlam_ref, dw_ref, db_ref,
             xpad, dxpad, hf_s, hb_s, gate_s, dhs, a_f, a_b, lam_f, lam_b, sems):
        h = pl.program_id(0)
        b = pl.program_id(1)
        row0 = b * s
        col = pl.multiple_of(h * HEAD, HEAD)

        def rows_of(ref, r0, nr, c0):
            return ref.at[pl.ds(row0 + r0, nr), pl.ds(c0, HEAD)]

        copies = [
            pltpu.make_async_copy(rows_of(p_hbm, 0, lc, col), xpad.at[pl.ds(8, lc), :], sems.at[0]),
            pltpu.make_async_copy(rows_of(p_hbm, lc, ll, col), xpad.at[pl.ds(16 + lc, ll), :], sems.at[1]),
            pltpu.make_async_copy(rows_of(p_hbm, 0, s, col + D_MODEL), gate_s, sems.at[2]),
            pltpu.make_async_copy(rows_of(dcat_hbm, 0, s, col), dhs, sems.at[3]),
            pltpu.make_async_copy(rows_of(hf_hbm, 0, s, col), hf_s.at[pl.ds(8, s), :], sems.at[4]),
            pltpu.make_async_copy(rows_of(hb_hbm, 0, s, col), hb_s.at[pl.ds(8, s), :], sems.at[5]),
        ]
        for cp in copies:
            cp.start()
        _zero_pads(xpad, s, lc)
        _zero_pads(dxpad, s, lc)
        for buf in (hf_s, hb_s):
            buf[0:8, :] = jnp.zeros((8, HEAD), F32)
            buf[8 + s:16 + s, :] = jnp.zeros((8, HEAD), F32)

        @pl.when(b == 0)
        def _():
            dcw_ref[...] = jnp.zeros_like(dcw_ref)
            dcb_ref[...] = jnp.zeros_like(dcb_ref)
            dlam_ref[...] = jnp.zeros_like(dlam_ref)
            dw_ref[...] = jnp.zeros_like(dw_ref)
            db_ref[...] = jnp.zeros_like(db_ref)

        for cp in copies:
            cp.wait()
        lam_v = lam_ref[...]
        spl = _softplus(-lam_v)
        cw = cw_ref[...]

        def conv(c):
            xm1, x0, xp1, xp2 = _conv_window(xpad, c)
            return cw[0:1] * xm1 + cw[1:2] * x0 + cw[2:3] * xp1 + cw[3:4] * xp2 + cb_ref[...]

        def pass_a(c, _):
            rows = pl.ds(pl.multiple_of(c * TM, TM), TM)
            xc = conv(c)
            z = _dot(xc.astype(BF16), w_ref[0]) + b_ref[0]
            for d, a_s in enumerate((a_f, a_b)):
                a_s[rows, :] = _rg_coeffs(z, d, spl, xc)[2]
            g, gg = _gelu_and_grad(gate_s[rows, :])
            drg = dhs[rows, :]
            hrows = pl.ds(pl.multiple_of(c * TM + 8, 8), TM)
            dgate_ref[rows, :] = (drg * (hf_s[hrows, :] + hb_s[hrows, :]) * gg).astype(BF16)
            dhs[rows, :] = drg * g
            return 0

        lax.fori_loop(0, nch, pass_a, 0)

        row = lax.broadcasted_iota(jnp.int32, (TM, HEAD), 0)

        def adj_pair(j, carry):
            cf, cb_ = carry
            rf = pl.ds(pl.multiple_of((nch - 1 - j) * TM, TM), TM)
            rb = pl.ds(pl.multiple_of(jnp.where(j == nch - 1, 0, j + 1) * TM, TM), TM)
            d1, a1 = dhs[rf, :], a_f[rf, :]
            p1, m1 = _chunk_scan(a1, a1 * d1, True)
            m1 = m1 + p1 * cf
            lam_f[rf, :] = d1 + jnp.where(row == TM - 1, cf, pltpu.roll(m1, TM - 1, 0))
            d2, a2 = dhs[rb, :], a_b[rb, :]
            p2, m2 = _chunk_scan(a2, a2 * d2, False)
            m2 = m2 + p2 * cb_
            lam_b[rb, :] = d2 + jnp.where(row == 0, cb_, pltpu.roll(m2, 1, 0))
            return m1[0:1], m2[TM - 1:TM]

        zero = jnp.zeros((1, HEAD), F32)
        lax.fori_loop(0, nch, adj_pair, (zero, zero))

        sig_neg = _sigmoid(-lam_v)
        last_row = lax.broadcasted_iota(jnp.int32, (TM, HEAD), 0) == TM - 1
        hb_first = hb_s[8:9, :]

        def pass_b(c, _):
            rows = pl.ds(pl.multiple_of(c * TM, TM), TM)
            xc = conv(c)
            xcb = xc.astype(BF16)
            z = _dot(xcb, w_ref[0]) + b_ref[0]
            dxc = jnp.zeros((TM, HEAD), F32)
            dzs = []
            n = TM + 16
            hp_f = pltpu.roll(hf_s[pl.ds(pl.multiple_of(c * TM, TM), n), :], 1, 0)[8:8 + TM]
            hp_b = pltpu.roll(hb_s[pl.ds(pl.multiple_of(c * TM, TM), n), :], n - 1, 0)[8:8 + TM]
            hp_b = jnp.where(last_row & (c == 0), 0.0, hp_b)
            hp_b = jnp.where(last_row & (c == nch - 1), hb_first, hp_b)
            for d, (l_s, hp) in enumerate(((lam_f, hp_f), (lam_b, hp_b))):
                r, i, a, mult, e2 = _rg_coeffs(z, d, spl, xc)
                dbt = l_s[rows, :]
                dla = dbt * hp * a - dbt * i * xc * (e2 / mult)
                dlam_ref[0, d:d + 1, :] += jnp.sum(dla * r, axis=0, keepdims=True) * (RG_C * sig_neg[d:d + 1, :])
                dr = dla * (-RG_C * spl[d:d + 1, :])
                di = dbt * mult * xc
                dxc = dxc + dbt * mult * i
                dzs += [dr * r * (1.0 - r), di * i * (1.0 - i)]
            dz = jnp.concatenate(dzs, axis=1)
            dzb = dz.astype(BF16)
            dxc = dxc + _dot(dzb, w_ref[0], _NT)
            dw_ref[0] += _dot(xcb, dzb, _TN)
            db_ref[0] += jnp.sum(dz, axis=0, keepdims=True)
            dcb_ref[0] += jnp.sum(dxc, axis=0, keepdims=True)
            dxpad[pl.ds(_pad_off(c), TM), :] = dxc
            return 0

        lax.fori_loop(0, nch, pass_b, 0)

        def pass_c(c, _):
            rows = pl.ds(pl.multiple_of(c * TM, TM), TM)
            gp1, g0, gm1, gm2 = _windows(dxpad, c, (1, 0, -1, -2))
            drgx_ref[rows, :] = (cw[0:1] * gp1 + cw[1:2] * g0 + cw[2:3] * gm1 + cw[3:4] * gm2).astype(BF16)
            xm1, x0, xp1, xp2 = _conv_window(xpad, c)
            dcw_ref[0] += jnp.concatenate([jnp.sum(g0 * t, axis=0, keepdims=True) for t in (xm1, x0, xp1, xp2)],
                                          axis=0)
            return 0

        lax.fori_loop(0, nch, pass_c, 0)

    seq = pl.BlockSpec((s, HEAD), lambda h, b: (b, h))
    par = lambda r: pl.BlockSpec((r, HEAD), lambda h, b: (0, h))
    acc = lambda r, w: pl.BlockSpec((1, r, w), lambda h, b: (h, 0, 0))
    anyspec = pl.BlockSpec(memory_space=pl.ANY)
    n = bl * s
    big = pltpu.VMEM((s, HEAD), F32)
    return pl.pallas_call(
        body, name=name, grid=(RG_HEADS, bl),
        in_specs=[anyspec, anyspec, anyspec, anyspec, par(4), par(1), par(2), acc(HEAD, 4 * HEAD), acc(1, 4 * HEAD)],
        out_specs=(seq, seq, acc(4, HEAD), acc(1, HEAD), acc(2, HEAD), acc(HEAD, 4 * HEAD), acc(1, 4 * HEAD)),
        out_shape=(jax.ShapeDtypeStruct((n, D_MODEL), BF16), jax.ShapeDtypeStruct((n, D_MODEL), BF16),
                   jax.ShapeDtypeStruct((RG_HEADS, 4, HEAD), F32), jax.ShapeDtypeStruct((RG_HEADS, 1, HEAD), F32),
                   jax.ShapeDtypeStruct((RG_HEADS, 2, HEAD), F32),
                   jax.ShapeDtypeStruct((RG_HEADS, HEAD, 4 * HEAD), F32),
                   jax.ShapeDtypeStruct((RG_HEADS, 1, 4 * HEAD), F32)),
        scratch_shapes=[pltpu.VMEM((s + 24, HEAD), F32)] * 2 + [pltpu.VMEM((s + 16, HEAD), F32)] * 2 + [big] * 6
        + [pltpu.SemaphoreType.DMA((6,))],
        compiler_params=_params(2, 52))(p, dcat, hf, hb, conv_w, conv_b, lam, wcat, bcat)


def _s5_mats(a_re, a_im, log_dt, b_re, b_im, c_re, c_im):
    t = T_CH
    g = a_re.shape[1]
    dt = jnp.exp(log_dt)[..., None]
    lr, li = a_re * dt, a_im * dt
    steps = jnp.arange(t + 1, dtype=F32)[:, None]
    mag = jnp.exp(lr[:, :, None, :] * steps)
    ang = li[:, :, None, :] * steps
    pr, pi = mag * jnp.cos(ang), mag * jnp.sin(ang)
    xr, xi = pr[:, :, 1] - 1.0, pi[:, :, 1]
    den = a_re * a_re + a_im * a_im
    qr, qi = (xr * a_re + xi * a_im) / den, (xi * a_re - xr * a_im) / den
    btr, bti = b_re.transpose(0, 1, 3, 2), b_im.transpose(0, 1, 3, 2)
    bbr = qr[:, :, None, :] * btr - qi[:, :, None, :] * bti
    bbi = qr[:, :, None, :] * bti + qi[:, :, None, :] * btr
    up, down = slice(0, t), slice(t - 1, None, -1)

    def pow_c(d, sl):
        wr, wi = pr[d][:, sl, None, :], pi[d][:, sl, None, :]
        cr, ci = c_re[d][:, None], c_im[d][:, None]
        return (wr * cr - wi * ci).reshape(g, CW, S5_STATE), (wr * ci + wi * cr).reshape(g, CW, S5_STATE)

    hp = lax.Precision.HIGHEST

    def lag_map(d, sl):
        re, im = pow_c(d, sl)
        return (jnp.einsum('gkp,gmp->gkm', bbr[d], re, precision=hp)
                - jnp.einsum('gkp,gmp->gkm', bbi[d], im, precision=hp))

    z_f, z_b = lag_map(0, up), lag_map(1, down)
    kf = jnp.stack([jnp.pad(z_f, ((0, 0), (0, 0), (S5_GROUP * s, 0)))[:, :, :CW] for s in range(t)], axis=1)
    kb = jnp.stack([jnp.pad(z_b, ((0, 0), (0, 0), (0, S5_GROUP * (t - 1 - s))))[:, :, S5_GROUP * (t - 1 - s):]
                    for s in range(t)], axis=1)
    kcat = (kf + kb).reshape(g, CW, CW)

    def state_in(d, sl):
        wr, wi = pr[d][:, sl, None, :], pi[d][:, sl, None, :]
        br, bi = bbr[d][:, None], bbi[d][:, None]
        return jnp.concatenate([wr * br - wi * bi, wr * bi + wi * br], axis=-1).reshape(g, CW, SW)

    wcat = jnp.concatenate([kcat, state_in(0, down), state_in(1, up)], axis=2)
    of_r, of_i = pow_c(0, slice(1, t + 1))
    ob_r, ob_i = pow_c(1, slice(t, 0, -1))
    mout_t = jnp.concatenate([of_r, -of_i, ob_r, -ob_i], axis=2)
    rows = []
    for d in range(2):
        art, ait = pr[d][:, t], pi[d][:, t]
        rows += [jnp.concatenate([art, art], axis=1).reshape(-1), jnp.concatenate([-ait, ait], axis=1).reshape(-1)]
    return wcat, mout_t, jnp.stack(rows)


def _lane_swap(v):
    return pltpu.roll(v, S5_STATE, 1)


def _grp(g, w):
    return slice(g * w, (g + 1) * w)


def _s5_fwd(u, wcat, mout, a2, ncc, *, name):
    bl, nc, _ = u.shape

    def body(u_ref, w_ref, mo_ref, a_ref, y_ref, sf_ref, sb_ref, vf, vb):
        for g in range(GB):
            zu = _dot(u_ref[:, _grp(g, CW)], w_ref[g])
            y_ref[:, _grp(g, CW)] = zu[:, :CW]
            vf[:, _grp(g, SW)] = zu[:, CW:CW + SW]
            vb[:, _grp(g, SW)] = zu[:, CW + SW:]
        co = [[a_ref[r:r + 1, _grp(g, SW)] for g in range(GB)] for r in range(4)]

        rid = lax.broadcasted_iota(jnp.int32, (8, SW), 0)

        def step8(kf, kb, carry):
            rf, rb = _rows8(kf), _rows8(kb)
            vfb, vbb = vf[rf, :], vb[rb, :]
            out = []
            for g in range(GB):
                sf, sb = carry[2 * g], carry[2 * g + 1]
                of = ob = jnp.zeros((8, SW), F32)
                for i in range(8):
                    k = 7 - i
                    of = jnp.where(rid == i, sf, of)
                    sf = co[0][g] * sf + co[1][g] * _lane_swap(sf) + vfb[i:i + 1, _grp(g, SW)]
                    ob = jnp.where(rid == k, sb, ob)
                    sb = co[2][g] * sb + co[3][g] * _lane_swap(sb) + vbb[k:k + 1, _grp(g, SW)]
                sf_ref[rf, _grp(g, SW)] = of
                sb_ref[rb, _grp(g, SW)] = ob
                out += [sf, sb]
            return tuple(out)

        zero = jnp.zeros((1, SW), F32)
        nbc, nb = ncc // 8, nc // 8
        carry = lax.fori_loop(0, nbc, lambda j, cr: step8(j, nbc - 1 - j, cr), (zero,) * (2 * GB))
        lax.fori_loop(nbc, nb, lambda j, cr: step8(j, nb + nbc - 1 - j, cr), carry)
        for g in range(GB):
            st = jnp.concatenate([sf_ref[:, _grp(g, SW)], sb_ref[:, _grp(g, SW)]], axis=1).astype(BF16)
            y_ref[:, _grp(g, CW)] += _dot(st, mo_ref[g], _NT)

    blk = lambda w: pl.BlockSpec((None, nc, GB * w), lambda b, gb: (b, 0, gb))
    return pl.pallas_call(
        body, name=name, grid=(bl, S5_GROUPS // GB),
        in_specs=[blk(CW), pl.BlockSpec((GB, CW, 2 * CW), lambda b, gb: (gb, 0, 0)),
                  pl.BlockSpec((GB, CW, CW), lambda b, gb: (gb, 0, 0)),
                  pl.BlockSpec((4, GB * SW), lambda b, gb: (0, gb))],
        out_specs=(blk(CW), blk(SW), blk(SW)),
        out_shape=(jax.ShapeDtypeStruct(u.shape, F32), jax.ShapeDtypeStruct((bl, nc, S5_GROUPS * SW), F32),
                   jax.ShapeDtypeStruct((bl, nc, S5_GROUPS * SW), F32)),
        scratch_shapes=[pltpu.VMEM((nc, GB * SW), F32)] * 2, compiler_params=_params(2))(u, wcat, mout, a2)


def _s5_bwd(dy, u, sf, sb, wcat, mout, a2, ncc, *, name):
    bl, nc, _ = u.shape

    def body(dy_ref, u_ref, sf_ref, sb_ref, w_ref, mo_ref, a_ref, du_ref, dw_ref, dmo_ref, dacc_ref, gsf, gsb, dvf, dvb):
        b = pl.program_id(1)

        @pl.when(b == 0)
        def _():
            dw_ref[...] = jnp.zeros_like(dw_ref)
            dmo_ref[...] = jnp.zeros_like(dmo_ref)
            dacc_ref[...] = jnp.zeros_like(dacc_ref)

        for g in range(GB):
            ds = _dot(dy_ref[:, _grp(g, CW)], mo_ref[g])
            gsf[:, _grp(g, SW)] = ds[:, :SW]
            gsb[:, _grp(g, SW)] = ds[:, SW:]
        co = [[a_ref[r:r + 1, _grp(g, SW)] for g in range(GB)] for r in range(4)]

        rid = lax.broadcasted_iota(jnp.int32, (8, SW), 0)

        def step8(kf, kb, carry):
            rf, rb = _rows8(kf), _rows8(kb)
            gfb, gbb = gsf[rf, :], gsb[rb, :]
            out = []
            for g in range(GB):
                gf, gb_ = carry[2 * g], carry[2 * g + 1]
                of = ob = jnp.zeros((8, SW), F32)
                for i in range(8):
                    k = 7 - i
                    of = jnp.where(rid == k, gf, of)
                    gf = gfb[k:k + 1, _grp(g, SW)] + co[0][g] * gf - co[1][g] * _lane_swap(gf)
                    ob = jnp.where(rid == i, gb_, ob)
                    gb_ = gbb[i:i + 1, _grp(g, SW)] + co[2][g] * gb_ - co[3][g] * _lane_swap(gb_)
                dvf[rf, _grp(g, SW)] = of
                dvb[rb, _grp(g, SW)] = ob
                out += [gf, gb_]
            return tuple(out)

        zero = jnp.zeros((1, SW), F32)
        nbc, nb = ncc // 8, nc // 8
        carry = lax.fori_loop(0, nb - nbc, lambda j, cr: step8(nb - 1 - j, nbc + j, cr), (zero,) * (2 * GB))
        lax.fori_loop(0, nbc, lambda j, cr: step8(nbc - 1 - j, j, cr), carry)
        for g in range(GB):
            dyg = dy_ref[:, _grp(g, CW)]
            dvf_g, dvb_g = dvf[:, _grp(g, SW)], dvb[:, _grp(g, SW)]
            sf_g, sb_g = sf_ref[:, _grp(g, SW)], sb_ref[:, _grp(g, SW)]
            dz = jnp.concatenate([dyg, dvf_g.astype(BF16), dvb_g.astype(BF16)], axis=1)
            du_ref[:, _grp(g, CW)] = _dot(dz, w_ref[g], _NT)
            dw_ref[g] += _dot(u_ref[:, _grp(g, CW)], dz, _TN)
            st = jnp.concatenate([sf_g, sb_g], axis=1).astype(BF16)
            dmo_ref[g] += _dot(dyg, st, _TN)
            dacc_ref[:, _grp(g, SW)] += jnp.concatenate(
                [jnp.sum(dvf_g * sf_g, axis=0, keepdims=True), jnp.sum(dvf_g * _lane_swap(sf_g), axis=0, keepdims=True),
                 jnp.sum(dvb_g * sb_g, axis=0, keepdims=True), jnp.sum(dvb_g * _lane_swap(sb_g), axis=0, keepdims=True)],
                axis=0)

    blk = lambda w: pl.BlockSpec((None, nc, GB * w), lambda gb, b: (b, 0, gb))
    wspec = pl.BlockSpec((GB, CW, 2 * CW), lambda gb, b: (gb, 0, 0))
    mspec = pl.BlockSpec((GB, CW, CW), lambda gb, b: (gb, 0, 0))
    aspec = pl.BlockSpec((4, GB * SW), lambda gb, b: (0, gb))
    return pl.pallas_call(
        body, name=name, grid=(S5_GROUPS // GB, bl),
        in_specs=[blk(CW), blk(CW), blk(SW), blk(SW), wspec, mspec, aspec],
        out_specs=(blk(CW), wspec, mspec, aspec),
        out_shape=(jax.ShapeDtypeStruct(u.shape, F32), jax.ShapeDtypeStruct(wcat.shape, F32),
                   jax.ShapeDtypeStruct(mout.shape, F32), jax.ShapeDtypeStruct(a2.shape, F32)),
        scratch_shapes=[pltpu.VMEM((nc, GB * SW), F32)] * 4,
        compiler_params=_params(2, 48))(dy, u, sf, sb, wcat, mout, a2)


def _lane_slot():
    return lax.broadcasted_iota(jnp.int32, (GRID_W, HEAD), 1) // S5_GROUP


def _lat_to_chunks(src, col0, bl, s, lc, *, name):
    nrh = (s - lc) // GRID_W // T_CH

    def body(x_ref, o_ref):
        slot = _lane_slot()

        def one(rh, _):
            tiles = [x_ref[pl.ds(pl.multiple_of(lc + (rh * T_CH + t) * GRID_W, GRID_W), GRID_W), :]
                     for t in range(T_CH)]
            for q in range(HEAD // S5_GROUP):
                for j in range(CW // HEAD):
                    acc = jnp.zeros((GRID_W, HEAD), F32)
                    for m in range(HEAD // S5_GROUP):
                        shift = ((m - q) * S5_GROUP) % HEAD
                        v = tiles[8 * j + m]
                        acc = jnp.where(slot == m, v if shift == 0 else pltpu.roll(v, shift, 1), acc)
                    o_ref[rh, :, q * CW + j * HEAD:q * CW + (j + 1) * HEAD] = acc.astype(BF16)
            return 0

        lax.fori_loop(0, nrh, one, 0)

    return pl.pallas_call(
        body, name=name, grid=(bl, S5_GROUPS // GB),
        in_specs=[pl.BlockSpec((s, HEAD), lambda b, gb: (b, col0 + gb))],
        out_specs=pl.BlockSpec((None, nrh, GRID_W, GB * CW), lambda b, gb: (b, 0, 0, gb)),
        out_shape=jax.ShapeDtypeStruct((bl, nrh, GRID_W, S5_GROUPS * CW), BF16),
        compiler_params=_params(2))(src)


def _lat_from_chunks(v4, ctx_nat, bl, s, lc, *, name):
    nrh = v4.shape[1]

    def body(v_ref, c_ref, o_ref):
        o_ref[0:lc, :] = c_ref[...]
        slot = _lane_slot()

        def one(rh, _):
            for t in range(T_CH):
                j, m = t // 8, t % 8
                acc = jnp.zeros((GRID_W, HEAD), F32)
                for q in range(HEAD // S5_GROUP):
                    shift = ((q - m) * S5_GROUP) % HEAD
                    v = v_ref[rh, :, q * CW + j * HEAD:q * CW + (j + 1) * HEAD]
                    acc = jnp.where(slot == q, v if shift == 0 else pltpu.roll(v, shift, 1), acc)
                o_ref[pl.ds(pl.multiple_of(lc + (rh * T_CH + t) * GRID_W, GRID_W), GRID_W), :] = acc
            return 0

        lax.fori_loop(0, nrh, one, 0)

    return pl.pallas_call(
        body, name=name, grid=(bl, S5_GROUPS // GB),
        in_specs=[pl.BlockSpec((None, nrh, GRID_W, GB * CW), lambda b, gb: (b, 0, 0, gb)),
                  pl.BlockSpec((lc, HEAD), lambda b, gb: (b, gb))],
        out_specs=pl.BlockSpec((s, HEAD), lambda b, gb: (b, gb)),
        out_shape=jax.ShapeDtypeStruct((bl * s, D_MODEL), F32), compiler_params=_params(2))(v4, ctx_nat)


def _to_chunks(src, col0, bl, s, lc, *, name):
    ll = s - lc
    lat = _lat_to_chunks(src, col0, bl, s, lc, name=name)
    lat = lat.transpose(0, 2, 1, 3).reshape(bl, ll // T_CH, S5_GROUPS * CW)
    ctx = src.reshape(bl, s, -1)[:, :lc, col0 * HEAD:col0 * HEAD + D_MODEL].astype(BF16)
    ctx = ctx.reshape(bl, lc // T_CH, T_CH, S5_GROUPS, S5_GROUP).transpose(0, 1, 3, 2, 4)
    return jnp.concatenate([ctx.reshape(bl, lc // T_CH, -1), lat], axis=1)


def _from_chunks(v, bl, s, lc, *, name):
    ncc = lc // T_CH
    nrh = (s - lc) // GRID_W // T_CH
    ctx = v[:, :ncc].reshape(bl, ncc, S5_GROUPS, T_CH, S5_GROUP).transpose(0, 1, 3, 2, 4).reshape(bl * lc, D_MODEL)
    lat = v[:, ncc:].reshape(bl, GRID_W, nrh, S5_GROUPS * CW).transpose(0, 2, 1, 3)
    return _lat_from_chunks(lat, ctx, bl, s, lc, name=name)


_ANY = pl.BlockSpec(memory_space=pl.ANY)


def _xy_peers():
    x, y, c = lax.axis_index("x"), lax.axis_index("y"), lax.axis_index("c")
    return x, y, c, [(1 - x, y), (x, 1 - y), (1 - x, 1 - y)]


def _all_gather_xy(shard, *, name):
    def body(x_ref, out_ref, send_sems, recv_sems, local_sem):
        x, y, c, peers = _xy_peers()
        me = 2 * x + y
        mine = pltpu.make_async_copy(x_ref, out_ref.at[me], local_sem)
        mine.start()

        def copy(k, px, py, slot):
            return pltpu.make_async_remote_copy(src_ref=x_ref, dst_ref=out_ref.at[slot], send_sem=send_sems.at[k],
                                                recv_sem=recv_sems.at[k], device_id=(px, py, c), device_id_type=MESH)

        sends = [copy(k, px, py, me) for k, (px, py) in enumerate(peers)]
        for cp in sends:
            cp.start()
        for k, (px, py) in enumerate(peers):
            copy(k, px, py, 2 * px + py).wait_recv()
        for cp in sends:
            cp.wait_send()
        mine.wait()

    return pl.pallas_call(body, name=name, in_specs=[_ANY], out_specs=_ANY,
                          out_shape=jax.ShapeDtypeStruct((4,) + shard.shape, shard.dtype),
                          scratch_shapes=[pltpu.SemaphoreType.DMA((3,)), pltpu.SemaphoreType.DMA((3,)),
                                          pltpu.SemaphoreType.DMA])(shard)


def _scatter_xy(parts, *, name):
    def body(p_ref, out_ref, send_sems, recv_sems, local_sem):
        x, y, c, peers = _xy_peers()
        mine = pltpu.make_async_copy(p_ref.at[2 * x + y], out_ref.at[0], local_sem)
        mine.start()

        def copy(k, px, py):
            return pltpu.make_async_remote_copy(src_ref=p_ref.at[2 * px + py], dst_ref=out_ref.at[1 + k],
                                                send_sem=send_sems.at[k], recv_sem=recv_sems.at[k],
                                                device_id=(px, py, c), device_id_type=MESH)

        sends = [copy(k, px, py) for k, (px, py) in enumerate(peers)]
        for cp in sends:
            cp.start()
        for cp in sends:
            cp.wait_recv()
        for cp in sends:
            cp.wait_send()
        mine.wait()

    return pl.pallas_call(body, name=name, in_specs=[_ANY], out_specs=_ANY,
                          out_shape=jax.ShapeDtypeStruct(parts.shape, parts.dtype),
                          scratch_shapes=[pltpu.SemaphoreType.DMA((3,)), pltpu.SemaphoreType.DMA((3,)),
                                          pltpu.SemaphoreType.DMA])(parts)


def _swap_sibling(v, *, name):
    def body(v_ref, out_ref, send_sem, recv_sem):
        x, y, c = lax.axis_index("x"), lax.axis_index("y"), lax.axis_index("c")
        cp = pltpu.make_async_remote_copy(src_ref=v_ref, dst_ref=out_ref, send_sem=send_sem, recv_sem=recv_sem,
                                          device_id=(x, y, 1 - c), device_id_type=MESH)
        cp.start()
        cp.wait()

    return pl.pallas_call(body, name=name, in_specs=[_ANY], out_specs=_ANY,
                          out_shape=jax.ShapeDtypeStruct(v.shape, v.dtype),
                          scratch_shapes=[pltpu.SemaphoreType.DMA, pltpu.SemaphoreType.DMA])(v)


BIG_COLS = {'ada_w': True, 'w_in': True, 'mlp_w1': True, 's5_glu_w': False, 'w_out': False, 'mlp_w2': False}
BIG = list(BIG_COLS)


def _block(ref2d, j, cols, size):
    if cols:
        return ref2d.at[:, pl.ds(pl.multiple_of(j * size, 128), size)]
    return ref2d.at[pl.ds(pl.multiple_of(j * size, 8), size), :]


def _shard_size(shape, cols):
    return shape[-1] if cols else shape[-2]


def _gather_big(shards, cols, *, name):
    n = len(shards)

    def body(*refs):
        ins, outs = refs[:n], refs[n:2 * n]
        ici_send, ici_recv, d2d_send, d2d_recv, loc = refs[2 * n:]
        x, y, c, peers = _xy_peers()
        me = 2 * x + y

        def blk(w, layer, j):
            return _block(outs[w].at[layer], j, cols[w], _shard_size(ins[w].shape, cols[w]))

        def ici(w, k, px, py, j):
            return pltpu.make_async_remote_copy(src_ref=ins[w].at[c], dst_ref=blk(w, c, j),
                                                send_sem=ici_send.at[3 * w + k], recv_sem=ici_recv.at[3 * w + k],
                                                device_id=(px, py, c), device_id_type=MESH)

        def d2d(w, k, j, layer):
            return pltpu.make_async_remote_copy(src_ref=blk(w, layer, j), dst_ref=blk(w, layer, j),
                                                send_sem=d2d_send.at[3 * w + k], recv_sem=d2d_recv.at[3 * w + k],
                                                device_id=(x, y, 1 - c), device_id_type=MESH)

        started = []
        for w in range(n):
            for layer in range(DEPTH):
                started.append(pltpu.make_async_copy(ins[w].at[layer], blk(w, layer, me), loc.at[DEPTH * w + layer]))
            started += [ici(w, k, px, py, me) for k, (px, py) in enumerate(peers)]
        for cp in started:
            cp.start()
        passed = []
        for w in range(n):
            for k, (px, py) in enumerate(peers):
                ici(w, k, px, py, 2 * px + py).wait_recv()
                passed.append(d2d(w, k, 2 * px + py, c))
                passed[-1].start()
        for w in range(n):
            for k, (px, py) in enumerate(peers):
                d2d(w, k, 2 * px + py, 1 - c).wait_recv()
        for i, cp in enumerate(started):
            if i % (DEPTH + 3) < DEPTH:
                cp.wait()
            else:
                cp.wait_send()
        for cp in passed:
            cp.wait_send()

    def full(s, cf):
        return (DEPTH, s.shape[1], 4 * s.shape[2]) if cf else (DEPTH, 4 * s.shape[1], s.shape[2])

    return pl.pallas_call(
        body, name=name, in_specs=[_ANY] * n, out_specs=[_ANY] * n,
        out_shape=[jax.ShapeDtypeStruct(full(s, cf), s.dtype) for s, cf in zip(shards, cols)],
        scratch_shapes=[pltpu.SemaphoreType.DMA((3 * n,))] * 4 + [pltpu.SemaphoreType.DMA((DEPTH * n,))])(*shards)


def _sibling_partials(gbufs, *, name):
    n = len(gbufs)

    def body(*refs):
        ins, outs, send, recv = refs[:n], refs[n:2 * n], refs[2 * n], refs[2 * n + 1]
        x, y, c = lax.axis_index("x"), lax.axis_index("y"), lax.axis_index("c")
        cps = [pltpu.make_async_remote_copy(src_ref=ins[w].at[1 - c], dst_ref=outs[w], send_sem=send.at[w],
                                            recv_sem=recv.at[w], device_id=(x, y, 1 - c), device_id_type=MESH)
               for w in range(n)]
        for cp in cps:
            cp.start()
        for cp in cps:
            cp.wait()

    return pl.pallas_call(body, name=name, in_specs=[_ANY] * n, out_specs=[_ANY] * n,
                          out_shape=[jax.ShapeDtypeStruct(g.shape[1:], g.dtype) for g in gbufs],
                          scratch_shapes=[pltpu.SemaphoreType.DMA((n,))] * 2)(*gbufs)


def _chip_sum(gbuf, other, my_c, *, name):
    _, k, n = gbuf.shape
    tr, tc = _tile(k, (512,)), _tile(n, (1024,))

    def body(c_ref, a_ref, b_ref, o_ref):
        o_ref[...] = (a_ref[...] + b_ref[...]).astype(BF16)

    spec = pl.BlockSpec((tr, tc), lambda i, j, c_ref: (i, j))
    return pl.pallas_call(
        body, name=name,
        grid_spec=pltpu.PrefetchScalarGridSpec(
            num_scalar_prefetch=1, grid=(k // tr, n // tc),
            in_specs=[pl.BlockSpec((None, tr, tc), lambda i, j, c_ref: (c_ref[0], i, j)), spec], out_specs=spec),
        out_shape=jax.ShapeDtypeStruct((k, n), BF16), compiler_params=_params(2))(my_c, gbuf, other)


def _scatter_big(sums, cols, *, name):
    n = len(sums)

    def shard(s, cf):
        return (s.shape[0], s.shape[1] // 4) if cf else (s.shape[0] // 4, s.shape[1])

    def body(*refs):
        ins, outs, send, recv = refs[:n], refs[n:2 * n], refs[2 * n], refs[2 * n + 1]
        x, y, c, peers = _xy_peers()
        cps = []
        for w in range(n):
            size = _shard_size(shard(ins[w], cols[w]), cols[w])
            for k, (px, py) in enumerate(peers):
                cps.append(pltpu.make_async_remote_copy(
                    src_ref=_block(ins[w], 2 * px + py, cols[w], size), dst_ref=outs[w].at[k],
                    send_sem=send.at[3 * w + k], recv_sem=recv.at[3 * w + k], device_id=(px, py, c),
                    device_id_type=MESH))
        for cp in cps:
            cp.start()
        for cp in cps:
            cp.wait()

    return pl.pallas_call(body, name=name, in_specs=[_ANY] * n, out_specs=[_ANY] * n,
                          out_shape=[jax.ShapeDtypeStruct((3,) + shard(s, cf), s.dtype) for s, cf in zip(sums, cols)],
                          scratch_shapes=[pltpu.SemaphoreType.DMA((3 * n,))] * 2)(*sums)


def _block_sum(own, got, cols, my_j, my_c, *, name):
    _, r, c = got.shape
    tr, tc = _tile(r, (256,)), _tile(c, (1024, 768, 512))

    def body(j_ref, c_ref, a_ref, g_ref, o_ref):
        o_ref[...] = ((a_ref[...].astype(F32) + g_ref[0].astype(F32)) + g_ref[1].astype(F32)) + g_ref[2].astype(F32)

    if cols:
        own_spec = pl.BlockSpec((tr, tc), lambda i, j, j_ref, c_ref: (i, j_ref[0] * (c // tc) + j))
    else:
        own_spec = pl.BlockSpec((tr, tc), lambda i, j, j_ref, c_ref: (j_ref[0] * (r // tr) + i, j))
    return pl.pallas_call(
        body, name=name,
        grid_spec=pltpu.PrefetchScalarGridSpec(
            num_scalar_prefetch=2, grid=(r // tr, c // tc),
            in_specs=[own_spec, pl.BlockSpec((3, tr, tc), lambda i, j, j_ref, c_ref: (0, i, j))],
            out_specs=pl.BlockSpec((None, tr, tc), lambda i, j, j_ref, c_ref: (c_ref[0], i, j))),
        out_shape=jax.ShapeDtypeStruct((DEPTH, r, c), F32), compiler_params=_params(2))(my_j, my_c, own, got)


def _share_final(bufs, *, name):
    n = len(bufs)

    def body(*refs):
        outs, send, recv = refs[n:2 * n], refs[2 * n], refs[2 * n + 1]
        x, y, c = lax.axis_index("x"), lax.axis_index("y"), lax.axis_index("c")

        def copy(w, slot):
            return pltpu.make_async_remote_copy(src_ref=outs[w].at[slot], dst_ref=outs[w].at[slot],
                                                send_sem=send.at[w], recv_sem=recv.at[w],
                                                device_id=(x, y, 1 - c), device_id_type=MESH)

        away = [copy(w, c) for w in range(n)]
        for cp in away:
            cp.start()
        for w in range(n):
            copy(w, 1 - c).wait_recv()
        for cp in away:
            cp.wait_send()

    return pl.pallas_call(body, name=name, in_specs=[_ANY] * n, out_specs=[_ANY] * n,
                          out_shape=[jax.ShapeDtypeStruct(b.shape, b.dtype) for b in bufs],
                          input_output_aliases={w: w for w in range(n)},
                          scratch_shapes=[pltpu.SemaphoreType.DMA((n,))] * 2)(*bufs)


def _adamw_native(w, g, m, v, *, name):
    r, c = w.shape
    tr = _tile(r, (256, 128, 64, 32, 16, 8))
    spec = pl.BlockSpec((tr, c), lambda i: (i, 0))
    c1 = 1.0 / (1.0 - ADAM_B1 ** ADAM_STEP)
    c2 = 1.0 / (1.0 - ADAM_B2 ** ADAM_STEP)

    def body(w_ref, g_ref, m_ref, v_ref, d_ref, nm_ref, nv_ref):
        g_t = g_ref[...]
        nm = ADAM_B1 * m_ref[...] + (1.0 - ADAM_B1) * g_t
        nv = ADAM_B2 * v_ref[...] + (1.0 - ADAM_B2) * (g_t * g_t)
        nm_ref[...] = nm
        nv_ref[...] = nv
        d_ref[...] = -ADAM_LR * ((nm * c1) / (jnp.sqrt(nv * c2) + ADAM_EPS) + ADAM_WD * w_ref[...])

    shp = jax.ShapeDtypeStruct((r, c), F32)
    return pl.pallas_call(body, name=name, grid=(r // tr,), in_specs=[spec] * 4, out_specs=(spec,) * 3,
                          out_shape=(shp,) * 3, compiler_params=_params(1))(w, g, m, v)


def _flat_tile(r):
    return _tile(r, (512, 256, 128, 64, 32, 16, 8))


def _sum4(parts, *, name):
    r = parts.shape[1]
    tr = _flat_tile(r)

    def body(p_ref, o_ref):
        o_ref[...] = ((p_ref[0] + p_ref[1]) + p_ref[2]) + p_ref[3]

    return pl.pallas_call(body, name=name, grid=(r // tr,),
                          in_specs=[pl.BlockSpec((4, tr, LANES), lambda i: (0, i, 0))],
                          out_specs=pl.BlockSpec((tr, LANES), lambda i: (i, 0)),
                          out_shape=jax.ShapeDtypeStruct((r, LANES), F32), compiler_params=_params(1))(parts)


def _add2(a, b, *, name):
    r = a.shape[0]
    tr = _flat_tile(r)
    spec = pl.BlockSpec((tr, LANES), lambda i: (i, 0))

    def body(a_ref, b_ref, o_ref):
        o_ref[...] = a_ref[...] + b_ref[...]

    return pl.pallas_call(body, name=name, grid=(r // tr,), in_specs=[spec, spec], out_specs=spec,
                          out_shape=jax.ShapeDtypeStruct((r, LANES), F32), compiler_params=_params(1))(a, b)


def _adamw(w, ga, gb, m, v, *, name):
    r = w.shape[0]
    tr = _flat_tile(r)
    spec = pl.BlockSpec((tr, LANES), lambda i: (i, 0))
    two = gb is not None
    c1 = 1.0 / (1.0 - ADAM_B1 ** ADAM_STEP)
    c2 = 1.0 / (1.0 - ADAM_B2 ** ADAM_STEP)

    def body(*refs):
        w_ref, ga_ref = refs[0], refs[1]
        m_ref, v_ref, g_ref, d_ref, nm_ref, nv_ref = refs[2 + two:]
        g = ga_ref[...] + refs[2][...] if two else ga_ref[...]
        nm = ADAM_B1 * m_ref[...] + (1.0 - ADAM_B1) * g
        nv = ADAM_B2 * v_ref[...] + (1.0 - ADAM_B2) * (g * g)
        g_ref[...] = g
        nm_ref[...] = nm
        nv_ref[...] = nv
        d_ref[...] = -ADAM_LR * ((nm * c1) / (jnp.sqrt(nv * c2) + ADAM_EPS) + ADAM_WD * w_ref[...])

    args = [w, ga] + ([gb] if two else []) + [m, v]
    shp = jax.ShapeDtypeStruct((r, LANES), F32)
    return pl.pallas_call(body, name=name, grid=(r // tr,), in_specs=[spec] * len(args), out_specs=(spec,) * 4,
                          out_shape=(shp,) * 4, compiler_params=_params(1))(*args)


def _pack(arrs, dtype=F32):
    flat = jnp.concatenate([a.astype(dtype).reshape(-1) for a in arrs])
    pad = (-flat.shape[0]) % (32 * LANES)
    return jnp.pad(flat, (0, pad)).reshape(-1, LANES)


def _unpack(buf, shapes):
    flat = buf.reshape(-1)
    out, off = [], 0
    for shp in shapes:
        sz = math.prod(shp)
        out.append(flat[off:off + sz].reshape(shp))
        off += sz
    return out


def _stack_shards(full, axis):
    shp = full.shape
    return jnp.moveaxis(full.reshape(shp[:axis] + (4, shp[axis] // 4) + shp[axis + 1:]), axis, 0)


def _unstack_shards(st, axis):
    v = jnp.moveaxis(st, 0, axis)
    shp = v.shape
    return v.reshape(shp[:axis] + (shp[axis] * shp[axis + 1],) + shp[axis + 2:])


def _layer_weights(w, l):
    lw = {n: w[n][l] for n in w}
    lw['wcat'] = jnp.concatenate([lw['rg_wa'][0], lw['rg_wi'][0], lw['rg_wa'][1], lw['rg_wi'][1]],
                                 axis=-1).astype(BF16)
    ba, bi = lw['rg_ba'].reshape(2, RG_HEADS, HEAD), lw['rg_bi'].reshape(2, RG_HEADS, HEAD)
    lw['bcat'] = jnp.concatenate([ba[0], bi[0], ba[1], bi[1]], axis=-1)[:, None, :]
    s5_names = ['s5_a_re', 's5_a_im', 's5_log_dt', 's5_b_re', 's5_b_im', 's5_c_re', 's5_c_im']
    (wcat, mout, a2), lw['s5_vjp'] = jax.vjp(_s5_mats, *[lw[n] for n in s5_names])
    lw['s5_wcat'], lw['s5_mout'], lw['s5_a2'] = wcat.astype(BF16), mout.astype(BF16), a2
    for n in ('conv_b', 's5_d', 's5_glu_b', 'b_out', 'mlp_b1', 'mlp_b2', 'ln1_g', 'ln1_b', 'ln2_g', 'ln2_b'):
        lw[n] = lw[n][None, :]
    return lw


def _layer_fwd(l, x0, modall, lw, dims):
    bl, s, lc, tps = dims
    ll = s - lc
    tag = f"l{l}_"
    sv = {'x0': x0}
    sv['u1'] = _modulate(x0, modall, 0, 1, tps, name=tag + "mod1")
    sv['p'] = p = _mm_nn(sv['u1'], lw['w_in'], name=tag + "w_in")
    rg, sv['hf'], sv['hb'] = _rg_fwd(p, lw['conv_w'], lw['conv_b'], lw['rg_lambda'], lw['wcat'], lw['bcat'],
                                     bl, s, lc, name=tag + "rg_fwd")
    sv['u_ch'] = _to_chunks(p, 2 * D_MODEL // HEAD, bl, s, lc, name=tag + "u_chunks")
    y_ch, sv['sf'], sv['sb'] = _s5_fwd(sv['u_ch'], lw['s5_wcat'], lw['s5_mout'], lw['s5_a2'], lc // T_CH,
                                       name=tag + "s5_fwd")
    sv['y'] = _from_chunks(y_ch, bl, s, lc, name=tag + "y_rows")
    s5 = _glu_fwd(sv['y'], p, lw['s5_d'], lw['s5_glu_w'], lw['s5_glu_b'], name=tag + "glu_fwd")
    sv['cat'] = jnp.concatenate([rg, s5], axis=1)
    sv['m'] = _mm_nn(sv['cat'], lw['w_out'], lw['b_out'], name=tag + "w_out")
    sv['x1'] = _resid_ln(x0, sv['m'], modall, 2, lw['ln1_g'], lw['ln1_b'], tps, name=tag + "ln1")
    sv['u2'] = _modulate(sv['x1'], modall, 3, 4, tps, name=tag + "mod2")
    sv['a'], sv['h'] = _mm_nn(sv['u2'], lw['mlp_w1'], lw['mlp_b1'], relu2=True, name=tag + "mlp1")
    sv['f'] = _mm_nn(sv['a'], lw['mlp_w2'], lw['mlp_b2'], name=tag + "mlp2")
    x2 = _resid_ln(sv['x1'], sv['f'], modall, 5, lw['ln2_g'], lw['ln2_b'], tps, name=tag + "ln2")
    return x2, sv


def _layer_bwd(l, dx2, modall, lw, sv, dims, gbufs):
    bl, s, lc, tps = dims
    ll = s - lc
    tag = f"l{l}_"
    g = {}

    def big_grad(n, a_mat, b_mat, label):
        gbufs[n] = _mm_tn(a_mat, b_mat, name=tag + label, layer=l, into=gbufs.get(n))
    dx1a, df, db2, g['ln2_g'], g['ln2_b'], dg2 = _resid_ln_bwd(sv['x1'], sv['f'], modall, 5, lw['ln2_g'], dx2, tps,
                                                              name=tag + "ln2_bwd")
    g['mlp_b2'] = db2
    big_grad('mlp_w2', sv['a'], df, "mlp2_dw")
    dh = _mm_nt(df, lw['mlp_w2'], sv['h'], name=tag + "mlp2_dx")
    g['mlp_b1'] = _colsum(dh, name=tag + "mlp1_db")
    big_grad('mlp_w1', sv['u2'], dh, "mlp1_dw")
    du2 = _mm_nt(dh, lw['mlp_w1'], name=tag + "mlp1_dx")
    dx1, dsc2, dsh2 = _modulate_bwd(du2, sv['x1'], modall, 4, dx1a, tps, name=tag + "mod2_bwd")
    dx0a, dm, g['b_out'], g['ln1_g'], g['ln1_b'], dg1 = _resid_ln_bwd(sv['x0'], sv['m'], modall, 2, lw['ln1_g'], dx1,
                                                                     tps, name=tag + "ln1_bwd")
    big_grad('w_out', sv['cat'], dm, "w_out_dw")
    dcat = _mm_nt(dm, lw['w_out'], name=tag + "w_out_dx")
    dy, dskip, g_bf, dz_bf, g['s5_d'], g['s5_glu_b'] = _glu_bwd(dcat, sv['y'], sv['p'], lw['s5_d'], lw['s5_glu_w'],
                                                                lw['s5_glu_b'], name=tag + "glu_bwd")
    big_grad('s5_glu_w', g_bf, dz_bf, "glu_dw")
    dy_ch = _to_chunks(dy, 0, bl, s, lc, name=tag + "dy_chunks")
    du_ch, dwcat, dmout, dacc = _s5_bwd(dy_ch, sv['u_ch'], sv['sf'], sv['sb'], lw['s5_wcat'],
                                        lw['s5_mout'], lw['s5_a2'], lc // T_CH, name=tag + "s5_bwd")
    s5g = lw['s5_vjp']((dwcat, dmout, dacc))
    for n, v in zip(['s5_a_re', 's5_a_im', 's5_log_dt', 's5_b_re', 's5_b_im', 's5_c_re', 's5_c_im'], s5g):
        g[n] = v
    ds5u = _add_cast(_from_chunks(du_ch, bl, s, lc, name=tag + "du_rows"), dskip, name=tag + "ds5u")
    drgx, dgate, dcw, dcb, dlam, dwc, dbc = _rg_bwd(sv['p'], dcat, sv['hf'], sv['hb'], lw['conv_w'], lw['conv_b'],
                                                    lw['rg_lambda'], lw['wcat'], lw['bcat'], bl, s, lc,
                                                    name=tag + "rg_bwd")
    g['conv_w'] = dcw.transpose(1, 0, 2).reshape(4, D_MODEL)
    g['conv_b'] = dcb.reshape(D_MODEL)
    g['rg_lambda'] = dlam.transpose(1, 0, 2).reshape(2, D_MODEL)
    g['rg_wa'] = jnp.stack([dwc[:, :, 0:HEAD], dwc[:, :, 2 * HEAD:3 * HEAD]])
    g['rg_wi'] = jnp.stack([dwc[:, :, HEAD:2 * HEAD], dwc[:, :, 3 * HEAD:]])
    dbc = dbc.reshape(RG_HEADS, 4, HEAD)
    g['rg_ba'] = jnp.stack([dbc[:, 0], dbc[:, 2]]).reshape(2, D_MODEL)
    g['rg_bi'] = jnp.stack([dbc[:, 1], dbc[:, 3]]).reshape(2, D_MODEL)
    dp = jnp.concatenate([drgx, dgate, ds5u], axis=1)
    big_grad('w_in', sv['u1'], dp, "w_in_dw")
    du1 = _mm_nt(dp, lw['w_in'], name=tag + "w_in_dx")
    dx0, dsc1, dsh1 = _modulate_bwd(du1, sv['x0'], modall, 1, dx0a, tps, name=tag + "mod1_bwd")
    dmod = jnp.concatenate([dsh1, dsc1, dg1, dsh2, dsc2, dg2], axis=1)
    return dx0, g, dmod


def _kernel_impl(*args):
    nin = len(IN_NAMES)
    a = dict(zip(IN_NAMES, args[:nin]))
    target = args[nin]
    nw = len(WEIGHTS)
    mom = dict(zip(WEIGHTS, args[nin + 1:nin + 1 + nw]))
    var = dict(zip(WEIGHTS, args[nin + 1 + nw:nin + 1 + 2 * nw]))
    bl, ll, d = a['x'].shape
    lc = a['ctx'].shape[1]
    assert d == D_MODEL and lc == TM and bl == 2 and ll % (GRID_W * T_CH) == 0
    s = lc + ll
    tps = s // TM
    dims = (bl, s, lc, tps)

    def gather(names, dtype, tag):
        shards = [a[n] for n in names]
        got = _all_gather_xy(_pack(shards, dtype), name="gather_" + tag)
        per = [_unpack(got[j], [w.shape for w in shards]) for j in range(4)]
        return {n: _unstack_shards(jnp.stack([per[j][i] for j in range(4)]), SHARD_AXIS[n])
                for i, n in enumerate(names)}

    big_cols = [BIG_COLS[n] for n in BIG]
    w = dict(zip(BIG, _gather_big([a[n].astype(BF16) for n in BIG], big_cols, name="gather_big")))
    w.update(gather(GATHER_F32, F32, "f32"))
    for n in REPLICATED:
        w[n] = a[n]
    my_c = lax.axis_index("c").astype(jnp.int32).reshape(1)
    my_j = (2 * lax.axis_index("x") + lax.axis_index("y")).astype(jnp.int32).reshape(1)

    xs = jnp.concatenate([a['ctx'], a['x']], axis=1).reshape(bl * s, D_MODEL)
    c16 = jnp.zeros((16, D_MODEL), F32).at[0:2].set(a['c']).at[2].set(a['c_ctx'])
    s16, ds16 = _silu_rows(c16, name="silu")
    s16b = s16.astype(BF16)
    layers, saved, mods = [], [], []
    for l in range(DEPTH):
        lw = _layer_weights({n: w[n] for n in WEIGHTS if n not in ('c_ctx',)}, l)
        mod16 = _mm_nn(s16b, lw['ada_w'], lw['ada_b'][None, :], name=f"l{l}_ada").reshape(16, N_MOD, D_MODEL)
        modall = jnp.stack([mod16[2], mod16[0], mod16[2], mod16[1]])
        xs, sv = _layer_fwd(l, xs, modall, lw, dims)
        layers.append(lw)
        saved.append(sv)
        mods.append(modall)
    lossrow, dx = _loss_head(xs, target.reshape(bl * ll, D_MODEL), tps, name="loss_head")
    loss = lax.psum(0.5 / D_MODEL * jnp.sum(lossrow), ("x", "y", "c"))

    small = [n for n in WEIGHTS if n != 'c_ctx' and n not in BIG_COLS]
    grads = {n: [None] * DEPTH for n in small}
    gbufs = {}
    ds_rows = jnp.zeros((16, D_MODEL), F32)
    for l in reversed(range(DEPTH)):
        dx, g, dmod = _layer_bwd(l, dx, mods[l], layers[l], saved[l], dims, gbufs)
        dmod16 = jnp.zeros((16, N_MOD * D_MODEL), F32).at[0].set(dmod[1].reshape(-1)).at[1].set(
            dmod[3].reshape(-1)).at[2].set((dmod[0] + dmod[2]).reshape(-1))
        dmod16b = dmod16.astype(BF16)
        gbufs['ada_w'] = _mm_tn(s16b, dmod16b, name=f"l{l}_ada_dw", layer=l, into=gbufs.get('ada_w'))
        g['ada_b'] = _colsum(dmod16, name=f"l{l}_ada_db")
        ds_rows = ds_rows + _mm_nt(dmod16b, layers[l]['ada_w'], name=f"l{l}_ada_dx")
        for n, v in g.items():
            grads[n][l] = v.reshape(a[n].shape[1:] if n in REPLICATED else w[n].shape[1:])
    full = {n: jnp.stack(v) for n, v in grads.items()}
    full['c_ctx'] = _mul_rows(ds_rows, ds16, name="silu_bwd")[2]
    grad_x = dx.reshape(bl, s, D_MODEL)[:, lc:]

    from_sib = _sibling_partials([gbufs[n] for n in BIG], name="grad_big_sibling")
    sums = [_chip_sum(gbufs[n], o, my_c, name=f"grad_chip_sum_{n}") for n, o in zip(BIG, from_sib)]
    got = _scatter_big(sums, big_cols, name="grad_big_scatter")
    finals = [_block_sum(sm, gt, cf, my_j, my_c, name=f"grad_block_sum_{n}")
              for n, sm, gt, cf in zip(BIG, sums, got, big_cols)]
    res_big = {}
    for n, gfull in zip(BIG, _share_final(finals, name="grad_big_share")):
        flat = lambda t: t.reshape(-1, t.shape[-1])
        d_w, n_m, n_v = _adamw_native(flat(a[n]), flat(gfull), flat(mom[n]), flat(var[n]), name=f"adamw_{n}")
        res_big[n] = [gfull] + [t.reshape(a[n].shape) for t in (d_w, n_m, n_v)]

    rep_flat = _pack([full[n] for n in REPLICATED])
    rr = rep_flat.shape[0]
    sh_stacked = [_stack_shards(full[n], SHARD_AXIS[n] + 0).reshape(4, -1) for n in SHARDED]
    parts = jnp.concatenate(sh_stacked + [rep_flat.reshape(4, -1)], axis=1)
    pad = (-parts.shape[1]) % (32 * LANES)
    parts = jnp.pad(parts, ((0, 0), (0, pad))).reshape(4, -1, LANES)
    mine = _sum4(_scatter_xy(parts, name="grad_scatter"), name="grad_sum4")
    other = _swap_sibling(mine, name="grad_swap")
    n_sh = sum(math.prod(a[n].shape) for n in SHARDED)
    r_sh = n_sh // LANES
    assert n_sh % LANES == 0
    rq = rr // 4

    sh_shapes = [a[n].shape for n in SHARDED]
    pk = lambda dct: _pack([dct[n] for n in SHARDED])
    r_pk = pk(a).shape[0]
    take = lambda buf: jnp.pad(buf[:r_sh], ((0, r_pk - r_sh), (0, 0)))
    outs_sh = _adamw(pk(a), take(mine), take(other), pk(mom), pk(var), name="adamw_sharded")
    res_sh = [dict(zip(SHARDED, _unpack(o, sh_shapes))) for o in outs_sh]

    quarter = _add2(mine[r_sh:r_sh + rq], other[r_sh:r_sh + rq], name="grad_rep_sum")
    rep_g = _all_gather_xy(quarter, name="grad_rep_gather").reshape(rr, LANES)
    rep_shapes = [a[n].shape for n in REPLICATED]
    pr = lambda dct: _pack([dct[n] for n in REPLICATED])
    outs_rep = _adamw(pr(a), rep_g, None, pr(mom), pr(var), name="adamw_replicated")
    res_rep = [dict(zip(REPLICATED, _unpack(o, rep_shapes))) for o in outs_rep]

    out = [loss, grad_x]
    for k in range(4):
        out += [res_big[n][k] if n in BIG_COLS else res_sh[k][n] if n in SHARDED else res_rep[k][n] for n in WEIGHTS]
    return tuple(out)


def kernel(x, c, ctx, c_ctx, ada_w, ada_b, ln1_g, ln1_b, w_in, conv_w, conv_b, rg_lambda, rg_wa, rg_ba, rg_wi, rg_bi, s5_a_re, s5_a_im, s5_log_dt, s5_b_re, s5_b_im, s5_c_re, s5_c_im, s5_d, s5_glu_w, s5_glu_b, w_out, b_out, ln2_g, ln2_b, mlp_w1, mlp_b1, mlp_w2, mlp_b2, loss_target, m_c_ctx, m_ada_w, m_ada_b, m_ln1_g, m_ln1_b, m_w_in, m_conv_w, m_conv_b, m_rg_lambda, m_rg_wa, m_rg_ba, m_rg_wi, m_rg_bi, m_s5_a_re, m_s5_a_im, m_s5_log_dt, m_s5_b_re, m_s5_b_im, m_s5_c_re, m_s5_c_im, m_s5_d, m_s5_glu_w, m_s5_glu_b, m_w_out, m_b_out, m_ln2_g, m_ln2_b, m_mlp_w1, m_mlp_b1, m_mlp_w2, m_mlp_b2, v_c_ctx, v_ada_w, v_ada_b, v_ln1_g, v_ln1_b, v_w_in, v_conv_w, v_conv_b, v_rg_lambda, v_rg_wa, v_rg_ba, v_rg_wi, v_rg_bi, v_s5_a_re, v_s5_a_im, v_s5_log_dt, v_s5_b_re, v_s5_b_im, v_s5_c_re, v_s5_c_im, v_s5_d, v_s5_glu_w, v_s5_glu_b, v_w_out, v_b_out, v_ln2_g, v_ln2_b, v_mlp_w1, v_mlp_b1, v_mlp_w2, v_mlp_b2):
    return _kernel_impl(x, c, ctx, c_ctx, ada_w, ada_b, ln1_g, ln1_b, w_in, conv_w, conv_b, rg_lambda, rg_wa, rg_ba, rg_wi, rg_bi, s5_a_re, s5_a_im, s5_log_dt, s5_b_re, s5_b_im, s5_c_re, s5_c_im, s5_d, s5_glu_w, s5_glu_b, w_out, b_out, ln2_g, ln2_b, mlp_w1, mlp_b1, mlp_w2, mlp_b2, loss_target, m_c_ctx, m_ada_w, m_ada_b, m_ln1_g, m_ln1_b, m_w_in, m_conv_w, m_conv_b, m_rg_lambda, m_rg_wa, m_rg_ba, m_rg_wi, m_rg_bi, m_s5_a_re, m_s5_a_im, m_s5_log_dt, m_s5_b_re, m_s5_b_im, m_s5_c_re, m_s5_c_im, m_s5_d, m_s5_glu_w, m_s5_glu_b, m_w_out, m_b_out, m_ln2_g, m_ln2_b, m_mlp_w1, m_mlp_b1, m_mlp_w2, m_mlp_b2, v_c_ctx, v_ada_w, v_ada_b, v_ln1_g, v_ln1_b, v_w_in, v_conv_w, v_conv_b, v_rg_lambda, v_rg_wa, v_rg_ba, v_rg_wi, v_rg_bi, v_s5_a_re, v_s5_a_im, v_s5_log_dt, v_s5_b_re, v_s5_b_im, v_s5_c_re, v_s5_c_im, v_s5_d, v_s5_glu_w, v_s5_glu_b, v_w_out, v_b_out, v_ln2_g, v_ln2_b, v_mlp_w1, v_mlp_b1, v_mlp_w2, v_mlp_b2)
```

```python
import functools
import math

import jax
import jax.numpy as jnp
from jax import lax
from jax.experimental import pallas as pl
from jax.experimental.pallas import tpu as pltpu

F32 = jnp.float32
BF16 = jnp.bfloat16
MESH = pl.DeviceIdType.MESH

D_MODEL = 1024
N_MOD = 6
GRID_W = 64
RG_HEADS = 8
HEAD = 128
RG_C = 8.0
S5_GROUPS = 64
S5_GROUP = 16
S5_STATE = 64
T_CH = 16
GB = 8
CW = T_CH * S5_GROUP
SW = 2 * S5_STATE
DEPTH = 2
ALPHA = (2.0 * DEPTH) ** 0.25
LN_EPS = 1e-5
TM = 256
LANES = 1024
ADAM_LR, ADAM_B1, ADAM_B2, ADAM_EPS, ADAM_WD, ADAM_STEP = 0.001, 0.9, 0.999, 1e-08, 0.01, 10
MIB = 2 ** 20

IN_NAMES = ['x', 'c', 'ctx', 'c_ctx', 'ada_w', 'ada_b', 'ln1_g', 'ln1_b', 'w_in', 'conv_w', 'conv_b', 'rg_lambda',
            'rg_wa', 'rg_ba', 'rg_wi', 'rg_bi', 's5_a_re', 's5_a_im', 's5_log_dt', 's5_b_re', 's5_b_im', 's5_c_re',
            's5_c_im', 's5_d', 's5_glu_w', 's5_glu_b', 'w_out', 'b_out', 'ln2_g', 'ln2_b', 'mlp_w1', 'mlp_b1',
            'mlp_w2', 'mlp_b2']
WEIGHTS = IN_NAMES[3:]
SHARD_AXIS = {'ada_w': 2, 'w_in': 2, 'conv_w': 2, 'rg_lambda': 2, 'rg_ba': 2, 'rg_bi': 2, 's5_glu_w': 1, 'w_out': 1,
              'mlp_w1': 2, 'mlp_w2': 1}
SHARDED = ['conv_w', 'rg_lambda', 'rg_ba', 'rg_bi']
REPLICATED = [n for n in WEIGHTS if n not in SHARD_AXIS]
GATHER_BF16 = ['ada_w', 'w_in', 's5_glu_w', 'w_out', 'mlp_w1', 'mlp_w2']
GATHER_F32 = ['conv_w', 'rg_lambda', 'rg_ba', 'rg_bi']


def _params(n_axes, vmem_mb=40):
    return pltpu.CompilerParams(dimension_semantics=("arbitrary",) * n_axes, vmem_limit_bytes=vmem_mb * MIB)


def _tile(n, options):
    for t in options:
        if n % t == 0:
            return t
    return n


def _sigmoid(z):
    return 1.0 / (1.0 + jnp.exp(-z))


def _softplus(z):
    return jnp.maximum(z, 0.0) + jnp.log(1.0 + jnp.exp(-jnp.abs(z)))


def _neg_expm1_small(z):
    return -z * (1.0 + 0.5 * z * (1.0 + (1.0 / 3.0) * z * (1.0 + 0.25 * z)))


_G0 = math.sqrt(2.0 / math.pi)
_G1 = 0.044715


def _gelu(v):
    return 0.5 * v * (1.0 + jnp.tanh(_G0 * (v + _G1 * v * v * v)))


def _gelu_and_grad(v):
    t = jnp.tanh(_G0 * (v + _G1 * v * v * v))
    g = 0.5 * v * (1.0 + t)
    dg = 0.5 * (1.0 + t) + 0.5 * v * (1.0 - t * t) * _G0 * (1.0 + 3.0 * _G1 * v * v)
    return g, dg


def _seq_of_tile(i, tps):
    return 2 * (i // tps) + jnp.minimum(i % tps, 1)


def _dot(a, b, dims=(((1,), (0,)), ((), ()))):
    return lax.dot_general(a, b, dims, preferred_element_type=F32)


_NT = (((1,), (1,)), ((), ()))
_TN = (((0,), (0,)), ((), ()))


def _layer_spec(b, block, index):
    if isinstance(b, tuple):
        arr, layer = b
        return arr, pl.BlockSpec((None,) + block, lambda *g: (layer,) + index(*g))
    return b, pl.BlockSpec(block, index)


def _mm_nn(a, b, bias=None, *, relu2=False, name):
    m, k = a.shape
    n = (b[0] if isinstance(b, tuple) else b).shape[-1]
    tm, tn, tk = _tile(m, (512, 256)), _tile(n, (1024,)), _tile(k, (1024,))
    nk = k // tk
    has_bias = bias is not None

    def body(*refs):
        a_ref, b_ref = refs[0], refs[1]
        bias_ref = refs[2] if has_bias else None
        outs = refs[2 + has_bias:-1]
        acc = refs[-1]
        kk = pl.program_id(2)

        @pl.when(kk == 0)
        def _():
            acc[...] = jnp.zeros_like(acc)

        acc[...] += _dot(a_ref[...], b_ref[...])

        @pl.when(kk == nk - 1)
        def _():
            h = acc[...]
            if has_bias:
                h = h + bias_ref[...]
            if relu2:
                r = jnp.maximum(h, 0.0)
                outs[0][...] = (r * r).astype(BF16)
                outs[1][...] = h.astype(BF16)
            else:
                outs[0][...] = h

    b_arr, b_spec = _layer_spec(b, (tk, tn), lambda j, i, kk: (kk, j))
    in_specs = [pl.BlockSpec((tm, tk), lambda j, i, kk: (i, kk)), b_spec]
    args = [a, b_arr]
    if has_bias:
        in_specs.append(pl.BlockSpec((1, tn), lambda j, i, kk: (0, j)))
        args.append(bias)
    o_spec = pl.BlockSpec((tm, tn), lambda j, i, kk: (i, j))
    if relu2:
        out_shape = (jax.ShapeDtypeStruct((m, n), BF16), jax.ShapeDtypeStruct((m, n), BF16))
        out_specs = (o_spec, o_spec)
    else:
        out_shape, out_specs = jax.ShapeDtypeStruct((m, n), F32), o_spec
    return pl.pallas_call(body, name=name, grid=(n // tn, m // tm, nk), in_specs=in_specs, out_specs=out_specs,
                          out_shape=out_shape, scratch_shapes=[pltpu.VMEM((tm, tn), F32)],
                          compiler_params=_params(3))(*args)


def _mm_nt(a, b, hb=None, *, name):
    m, n = a.shape
    k = (b[0] if isinstance(b, tuple) else b).shape[-2]
    tm, tn, tk = _tile(m, (512, 256)), _tile(k, (1024,)), _tile(n, (1024,))
    nk = n // tk
    fused = hb is not None

    def body(*refs):
        a_ref, b_ref = refs[0], refs[1]
        hb_ref = refs[2] if fused else None
        o_ref, acc = refs[-2], refs[-1]
        kk = pl.program_id(2)

        @pl.when(kk == 0)
        def _():
            acc[...] = jnp.zeros_like(acc)

        acc[...] += _dot(a_ref[...], b_ref[...], _NT)

        @pl.when(kk == nk - 1)
        def _():
            if fused:
                o_ref[...] = (acc[...] * (2.0 * jnp.maximum(hb_ref[...].astype(F32), 0.0))).astype(BF16)
            else:
                o_ref[...] = acc[...]

    b_arr, b_spec = _layer_spec(b, (tn, tk), lambda j, i, kk: (j, kk))
    in_specs = [pl.BlockSpec((tm, tk), lambda j, i, kk: (i, kk)), b_spec]
    args = [a, b_arr]
    if fused:
        in_specs.append(pl.BlockSpec((tm, tn), lambda j, i, kk: (i, j)))
        args.append(hb)
    return pl.pallas_call(body, name=name, grid=(k // tn, m // tm, nk), in_specs=in_specs,
                          out_specs=pl.BlockSpec((tm, tn), lambda j, i, kk: (i, j)),
                          out_shape=jax.ShapeDtypeStruct((m, k), BF16 if fused else F32),
                          scratch_shapes=[pltpu.VMEM((tm, tn), F32)], compiler_params=_params(3))(*args)


def _mm_tn(a, b, *, name, layer=None, into=None):
    m, k = a.shape
    n = b.shape[1]
    tk, tn, tr = _tile(k, (1024,)), _tile(n, (1024,)), _tile(m, (512, 256))
    nr = m // tr

    def body(a_ref, b_ref, *rest):
        o_ref, acc = rest[-2], rest[-1]
        r = pl.program_id(2)

        @pl.when(r == 0)
        def _():
            acc[...] = jnp.zeros_like(acc)

        acc[...] += _dot(a_ref[...], b_ref[...], _TN)

        @pl.when(r == nr - 1)
        def _():
            o_ref[...] = acc[...]

    in_specs = [pl.BlockSpec((tr, tk), lambda i, j, r: (r, i)), pl.BlockSpec((tr, tn), lambda i, j, r: (r, j))]
    args, aliases = [a, b], {}
    if layer is None:
        out_spec, out_shape = pl.BlockSpec((tk, tn), lambda i, j, r: (i, j)), jax.ShapeDtypeStruct((k, n), F32)
    else:
        out_spec = pl.BlockSpec((None, tk, tn), lambda i, j, r: (layer, i, j))
        out_shape = jax.ShapeDtypeStruct((DEPTH, k, n), F32)
        if into is not None:
            in_specs.append(_ANY)
            args.append(into)
            aliases = {2: 0}
    return pl.pallas_call(body, name=name, grid=(k // tk, n // tn, nr), in_specs=in_specs, out_specs=out_spec,
                          out_shape=out_shape, input_output_aliases=aliases,
                          scratch_shapes=[pltpu.VMEM((tk, tn), F32)], compiler_params=_params(3))(*args)


def _colsum(v, *, name):
    m, n = v.shape
    tn, tr = _tile(n, (1024,)), _tile(m, (512, 256))

    def body(v_ref, o_ref):
        @pl.when(pl.program_id(1) == 0)
        def _():
            o_ref[...] = jnp.zeros_like(o_ref)

        o_ref[...] += jnp.sum(v_ref[...].astype(F32), axis=0, keepdims=True)

    return pl.pallas_call(body, name=name, grid=(n // tn, m // tr),
                          in_specs=[pl.BlockSpec((tr, tn), lambda j, r: (r, j))],
                          out_specs=pl.BlockSpec((1, tn), lambda j, r: (0, j)),
                          out_shape=jax.ShapeDtypeStruct((1, n), F32), compiler_params=_params(2))(v)


def _tok_spec(d=D_MODEL, col=0):
    return pl.BlockSpec((TM, d), lambda i: (i, col))


def _mod_spec(tps):
    return pl.BlockSpec((1, N_MOD, D_MODEL), lambda i: (_seq_of_tile(i, tps), 0, 0))


def _row_spec(d=D_MODEL):
    return pl.BlockSpec((1, d), lambda i: (0, 0))


def _seq_acc_spec(tps):
    return pl.BlockSpec((1, 1, D_MODEL), lambda i: (_seq_of_tile(i, tps), 0, 0))


def _modulate(xs, modall, k_shift, k_scale, tps, *, name):
    n = xs.shape[0]

    def body(x_ref, m_ref, o_ref):
        sh = m_ref[0, k_shift:k_shift + 1, :]
        sc = m_ref[0, k_scale:k_scale + 1, :]
        o_ref[...] = (x_ref[...] * (1.0 + sc) + sh).astype(BF16)

    return pl.pallas_call(body, name=name, grid=(n // TM,), in_specs=[_tok_spec(), _mod_spec(tps)],
                          out_specs=_tok_spec(), out_shape=jax.ShapeDtypeStruct((n, D_MODEL), BF16),
                          compiler_params=_params(1))(xs, modall)


def _resid_ln(xs, ms, modall, k_gate, g, b, tps, *, name):
    n = xs.shape[0]

    def body(x_ref, m_ref, mod_ref, g_ref, b_ref, o_ref):
        z = ALPHA * x_ref[...] + mod_ref[0, k_gate:k_gate + 1, :] * m_ref[...]
        mu = jnp.mean(z, axis=-1, keepdims=True)
        zc = z - mu
        var = jnp.mean(zc * zc, axis=-1, keepdims=True)
        o_ref[...] = zc * lax.rsqrt(var + LN_EPS) * g_ref[...] + b_ref[...]

    return pl.pallas_call(body, name=name, grid=(n // TM,),
                          in_specs=[_tok_spec(), _tok_spec(), _mod_spec(tps), _row_spec(), _row_spec()],
                          out_specs=_tok_spec(), out_shape=jax.ShapeDtypeStruct((n, D_MODEL), F32),
                          compiler_params=_params(1))(xs, ms, modall, g, b)


def _resid_ln_bwd(xs, ms, modall, k_gate, g, dout, tps, *, name):
    n = xs.shape[0]

    def body(x_ref, m_ref, mod_ref, g_ref, d_ref, dxa_ref, dm_ref, dbias_ref, dg_ref, db_ref, dgate_ref):
        i = pl.program_id(0)
        gate = mod_ref[0, k_gate:k_gate + 1, :]
        m = m_ref[...]
        z = ALPHA * x_ref[...] + gate * m
        mu = jnp.mean(z, axis=-1, keepdims=True)
        zc = z - mu
        var = jnp.mean(zc * zc, axis=-1, keepdims=True)
        rstd = lax.rsqrt(var + LN_EPS)
        xhat = zc * rstd
        d = d_ref[...]
        dxh = d * g_ref[...]
        dz = rstd * (dxh - jnp.mean(dxh, axis=-1, keepdims=True)
                     - xhat * jnp.mean(dxh * xhat, axis=-1, keepdims=True))
        dxa_ref[...] = ALPHA * dz
        dm = gate * dz
        dm_ref[...] = dm.astype(BF16)

        @pl.when(i == 0)
        def _():
            dbias_ref[...] = jnp.zeros_like(dbias_ref)
            dg_ref[...] = jnp.zeros_like(dg_ref)
            db_ref[...] = jnp.zeros_like(db_ref)

        dbias_ref[...] += jnp.sum(dm, axis=0, keepdims=True)
        dg_ref[...] += jnp.sum(d * xhat, axis=0, keepdims=True)
        db_ref[...] += jnp.sum(d, axis=0, keepdims=True)
        part = jnp.sum(dz * m, axis=0, keepdims=True)

        @pl.when(i % tps <= 1)
        def _():
            dgate_ref[0] = part

        @pl.when(i % tps > 1)
        def _():
            dgate_ref[0] += part

    row = jax.ShapeDtypeStruct((1, D_MODEL), F32)
    return pl.pallas_call(
        body, name=name, grid=(n // TM,),
        in_specs=[_tok_spec(), _tok_spec(), _mod_spec(tps), _row_spec(), _tok_spec()],
        out_specs=(_tok_spec(), _tok_spec(), _row_spec(), _row_spec(), _row_spec(), _seq_acc_spec(tps)),
        out_shape=(jax.ShapeDtypeStruct((n, D_MODEL), F32), jax.ShapeDtypeStruct((n, D_MODEL), BF16), row, row, row,
                   jax.ShapeDtypeStruct((n // TM // tps * 2, 1, D_MODEL), F32)),
        compiler_params=_params(1))(xs, ms, modall, g, dout)


def _modulate_bwd(du, xs, modall, k_scale, dxa, tps, *, name):
    n = xs.shape[0]

    def body(du_ref, x_ref, mod_ref, dxa_ref, dx_ref, dsc_ref, dsh_ref):
        i = pl.program_id(0)
        du_t = du_ref[...]
        dx_ref[...] = dxa_ref[...] + du_t * (1.0 + mod_ref[0, k_scale:k_scale + 1, :])
        psc = jnp.sum(du_t * x_ref[...], axis=0, keepdims=True)
        psh = jnp.sum(du_t, axis=0, keepdims=True)

        @pl.when(i % tps <= 1)
        def _():
            dsc_ref[0] = psc
            dsh_ref[0] = psh

        @pl.when(i % tps > 1)
        def _():
            dsc_ref[0] += psc
            dsh_ref[0] += psh

    acc = jax.ShapeDtypeStruct((n // TM // tps * 2, 1, D_MODEL), F32)
    return pl.pallas_call(body, name=name, grid=(n // TM,),
                          in_specs=[_tok_spec(), _tok_spec(), _mod_spec(tps), _tok_spec()],
                          out_specs=(_tok_spec(), _seq_acc_spec(tps), _seq_acc_spec(tps)),
                          out_shape=(jax.ShapeDtypeStruct((n, D_MODEL), F32), acc, acc),
                          compiler_params=_params(1))(du, xs, modall, dxa)


def _loss_head(ys, target, tps, *, name):
    n = ys.shape[0]
    lat_tiles = tps - 1

    def body(y_ref, t_ref, acc_ref, dy_ref):
        i = pl.program_id(0)

        @pl.when(i == 0)
        def _():
            acc_ref[...] = jnp.zeros_like(acc_ref)

        @pl.when(i % tps == 0)
        def _():
            dy_ref[...] = jnp.zeros_like(dy_ref)

        @pl.when(i % tps > 0)
        def _():
            e = y_ref[...] - t_ref[...]
            dy_ref[...] = e * (1.0 / D_MODEL)
            acc_ref[...] += jnp.sum(e * e, axis=0, keepdims=True)

    t_spec = pl.BlockSpec((TM, D_MODEL), lambda i: ((i // tps) * lat_tiles + jnp.maximum(i % tps - 1, 0), 0))
    return pl.pallas_call(body, name=name, grid=(n // TM,), in_specs=[_tok_spec(), t_spec],
                          out_specs=(_row_spec(), _tok_spec()),
                          out_shape=(jax.ShapeDtypeStruct((1, D_MODEL), F32), jax.ShapeDtypeStruct((n, D_MODEL), F32)),
                          compiler_params=_params(1))(ys, target)


def _glu_fwd(y_nat, p, d_skip, w, b, *, name):
    n = y_nat.shape[0]

    def body(y_ref, u_ref, d_ref, w_ref, b_ref, o_ref):
        g = _gelu(y_ref[...] + d_ref[...] * u_ref[...])
        z = _dot(g.astype(BF16), w_ref[...]) + b_ref[...]
        o_ref[...] = (g * _sigmoid(z)).astype(BF16)

    w_arr, w_spec = _layer_spec(w, (D_MODEL, D_MODEL), lambda i: (0, 0))
    return pl.pallas_call(body, name=name, grid=(n // TM,),
                          in_specs=[_tok_spec(), _tok_spec(col=2), _row_spec(), w_spec, _row_spec()],
                          out_specs=_tok_spec(), out_shape=jax.ShapeDtypeStruct((n, D_MODEL), BF16),
                          compiler_params=_params(1))(y_nat, p, d_skip, w_arr, b)


def _glu_bwd(dcat, y_nat, p, d_skip, w, b, *, name):
    n = y_nat.shape[0]

    def body(ds_ref, y_ref, u_ref, d_ref, w_ref, b_ref, dy_ref, dsk_ref, g_ref, dz_ref, dd_ref, dbz_ref):
        u = u_ref[...]
        g, gg = _gelu_and_grad(y_ref[...] + d_ref[...] * u)
        s = _sigmoid(_dot(g.astype(BF16), w_ref[...]) + b_ref[...])
        ds = ds_ref[...]
        dz = ds * g * s * (1.0 - s)
        dzb = dz.astype(BF16)
        dg = ds * s + _dot(dzb, w_ref[...], _NT)
        dyp = dg * gg
        dy_ref[...] = dyp
        dsk_ref[...] = dyp * d_ref[...]
        g_ref[...] = g.astype(BF16)
        dz_ref[...] = dzb

        @pl.when(pl.program_id(0) == 0)
        def _():
            dd_ref[...] = jnp.zeros_like(dd_ref)
            dbz_ref[...] = jnp.zeros_like(dbz_ref)

        dd_ref[...] += jnp.sum(dyp * u, axis=0, keepdims=True)
        dbz_ref[...] += jnp.sum(dz, axis=0, keepdims=True)

    tok_bf = jax.ShapeDtypeStruct((n, D_MODEL), BF16)
    tok_f32 = jax.ShapeDtypeStruct((n, D_MODEL), F32)
    row = jax.ShapeDtypeStruct((1, D_MODEL), F32)
    w_arr, w_spec = _layer_spec(w, (D_MODEL, D_MODEL), lambda i: (0, 0))
    return pl.pallas_call(
        body, name=name, grid=(n // TM,),
        in_specs=[_tok_spec(col=1), _tok_spec(), _tok_spec(col=2), _row_spec(), w_spec, _row_spec()],
        out_specs=(_tok_spec(), _tok_spec(), _tok_spec(), _tok_spec(), _row_spec(), _row_spec()),
        out_shape=(tok_f32, tok_f32, tok_bf, tok_bf, row, row),
        compiler_params=_params(1))(dcat, y_nat, p, d_skip, w_arr, b)


def _add_cast(a, b, *, name):
    n = a.shape[0]

    def body(a_ref, b_ref, o_ref):
        o_ref[...] = (a_ref[...] + b_ref[...]).astype(BF16)

    return pl.pallas_call(body, name=name, grid=(n // TM,), in_specs=[_tok_spec(), _tok_spec()],
                          out_specs=_tok_spec(), out_shape=jax.ShapeDtypeStruct((n, D_MODEL), BF16),
                          compiler_params=_params(1))(a, b)


def _silu_rows(c16, *, name):
    def body(c_ref, s_ref, ds_ref):
        v = c_ref[...]
        sg = _sigmoid(v)
        s_ref[...] = v * sg
        ds_ref[...] = sg * (1.0 + v * (1.0 - sg))

    shp = jax.ShapeDtypeStruct(c16.shape, F32)
    return pl.pallas_call(body, name=name, out_shape=(shp, shp))(c16)


def _mul_rows(a, b, *, name):
    def body(a_ref, b_ref, o_ref):
        o_ref[...] = a_ref[...] * b_ref[...]

    return pl.pallas_call(body, name=name, out_shape=jax.ShapeDtypeStruct(a.shape, F32))(a, b)


def _pad_off(c):
    return pl.multiple_of(c * TM + 8 + 8 * jnp.minimum(c, 1), 8)


def _rows8(k):
    return pl.ds(pl.multiple_of(k * 8, 8), 8)


def _windows(buf, c, shifts):
    n = TM + 16
    win = buf[pl.ds(pl.multiple_of(_pad_off(c) - 8, 8), n), :]
    return [win[8:8 + TM] if k == 0 else pltpu.roll(win, (-k) % n, 0)[8:8 + TM] for k in shifts]


def _conv_window(xpad, c):
    return _windows(xpad, c, (-1, 0, 1, 2))


def _rg_coeffs(z, d, spl, xc):
    r = 0.5 + 0.5 * jnp.tanh(0.5 * z[:, 256 * d:256 * d + HEAD])
    i = 0.5 + 0.5 * jnp.tanh(0.5 * z[:, 256 * d + HEAD:256 * d + 2 * HEAD])
    la = -RG_C * spl[d:d + 1, :] * r
    a = jnp.exp(la)
    one_minus_a = jnp.where(jnp.abs(la) < 1e-2, _neg_expm1_small(la), 1.0 - a)
    mult = jnp.sqrt(one_minus_a * (1.0 + a))
    return r, i, a, mult, a * a


def _chunk_scan(a, b, reverse):
    row = lax.broadcasted_iota(jnp.int32, (TM, HEAD), 0)
    sft = 1
    while sft < TM:
        keep = (row < TM - sft) if reverse else (row >= sft)
        amt = TM - sft if reverse else sft
        a_prev = jnp.where(keep, pltpu.roll(a, amt, 0), 1.0)
        b_prev = jnp.where(keep, pltpu.roll(b, amt, 0), 0.0)
        b = a * b_prev + b
        a = a * a_prev
        sft *= 2
    return a, b


def _zero_pads(buf, s, lc):
    z8 = jnp.zeros((8, HEAD), F32)
    buf[0:8, :] = z8
    buf[8 + lc:16 + lc, :] = z8
    buf[16 + s:24 + s, :] = z8


def _rg_fwd(p, conv_w, conv_b, lam, wcat, bcat, bl, s, lc, *, name):
    nch = s // TM

    def body(x_ref, gate_ref, cw_ref, cb_ref, lam_ref, w_ref, b_ref, rg_ref, hf_ref, hb_ref, xpad, af, bf, ab, bb):
        _zero_pads(xpad, s, lc)

        def copy_chunk(c, _):
            xpad[pl.ds(_pad_off(c), TM), :] = x_ref[pl.ds(pl.multiple_of(c * TM, TM), TM), :]
            return 0

        lax.fori_loop(0, nch, copy_chunk, 0)
        spl = _softplus(-lam_ref[...])
        cw = cw_ref[...]

        def coef_chunk(c, _):
            xm1, x0, xp1, xp2 = _conv_window(xpad, c)
            xc = cw[0:1] * xm1 + cw[1:2] * x0 + cw[2:3] * xp1 + cw[3:4] * xp2 + cb_ref[...]
            z = _dot(xc.astype(BF16), w_ref[0]) + b_ref[0]
            rows = pl.ds(pl.multiple_of(c * TM, TM), TM)
            for d, (a_s, b_s) in enumerate(((af, bf), (ab, bb))):
                _, i, a, mult, _ = _rg_coeffs(z, d, spl, xc)
                a_s[rows, :] = a
                b_s[rows, :] = mult * i * xc
            return 0

        lax.fori_loop(0, nch, coef_chunk, 0)

        def scan_pair(j, carry):
            cf, cb_ = carry
            rf = pl.ds(pl.multiple_of(j * TM, TM), TM)
            rb = pl.ds(pl.multiple_of(jnp.where(j == 0, 0, nch - j) * TM, TM), TM)
            a1, h1 = _chunk_scan(af[rf, :], bf[rf, :], False)
            h1 = h1 + a1 * cf
            hf_ref[rf, :] = h1
            a2, h2 = _chunk_scan(ab[rb, :], bb[rb, :], True)
            h2 = h2 + a2 * cb_
            hb_ref[rb, :] = h2
            return h1[TM - 1:TM], h2[0:1]

        zero = jnp.zeros((1, HEAD), F32)
        lax.fori_loop(0, nch, scan_pair, (zero, zero))

        def out_chunk(c, _):
            rows = pl.ds(pl.multiple_of(c * TM, TM), TM)
            rg_ref[rows, :] = ((hf_ref[rows, :] + hb_ref[rows, :]) * _gelu(gate_ref[rows, :])).astype(BF16)
            return 0

        lax.fori_loop(0, nch, out_chunk, 0)

    seq = lambda col0: pl.BlockSpec((s, HEAD), lambda b, h: (b, col0 + h))
    par = lambda r: pl.BlockSpec((r, HEAD), lambda b, h: (0, h))
    n = bl * s
    return pl.pallas_call(
        body, name=name, grid=(bl, RG_HEADS),
        in_specs=[seq(0), seq(RG_HEADS), par(4), par(1), par(2),
                  pl.BlockSpec((1, HEAD, 4 * HEAD), lambda b, h: (h, 0, 0)),
                  pl.BlockSpec((1, 1, 4 * HEAD), lambda b, h: (h, 0, 0))],
        out_specs=(seq(0), seq(0), seq(0)),
        out_shape=(jax.ShapeDtypeStruct((n, D_MODEL), BF16), jax.ShapeDtypeStruct((n, D_MODEL), F32),
                   jax.ShapeDtypeStruct((n, D_MODEL), F32)),
        scratch_shapes=[pltpu.VMEM((s + 24, HEAD), F32)] + [pltpu.VMEM((s, HEAD), F32)] * 4,
        compiler_params=_params(2, 48))(p, p, conv_w, conv_b, lam, wcat, bcat)


def _rg_bwd(p, dcat, hf, hb, conv_w, conv_b, lam, wcat, bcat, bl, s, lc, *, name):
    nch = s // TM
    ll = s - lc

    def body(p_hbm, dcat_hbm, hf_hbm, hb_hbm, gate_s, cw_ref, cb_ref, lam_ref, w_ref, b_ref,
             drgx_ref, dgate_ref, dcw_ref, dcb_ref, dlam_ref, dw_ref, db_ref,
             xpad, dxpad, hf_s, hb_s, dhs, a_f, a_b, lam_f, lam_b, sems):
        h = pl.program_id(0)
        b = pl.program_id(1)
        row0 = b * s
        col = pl.multiple_of(h * HEAD, HEAD)

        def rows_of(ref, r0, nr, c0):
            return ref.at[pl.ds(row0 + r0, nr), pl.ds(c0, HEAD)]

        copies = [
            pltpu.make_async_copy(rows_of(p_hbm, 0, lc, col), xpad.at[pl.ds(8, lc), :], sems.at[0]),
            pltpu.make_async_copy(rows_of(p_hbm, lc, ll, col), xpad.at[pl.ds(16 + lc, ll), :], sems.at[1]),
            pltpu.make_async_copy(rows_of(dcat_hbm, 0, s, col), dhs, sems.at[2]),
            pltpu.make_async_copy(rows_of(hf_hbm, 0, s, col), hf_s.at[pl.ds(8, s), :], sems.at[3]),
            pltpu.make_async_copy(rows_of(hb_hbm, 0, s, col), hb_s.at[pl.ds(8, s), :], sems.at[4]),
        ]
        for cp in copies:
            cp.start()
        _zero_pads(xpad, s, lc)
        _zero_pads(dxpad, s, lc)
        for buf in (hf_s, hb_s):
            buf[0:8, :] = jnp.zeros((8, HEAD), F32)
            buf[8 + s:16 + s, :] = jnp.zeros((8, HEAD), F32)

        @pl.when(b == 0)
        def _():
            dcw_ref[...] = jnp.zeros_like(dcw_ref)
            dcb_ref[...] = jnp.zeros_like(dcb_ref)
            dlam_ref[...] = jnp.zeros_like(dlam_ref)
            dw_ref[...] = jnp.zeros_like(dw_ref)
            db_ref[...] = jnp.zeros_like(db_ref)

        for cp in copies:
            cp.wait()
        lam_v = lam_ref[...]
        spl = _softplus(-lam_v)
        cw = cw_ref[...]

        def conv(c):
            xm1, x0, xp1, xp2 = _conv_window(xpad, c)
            return cw[0:1] * xm1 + cw[1:2] * x0 + cw[2:3] * xp1 + cw[3:4] * xp2 + cb_ref[...]

        def pass_a(c, _):
            rows = pl.ds(pl.multiple_of(c * TM, TM), TM)
            xc = conv(c)
            z = _dot(xc.astype(BF16), w_ref[0]) + b_ref[0]
            for d, a_s in enumerate((a_f, a_b)):
                a_s[rows, :] = _rg_coeffs(z, d, spl, xc)[2]
            g, gg = _gelu_and_grad(gate_s[rows, :])
            drg = dhs[rows, :]
            hrows = pl.ds(pl.multiple_of(c * TM + 8, 8), TM)
            dgate_ref[rows, :] = (drg * (hf_s[hrows, :] + hb_s[hrows, :]) * gg).astype(BF16)
            dhs[rows, :] = drg * g
            return 0

        lax.fori_loop(0, nch, pass_a, 0)

        row = lax.broadcasted_iota(jnp.int32, (TM, HEAD), 0)

        def adj_pair(j, carry):
            cf, cb_ = carry
            rf = pl.ds(pl.multiple_of((nch - 1 - j) * TM, TM), TM)
            rb = pl.ds(pl.multiple_of(jnp.where(j == nch - 1, 0, j + 1) * TM, TM), TM)
            d1, a1 = dhs[rf, :], a_f[rf, :]
            p1, m1 = _chunk_scan(a1, a1 * d1, True)
            m1 = m1 + p1 * cf
            lam_f[rf, :] = d1 + jnp.where(row == TM - 1, cf, pltpu.roll(m1, TM - 1, 0))
            d2, a2 = dhs[rb, :], a_b[rb, :]
            p2, m2 = _chunk_scan(a2, a2 * d2, False)
            m2 = m2 + p2 * cb_
            lam_b[rb, :] = d2 + jnp.where(row == 0, cb_, pltpu.roll(m2, 1, 0))
            return m1[0:1], m2[TM - 1:TM]

        zero = jnp.zeros((1, HEAD), F32)
        lax.fori_loop(0, nch, adj_pair, (zero, zero))

        sig_neg = _sigmoid(-lam_v)
        last_row = lax.broadcasted_iota(jnp.int32, (TM, HEAD), 0) == TM - 1
        hb_first = hb_s[8:9, :]

        def pass_b(c, _):
            rows = pl.ds(pl.multiple_of(c * TM, TM), TM)
            xc = conv(c)
            xcb = xc.astype(BF16)
            z = _dot(xcb, w_ref[0]) + b_ref[0]
            dxc = jnp.zeros((TM, HEAD), F32)
            dzs = []
            n = TM + 16
            hp_f = pltpu.roll(hf_s[pl.ds(pl.multiple_of(c * TM, TM), n), :], 1, 0)[8:8 + TM]
            hp_b = pltpu.roll(hb_s[pl.ds(pl.multiple_of(c * TM, TM), n), :], n - 1, 0)[8:8 + TM]
            hp_b = jnp.where(last_row & (c == 0), 0.0, hp_b)
            hp_b = jnp.where(last_row & (c == nch - 1), hb_first, hp_b)
            for d, (l_s, hp) in enumerate(((lam_f, hp_f), (lam_b, hp_b))):
                r, i, a, mult, e2 = _rg_coeffs(z, d, spl, xc)
                dbt = l_s[rows, :]
                dla = dbt * hp * a - dbt * i * xc * (e2 / mult)
                dlam_ref[0, d:d + 1, :] += jnp.sum(dla * r, axis=0, keepdims=True) * (RG_C * sig_neg[d:d + 1, :])
                dr = dla * (-RG_C * spl[d:d + 1, :])
                di = dbt * mult * xc
                dxc = dxc + dbt * mult * i
                dzs += [dr * r * (1.0 - r), di * i * (1.0 - i)]
            dz = jnp.concatenate(dzs, axis=1)
            dzb = dz.astype(BF16)
            dxc = dxc + _dot(dzb, w_ref[0], _NT)
            dw_ref[0] += _dot(xcb, dzb, _TN)
            db_ref[0] += jnp.sum(dz, axis=0, keepdims=True)
            dcb_ref[0] += jnp.sum(dxc, axis=0, keepdims=True)
            dxpad[pl.ds(_pad_off(c), TM), :] = dxc
            return 0

        lax.fori_loop(0, nch, pass_b, 0)

        def pass_c(c, _):
            rows = pl.ds(pl.multiple_of(c * TM, TM), TM)
            gp1, g0, gm1, gm2 = _windows(dxpad, c, (1, 0, -1, -2))
            drgx_ref[rows, :] = (cw[0:1] * gp1 + cw[1:2] * g0 + cw[2:3] * gm1 + cw[3:4] * gm2).astype(BF16)
            xm1, x0, xp1, xp2 = _conv_window(xpad, c)
            dcw_ref[0] += jnp.concatenate([jnp.sum(g0 * t, axis=0, keepdims=True) for t in (xm1, x0, xp1, xp2)],
                                          axis=0)
            return 0

        lax.fori_loop(0, nch, pass_c, 0)

    seq = pl.BlockSpec((s, HEAD), lambda h, b: (b, h))
    par = lambda r: pl.BlockSpec((r, HEAD), lambda h, b: (0, h))
    acc = lambda r, w: pl.BlockSpec((1, r, w), lambda h, b: (h, 0, 0))
    anyspec = pl.BlockSpec(memory_space=pl.ANY)
    n = bl * s
    big = pltpu.VMEM((s, HEAD), F32)
    return pl.pallas_call(
        body, name=name, grid=(RG_HEADS, bl),
        in_specs=[anyspec, anyspec, anyspec, anyspec, pl.BlockSpec((s, HEAD), lambda h, b: (b, RG_HEADS + h)),
                  par(4), par(1), par(2), acc(HEAD, 4 * HEAD), acc(1, 4 * HEAD)],
        out_specs=(seq, seq, acc(4, HEAD), acc(1, HEAD), acc(2, HEAD), acc(HEAD, 4 * HEAD), acc(1, 4 * HEAD)),
        out_shape=(jax.ShapeDtypeStruct((n, D_MODEL), BF16), jax.ShapeDtypeStruct((n, D_MODEL), BF16),
                   jax.ShapeDtypeStruct((RG_HEADS, 4, HEAD), F32), jax.ShapeDtypeStruct((RG_HEADS, 1, HEAD), F32),
                   jax.ShapeDtypeStruct((RG_HEADS, 2, HEAD), F32),
                   jax.ShapeDtypeStruct((RG_HEADS, HEAD, 4 * HEAD), F32),
                   jax.ShapeDtypeStruct((RG_HEADS, 1, 4 * HEAD), F32)),
        scratch_shapes=[pltpu.VMEM((s + 24, HEAD), F32)] * 2 + [pltpu.VMEM((s + 16, HEAD), F32)] * 2 + [big] * 5
        + [pltpu.SemaphoreType.DMA((5,))],
        compiler_params=_params(2, 52))(p, dcat, hf, hb, p, conv_w, conv_b, lam, wcat, bcat)


def _s5_mats(a_re, a_im, log_dt, b_re, b_im, c_re, c_im):
    t = T_CH
    g = a_re.shape[1]
    dt = jnp.exp(log_dt)[..., None]
    lr, li = a_re * dt, a_im * dt
    steps = jnp.arange(t + 1, dtype=F32)[:, None]
    mag = jnp.exp(lr[:, :, None, :] * steps)
    ang = li[:, :, None, :] * steps
    pr, pi = mag * jnp.cos(ang), mag * jnp.sin(ang)
    xr, xi = pr[:, :, 1] - 1.0, pi[:, :, 1]
    den = a_re * a_re + a_im * a_im
    qr, qi = (xr * a_re + xi * a_im) / den, (xi * a_re - xr * a_im) / den
    btr, bti = b_re.transpose(0, 1, 3, 2), b_im.transpose(0, 1, 3, 2)
    bbr = qr[:, :, None, :] * btr - qi[:, :, None, :] * bti
    bbi = qr[:, :, None, :] * bti + qi[:, :, None, :] * btr
    up, down = slice(0, t), slice(t - 1, None, -1)

    def pow_c(d, sl):
        wr, wi = pr[d][:, sl, None, :], pi[d][:, sl, None, :]
        cr, ci = c_re[d][:, None], c_im[d][:, None]
        return (wr * cr - wi * ci).reshape(g, CW, S5_STATE), (wr * ci + wi * cr).reshape(g, CW, S5_STATE)

    hp = lax.Precision.HIGHEST

    def lag_map(d, sl):
        re, im = pow_c(d, sl)
        return (jnp.einsum('gkp,gmp->gkm', bbr[d], re, precision=hp)
                - jnp.einsum('gkp,gmp->gkm', bbi[d], im, precision=hp))

    z_f, z_b = lag_map(0, up), lag_map(1, down)
    kf = jnp.stack([jnp.pad(z_f, ((0, 0), (0, 0), (S5_GROUP * s, 0)))[:, :, :CW] for s in range(t)], axis=1)
    kb = jnp.stack([jnp.pad(z_b, ((0, 0), (0, 0), (0, S5_GROUP * (t - 1 - s))))[:, :, S5_GROUP * (t - 1 - s):]
                    for s in range(t)], axis=1)
    kcat = (kf + kb).reshape(g, CW, CW)

    def state_in(d, sl):
        wr, wi = pr[d][:, sl, None, :], pi[d][:, sl, None, :]
        br, bi = bbr[d][:, None], bbi[d][:, None]
        return jnp.concatenate([wr * br - wi * bi, wr * bi + wi * br], axis=-1).reshape(g, CW, SW)

    wcat = jnp.concatenate([kcat, state_in(0, down), state_in(1, up)], axis=2)
    of_r, of_i = pow_c(0, slice(1, t + 1))
    ob_r, ob_i = pow_c(1, slice(t, 0, -1))
    mout_t = jnp.concatenate([of_r, -of_i, ob_r, -ob_i], axis=2)
    rows = []
    for d in range(2):
        art, ait = pr[d][:, t], pi[d][:, t]
        rows += [jnp.concatenate([art, art], axis=1).reshape(-1), jnp.concatenate([-ait, ait], axis=1).reshape(-1)]
    return wcat, mout_t, jnp.stack(rows)


def _lane_swap(v):
    return pltpu.roll(v, S5_STATE, 1)


def _grp(g, w):
    return slice(g * w, (g + 1) * w)


def _s5_fwd(u, wcat, mout, a2, ncc, *, name):
    bl, nc, _ = u.shape

    def body(u_ref, w_ref, mo_ref, a_ref, y_ref, sf_ref, sb_ref, vf, vb):
        for g in range(GB):
            zu = _dot(u_ref[:, _grp(g, CW)], w_ref[g])
            y_ref[:, _grp(g, CW)] = zu[:, :CW]
            vf[:, _grp(g, SW)] = zu[:, CW:CW + SW]
            vb[:, _grp(g, SW)] = zu[:, CW + SW:]
        co = [[a_ref[r:r + 1, _grp(g, SW)] for g in range(GB)] for r in range(4)]

        rid = lax.broadcasted_iota(jnp.int32, (8, SW), 0)

        def step8(groups, kf, kb, carry):
            rf, rb = _rows8(kf), _rows8(kb)
            lanes = slice(groups[0] * SW, (groups[-1] + 1) * SW)
            vfb, vbb = vf[rf, lanes], vb[rb, lanes]
            st = list(carry)
            of = [jnp.zeros((8, SW), F32)] * len(groups)
            ob = list(of)
            for i in range(8):
                k = 7 - i
                for n, g in enumerate(groups):
                    sf, sb = st[2 * n], st[2 * n + 1]
                    of[n] = jnp.where(rid == i, sf, of[n])
                    ob[n] = jnp.where(rid == k, sb, ob[n])
                    st[2 * n] = co[0][g] * sf + co[1][g] * _lane_swap(sf) + vfb[i:i + 1, _grp(n, SW)]
                    st[2 * n + 1] = co[2][g] * sb + co[3][g] * _lane_swap(sb) + vbb[k:k + 1, _grp(n, SW)]
            for n, g in enumerate(groups):
                sf_ref[rf, _grp(g, SW)] = of[n]
                sb_ref[rb, _grp(g, SW)] = ob[n]
            return tuple(st)

        zero = jnp.zeros((1, SW), F32)
        nbc, nb = ncc // 8, nc // 8
        for groups in (tuple(range(0, GB // 2)), tuple(range(GB // 2, GB))):
            carry = lax.fori_loop(0, nbc, lambda j, cr, gs=groups: step8(gs, j, nbc - 1 - j, cr),
                                  (zero,) * (2 * len(groups)))
            lax.fori_loop(nbc, nb, lambda j, cr, gs=groups: step8(gs, j, nb + nbc - 1 - j, cr), carry)
        for g in range(GB):
            st = jnp.concatenate([sf_ref[:, _grp(g, SW)], sb_ref[:, _grp(g, SW)]], axis=1).astype(BF16)
            y_ref[:, _grp(g, CW)] += _dot(st, mo_ref[g], _NT)

    blk = lambda w: pl.BlockSpec((None, nc, GB * w), lambda b, gb: (b, 0, gb))
    return pl.pallas_call(
        body, name=name, grid=(bl, S5_GROUPS // GB),
        in_specs=[blk(CW), pl.BlockSpec((GB, CW, 2 * CW), lambda b, gb: (gb, 0, 0)),
                  pl.BlockSpec((GB, CW, CW), lambda b, gb: (gb, 0, 0)),
                  pl.BlockSpec((4, GB * SW), lambda b, gb: (0, gb))],
        out_specs=(blk(CW), blk(SW), blk(SW)),
        out_shape=(jax.ShapeDtypeStruct(u.shape, F32), jax.ShapeDtypeStruct((bl, nc, S5_GROUPS * SW), F32),
                   jax.ShapeDtypeStruct((bl, nc, S5_GROUPS * SW), F32)),
        scratch_shapes=[pltpu.VMEM((nc, GB * SW), F32)] * 2, compiler_params=_params(2))(u, wcat, mout, a2)


def _s5_bwd(dy, u, sf, sb, wcat, mout, a2, ncc, *, name):
    bl, nc, _ = u.shape

    def body(dy_ref, u_ref, sf_ref, sb_ref, w_ref, mo_ref, a_ref, du_ref, dw_ref, dmo_ref, dacc_ref, gsf, gsb, dvf, dvb):
        b = pl.program_id(1)

        @pl.when(b == 0)
        def _():
            dw_ref[...] = jnp.zeros_like(dw_ref)
            dmo_ref[...] = jnp.zeros_like(dmo_ref)
            dacc_ref[...] = jnp.zeros_like(dacc_ref)

        for g in range(GB):
            ds = _dot(dy_ref[:, _grp(g, CW)], mo_ref[g])
            gsf[:, _grp(g, SW)] = ds[:, :SW]
            gsb[:, _grp(g, SW)] = ds[:, SW:]
        co = [[a_ref[r:r + 1, _grp(g, SW)] for g in range(GB)] for r in range(4)]

        rid = lax.broadcasted_iota(jnp.int32, (8, SW), 0)

        def step8(groups, kf, kb, carry):
            rf, rb = _rows8(kf), _rows8(kb)
            lanes = slice(groups[0] * SW, (groups[-1] + 1) * SW)
            gfb, gbb = gsf[rf, lanes], gsb[rb, lanes]
            st = list(carry)
            of = [jnp.zeros((8, SW), F32)] * len(groups)
            ob = list(of)
            for i in range(8):
                k = 7 - i
                for n, g in enumerate(groups):
                    gf, gb_ = st[2 * n], st[2 * n + 1]
                    of[n] = jnp.where(rid == k, gf, of[n])
                    ob[n] = jnp.where(rid == i, gb_, ob[n])
                    st[2 * n] = gfb[k:k + 1, _grp(n, SW)] + co[0][g] * gf - co[1][g] * _lane_swap(gf)
                    st[2 * n + 1] = gbb[i:i + 1, _grp(n, SW)] + co[2][g] * gb_ - co[3][g] * _lane_swap(gb_)
            for n, g in enumerate(groups):
                dvf[rf, _grp(g, SW)] = of[n]
                dvb[rb, _grp(g, SW)] = ob[n]
            return tuple(st)

        zero = jnp.zeros((1, SW), F32)
        nbc, nb = ncc // 8, nc // 8
        for groups in (tuple(range(0, GB // 2)), tuple(range(GB // 2, GB))):
            carry = lax.fori_loop(0, nb - nbc, lambda j, cr, gs=groups: step8(gs, nb - 1 - j, nbc + j, cr),
                                  (zero,) * (2 * len(groups)))
            lax.fori_loop(0, nbc, lambda j, cr, gs=groups: step8(gs, nbc - 1 - j, j, cr), carry)
        for g in range(GB):
            dyg = dy_ref[:, _grp(g, CW)]
            dvf_g, dvb_g = dvf[:, _grp(g, SW)], dvb[:, _grp(g, SW)]
            sf_g, sb_g = sf_ref[:, _grp(g, SW)], sb_ref[:, _grp(g, SW)]
            dz = jnp.concatenate([dyg, dvf_g.astype(BF16), dvb_g.astype(BF16)], axis=1)
            du_ref[:, _grp(g, CW)] = _dot(dz, w_ref[g], _NT)
            dw_ref[g] += _dot(u_ref[:, _grp(g, CW)], dz, _TN)
            st = jnp.concatenate([sf_g, sb_g], axis=1).astype(BF16)
            dmo_ref[g] += _dot(dyg, st, _TN)
            dacc_ref[:, _grp(g, SW)] += jnp.concatenate(
                [jnp.sum(dvf_g * sf_g, axis=0, keepdims=True), jnp.sum(dvf_g * _lane_swap(sf_g), axis=0, keepdims=True),
                 jnp.sum(dvb_g * sb_g, axis=0, keepdims=True), jnp.sum(dvb_g * _lane_swap(sb_g), axis=0, keepdims=True)],
                axis=0)

    blk = lambda w: pl.BlockSpec((None, nc, GB * w), lambda gb, b: (b, 0, gb))
    wspec = pl.BlockSpec((GB, CW, 2 * CW), lambda gb, b: (gb, 0, 0))
    mspec = pl.BlockSpec((GB, CW, CW), lambda gb, b: (gb, 0, 0))
    aspec = pl.BlockSpec((4, GB * SW), lambda gb, b: (0, gb))
    return pl.pallas_call(
        body, name=name, grid=(S5_GROUPS // GB, bl),
        in_specs=[blk(CW), blk(CW), blk(SW), blk(SW), wspec, mspec, aspec],
        out_specs=(blk(CW), wspec, mspec, aspec),
        out_shape=(jax.ShapeDtypeStruct(u.shape, F32), jax.ShapeDtypeStruct(wcat.shape, F32),
                   jax.ShapeDtypeStruct(mout.shape, F32), jax.ShapeDtypeStruct(a2.shape, F32)),
        scratch_shapes=[pltpu.VMEM((nc, GB * SW), F32)] * 4,
        compiler_params=_params(2, 48))(dy, u, sf, sb, wcat, mout, a2)


def _lane_slot():
    return lax.broadcasted_iota(jnp.int32, (GRID_W, HEAD), 1) // S5_GROUP


def _lat_to_chunks(src, col0, bl, s, lc, *, name):
    nrh = (s - lc) // GRID_W // T_CH

    def body(x_ref, o_ref):
        slot = _lane_slot()

        def one(rh, _):
            tiles = [x_ref[pl.ds(pl.multiple_of(lc + (rh * T_CH + t) * GRID_W, GRID_W), GRID_W), :]
                     for t in range(T_CH)]
            for q in range(HEAD // S5_GROUP):
                for j in range(CW // HEAD):
                    acc = jnp.zeros((GRID_W, HEAD), F32)
                    for m in range(HEAD // S5_GROUP):
                        shift = ((m - q) * S5_GROUP) % HEAD
                        v = tiles[8 * j + m]
                        acc = jnp.where(slot == m, v if shift == 0 else pltpu.roll(v, shift, 1), acc)
                    o_ref[rh, :, q * CW + j * HEAD:q * CW + (j + 1) * HEAD] = acc.astype(BF16)
            return 0

        lax.fori_loop(0, nrh, one, 0)

    return pl.pallas_call(
        body, name=name, grid=(bl, S5_GROUPS // GB),
        in_specs=[pl.BlockSpec((s, HEAD), lambda b, gb: (b, col0 + gb))],
        out_specs=pl.BlockSpec((None, nrh, GRID_W, GB * CW), lambda b, gb: (b, 0, 0, gb)),
        out_shape=jax.ShapeDtypeStruct((bl, nrh, GRID_W, S5_GROUPS * CW), BF16),
        compiler_params=_params(2))(src)


def _lat_from_chunks(v4, ctx_nat, bl, s, lc, *, name):
    nrh = v4.shape[1]

    def body(v_ref, c_ref, o_ref):
        o_ref[0:lc, :] = c_ref[...]
        slot = _lane_slot()

        def one(rh, _):
            for t in range(T_CH):
                j, m = t // 8, t % 8
                acc = jnp.zeros((GRID_W, HEAD), F32)
                for q in range(HEAD // S5_GROUP):
                    shift = ((q - m) * S5_GROUP) % HEAD
                    v = v_ref[rh, :, q * CW + j * HEAD:q * CW + (j + 1) * HEAD]
                    acc = jnp.where(slot == q, v if shift == 0 else pltpu.roll(v, shift, 1), acc)
                o_ref[pl.ds(pl.multiple_of(lc + (rh * T_CH + t) * GRID_W, GRID_W), GRID_W), :] = acc
            return 0

        lax.fori_loop(0, nrh, one, 0)

    return pl.pallas_call(
        body, name=name, grid=(bl, S5_GROUPS // GB),
        in_specs=[pl.BlockSpec((None, nrh, GRID_W, GB * CW), lambda b, gb: (b, 0, 0, gb)),
                  pl.BlockSpec((lc, HEAD), lambda b, gb: (b, gb))],
        out_specs=pl.BlockSpec((s, HEAD), lambda b, gb: (b, gb)),
        out_shape=jax.ShapeDtypeStruct((bl * s, D_MODEL), F32), compiler_params=_params(2))(v4, ctx_nat)


def _to_chunks(src, col0, bl, s, lc, *, name):
    ll = s - lc
    lat = _lat_to_chunks(src, col0, bl, s, lc, name=name)
    lat = lat.transpose(0, 2, 1, 3).reshape(bl, ll // T_CH, S5_GROUPS * CW)
    ctx = src.reshape(bl, s, -1)[:, :lc, col0 * HEAD:col0 * HEAD + D_MODEL].astype(BF16)
    ctx = ctx.reshape(bl, lc // T_CH, T_CH, S5_GROUPS, S5_GROUP).transpose(0, 1, 3, 2, 4)
    return jnp.concatenate([ctx.reshape(bl, lc // T_CH, -1), lat], axis=1)


def _from_chunks(v, bl, s, lc, *, name):
    ncc = lc // T_CH
    nrh = (s - lc) // GRID_W // T_CH
    ctx = v[:, :ncc].reshape(bl, ncc, S5_GROUPS, T_CH, S5_GROUP).transpose(0, 1, 3, 2, 4).reshape(bl * lc, D_MODEL)
    lat = v[:, ncc:].reshape(bl, GRID_W, nrh, S5_GROUPS * CW).transpose(0, 2, 1, 3)
    return _lat_from_chunks(lat, ctx, bl, s, lc, name=name)


_ANY = pl.BlockSpec(memory_space=pl.ANY)


def _xy_peers():
    x, y, c = lax.axis_index("x"), lax.axis_index("y"), lax.axis_index("c")
    return x, y, c, [(1 - x, y), (x, 1 - y), (1 - x, 1 - y)]


def _all_gather_xy(shard, *, name):
    def body(x_ref, out_ref, send_sems, recv_sems, local_sem):
        x, y, c, peers = _xy_peers()
        me = 2 * x + y
        mine = pltpu.make_async_copy(x_ref, out_ref.at[me], local_sem)
        mine.start()

        def copy(k, px, py, slot):
            return pltpu.make_async_remote_copy(src_ref=x_ref, dst_ref=out_ref.at[slot], send_sem=send_sems.at[k],
                                                recv_sem=recv_sems.at[k], device_id=(px, py, c), device_id_type=MESH)

        sends = [copy(k, px, py, me) for k, (px, py) in enumerate(peers)]
        for cp in sends:
            cp.start()
        for k, (px, py) in enumerate(peers):
            copy(k, px, py, 2 * px + py).wait_recv()
        for cp in sends:
            cp.wait_send()
        mine.wait()

    return pl.pallas_call(body, name=name, in_specs=[_ANY], out_specs=_ANY,
                          out_shape=jax.ShapeDtypeStruct((4,) + shard.shape, shard.dtype),
                          scratch_shapes=[pltpu.SemaphoreType.DMA((3,)), pltpu.SemaphoreType.DMA((3,)),
                                          pltpu.SemaphoreType.DMA])(shard)


def _scatter_xy(parts, *, name):
    def body(p_ref, out_ref, send_sems, recv_sems, local_sem):
        x, y, c, peers = _xy_peers()
        mine = pltpu.make_async_copy(p_ref.at[2 * x + y], out_ref.at[0], local_sem)
        mine.start()

        def copy(k, px, py):
            return pltpu.make_async_remote_copy(src_ref=p_ref.at[2 * px + py], dst_ref=out_ref.at[1 + k],
                                                send_sem=send_sems.at[k], recv_sem=recv_sems.at[k],
                                                device_id=(px, py, c), device_id_type=MESH)

        sends = [copy(k, px, py) for k, (px, py) in enumerate(peers)]
        for cp in sends:
            cp.start()
        for cp in sends:
            cp.wait_recv()
        for cp in sends:
            cp.wait_send()
        mine.wait()

    return pl.pallas_call(body, name=name, in_specs=[_ANY], out_specs=_ANY,
                          out_shape=jax.ShapeDtypeStruct(parts.shape, parts.dtype),
                          scratch_shapes=[pltpu.SemaphoreType.DMA((3,)), pltpu.SemaphoreType.DMA((3,)),
                                          pltpu.SemaphoreType.DMA])(parts)


def _swap_sibling(v, *, name):
    def body(v_ref, out_ref, send_sem, recv_sem):
        x, y, c = lax.axis_index("x"), lax.axis_index("y"), lax.axis_index("c")
        cp = pltpu.make_async_remote_copy(src_ref=v_ref, dst_ref=out_ref, send_sem=send_sem, recv_sem=recv_sem,
                                          device_id=(x, y, 1 - c), device_id_type=MESH)
        cp.start()
        cp.wait()

    return pl.pallas_call(body, name=name, in_specs=[_ANY], out_specs=_ANY,
                          out_shape=jax.ShapeDtypeStruct(v.shape, v.dtype),
                          scratch_shapes=[pltpu.SemaphoreType.DMA, pltpu.SemaphoreType.DMA])(v)


BIG_COLS = {'ada_w': True, 'w_in': True, 'mlp_w1': True, 's5_glu_w': False, 'w_out': False, 'mlp_w2': False}
BIG = list(BIG_COLS)


def _block(ref2d, j, cols, size):
    if cols:
        return ref2d.at[:, pl.ds(pl.multiple_of(j * size, 128), size)]
    return ref2d.at[pl.ds(pl.multiple_of(j * size, 8), size), :]


def _shard_size(shape, cols):
    return shape[-1] if cols else shape[-2]


def _cast_into_full(shard, cols, my_j, *, name):
    _, r, c = shard.shape
    tr, tc = _tile(r, (256,)), _tile(c, (1024, 768, 512))

    def body(j_ref, x_ref, o_ref):
        o_ref[...] = x_ref[...].astype(BF16)

    if cols:
        out_spec = pl.BlockSpec((None, tr, tc), lambda l, i, j, j_ref: (l, i, j_ref[0] * (c // tc) + j))
    else:
        out_spec = pl.BlockSpec((None, tr, tc), lambda l, i, j, j_ref: (l, j_ref[0] * (r // tr) + i, j))
    return pl.pallas_call(
        body, name=name,
        grid_spec=pltpu.PrefetchScalarGridSpec(
            num_scalar_prefetch=1, grid=(DEPTH, r // tr, c // tc),
            in_specs=[pl.BlockSpec((None, tr, tc), lambda l, i, j, j_ref: (l, i, j))], out_specs=out_spec),
        out_shape=jax.ShapeDtypeStruct((DEPTH, r, 4 * c) if cols else (DEPTH, 4 * r, c), BF16),
        compiler_params=_params(3))(my_j, shard)


def _gather_big(fulls, cols, *, name):
    n = len(fulls)

    def body(*refs):
        outs = refs[n:2 * n]
        ici_send, ici_recv, d2d_send, d2d_recv = refs[2 * n:]
        x, y, c, peers = _xy_peers()
        me = 2 * x + y

        def blk(w, layer, j):
            shape = outs[w].shape
            return _block(outs[w].at[layer], j, cols[w], (shape[2] if cols[w] else shape[1]) // 4)

        def ici(w, k, px, py, j):
            return pltpu.make_async_remote_copy(src_ref=blk(w, c, j), dst_ref=blk(w, c, j),
                                                send_sem=ici_send.at[3 * w + k], recv_sem=ici_recv.at[3 * w + k],
                                                device_id=(px, py, c), device_id_type=MESH)

        def d2d(w, k, j, layer):
            return pltpu.make_async_remote_copy(src_ref=blk(w, layer, j), dst_ref=blk(w, layer, j),
                                                send_sem=d2d_send.at[3 * w + k], recv_sem=d2d_recv.at[3 * w + k],
                                                device_id=(x, y, 1 - c), device_id_type=MESH)

        started = [ici(w, k, px, py, me) for w in range(n) for k, (px, py) in enumerate(peers)]
        for cp in started:
            cp.start()
        passed = []
        for w in range(n):
            for k, (px, py) in enumerate(peers):
                ici(w, k, px, py, 2 * px + py).wait_recv()
                passed.append(d2d(w, k, 2 * px + py, c))
                passed[-1].start()
        for w in range(n):
            for k, (px, py) in enumerate(peers):
                d2d(w, k, 2 * px + py, 1 - c).wait_recv()
        for cp in started + passed:
            cp.wait_send()

    return pl.pallas_call(
        body, name=name, in_specs=[_ANY] * n, out_specs=[_ANY] * n,
        out_shape=[jax.ShapeDtypeStruct(f.shape, f.dtype) for f in fulls],
        input_output_aliases={w: w for w in range(n)},
        scratch_shapes=[pltpu.SemaphoreType.DMA((3 * n,))] * 4)(*fulls)


def _sibling_partials(gbufs, *, name):
    n = len(gbufs)

    def body(*refs):
        ins, outs, send, recv = refs[:n], refs[n:2 * n], refs[2 * n], refs[2 * n + 1]
        x, y, c = lax.axis_index("x"), lax.axis_index("y"), lax.axis_index("c")
        cps = [pltpu.make_async_remote_copy(src_ref=ins[w].at[1 - c], dst_ref=outs[w], send_sem=send.at[w],
                                            recv_sem=recv.at[w], device_id=(x, y, 1 - c), device_id_type=MESH)
               for w in range(n)]
        for cp in cps:
            cp.start()
        for cp in cps:
            cp.wait()

    return pl.pallas_call(body, name=name, in_specs=[_ANY] * n, out_specs=[_ANY] * n,
                          out_shape=[jax.ShapeDtypeStruct(g.shape[1:], g.dtype) for g in gbufs],
                          scratch_shapes=[pltpu.SemaphoreType.DMA((n,))] * 2)(*gbufs)


def _chip_sum(gbuf, other, my_c, *, name):
    _, k, n = gbuf.shape
    tr, tc = _tile(k, (512,)), _tile(n, (1024,))

    def body(c_ref, a_ref, b_ref, o_ref):
        o_ref[...] = (a_ref[...] + b_ref[...]).astype(BF16)

    spec = pl.BlockSpec((tr, tc), lambda i, j, c_ref: (i, j))
    return pl.pallas_call(
        body, name=name,
        grid_spec=pltpu.PrefetchScalarGridSpec(
            num_scalar_prefetch=1, grid=(k // tr, n // tc),
            in_specs=[pl.BlockSpec((None, tr, tc), lambda i, j, c_ref: (c_ref[0], i, j)), spec], out_specs=spec),
        out_shape=jax.ShapeDtypeStruct((k, n), BF16), compiler_params=_params(2))(my_c, gbuf, other)


def _scatter_big(sums, cols, *, name):
    n = len(sums)

    def shard(s, cf):
        return (s.shape[0], s.shape[1] // 4) if cf else (s.shape[0] // 4, s.shape[1])

    def body(*refs):
        ins, outs, send, recv = refs[:n], refs[n:2 * n], refs[2 * n], refs[2 * n + 1]
        x, y, c, peers = _xy_peers()
        cps = []
        for w in range(n):
            size = _shard_size(shard(ins[w], cols[w]), cols[w])
            for k, (px, py) in enumerate(peers):
                cps.append(pltpu.make_async_remote_copy(
                    src_ref=_block(ins[w], 2 * px + py, cols[w], size), dst_ref=outs[w].at[k],
                    send_sem=send.at[3 * w + k], recv_sem=recv.at[3 * w + k], device_id=(px, py, c),
                    device_id_type=MESH))
        for cp in cps:
            cp.start()
        for cp in cps:
            cp.wait()

    return pl.pallas_call(body, name=name, in_specs=[_ANY] * n, out_specs=[_ANY] * n,
                          out_shape=[jax.ShapeDtypeStruct((3,) + shard(s, cf), s.dtype) for s, cf in zip(sums, cols)],
                          scratch_shapes=[pltpu.SemaphoreType.DMA((3 * n,))] * 2)(*sums)


def _block_sum(own, got, cols, my_j, my_c, *, name):
    _, r, c = got.shape
    tr, tc = _tile(r, (256,)), _tile(c, (1024, 768, 512))

    def body(j_ref, c_ref, a_ref, g_ref, o_ref):
        o_ref[...] = ((a_ref[...].astype(F32) + g_ref[0].astype(F32)) + g_ref[1].astype(F32)) + g_ref[2].astype(F32)

    if cols:
        own_spec = pl.BlockSpec((tr, tc), lambda i, j, j_ref, c_ref: (i, j_ref[0] * (c // tc) + j))
    else:
        own_spec = pl.BlockSpec((tr, tc), lambda i, j, j_ref, c_ref: (j_ref[0] * (r // tr) + i, j))
    return pl.pallas_call(
        body, name=name,
        grid_spec=pltpu.PrefetchScalarGridSpec(
            num_scalar_prefetch=2, grid=(r // tr, c // tc),
            in_specs=[own_spec, pl.BlockSpec((3, tr, tc), lambda i, j, j_ref, c_ref: (0, i, j))],
            out_specs=pl.BlockSpec((None, tr, tc), lambda i, j, j_ref, c_ref: (c_ref[0], i, j))),
        out_shape=jax.ShapeDtypeStruct((DEPTH, r, c), F32), compiler_params=_params(2))(my_j, my_c, own, got)


def _share_final(bufs, *, name):
    n = len(bufs)

    def body(*refs):
        outs, send, recv = refs[n:2 * n], refs[2 * n], refs[2 * n + 1]
        x, y, c = lax.axis_index("x"), lax.axis_index("y"), lax.axis_index("c")

        def copy(w, slot):
            return pltpu.make_async_remote_copy(src_ref=outs[w].at[slot], dst_ref=outs[w].at[slot],
                                                send_sem=send.at[w], recv_sem=recv.at[w],
                                                device_id=(x, y, 1 - c), device_id_type=MESH)

        away = [copy(w, c) for w in range(n)]
        for cp in away:
            cp.start()
        for w in range(n):
            copy(w, 1 - c).wait_recv()
        for cp in away:
            cp.wait_send()

    return pl.pallas_call(body, name=name, in_specs=[_ANY] * n, out_specs=[_ANY] * n,
                          out_shape=[jax.ShapeDtypeStruct(b.shape, b.dtype) for b in bufs],
                          input_output_aliases={w: w for w in range(n)},
                          scratch_shapes=[pltpu.SemaphoreType.DMA((n,))] * 2)(*bufs)


def _adamw_native(w, g, m, v, *, name):
    r, c = w.shape
    tr = _tile(r, (256, 128, 64, 32, 16, 8))
    spec = pl.BlockSpec((tr, c), lambda i: (i, 0))
    c1 = 1.0 / (1.0 - ADAM_B1 ** ADAM_STEP)
    c2 = 1.0 / (1.0 - ADAM_B2 ** ADAM_STEP)

    def body(w_ref, g_ref, m_ref, v_ref, d_ref, nm_ref, nv_ref):
        g_t = g_ref[...]
        nm = ADAM_B1 * m_ref[...] + (1.0 - ADAM_B1) * g_t
        nv = ADAM_B2 * v_ref[...] + (1.0 - ADAM_B2) * (g_t * g_t)
        nm_ref[...] = nm
        nv_ref[...] = nv
        d_ref[...] = -ADAM_LR * ((nm * c1) / (jnp.sqrt(nv * c2) + ADAM_EPS) + ADAM_WD * w_ref[...])

    shp = jax.ShapeDtypeStruct((r, c), F32)
    return pl.pallas_call(body, name=name, grid=(r // tr,), in_specs=[spec] * 4, out_specs=(spec,) * 3,
                          out_shape=(shp,) * 3, compiler_params=_params(1))(w, g, m, v)


def _flat_tile(r):
    return _tile(r, (512, 256, 128, 64, 32, 16, 8))


def _sum4(parts, *, name):
    r = parts.shape[1]
    tr = _flat_tile(r)

    def body(p_ref, o_ref):
        o_ref[...] = ((p_ref[0] + p_ref[1]) + p_ref[2]) + p_ref[3]

    return pl.pallas_call(body, name=name, grid=(r // tr,),
                          in_specs=[pl.BlockSpec((4, tr, LANES), lambda i: (0, i, 0))],
                          out_specs=pl.BlockSpec((tr, LANES), lambda i: (i, 0)),
                          out_shape=jax.ShapeDtypeStruct((r, LANES), F32), compiler_params=_params(1))(parts)


def _add2(a, b, *, name):
    r = a.shape[0]
    tr = _flat_tile(r)
    spec = pl.BlockSpec((tr, LANES), lambda i: (i, 0))

    def body(a_ref, b_ref, o_ref):
        o_ref[...] = a_ref[...] + b_ref[...]

    return pl.pallas_call(body, name=name, grid=(r // tr,), in_specs=[spec, spec], out_specs=spec,
                          out_shape=jax.ShapeDtypeStruct((r, LANES), F32), compiler_params=_params(1))(a, b)


def _adamw(w, ga, gb, m, v, *, name):
    r = w.shape[0]
    tr = _flat_tile(r)
    spec = pl.BlockSpec((tr, LANES), lambda i: (i, 0))
    two = gb is not None
    c1 = 1.0 / (1.0 - ADAM_B1 ** ADAM_STEP)
    c2 = 1.0 / (1.0 - ADAM_B2 ** ADAM_STEP)

    def body(*refs):
        w_ref, ga_ref = refs[0], refs[1]
        m_ref, v_ref, g_ref, d_ref, nm_ref, nv_ref = refs[2 + two:]
        g = ga_ref[...] + refs[2][...] if two else ga_ref[...]
        nm = ADAM_B1 * m_ref[...] + (1.0 - ADAM_B1) * g
        nv = ADAM_B2 * v_ref[...] + (1.0 - ADAM_B2) * (g * g)
        g_ref[...] = g
        nm_ref[...] = nm
        nv_ref[...] = nv
        d_ref[...] = -ADAM_LR * ((nm * c1) / (jnp.sqrt(nv * c2) + ADAM_EPS) + ADAM_WD * w_ref[...])

    args = [w, ga] + ([gb] if two else []) + [m, v]
    shp = jax.ShapeDtypeStruct((r, LANES), F32)
    return pl.pallas_call(body, name=name, grid=(r // tr,), in_specs=[spec] * len(args), out_specs=(spec,) * 4,
                          out_shape=(shp,) * 4, compiler_params=_params(1))(*args)


def _pack(arrs, dtype=F32):
    flat = jnp.concatenate([a.astype(dtype).reshape(-1) for a in arrs])
    pad = (-flat.shape[0]) % (32 * LANES)
    return jnp.pad(flat, (0, pad)).reshape(-1, LANES)


def _unpack(buf, shapes):
    flat = buf.reshape(-1)
    out, off = [], 0
    for shp in shapes:
        sz = math.prod(shp)
        out.append(flat[off:off + sz].reshape(shp))
        off += sz
    return out


def _stack_shards(full, axis):
    shp = full.shape
    return jnp.moveaxis(full.reshape(shp[:axis] + (4, shp[axis] // 4) + shp[axis + 1:]), axis, 0)


def _unstack_shards(st, axis):
    v = jnp.moveaxis(st, 0, axis)
    shp = v.shape
    return v.reshape(shp[:axis] + (shp[axis] * shp[axis + 1],) + shp[axis + 2:])


def _layer_weights(w, l):
    lw = {n: (w[n], l) if n in BIG_COLS else w[n][l] for n in w}
    lw['wcat'] = jnp.concatenate([lw['rg_wa'][0], lw['rg_wi'][0], lw['rg_wa'][1], lw['rg_wi'][1]],
                                 axis=-1).astype(BF16)
    ba, bi = lw['rg_ba'].reshape(2, RG_HEADS, HEAD), lw['rg_bi'].reshape(2, RG_HEADS, HEAD)
    lw['bcat'] = jnp.concatenate([ba[0], bi[0], ba[1], bi[1]], axis=-1)[:, None, :]
    s5_names = ['s5_a_re', 's5_a_im', 's5_log_dt', 's5_b_re', 's5_b_im', 's5_c_re', 's5_c_im']
    (wcat, mout, a2), lw['s5_vjp'] = jax.vjp(_s5_mats, *[lw[n] for n in s5_names])
    lw['s5_wcat'], lw['s5_mout'], lw['s5_a2'] = wcat.astype(BF16), mout.astype(BF16), a2
    for n in ('conv_b', 's5_d', 's5_glu_b', 'b_out', 'mlp_b1', 'mlp_b2', 'ln1_g', 'ln1_b', 'ln2_g', 'ln2_b'):
        lw[n] = lw[n][None, :]
    return lw


def _layer_fwd(l, x0, modall, lw, dims):
    bl, s, lc, tps = dims
    ll = s - lc
    tag = f"l{l}_"
    sv = {'x0': x0}
    sv['u1'] = _modulate(x0, modall, 0, 1, tps, name=tag + "mod1")
    sv['p'] = p = _mm_nn(sv['u1'], lw['w_in'], name=tag + "w_in")
    rg, sv['hf'], sv['hb'] = _rg_fwd(p, lw['conv_w'], lw['conv_b'], lw['rg_lambda'], lw['wcat'], lw['bcat'],
                                     bl, s, lc, name=tag + "rg_fwd")
    sv['u_ch'] = _to_chunks(p, 2 * D_MODEL // HEAD, bl, s, lc, name=tag + "u_chunks")
    y_ch, sv['sf'], sv['sb'] = _s5_fwd(sv['u_ch'], lw['s5_wcat'], lw['s5_mout'], lw['s5_a2'], lc // T_CH,
                                       name=tag + "s5_fwd")
    sv['y'] = _from_chunks(y_ch, bl, s, lc, name=tag + "y_rows")
    s5 = _glu_fwd(sv['y'], p, lw['s5_d'], lw['s5_glu_w'], lw['s5_glu_b'], name=tag + "glu_fwd")
    sv['cat'] = jnp.concatenate([rg, s5], axis=1)
    sv['m'] = _mm_nn(sv['cat'], lw['w_out'], lw['b_out'], name=tag + "w_out")
    sv['x1'] = _resid_ln(x0, sv['m'], modall, 2, lw['ln1_g'], lw['ln1_b'], tps, name=tag + "ln1")
    sv['u2'] = _modulate(sv['x1'], modall, 3, 4, tps, name=tag + "mod2")
    sv['a'], sv['h'] = _mm_nn(sv['u2'], lw['mlp_w1'], lw['mlp_b1'], relu2=True, name=tag + "mlp1")
    sv['f'] = _mm_nn(sv['a'], lw['mlp_w2'], lw['mlp_b2'], name=tag + "mlp2")
    x2 = _resid_ln(sv['x1'], sv['f'], modall, 5, lw['ln2_g'], lw['ln2_b'], tps, name=tag + "ln2")
    return x2, sv


def _layer_bwd(l, dx2, modall, lw, sv, dims, gbufs):
    bl, s, lc, tps = dims
    ll = s - lc
    tag = f"l{l}_"
    g = {}

    def big_grad(n, a_mat, b_mat, label):
        gbufs[n] = _mm_tn(a_mat, b_mat, name=tag + label, layer=l, into=gbufs.get(n))
    dx1a, df, db2, g['ln2_g'], g['ln2_b'], dg2 = _resid_ln_bwd(sv['x1'], sv['f'], modall, 5, lw['ln2_g'], dx2, tps,
                                                              name=tag + "ln2_bwd")
    g['mlp_b2'] = db2
    big_grad('mlp_w2', sv['a'], df, "mlp2_dw")
    dh = _mm_nt(df, lw['mlp_w2'], sv['h'], name=tag + "mlp2_dx")
    g['mlp_b1'] = _colsum(dh, name=tag + "mlp1_db")
    big_grad('mlp_w1', sv['u2'], dh, "mlp1_dw")
    du2 = _mm_nt(dh, lw['mlp_w1'], name=tag + "mlp1_dx")
    dx1, dsc2, dsh2 = _modulate_bwd(du2, sv['x1'], modall, 4, dx1a, tps, name=tag + "mod2_bwd")
    dx0a, dm, g['b_out'], g['ln1_g'], g['ln1_b'], dg1 = _resid_ln_bwd(sv['x0'], sv['m'], modall, 2, lw['ln1_g'], dx1,
                                                                     tps, name=tag + "ln1_bwd")
    big_grad('w_out', sv['cat'], dm, "w_out_dw")
    dcat = _mm_nt(dm, lw['w_out'], name=tag + "w_out_dx")
    dy, dskip, g_bf, dz_bf, g['s5_d'], g['s5_glu_b'] = _glu_bwd(dcat, sv['y'], sv['p'], lw['s5_d'], lw['s5_glu_w'],
                                                                lw['s5_glu_b'], name=tag + "glu_bwd")
    big_grad('s5_glu_w', g_bf, dz_bf, "glu_dw")
    dy_ch = _to_chunks(dy, 0, bl, s, lc, name=tag + "dy_chunks")
    du_ch, dwcat, dmout, dacc = _s5_bwd(dy_ch, sv['u_ch'], sv['sf'], sv['sb'], lw['s5_wcat'],
                                        lw['s5_mout'], lw['s5_a2'], lc // T_CH, name=tag + "s5_bwd")
    s5g = lw['s5_vjp']((dwcat, dmout, dacc))
    for n, v in zip(['s5_a_re', 's5_a_im', 's5_log_dt', 's5_b_re', 's5_b_im', 's5_c_re', 's5_c_im'], s5g):
        g[n] = v
    ds5u = _add_cast(_from_chunks(du_ch, bl, s, lc, name=tag + "du_rows"), dskip, name=tag + "ds5u")
    drgx, dgate, dcw, dcb, dlam, dwc, dbc = _rg_bwd(sv['p'], dcat, sv['hf'], sv['hb'], lw['conv_w'], lw['conv_b'],
                                                    lw['rg_lambda'], lw['wcat'], lw['bcat'], bl, s, lc,
                                                    name=tag + "rg_bwd")
    g['conv_w'] = dcw.transpose(1, 0, 2).reshape(4, D_MODEL)
    g['conv_b'] = dcb.reshape(D_MODEL)
    g['rg_lambda'] = dlam.transpose(1, 0, 2).reshape(2, D_MODEL)
    g['rg_wa'] = jnp.stack([dwc[:, :, 0:HEAD], dwc[:, :, 2 * HEAD:3 * HEAD]])
    g['rg_wi'] = jnp.stack([dwc[:, :, HEAD:2 * HEAD], dwc[:, :, 3 * HEAD:]])
    dbc = dbc.reshape(RG_HEADS, 4, HEAD)
    g['rg_ba'] = jnp.stack([dbc[:, 0], dbc[:, 2]]).reshape(2, D_MODEL)
    g['rg_bi'] = jnp.stack([dbc[:, 1], dbc[:, 3]]).reshape(2, D_MODEL)
    dp = jnp.concatenate([drgx, dgate, ds5u], axis=1)
    big_grad('w_in', sv['u1'], dp, "w_in_dw")
    du1 = _mm_nt(dp, lw['w_in'], name=tag + "w_in_dx")
    dx0, dsc1, dsh1 = _modulate_bwd(du1, sv['x0'], modall, 1, dx0a, tps, name=tag + "mod1_bwd")
    dmod = jnp.concatenate([dsh1, dsc1, dg1, dsh2, dsc2, dg2], axis=1)
    return dx0, g, dmod


def _kernel_impl(*args):
    nin = len(IN_NAMES)
    a = dict(zip(IN_NAMES, args[:nin]))
    target = args[nin]
    nw = len(WEIGHTS)
    mom = dict(zip(WEIGHTS, args[nin + 1:nin + 1 + nw]))
    var = dict(zip(WEIGHTS, args[nin + 1 + nw:nin + 1 + 2 * nw]))
    bl, ll, d = a['x'].shape
    lc = a['ctx'].shape[1]
    assert d == D_MODEL and lc == TM and bl == 2 and ll % (GRID_W * T_CH) == 0
    s = lc + ll
    tps = s // TM
    dims = (bl, s, lc, tps)

    def gather(names, dtype, tag):
        shards = [a[n] for n in names]
        got = _all_gather_xy(_pack(shards, dtype), name="gather_" + tag)
        per = [_unpack(got[j], [w.shape for w in shards]) for j in range(4)]
        return {n: _unstack_shards(jnp.stack([per[j][i] for j in range(4)]), SHARD_AXIS[n])
                for i, n in enumerate(names)}

    big_cols = [BIG_COLS[n] for n in BIG]
    my_c = lax.axis_index("c").astype(jnp.int32).reshape(1)
    my_j = (2 * lax.axis_index("x") + lax.axis_index("y")).astype(jnp.int32).reshape(1)
    mine = [_cast_into_full(a[n], BIG_COLS[n], my_j, name=f"cast_{n}") for n in BIG]
    w = dict(zip(BIG, _gather_big(mine, big_cols, name="gather_big")))
    w.update(gather(GATHER_F32, F32, "f32"))
    for n in REPLICATED:
        w[n] = a[n]

    xs = jnp.concatenate([a['ctx'], a['x']], axis=1).reshape(bl * s, D_MODEL)
    c16 = jnp.zeros((16, D_MODEL), F32).at[0:2].set(a['c']).at[2].set(a['c_ctx'])
    s16, ds16 = _silu_rows(c16, name="silu")
    s16b = s16.astype(BF16)
    layers, saved, mods = [], [], []
    for l in range(DEPTH):
        lw = _layer_weights({n: w[n] for n in WEIGHTS if n not in ('c_ctx',)}, l)
        mod16 = _mm_nn(s16b, lw['ada_w'], lw['ada_b'][None, :], name=f"l{l}_ada").reshape(16, N_MOD, D_MODEL)
        modall = jnp.stack([mod16[2], mod16[0], mod16[2], mod16[1]])
        xs, sv = _layer_fwd(l, xs, modall, lw, dims)
        layers.append(lw)
        saved.append(sv)
        mods.append(modall)
    lossrow, dx = _loss_head(xs, target.reshape(bl * ll, D_MODEL), tps, name="loss_head")
    loss = lax.psum(0.5 / D_MODEL * jnp.sum(lossrow), ("x", "y", "c"))

    small = [n for n in WEIGHTS if n != 'c_ctx' and n not in BIG_COLS]
    grads = {n: [None] * DEPTH for n in small}
    gbufs = {}
    ds_rows = jnp.zeros((16, D_MODEL), F32)
    for l in reversed(range(DEPTH)):
        dx, g, dmod = _layer_bwd(l, dx, mods[l], layers[l], saved[l], dims, gbufs)
        dmod16 = jnp.zeros((16, N_MOD * D_MODEL), F32).at[0].set(dmod[1].reshape(-1)).at[1].set(
            dmod[3].reshape(-1)).at[2].set((dmod[0] + dmod[2]).reshape(-1))
        dmod16b = dmod16.astype(BF16)
        gbufs['ada_w'] = _mm_tn(s16b, dmod16b, name=f"l{l}_ada_dw", layer=l, into=gbufs.get('ada_w'))
        g['ada_b'] = _colsum(dmod16, name=f"l{l}_ada_db")
        ds_rows = ds_rows + _mm_nt(dmod16b, layers[l]['ada_w'], name=f"l{l}_ada_dx")
        for n, v in g.items():
            grads[n][l] = v.reshape(a[n].shape[1:] if n in REPLICATED else w[n].shape[1:])
    full = {n: jnp.stack(v) for n, v in grads.items()}
    full['c_ctx'] = _mul_rows(ds_rows, ds16, name="silu_bwd")[2]
    grad_x = dx.reshape(bl, s, D_MODEL)[:, lc:]

    from_sib = _sibling_partials([gbufs[n] for n in BIG], name="grad_big_sibling")
    sums = [_chip_sum(gbufs[n], o, my_c, name=f"grad_chip_sum_{n}") for n, o in zip(BIG, from_sib)]
    got = _scatter_big(sums, big_cols, name="grad_big_scatter")
    finals = [_block_sum(sm, gt, cf, my_j, my_c, name=f"grad_block_sum_{n}")
              for n, sm, gt, cf in zip(BIG, sums, got, big_cols)]
    res_big = {}
    for n, gfull in zip(BIG, _share_final(finals, name="grad_big_share")):
        flat = lambda t: t.reshape(-1, t.shape[-1])
        d_w, n_m, n_v = _adamw_native(flat(a[n]), flat(gfull), flat(mom[n]), flat(var[n]), name=f"adamw_{n}")
        res_big[n] = [gfull] + [t.reshape(a[n].shape) for t in (d_w, n_m, n_v)]

    rep_flat = _pack([full[n] for n in REPLICATED])
    rr = rep_flat.shape[0]
    sh_stacked = [_stack_shards(full[n], SHARD_AXIS[n] + 0).reshape(4, -1) for n in SHARDED]
    parts = jnp.concatenate(sh_stacked + [rep_flat.reshape(4, -1)], axis=1)
    pad = (-parts.shape[1]) % (32 * LANES)
    parts = jnp.pad(parts, ((0, 0), (0, pad))).reshape(4, -1, LANES)
    mine = _sum4(_scatter_xy(parts, name="grad_scatter"), name="grad_sum4")
    other = _swap_sibling(mine, name="grad_swap")
    n_sh = sum(math.prod(a[n].shape) for n in SHARDED)
    r_sh = n_sh // LANES
    assert n_sh % LANES == 0
    rq = rr // 4

    sh_shapes = [a[n].shape for n in SHARDED]
    pk = lambda dct: _pack([dct[n] for n in SHARDED])
    r_pk = pk(a).shape[0]
    take = lambda buf: jnp.pad(buf[:r_sh], ((0, r_pk - r_sh), (0, 0)))
    outs_sh = _adamw(pk(a), take(mine), take(other), pk(mom), pk(var), name="adamw_sharded")
    res_sh = [dict(zip(SHARDED, _unpack(o, sh_shapes))) for o in outs_sh]

    quarter = _add2(mine[r_sh:r_sh + rq], other[r_sh:r_sh + rq], name="grad_rep_sum")
    rep_g = _all_gather_xy(quarter, name="grad_rep_gather").reshape(rr, LANES)
    rep_shapes = [a[n].shape for n in REPLICATED]
    pr = lambda dct: _pack([dct[n] for n in REPLICATED])
    outs_rep = _adamw(pr(a), rep_g, None, pr(mom), pr(var), name="adamw_replicated")
    res_rep = [dict(zip(REPLICATED, _unpack(o, rep_shapes))) for o in outs_rep]

    out = [loss, grad_x]
    for k in range(4):
        out += [res_big[n][k] if n in BIG_COLS else res_sh[k][n] if n in SHARDED else res_rep[k][n] for n in WEIGHTS]
    return tuple(out)


def kernel(x, c, ctx, c_ctx, ada_w, ada_b, ln1_g, ln1_b, w_in, conv_w, conv_b, rg_lambda, rg_wa, rg_ba, rg_wi, rg_bi, s5_a_re, s5_a_im, s5_log_dt, s5_b_re, s5_b_im, s5_c_re, s5_c_im, s5_d, s5_glu_w, s5_glu_b, w_out, b_out, ln2_g, ln2_b, mlp_w1, mlp_b1, mlp_w2, mlp_b2, loss_target, m_c_ctx, m_ada_w, m_ada_b, m_ln1_g, m_ln1_b, m_w_in, m_conv_w, m_conv_b, m_rg_lambda, m_rg_wa, m_rg_ba, m_rg_wi, m_rg_bi, m_s5_a_re, m_s5_a_im, m_s5_log_dt, m_s5_b_re, m_s5_b_im, m_s5_c_re, m_s5_c_im, m_s5_d, m_s5_glu_w, m_s5_glu_b, m_w_out, m_b_out, m_ln2_g, m_ln2_b, m_mlp_w1, m_mlp_b1, m_mlp_w2, m_mlp_b2, v_c_ctx, v_ada_w, v_ada_b, v_ln1_g, v_ln1_b, v_w_in, v_conv_w, v_conv_b, v_rg_lambda, v_rg_wa, v_rg_ba, v_rg_wi, v_rg_bi, v_s5_a_re, v_s5_a_im, v_s5_log_dt, v_s5_b_re, v_s5_b_im, v_s5_c_re, v_s5_c_im, v_s5_d, v_s5_glu_w, v_s5_glu_b, v_w_out, v_b_out, v_ln2_g, v_ln2_b, v_mlp_w1, v_mlp_b1, v_mlp_w2, v_mlp_b2):
    return _kernel_impl(x, c, ctx, c_ctx, ada_w, ada_b, ln1_g, ln1_b, w_in, conv_w, conv_b, rg_lambda, rg_wa, rg_ba, rg_wi, rg_bi, s5_a_re, s5_a_im, s5_log_dt, s5_b_re, s5_b_im, s5_c_re, s5_c_im, s5_d, s5_glu_w, s5_glu_b, w_out, b_out, ln2_g, ln2_b, mlp_w1, mlp_b1, mlp_w2, mlp_b2, loss_target, m_c_ctx, m_ada_w, m_ada_b, m_ln1_g, m_ln1_b, m_w_in, m_conv_w, m_conv_b, m_rg_lambda, m_rg_wa, m_rg_ba, m_rg_wi, m_rg_bi, m_s5_a_re, m_s5_a_im, m_s5_log_dt, m_s5_b_re, m_s5_b_im, m_s5_c_re, m_s5_c_im, m_s5_d, m_s5_glu_w, m_s5_glu_b, m_w_out, m_b_out, m_ln2_g, m_ln2_b, m_mlp_w1, m_mlp_b1, m_mlp_w2, m_mlp_b2, v_c_ctx, v_ada_w, v_ada_b, v_ln1_g, v_ln1_b, v_w_in, v_conv_w, v_conv_b, v_rg_lambda, v_rg_wa, v_rg_ba, v_rg_wi, v_rg_bi, v_s5_a_re, v_s5_a_im, v_s5_log_dt, v_s5_b_re, v_s5_b_im, v_s5_c_re, v_s5_c_im, v_s5_d, v_s5_glu_w, v_s5_glu_b, v_w_out, v_b_out, v_ln2_g, v_ln2_b, v_mlp_w1, v_mlp_b1, v_mlp_w2, v_mlp_b2)
```

```python
import functools
import math

import jax
import jax.numpy as jnp
from jax import lax
from jax.experimental import pallas as pl
from jax.experimental.pallas import tpu as pltpu

F32 = jnp.float32
BF16 = jnp.bfloat16
MESH = pl.DeviceIdType.MESH

D_MODEL = 1024
N_MOD = 6
GRID_W = 64
RG_HEADS = 8
HEAD = 128
RG_C = 8.0
S5_GROUPS = 64
S5_GROUP = 16
S5_STATE = 64
T_CH = 16
GB = 8
CW = T_CH * S5_GROUP
SW = 2 * S5_STATE
DEPTH = 2
ALPHA = (2.0 * DEPTH) ** 0.25
LN_EPS = 1e-5
TM = 256
LANES = 1024
ADAM_LR, ADAM_B1, ADAM_B2, ADAM_EPS, ADAM_WD, ADAM_STEP = 0.001, 0.9, 0.999, 1e-08, 0.01, 10
MIB = 2 ** 20

IN_NAMES = ['x', 'c', 'ctx', 'c_ctx', 'ada_w', 'ada_b', 'ln1_g', 'ln1_b', 'w_in', 'conv_w', 'conv_b', 'rg_lambda',
            'rg_wa', 'rg_ba', 'rg_wi', 'rg_bi', 's5_a_re', 's5_a_im', 's5_log_dt', 's5_b_re', 's5_b_im', 's5_c_re',
            's5_c_im', 's5_d', 's5_glu_w', 's5_glu_b', 'w_out', 'b_out', 'ln2_g', 'ln2_b', 'mlp_w1', 'mlp_b1',
            'mlp_w2', 'mlp_b2']
WEIGHTS = IN_NAMES[3:]
SHARD_AXIS = {'ada_w': 2, 'w_in': 2, 'conv_w': 2, 'rg_lambda': 2, 'rg_ba': 2, 'rg_bi': 2, 's5_glu_w': 1, 'w_out': 1,
              'mlp_w1': 2, 'mlp_w2': 1}
SHARDED = ['conv_w', 'rg_lambda', 'rg_ba', 'rg_bi']
REPLICATED = [n for n in WEIGHTS if n not in SHARD_AXIS]
GATHER_BF16 = ['ada_w', 'w_in', 's5_glu_w', 'w_out', 'mlp_w1', 'mlp_w2']
GATHER_F32 = ['conv_w', 'rg_lambda', 'rg_ba', 'rg_bi']


def _params(n_axes, vmem_mb=40):
    return pltpu.CompilerParams(dimension_semantics=("arbitrary",) * n_axes, vmem_limit_bytes=vmem_mb * MIB)


def _tile(n, options):
    for t in options:
        if n % t == 0:
            return t
    return n


def _sigmoid(z):
    return 1.0 / (1.0 + jnp.exp(-z))


def _softplus(z):
    return jnp.maximum(z, 0.0) + jnp.log(1.0 + jnp.exp(-jnp.abs(z)))


def _neg_expm1_small(z):
    return -z * (1.0 + 0.5 * z * (1.0 + (1.0 / 3.0) * z * (1.0 + 0.25 * z)))


_G0 = math.sqrt(2.0 / math.pi)
_G1 = 0.044715


def _gelu(v):
    return 0.5 * v * (1.0 + jnp.tanh(_G0 * (v + _G1 * v * v * v)))


def _gelu_and_grad(v):
    t = jnp.tanh(_G0 * (v + _G1 * v * v * v))
    g = 0.5 * v * (1.0 + t)
    dg = 0.5 * (1.0 + t) + 0.5 * v * (1.0 - t * t) * _G0 * (1.0 + 3.0 * _G1 * v * v)
    return g, dg


def _seq_of_tile(i, tps):
    return 2 * (i // tps) + jnp.minimum(i % tps, 1)


def _dot(a, b, dims=(((1,), (0,)), ((), ()))):
    return lax.dot_general(a, b, dims, preferred_element_type=F32)


_NT = (((1,), (1,)), ((), ()))
_TN = (((0,), (0,)), ((), ()))


def _layer_spec(b, block, index):
    if isinstance(b, tuple):
        arr, layer = b
        return arr, pl.BlockSpec((None,) + block, lambda *g: (layer,) + index(*g))
    return b, pl.BlockSpec(block, index)


def _mm_nn(a, b, bias=None, *, relu2=False, name):
    m, k = a.shape
    n = (b[0] if isinstance(b, tuple) else b).shape[-1]
    tm, tn, tk = _tile(m, (512, 256)), _tile(n, (1024,)), _tile(k, (1024,))
    nk = k // tk
    has_bias = bias is not None

    def body(*refs):
        a_ref, b_ref = refs[0], refs[1]
        bias_ref = refs[2] if has_bias else None
        outs = refs[2 + has_bias:-1]
        acc = refs[-1]
        kk = pl.program_id(2)

        @pl.when(kk == 0)
        def _():
            acc[...] = jnp.zeros_like(acc)

        acc[...] += _dot(a_ref[...], b_ref[...])

        @pl.when(kk == nk - 1)
        def _():
            h = acc[...]
            if has_bias:
                h = h + bias_ref[...]
            if relu2:
                r = jnp.maximum(h, 0.0)
                outs[0][...] = (r * r).astype(BF16)
                outs[1][...] = h.astype(BF16)
            else:
                outs[0][...] = h

    b_arr, b_spec = _layer_spec(b, (tk, tn), lambda j, i, kk: (kk, j))
    in_specs = [pl.BlockSpec((tm, tk), lambda j, i, kk: (i, kk)), b_spec]
    args = [a, b_arr]
    if has_bias:
        in_specs.append(pl.BlockSpec((1, tn), lambda j, i, kk: (0, j)))
        args.append(bias)
    o_spec = pl.BlockSpec((tm, tn), lambda j, i, kk: (i, j))
    if relu2:
        out_shape = (jax.ShapeDtypeStruct((m, n), BF16), jax.ShapeDtypeStruct((m, n), BF16))
        out_specs = (o_spec, o_spec)
    else:
        out_shape, out_specs = jax.ShapeDtypeStruct((m, n), F32), o_spec
    return pl.pallas_call(body, name=name, grid=(n // tn, m // tm, nk), in_specs=in_specs, out_specs=out_specs,
                          out_shape=out_shape, scratch_shapes=[pltpu.VMEM((tm, tn), F32)],
                          compiler_params=_params(3))(*args)


def _mm_nt(a, b, hb=None, *, name):
    m, n = a.shape
    k = (b[0] if isinstance(b, tuple) else b).shape[-2]
    tm, tn, tk = _tile(m, (512, 256)), _tile(k, (1024,)), _tile(n, (1024,))
    nk = n // tk
    fused = hb is not None

    def body(*refs):
        a_ref, b_ref = refs[0], refs[1]
        hb_ref = refs[2] if fused else None
        o_ref, acc = refs[-2], refs[-1]
        kk = pl.program_id(2)

        @pl.when(kk == 0)
        def _():
            acc[...] = jnp.zeros_like(acc)

        acc[...] += _dot(a_ref[...], b_ref[...], _NT)

        @pl.when(kk == nk - 1)
        def _():
            if fused:
                o_ref[...] = (acc[...] * (2.0 * jnp.maximum(hb_ref[...].astype(F32), 0.0))).astype(BF16)
            else:
                o_ref[...] = acc[...]

    b_arr, b_spec = _layer_spec(b, (tn, tk), lambda j, i, kk: (j, kk))
    in_specs = [pl.BlockSpec((tm, tk), lambda j, i, kk: (i, kk)), b_spec]
    args = [a, b_arr]
    if fused:
        in_specs.append(pl.BlockSpec((tm, tn), lambda j, i, kk: (i, j)))
        args.append(hb)
    return pl.pallas_call(body, name=name, grid=(k // tn, m // tm, nk), in_specs=in_specs,
                          out_specs=pl.BlockSpec((tm, tn), lambda j, i, kk: (i, j)),
                          out_shape=jax.ShapeDtypeStruct((m, k), BF16 if fused else F32),
                          scratch_shapes=[pltpu.VMEM((tm, tn), F32)], compiler_params=_params(3))(*args)


def _mm_tn(a, b, *, name, layer=None, into=None):
    m, k = a.shape
    n = b.shape[1]
    tk, tn, tr = _tile(k, (1024,)), _tile(n, (1024,)), _tile(m, (512, 256))
    nr = m // tr

    def body(a_ref, b_ref, *rest):
        o_ref, acc = rest[-2], rest[-1]
        r = pl.program_id(2)

        @pl.when(r == 0)
        def _():
            acc[...] = jnp.zeros_like(acc)

        acc[...] += _dot(a_ref[...], b_ref[...], _TN)

        @pl.when(r == nr - 1)
        def _():
            o_ref[...] = acc[...]

    in_specs = [pl.BlockSpec((tr, tk), lambda i, j, r: (r, i)), pl.BlockSpec((tr, tn), lambda i, j, r: (r, j))]
    args, aliases = [a, b], {}
    if layer is None:
        out_spec, out_shape = pl.BlockSpec((tk, tn), lambda i, j, r: (i, j)), jax.ShapeDtypeStruct((k, n), F32)
    else:
        out_spec = pl.BlockSpec((None, tk, tn), lambda i, j, r: (layer, i, j))
        out_shape = jax.ShapeDtypeStruct((DEPTH, k, n), F32)
        if into is not None:
            in_specs.append(_ANY)
            args.append(into)
            aliases = {2: 0}
    return pl.pallas_call(body, name=name, grid=(k // tk, n // tn, nr), in_specs=in_specs, out_specs=out_spec,
                          out_shape=out_shape, input_output_aliases=aliases,
                          scratch_shapes=[pltpu.VMEM((tk, tn), F32)], compiler_params=_params(3))(*args)


def _colsum(v, *, name):
    m, n = v.shape
    tn, tr = _tile(n, (1024,)), _tile(m, (512, 256))

    def body(v_ref, o_ref):
        @pl.when(pl.program_id(1) == 0)
        def _():
            o_ref[...] = jnp.zeros_like(o_ref)

        o_ref[...] += jnp.sum(v_ref[...].astype(F32), axis=0, keepdims=True)

    return pl.pallas_call(body, name=name, grid=(n // tn, m // tr),
                          in_specs=[pl.BlockSpec((tr, tn), lambda j, r: (r, j))],
                          out_specs=pl.BlockSpec((1, tn), lambda j, r: (0, j)),
                          out_shape=jax.ShapeDtypeStruct((1, n), F32), compiler_params=_params(2))(v)


def _tok_spec(d=D_MODEL, col=0):
    return pl.BlockSpec((TM, d), lambda i: (i, col))


def _mod_spec(tps):
    return pl.BlockSpec((1, N_MOD, D_MODEL), lambda i: (_seq_of_tile(i, tps), 0, 0))


def _row_spec(d=D_MODEL):
    return pl.BlockSpec((1, d), lambda i: (0, 0))


def _seq_acc_spec(tps):
    return pl.BlockSpec((1, 1, D_MODEL), lambda i: (_seq_of_tile(i, tps), 0, 0))


def _modulate(xs, modall, k_shift, k_scale, tps, *, name):
    n = xs.shape[0]

    def body(x_ref, m_ref, o_ref):
        sh = m_ref[0, k_shift:k_shift + 1, :]
        sc = m_ref[0, k_scale:k_scale + 1, :]
        o_ref[...] = (x_ref[...] * (1.0 + sc) + sh).astype(BF16)

    return pl.pallas_call(body, name=name, grid=(n // TM,), in_specs=[_tok_spec(), _mod_spec(tps)],
                          out_specs=_tok_spec(), out_shape=jax.ShapeDtypeStruct((n, D_MODEL), BF16),
                          compiler_params=_params(1))(xs, modall)


def _resid_ln(xs, ms, modall, k_gate, g, b, tps, *, name):
    n = xs.shape[0]

    def body(x_ref, m_ref, mod_ref, g_ref, b_ref, o_ref):
        z = ALPHA * x_ref[...] + mod_ref[0, k_gate:k_gate + 1, :] * m_ref[...]
        mu = jnp.mean(z, axis=-1, keepdims=True)
        zc = z - mu
        var = jnp.mean(zc * zc, axis=-1, keepdims=True)
        o_ref[...] = zc * lax.rsqrt(var + LN_EPS) * g_ref[...] + b_ref[...]

    return pl.pallas_call(body, name=name, grid=(n // TM,),
                          in_specs=[_tok_spec(), _tok_spec(), _mod_spec(tps), _row_spec(), _row_spec()],
                          out_specs=_tok_spec(), out_shape=jax.ShapeDtypeStruct((n, D_MODEL), F32),
                          compiler_params=_params(1))(xs, ms, modall, g, b)


def _resid_ln_bwd(xs, ms, modall, k_gate, g, dout, tps, *, name):
    n = xs.shape[0]

    def body(x_ref, m_ref, mod_ref, g_ref, d_ref, dxa_ref, dm_ref, dbias_ref, dg_ref, db_ref, dgate_ref):
        i = pl.program_id(0)
        gate = mod_ref[0, k_gate:k_gate + 1, :]
        m = m_ref[...]
        z = ALPHA * x_ref[...] + gate * m
        mu = jnp.mean(z, axis=-1, keepdims=True)
        zc = z - mu
        var = jnp.mean(zc * zc, axis=-1, keepdims=True)
        rstd = lax.rsqrt(var + LN_EPS)
        xhat = zc * rstd
        d = d_ref[...]
        dxh = d * g_ref[...]
        dz = rstd * (dxh - jnp.mean(dxh, axis=-1, keepdims=True)
                     - xhat * jnp.mean(dxh * xhat, axis=-1, keepdims=True))
        dxa_ref[...] = ALPHA * dz
        dm = gate * dz
        dm_ref[...] = dm.astype(BF16)

        @pl.when(i == 0)
        def _():
            dbias_ref[...] = jnp.zeros_like(dbias_ref)
            dg_ref[...] = jnp.zeros_like(dg_ref)
            db_ref[...] = jnp.zeros_like(db_ref)

        dbias_ref[...] += jnp.sum(dm, axis=0, keepdims=True)
        dg_ref[...] += jnp.sum(d * xhat, axis=0, keepdims=True)
        db_ref[...] += jnp.sum(d, axis=0, keepdims=True)
        part = jnp.sum(dz * m, axis=0, keepdims=True)

        @pl.when(i % tps <= 1)
        def _():
            dgate_ref[0] = part

        @pl.when(i % tps > 1)
        def _():
            dgate_ref[0] += part

    row = jax.ShapeDtypeStruct((1, D_MODEL), F32)
    return pl.pallas_call(
        body, name=name, grid=(n // TM,),
        in_specs=[_tok_spec(), _tok_spec(), _mod_spec(tps), _row_spec(), _tok_spec()],
        out_specs=(_tok_spec(), _tok_spec(), _row_spec(), _row_spec(), _row_spec(), _seq_acc_spec(tps)),
        out_shape=(jax.ShapeDtypeStruct((n, D_MODEL), F32), jax.ShapeDtypeStruct((n, D_MODEL), BF16), row, row, row,
                   jax.ShapeDtypeStruct((n // TM // tps * 2, 1, D_MODEL), F32)),
        compiler_params=_params(1))(xs, ms, modall, g, dout)


def _modulate_bwd(du, xs, modall, k_scale, dxa, tps, *, name):
    n = xs.shape[0]

    def body(du_ref, x_ref, mod_ref, dxa_ref, dx_ref, dsc_ref, dsh_ref):
        i = pl.program_id(0)
        du_t = du_ref[...]
        dx_ref[...] = dxa_ref[...] + du_t * (1.0 + mod_ref[0, k_scale:k_scale + 1, :])
        psc = jnp.sum(du_t * x_ref[...], axis=0, keepdims=True)
        psh = jnp.sum(du_t, axis=0, keepdims=True)

        @pl.when(i % tps <= 1)
        def _():
            dsc_ref[0] = psc
            dsh_ref[0] = psh

        @pl.when(i % tps > 1)
        def _():
            dsc_ref[0] += psc
            dsh_ref[0] += psh

    acc = jax.ShapeDtypeStruct((n // TM // tps * 2, 1, D_MODEL), F32)
    return pl.pallas_call(body, name=name, grid=(n // TM,),
                          in_specs=[_tok_spec(), _tok_spec(), _mod_spec(tps), _tok_spec()],
                          out_specs=(_tok_spec(), _seq_acc_spec(tps), _seq_acc_spec(tps)),
                          out_shape=(jax.ShapeDtypeStruct((n, D_MODEL), F32), acc, acc),
                          compiler_params=_params(1))(du, xs, modall, dxa)


def _loss_head(ys, target, tps, *, name):
    n = ys.shape[0]
    lat_tiles = tps - 1

    def body(y_ref, t_ref, acc_ref, dy_ref):
        i = pl.program_id(0)

        @pl.when(i == 0)
        def _():
            acc_ref[...] = jnp.zeros_like(acc_ref)

        @pl.when(i % tps == 0)
        def _():
            dy_ref[...] = jnp.zeros_like(dy_ref)

        @pl.when(i % tps > 0)
        def _():
            e = y_ref[...] - t_ref[...]
            dy_ref[...] = e * (1.0 / D_MODEL)
            acc_ref[...] += jnp.sum(e * e, axis=0, keepdims=True)

    t_spec = pl.BlockSpec((TM, D_MODEL), lambda i: ((i // tps) * lat_tiles + jnp.maximum(i % tps - 1, 0), 0))
    return pl.pallas_call(body, name=name, grid=(n // TM,), in_specs=[_tok_spec(), t_spec],
                          out_specs=(_row_spec(), _tok_spec()),
                          out_shape=(jax.ShapeDtypeStruct((1, D_MODEL), F32), jax.ShapeDtypeStruct((n, D_MODEL), F32)),
                          compiler_params=_params(1))(ys, target)


def _glu_fwd(y_nat, p, d_skip, w, b, *, name):
    n = y_nat.shape[0]

    def body(y_ref, u_ref, d_ref, w_ref, b_ref, o_ref):
        g = _gelu(y_ref[...] + d_ref[...] * u_ref[...])
        z = _dot(g.astype(BF16), w_ref[...]) + b_ref[...]
        o_ref[...] = (g * _sigmoid(z)).astype(BF16)

    w_arr, w_spec = _layer_spec(w, (D_MODEL, D_MODEL), lambda i: (0, 0))
    return pl.pallas_call(body, name=name, grid=(n // TM,),
                          in_specs=[_tok_spec(), _tok_spec(col=2), _row_spec(), w_spec, _row_spec()],
                          out_specs=_tok_spec(), out_shape=jax.ShapeDtypeStruct((n, D_MODEL), BF16),
                          compiler_params=_params(1))(y_nat, p, d_skip, w_arr, b)


def _glu_bwd(dcat, y_nat, p, d_skip, w, b, *, name):
    n = y_nat.shape[0]

    def body(ds_ref, y_ref, u_ref, d_ref, w_ref, b_ref, dy_ref, dsk_ref, g_ref, dz_ref, dd_ref, dbz_ref):
        u = u_ref[...]
        g, gg = _gelu_and_grad(y_ref[...] + d_ref[...] * u)
        s = _sigmoid(_dot(g.astype(BF16), w_ref[...]) + b_ref[...])
        ds = ds_ref[...]
        dz = ds * g * s * (1.0 - s)
        dzb = dz.astype(BF16)
        dg = ds * s + _dot(dzb, w_ref[...], _NT)
        dyp = dg * gg
        dy_ref[...] = dyp
        dsk_ref[...] = dyp * d_ref[...]
        g_ref[...] = g.astype(BF16)
        dz_ref[...] = dzb

        @pl.when(pl.program_id(0) == 0)
        def _():
            dd_ref[...] = jnp.zeros_like(dd_ref)
            dbz_ref[...] = jnp.zeros_like(dbz_ref)

        dd_ref[...] += jnp.sum(dyp * u, axis=0, keepdims=True)
        dbz_ref[...] += jnp.sum(dz, axis=0, keepdims=True)

    tok_bf = jax.ShapeDtypeStruct((n, D_MODEL), BF16)
    tok_f32 = jax.ShapeDtypeStruct((n, D_MODEL), F32)
    row = jax.ShapeDtypeStruct((1, D_MODEL), F32)
    w_arr, w_spec = _layer_spec(w, (D_MODEL, D_MODEL), lambda i: (0, 0))
    return pl.pallas_call(
        body, name=name, grid=(n // TM,),
        in_specs=[_tok_spec(col=1), _tok_spec(), _tok_spec(col=2), _row_spec(), w_spec, _row_spec()],
        out_specs=(_tok_spec(), _tok_spec(), _tok_spec(), _tok_spec(), _row_spec(), _row_spec()),
        out_shape=(tok_f32, tok_f32, tok_bf, tok_bf, row, row),
        compiler_params=_params(1))(dcat, y_nat, p, d_skip, w_arr, b)


def _add_cast(a, b, *, name):
    n = a.shape[0]

    def body(a_ref, b_ref, o_ref):
        o_ref[...] = (a_ref[...] + b_ref[...]).astype(BF16)

    return pl.pallas_call(body, name=name, grid=(n // TM,), in_specs=[_tok_spec(), _tok_spec()],
                          out_specs=_tok_spec(), out_shape=jax.ShapeDtypeStruct((n, D_MODEL), BF16),
                          compiler_params=_params(1))(a, b)


def _silu_rows(c16, *, name):
    def body(c_ref, s_ref, ds_ref):
        v = c_ref[...]
        sg = _sigmoid(v)
        s_ref[...] = v * sg
        ds_ref[...] = sg * (1.0 + v * (1.0 - sg))

    shp = jax.ShapeDtypeStruct(c16.shape, F32)
    return pl.pallas_call(body, name=name, out_shape=(shp, shp))(c16)


def _mul_rows(a, b, *, name):
    def body(a_ref, b_ref, o_ref):
        o_ref[...] = a_ref[...] * b_ref[...]

    return pl.pallas_call(body, name=name, out_shape=jax.ShapeDtypeStruct(a.shape, F32))(a, b)


def _pad_off(c):
    return pl.multiple_of(c * TM + 8 + 8 * jnp.minimum(c, 1), 8)


def _rows8(k):
    return pl.ds(pl.multiple_of(k * 8, 8), 8)


def _windows(buf, c, shifts):
    n = TM + 16
    win = buf[pl.ds(pl.multiple_of(_pad_off(c) - 8, 8), n), :]
    return [win[8:8 + TM] if k == 0 else pltpu.roll(win, (-k) % n, 0)[8:8 + TM] for k in shifts]


def _conv_window(xpad, c):
    return _windows(xpad, c, (-1, 0, 1, 2))


def _rg_coeffs(z, d, spl, xc):
    r = 0.5 + 0.5 * jnp.tanh(0.5 * z[:, 256 * d:256 * d + HEAD])
    i = 0.5 + 0.5 * jnp.tanh(0.5 * z[:, 256 * d + HEAD:256 * d + 2 * HEAD])
    la = -RG_C * spl[d:d + 1, :] * r
    a = jnp.exp(la)
    one_minus_a = jnp.where(jnp.abs(la) < 1e-2, _neg_expm1_small(la), 1.0 - a)
    mult = jnp.sqrt(one_minus_a * (1.0 + a))
    return r, i, a, mult, a * a


def _chunk_scan(a, b, reverse):
    row = lax.broadcasted_iota(jnp.int32, (TM, HEAD), 0)
    sft = 1
    while sft < TM:
        keep = (row < TM - sft) if reverse else (row >= sft)
        amt = TM - sft if reverse else sft
        a_prev = jnp.where(keep, pltpu.roll(a, amt, 0), 1.0)
        b_prev = jnp.where(keep, pltpu.roll(b, amt, 0), 0.0)
        b = a * b_prev + b
        a = a * a_prev
        sft *= 2
    return a, b


def _zero_pads(buf, s, lc):
    z8 = jnp.zeros((8, HEAD), F32)
    buf[0:8, :] = z8
    buf[8 + lc:16 + lc, :] = z8
    buf[16 + s:24 + s, :] = z8


def _rg_fwd(p, conv_w, conv_b, lam, wcat, bcat, bl, s, lc, *, name):
    nch = s // TM

    def body(x_ref, gate_ref, cw_ref, cb_ref, lam_ref, w_ref, b_ref, rg_ref, hf_ref, hb_ref, xpad, af, bf, ab, bb):
        _zero_pads(xpad, s, lc)

        def copy_chunk(c, _):
            xpad[pl.ds(_pad_off(c), TM), :] = x_ref[pl.ds(pl.multiple_of(c * TM, TM), TM), :]
            return 0

        lax.fori_loop(0, nch, copy_chunk, 0)
        spl = _softplus(-lam_ref[...])
        cw = cw_ref[...]

        def coef_chunk(c, _):
            xm1, x0, xp1, xp2 = _conv_window(xpad, c)
            xc = cw[0:1] * xm1 + cw[1:2] * x0 + cw[2:3] * xp1 + cw[3:4] * xp2 + cb_ref[...]
            z = _dot(xc.astype(BF16), w_ref[0]) + b_ref[0]
            rows = pl.ds(pl.multiple_of(c * TM, TM), TM)
            for d, (a_s, b_s) in enumerate(((af, bf), (ab, bb))):
                _, i, a, mult, _ = _rg_coeffs(z, d, spl, xc)
                a_s[rows, :] = a
                b_s[rows, :] = mult * i * xc
            return 0

        lax.fori_loop(0, nch, coef_chunk, 0)

        def scan_pair(j, carry):
            cf, cb_ = carry
            rf = pl.ds(pl.multiple_of(j * TM, TM), TM)
            rb = pl.ds(pl.multiple_of(jnp.where(j == 0, 0, nch - j) * TM, TM), TM)
            a1, h1 = _chunk_scan(af[rf, :], bf[rf, :], False)
            h1 = h1 + a1 * cf
            hf_ref[rf, :] = h1
            a2, h2 = _chunk_scan(ab[rb, :], bb[rb, :], True)
            h2 = h2 + a2 * cb_
            hb_ref[rb, :] = h2
            return h1[TM - 1:TM], h2[0:1]

        zero = jnp.zeros((1, HEAD), F32)
        lax.fori_loop(0, nch, scan_pair, (zero, zero))

        def out_chunk(c, _):
            rows = pl.ds(pl.multiple_of(c * TM, TM), TM)
            rg_ref[rows, :] = ((hf_ref[rows, :] + hb_ref[rows, :]) * _gelu(gate_ref[rows, :])).astype(BF16)
            return 0

        lax.fori_loop(0, nch, out_chunk, 0)

    seq = lambda col0: pl.BlockSpec((s, HEAD), lambda b, h: (b, col0 + h))
    par = lambda r: pl.BlockSpec((r, HEAD), lambda b, h: (0, h))
    n = bl * s
    return pl.pallas_call(
        body, name=name, grid=(bl, RG_HEADS),
        in_specs=[seq(0), seq(RG_HEADS), par(4), par(1), par(2),
                  pl.BlockSpec((1, HEAD, 4 * HEAD), lambda b, h: (h, 0, 0)),
                  pl.BlockSpec((1, 1, 4 * HEAD), lambda b, h: (h, 0, 0))],
        out_specs=(seq(0), seq(0), seq(0)),
        out_shape=(jax.ShapeDtypeStruct((n, D_MODEL), BF16), jax.ShapeDtypeStruct((n, D_MODEL), F32),
                   jax.ShapeDtypeStruct((n, D_MODEL), F32)),
        scratch_shapes=[pltpu.VMEM((s + 24, HEAD), F32)] + [pltpu.VMEM((s, HEAD), F32)] * 4,
        compiler_params=_params(2, 48))(p, p, conv_w, conv_b, lam, wcat, bcat)


def _rg_bwd(p, dcat, hf, hb, conv_w, conv_b, lam, wcat, bcat, bl, s, lc, *, name):
    nch = s // TM
    ll = s - lc

    def body(p_hbm, dcat_hbm, hf_hbm, hb_hbm, gate_s, cw_ref, cb_ref, lam_ref, w_ref, b_ref,
             drgx_ref, dgate_ref, dcw_ref, dcb_ref, dlam_ref, dw_ref, db_ref,
             xpad, dxpad, hf_s, hb_s, dhs, a_f, a_b, lam_f, lam_b, sems):
        h = pl.program_id(0)
        b = pl.program_id(1)
        row0 = b * s
        col = pl.multiple_of(h * HEAD, HEAD)

        def rows_of(ref, r0, nr, c0):
            return ref.at[pl.ds(row0 + r0, nr), pl.ds(c0, HEAD)]

        copies = [
            pltpu.make_async_copy(rows_of(p_hbm, 0, lc, col), xpad.at[pl.ds(8, lc), :], sems.at[0]),
            pltpu.make_async_copy(rows_of(p_hbm, lc, ll, col), xpad.at[pl.ds(16 + lc, ll), :], sems.at[1]),
            pltpu.make_async_copy(rows_of(dcat_hbm, 0, s, col), dhs, sems.at[2]),
            pltpu.make_async_copy(rows_of(hf_hbm, 0, s, col), hf_s.at[pl.ds(8, s), :], sems.at[3]),
            pltpu.make_async_copy(rows_of(hb_hbm, 0, s, col), hb_s.at[pl.ds(8, s), :], sems.at[4]),
        ]
        for cp in copies:
            cp.start()
        _zero_pads(xpad, s, lc)
        _zero_pads(dxpad, s, lc)
        for buf in (hf_s, hb_s):
            buf[0:8, :] = jnp.zeros((8, HEAD), F32)
            buf[8 + s:16 + s, :] = jnp.zeros((8, HEAD), F32)

        @pl.when(b == 0)
        def _():
            dcw_ref[...] = jnp.zeros_like(dcw_ref)
            dcb_ref[...] = jnp.zeros_like(dcb_ref)
            dlam_ref[...] = jnp.zeros_like(dlam_ref)
            dw_ref[...] = jnp.zeros_like(dw_ref)
            db_ref[...] = jnp.zeros_like(db_ref)

        for cp in copies:
            cp.wait()
        lam_v = lam_ref[...]
        spl = _softplus(-lam_v)
        cw = cw_ref[...]

        def conv(c):
            xm1, x0, xp1, xp2 = _conv_window(xpad, c)
            return cw[0:1] * xm1 + cw[1:2] * x0 + cw[2:3] * xp1 + cw[3:4] * xp2 + cb_ref[...]

        def pass_a(c, _):
            rows = pl.ds(pl.multiple_of(c * TM, TM), TM)
            xc = conv(c)
            z = _dot(xc.astype(BF16), w_ref[0]) + b_ref[0]
            for d, a_s in enumerate((a_f, a_b)):
                a_s[rows, :] = _rg_coeffs(z, d, spl, xc)[2]
            g, gg = _gelu_and_grad(gate_s[rows, :])
            drg = dhs[rows, :]
            hrows = pl.ds(pl.multiple_of(c * TM + 8, 8), TM)
            dgate_ref[rows, :] = (drg * (hf_s[hrows, :] + hb_s[hrows, :]) * gg).astype(BF16)
            dhs[rows, :] = drg * g
            return 0

        lax.fori_loop(0, nch, pass_a, 0)

        row = lax.broadcasted_iota(jnp.int32, (TM, HEAD), 0)

        def adj_pair(j, carry):
            cf, cb_ = carry
            rf = pl.ds(pl.multiple_of((nch - 1 - j) * TM, TM), TM)
            rb = pl.ds(pl.multiple_of(jnp.where(j == nch - 1, 0, j + 1) * TM, TM), TM)
            d1, a1 = dhs[rf, :], a_f[rf, :]
            p1, m1 = _chunk_scan(a1, a1 * d1, True)
            m1 = m1 + p1 * cf
            lam_f[rf, :] = d1 + jnp.where(row == TM - 1, cf, pltpu.roll(m1, TM - 1, 0))
            d2, a2 = dhs[rb, :], a_b[rb, :]
            p2, m2 = _chunk_scan(a2, a2 * d2, False)
            m2 = m2 + p2 * cb_
            lam_b[rb, :] = d2 + jnp.where(row == 0, cb_, pltpu.roll(m2, 1, 0))
            return m1[0:1], m2[TM - 1:TM]

        zero = jnp.zeros((1, HEAD), F32)
        lax.fori_loop(0, nch, adj_pair, (zero, zero))

        sig_neg = _sigmoid(-lam_v)
        last_row = lax.broadcasted_iota(jnp.int32, (TM, HEAD), 0) == TM - 1
        hb_first = hb_s[8:9, :]

        def pass_b(c, _):
            rows = pl.ds(pl.multiple_of(c * TM, TM), TM)
            xc = conv(c)
            xcb = xc.astype(BF16)
            z = _dot(xcb, w_ref[0]) + b_ref[0]
            dxc = jnp.zeros((TM, HEAD), F32)
            dzs = []
            n = TM + 16
            hp_f = pltpu.roll(hf_s[pl.ds(pl.multiple_of(c * TM, TM), n), :], 1, 0)[8:8 + TM]
            hp_b = pltpu.roll(hb_s[pl.ds(pl.multiple_of(c * TM, TM), n), :], n - 1, 0)[8:8 + TM]
            hp_b = jnp.where(last_row & (c == 0), 0.0, hp_b)
            hp_b = jnp.where(last_row & (c == nch - 1), hb_first, hp_b)
            for d, (l_s, hp) in enumerate(((lam_f, hp_f), (lam_b, hp_b))):
                r, i, a, mult, e2 = _rg_coeffs(z, d, spl, xc)
                dbt = l_s[rows, :]
                dla = dbt * hp * a - dbt * i * xc * (e2 / mult)
                dlam_ref[0, d:d + 1, :] += jnp.sum(dla * r, axis=0, keepdims=True) * (RG_C * sig_neg[d:d + 1, :])
                dr = dla * (-RG_C * spl[d:d + 1, :])
                di = dbt * mult * xc
                dxc = dxc + dbt * mult * i
                dzs += [dr * r * (1.0 - r), di * i * (1.0 - i)]
            dz = jnp.concatenate(dzs, axis=1)
            dzb = dz.astype(BF16)
            dxc = dxc + _dot(dzb, w_ref[0], _NT)
            dw_ref[0] += _dot(xcb, dzb, _TN)
            db_ref[0] += jnp.sum(dz, axis=0, keepdims=True)
            dcb_ref[0] += jnp.sum(dxc, axis=0, keepdims=True)
            dxpad[pl.ds(_pad_off(c), TM), :] = dxc
            return 0

        lax.fori_loop(0, nch, pass_b, 0)

        def pass_c(c, _):
            rows = pl.ds(pl.multiple_of(c * TM, TM), TM)
            gp1, g0, gm1, gm2 = _windows(dxpad, c, (1, 0, -1, -2))
            drgx_ref[rows, :] = (cw[0:1] * gp1 + cw[1:2] * g0 + cw[2:3] * gm1 + cw[3:4] * gm2).astype(BF16)
            xm1, x0, xp1, xp2 = _conv_window(xpad, c)
            dcw_ref[0] += jnp.concatenate([jnp.sum(g0 * t, axis=0, keepdims=True) for t in (xm1, x0, xp1, xp2)],
                                          axis=0)
            return 0

        lax.fori_loop(0, nch, pass_c, 0)

    seq = pl.BlockSpec((s, HEAD), lambda h, b: (b, h))
    par = lambda r: pl.BlockSpec((r, HEAD), lambda h, b: (0, h))
    acc = lambda r, w: pl.BlockSpec((1, r, w), lambda h, b: (h, 0, 0))
    anyspec = pl.BlockSpec(memory_space=pl.ANY)
    n = bl * s
    big = pltpu.VMEM((s, HEAD), F32)
    return pl.pallas_call(
        body, name=name, grid=(RG_HEADS, bl),
        in_specs=[anyspec, anyspec, anyspec, anyspec, pl.BlockSpec((s, HEAD), lambda h, b: (b, RG_HEADS + h)),
                  par(4), par(1), par(2), acc(HEAD, 4 * HEAD), acc(1, 4 * HEAD)],
        out_specs=(seq, seq, acc(4, HEAD), acc(1, HEAD), acc(2, HEAD), acc(HEAD, 4 * HEAD), acc(1, 4 * HEAD)),
        out_shape=(jax.ShapeDtypeStruct((n, D_MODEL), BF16), jax.ShapeDtypeStruct((n, D_MODEL), BF16),
                   jax.ShapeDtypeStruct((RG_HEADS, 4, HEAD), F32), jax.ShapeDtypeStruct((RG_HEADS, 1, HEAD), F32),
                   jax.ShapeDtypeStruct((RG_HEADS, 2, HEAD), F32),
                   jax.ShapeDtypeStruct((RG_HEADS, HEAD, 4 * HEAD), F32),
                   jax.ShapeDtypeStruct((RG_HEADS, 1, 4 * HEAD), F32)),
        scratch_shapes=[pltpu.VMEM((s + 24, HEAD), F32)] * 2 + [pltpu.VMEM((s + 16, HEAD), F32)] * 2 + [big] * 5
        + [pltpu.SemaphoreType.DMA((5,))],
        compiler_params=_params(2, 52))(p, dcat, hf, hb, p, conv_w, conv_b, lam, wcat, bcat)


def _s5_mats(a_re, a_im, log_dt, b_re, b_im, c_re, c_im):
    t = T_CH
    g = a_re.shape[1]
    dt = jnp.exp(log_dt)[..., None]
    lr, li = a_re * dt, a_im * dt
    steps = jnp.arange(t + 1, dtype=F32)[:, None]
    mag = jnp.exp(lr[:, :, None, :] * steps)
    ang = li[:, :, None, :] * steps
    pr, pi = mag * jnp.cos(ang), mag * jnp.sin(ang)
    xr, xi = pr[:, :, 1] - 1.0, pi[:, :, 1]
    den = a_re * a_re + a_im * a_im
    qr, qi = (xr * a_re + xi * a_im) / den, (xi * a_re - xr * a_im) / den
    btr, bti = b_re.transpose(0, 1, 3, 2), b_im.transpose(0, 1, 3, 2)
    bbr = qr[:, :, None, :] * btr - qi[:, :, None, :] * bti
    bbi = qr[:, :, None, :] * bti + qi[:, :, None, :] * btr
    up, down = slice(0, t), slice(t - 1, None, -1)

    def pow_c(d, sl):
        wr, wi = pr[d][:, sl, None, :], pi[d][:, sl, None, :]
        cr, ci = c_re[d][:, None], c_im[d][:, None]
        return (wr * cr - wi * ci).reshape(g, CW, S5_STATE), (wr * ci + wi * cr).reshape(g, CW, S5_STATE)

    hp = lax.Precision.HIGHEST

    def lag_map(d, sl):
        re, im = pow_c(d, sl)
        return (jnp.einsum('gkp,gmp->gkm', bbr[d], re, precision=hp)
                - jnp.einsum('gkp,gmp->gkm', bbi[d], im, precision=hp))

    z_f, z_b = lag_map(0, up), lag_map(1, down)
    kf = jnp.stack([jnp.pad(z_f, ((0, 0), (0, 0), (S5_GROUP * s, 0)))[:, :, :CW] for s in range(t)], axis=1)
    kb = jnp.stack([jnp.pad(z_b, ((0, 0), (0, 0), (0, S5_GROUP * (t - 1 - s))))[:, :, S5_GROUP * (t - 1 - s):]
                    for s in range(t)], axis=1)
    kcat = (kf + kb).reshape(g, CW, CW)

    def state_in(d, sl):
        wr, wi = pr[d][:, sl, None, :], pi[d][:, sl, None, :]
        br, bi = bbr[d][:, None], bbi[d][:, None]
        return jnp.concatenate([wr * br - wi * bi, wr * bi + wi * br], axis=-1).reshape(g, CW, SW)

    wcat = jnp.concatenate([kcat, state_in(0, down), state_in(1, up)], axis=2)
    of_r, of_i = pow_c(0, slice(1, t + 1))
    ob_r, ob_i = pow_c(1, slice(t, 0, -1))
    mout_t = jnp.concatenate([of_r, -of_i, ob_r, -ob_i], axis=2)
    rows = []
    for d in range(2):
        art, ait = pr[d][:, t], pi[d][:, t]
        rows += [jnp.concatenate([art, art], axis=1).reshape(-1), jnp.concatenate([-ait, ait], axis=1).reshape(-1)]
    return wcat, mout_t, jnp.stack(rows)


def _lane_swap(v):
    return pltpu.roll(v, S5_STATE, 1)


def _grp(g, w):
    return slice(g * w, (g + 1) * w)


def _s5_fwd(u, wcat, mout, a2, ncc, *, name):
    bl, nc, _ = u.shape

    def body(u_ref, w_ref, mo_ref, a_ref, y_ref, sf_ref, sb_ref, vf, vb):
        for g in range(GB):
            zu = _dot(u_ref[:, _grp(g, CW)], w_ref[g])
            y_ref[:, _grp(g, CW)] = zu[:, :CW]
            vf[:, _grp(2 * g, SW)] = zu[:, CW:CW + SW]
            vb[:, _grp(2 * g, SW)] = zu[:, CW + SW:2 * CW]
            vf[:, _grp(2 * g + 1, SW)] = zu[:, 2 * CW:2 * CW + SW]
            vb[:, _grp(2 * g + 1, SW)] = zu[:, 2 * CW + SW:]
        co = [[a_ref[r:r + 1, _grp(g, SW)] for g in range(GB)] for r in range(4)]

        rid = lax.broadcasted_iota(jnp.int32, (8, SW), 0)

        def step8(groups, kf, kb, carry):
            rf, rb = _rows8(kf), _rows8(kb)
            lanes = slice(groups[0] * 2 * SW, (groups[-1] + 1) * 2 * SW)
            vfb, vbb = vf[rf, lanes], vb[rb, lanes]
            st = list(carry)
            of = [jnp.zeros((8, SW), F32)] * len(groups)
            ob = list(of)
            for i in range(8):
                k = 7 - i
                for n, g in enumerate(groups):
                    sf, sfs, sb, sbs = st[4 * n:4 * n + 4]
                    of[n] = jnp.where(rid == i, sf, of[n])
                    ob[n] = jnp.where(rid == k, sb, ob[n])
                    st[4 * n] = co[0][g] * sf + co[1][g] * sfs + vfb[i:i + 1, _grp(2 * n, SW)]
                    st[4 * n + 1] = co[0][g] * sfs - co[1][g] * sf + vfb[i:i + 1, _grp(2 * n + 1, SW)]
                    st[4 * n + 2] = co[2][g] * sb + co[3][g] * sbs + vbb[k:k + 1, _grp(2 * n, SW)]
                    st[4 * n + 3] = co[2][g] * sbs - co[3][g] * sb + vbb[k:k + 1, _grp(2 * n + 1, SW)]
            for n, g in enumerate(groups):
                sf_ref[rf, _grp(g, SW)] = of[n]
                sb_ref[rb, _grp(g, SW)] = ob[n]
            return tuple(st)

        zero = jnp.zeros((1, SW), F32)
        nbc, nb = ncc // 8, nc // 8
        for groups in (tuple(range(0, GB // 2)), tuple(range(GB // 2, GB))):
            carry = lax.fori_loop(0, nbc, lambda j, cr, gs=groups: step8(gs, j, nbc - 1 - j, cr),
                                  (zero,) * (4 * len(groups)))
            lax.fori_loop(nbc, nb, lambda j, cr, gs=groups: step8(gs, j, nb + nbc - 1 - j, cr), carry)
        for g in range(GB):
            st = jnp.concatenate([sf_ref[:, _grp(g, SW)], sb_ref[:, _grp(g, SW)]], axis=1).astype(BF16)
            y_ref[:, _grp(g, CW)] += _dot(st, mo_ref[g, :, 0:CW], _NT)

    blk = lambda w: pl.BlockSpec((None, nc, GB * w), lambda b, gb: (b, 0, gb))
    return pl.pallas_call(
        body, name=name, grid=(bl, S5_GROUPS // GB),
        in_specs=[blk(CW), pl.BlockSpec((GB, CW, 3 * CW), lambda b, gb: (gb, 0, 0)),
                  pl.BlockSpec((GB, CW, 2 * CW), lambda b, gb: (gb, 0, 0)),
                  pl.BlockSpec((4, GB * SW), lambda b, gb: (0, gb))],
        out_specs=(blk(CW), blk(SW), blk(SW)),
        out_shape=(jax.ShapeDtypeStruct(u.shape, F32), jax.ShapeDtypeStruct((bl, nc, S5_GROUPS * SW), F32),
                   jax.ShapeDtypeStruct((bl, nc, S5_GROUPS * SW), F32)),
        scratch_shapes=[pltpu.VMEM((nc, GB * 2 * SW), F32)] * 2, compiler_params=_params(2))(u, wcat, mout, a2)


def _s5_bwd(dy, u, sf, sb, wcat, mout, a2, ncc, *, name):
    bl, nc, _ = u.shape

    def body(dy_ref, u_ref, sf_ref, sb_ref, w_ref, mo_ref, a_ref, du_ref, dw_ref, dmo_ref, dacc_ref, gsf, gsb, dvf, dvb):
        b = pl.program_id(1)

        @pl.when(b == 0)
        def _():
            dw_ref[...] = jnp.zeros_like(dw_ref)
            dmo_ref[...] = jnp.zeros_like(dmo_ref)
            dacc_ref[...] = jnp.zeros_like(dacc_ref)

        for g in range(GB):
            ds = _dot(dy_ref[:, _grp(g, CW)], mo_ref[g])
            gsf[:, _grp(2 * g, SW)] = ds[:, :SW]
            gsb[:, _grp(2 * g, SW)] = ds[:, SW:CW]
            gsf[:, _grp(2 * g + 1, SW)] = ds[:, CW:CW + SW]
            gsb[:, _grp(2 * g + 1, SW)] = ds[:, CW + SW:]
        co = [[a_ref[r:r + 1, _grp(g, SW)] for g in range(GB)] for r in range(4)]

        rid = lax.broadcasted_iota(jnp.int32, (8, SW), 0)

        def step8(groups, kf, kb, carry):
            rf, rb = _rows8(kf), _rows8(kb)
            lanes = slice(groups[0] * 2 * SW, (groups[-1] + 1) * 2 * SW)
            gfb, gbb = gsf[rf, lanes], gsb[rb, lanes]
            st = list(carry)
            of = [jnp.zeros((8, SW), F32)] * len(groups)
            ob = list(of)
            for i in range(8):
                k = 7 - i
                for n, g in enumerate(groups):
                    gf, gfs, gb_, gbs = st[4 * n:4 * n + 4]
                    of[n] = jnp.where(rid == k, gf, of[n])
                    ob[n] = jnp.where(rid == i, gb_, ob[n])
                    st[4 * n] = gfb[k:k + 1, _grp(2 * n, SW)] + co[0][g] * gf - co[1][g] * gfs
                    st[4 * n + 1] = gfb[k:k + 1, _grp(2 * n + 1, SW)] + co[0][g] * gfs + co[1][g] * gf
                    st[4 * n + 2] = gbb[i:i + 1, _grp(2 * n, SW)] + co[2][g] * gb_ - co[3][g] * gbs
                    st[4 * n + 3] = gbb[i:i + 1, _grp(2 * n + 1, SW)] + co[2][g] * gbs + co[3][g] * gb_
            for n, g in enumerate(groups):
                dvf[rf, _grp(g, SW)] = of[n]
                dvb[rb, _grp(g, SW)] = ob[n]
            return tuple(st)

        zero = jnp.zeros((1, SW), F32)
        nbc, nb = ncc // 8, nc // 8
        for groups in (tuple(range(0, GB // 2)), tuple(range(GB // 2, GB))):
            carry = lax.fori_loop(0, nb - nbc, lambda j, cr, gs=groups: step8(gs, nb - 1 - j, nbc + j, cr),
                                  (zero,) * (4 * len(groups)))
            lax.fori_loop(0, nbc, lambda j, cr, gs=groups: step8(gs, nbc - 1 - j, j, cr), carry)
        for g in range(GB):
            dyg = dy_ref[:, _grp(g, CW)]
            dvf_g, dvb_g = dvf[:, _grp(g, SW)], dvb[:, _grp(g, SW)]
            sf_g, sb_g = sf_ref[:, _grp(g, SW)], sb_ref[:, _grp(g, SW)]
            dz = jnp.concatenate([dyg, dvf_g.astype(BF16), dvb_g.astype(BF16)], axis=1)
            du_ref[:, _grp(g, CW)] = _dot(dz, w_ref[g, :, 0:2 * CW], _NT)
            dw_ref[g] += _dot(u_ref[:, _grp(g, CW)], dz, _TN)
            st = jnp.concatenate([sf_g, sb_g], axis=1).astype(BF16)
            dmo_ref[g] += _dot(dyg, st, _TN)
            dacc_ref[:, _grp(g, SW)] += jnp.concatenate(
                [jnp.sum(dvf_g * sf_g, axis=0, keepdims=True), jnp.sum(dvf_g * _lane_swap(sf_g), axis=0, keepdims=True),
                 jnp.sum(dvb_g * sb_g, axis=0, keepdims=True), jnp.sum(dvb_g * _lane_swap(sb_g), axis=0, keepdims=True)],
                axis=0)

    blk = lambda w: pl.BlockSpec((None, nc, GB * w), lambda gb, b: (b, 0, gb))
    wspec = lambda mult: pl.BlockSpec((GB, CW, mult * CW), lambda gb, b: (gb, 0, 0))
    aspec = pl.BlockSpec((4, GB * SW), lambda gb, b: (0, gb))
    return pl.pallas_call(
        body, name=name, grid=(S5_GROUPS // GB, bl),
        in_specs=[blk(CW), blk(CW), blk(SW), blk(SW), wspec(3), wspec(2), aspec],
        out_specs=(blk(CW), wspec(2), wspec(1), aspec),
        out_shape=(jax.ShapeDtypeStruct(u.shape, F32), jax.ShapeDtypeStruct((S5_GROUPS, CW, 2 * CW), F32),
                   jax.ShapeDtypeStruct((S5_GROUPS, CW, CW), F32), jax.ShapeDtypeStruct(a2.shape, F32)),
        scratch_shapes=[pltpu.VMEM((nc, GB * 2 * SW), F32)] * 2 + [pltpu.VMEM((nc, GB * SW), F32)] * 2,
        compiler_params=_params(2, 48))(dy, u, sf, sb, wcat, mout, a2)


def _lane_slot():
    return lax.broadcasted_iota(jnp.int32, (GRID_W, HEAD), 1) // S5_GROUP


def _lat_to_chunks(src, col0, bl, s, lc, *, name):
    nrh = (s - lc) // GRID_W // T_CH

    def body(x_ref, o_ref):
        slot = _lane_slot()

        def one(rh, _):
            tiles = [x_ref[pl.ds(pl.multiple_of(lc + (rh * T_CH + t) * GRID_W, GRID_W), GRID_W), :]
                     for t in range(T_CH)]
            for q in range(HEAD // S5_GROUP):
                for j in range(CW // HEAD):
                    acc = jnp.zeros((GRID_W, HEAD), F32)
                    for m in range(HEAD // S5_GROUP):
                        shift = ((m - q) * S5_GROUP) % HEAD
                        v = tiles[8 * j + m]
                        acc = jnp.where(slot == m, v if shift == 0 else pltpu.roll(v, shift, 1), acc)
                    o_ref[rh, :, q * CW + j * HEAD:q * CW + (j + 1) * HEAD] = acc.astype(BF16)
            return 0

        lax.fori_loop(0, nrh, one, 0)

    return pl.pallas_call(
        body, name=name, grid=(bl, S5_GROUPS // GB),
        in_specs=[pl.BlockSpec((s, HEAD), lambda b, gb: (b, col0 + gb))],
        out_specs=pl.BlockSpec((None, nrh, GRID_W, GB * CW), lambda b, gb: (b, 0, 0, gb)),
        out_shape=jax.ShapeDtypeStruct((bl, nrh, GRID_W, S5_GROUPS * CW), BF16),
        compiler_params=_params(2))(src)


def _lat_from_chunks(v4, ctx_nat, bl, s, lc, *, name):
    nrh = v4.shape[1]

    def body(v_ref, c_ref, o_ref):
        o_ref[0:lc, :] = c_ref[...]
        slot = _lane_slot()

        def one(rh, _):
            for t in range(T_CH):
                j, m = t // 8, t % 8
                acc = jnp.zeros((GRID_W, HEAD), F32)
                for q in range(HEAD // S5_GROUP):
                    shift = ((q - m) * S5_GROUP) % HEAD
                    v = v_ref[rh, :, q * CW + j * HEAD:q * CW + (j + 1) * HEAD]
                    acc = jnp.where(slot == q, v if shift == 0 else pltpu.roll(v, shift, 1), acc)
                o_ref[pl.ds(pl.multiple_of(lc + (rh * T_CH + t) * GRID_W, GRID_W), GRID_W), :] = acc
            return 0

        lax.fori_loop(0, nrh, one, 0)

    return pl.pallas_call(
        body, name=name, grid=(bl, S5_GROUPS // GB),
        in_specs=[pl.BlockSpec((None, nrh, GRID_W, GB * CW), lambda b, gb: (b, 0, 0, gb)),
                  pl.BlockSpec((lc, HEAD), lambda b, gb: (b, gb))],
        out_specs=pl.BlockSpec((s, HEAD), lambda b, gb: (b, gb)),
        out_shape=jax.ShapeDtypeStruct((bl * s, D_MODEL), F32), compiler_params=_params(2))(v4, ctx_nat)


def _to_chunks(src, col0, bl, s, lc, *, name):
    ll = s - lc
    lat = _lat_to_chunks(src, col0, bl, s, lc, name=name)
    lat = lat.transpose(0, 2, 1, 3).reshape(bl, ll // T_CH, S5_GROUPS * CW)
    ctx = src.reshape(bl, s, -1)[:, :lc, col0 * HEAD:col0 * HEAD + D_MODEL].astype(BF16)
    ctx = ctx.reshape(bl, lc // T_CH, T_CH, S5_GROUPS, S5_GROUP).transpose(0, 1, 3, 2, 4)
    return jnp.concatenate([ctx.reshape(bl, lc // T_CH, -1), lat], axis=1)


def _from_chunks(v, bl, s, lc, *, name):
    ncc = lc // T_CH
    nrh = (s - lc) // GRID_W // T_CH
    ctx = v[:, :ncc].reshape(bl, ncc, S5_GROUPS, T_CH, S5_GROUP).transpose(0, 1, 3, 2, 4).reshape(bl * lc, D_MODEL)
    lat = v[:, ncc:].reshape(bl, GRID_W, nrh, S5_GROUPS * CW).transpose(0, 2, 1, 3)
    return _lat_from_chunks(lat, ctx, bl, s, lc, name=name)


_ANY = pl.BlockSpec(memory_space=pl.ANY)


def _xy_peers():
    x, y, c = lax.axis_index("x"), lax.axis_index("y"), lax.axis_index("c")
    return x, y, c, [(1 - x, y), (x, 1 - y), (1 - x, 1 - y)]


def _all_gather_xy(shard, *, name):
    def body(x_ref, out_ref, send_sems, recv_sems, local_sem):
        x, y, c, peers = _xy_peers()
        me = 2 * x + y
        mine = pltpu.make_async_copy(x_ref, out_ref.at[me], local_sem)
        mine.start()

        def copy(k, px, py, slot):
            return pltpu.make_async_remote_copy(src_ref=x_ref, dst_ref=out_ref.at[slot], send_sem=send_sems.at[k],
                                                recv_sem=recv_sems.at[k], device_id=(px, py, c), device_id_type=MESH)

        sends = [copy(k, px, py, me) for k, (px, py) in enumerate(peers)]
        for cp in sends:
            cp.start()
        for k, (px, py) in enumerate(peers):
            copy(k, px, py, 2 * px + py).wait_recv()
        for cp in sends:
            cp.wait_send()
        mine.wait()

    return pl.pallas_call(body, name=name, in_specs=[_ANY], out_specs=_ANY,
                          out_shape=jax.ShapeDtypeStruct((4,) + shard.shape, shard.dtype),
                          scratch_shapes=[pltpu.SemaphoreType.DMA((3,)), pltpu.SemaphoreType.DMA((3,)),
                                          pltpu.SemaphoreType.DMA])(shard)


def _scatter_xy(parts, *, name):
    def body(p_ref, out_ref, send_sems, recv_sems, local_sem):
        x, y, c, peers = _xy_peers()
        mine = pltpu.make_async_copy(p_ref.at[2 * x + y], out_ref.at[0], local_sem)
        mine.start()

        def copy(k, px, py):
            return pltpu.make_async_remote_copy(src_ref=p_ref.at[2 * px + py], dst_ref=out_ref.at[1 + k],
                                                send_sem=send_sems.at[k], recv_sem=recv_sems.at[k],
                                                device_id=(px, py, c), device_id_type=MESH)

        sends = [copy(k, px, py) for k, (px, py) in enumerate(peers)]
        for cp in sends:
            cp.start()
        for cp in sends:
            cp.wait_recv()
        for cp in sends:
            cp.wait_send()
        mine.wait()

    return pl.pallas_call(body, name=name, in_specs=[_ANY], out_specs=_ANY,
                          out_shape=jax.ShapeDtypeStruct(parts.shape, parts.dtype),
                          scratch_shapes=[pltpu.SemaphoreType.DMA((3,)), pltpu.SemaphoreType.DMA((3,)),
                                          pltpu.SemaphoreType.DMA])(parts)


def _swap_sibling(v, *, name):
    def body(v_ref, out_ref, send_sem, recv_sem):
        x, y, c = lax.axis_index("x"), lax.axis_index("y"), lax.axis_index("c")
        cp = pltpu.make_async_remote_copy(src_ref=v_ref, dst_ref=out_ref, send_sem=send_sem, recv_sem=recv_sem,
                                          device_id=(x, y, 1 - c), device_id_type=MESH)
        cp.start()
        cp.wait()

    return pl.pallas_call(body, name=name, in_specs=[_ANY], out_specs=_ANY,
                          out_shape=jax.ShapeDtypeStruct(v.shape, v.dtype),
                          scratch_shapes=[pltpu.SemaphoreType.DMA, pltpu.SemaphoreType.DMA])(v)


BIG_COLS = {'ada_w': True, 'w_in': True, 'mlp_w1': True, 's5_glu_w': False, 'w_out': False, 'mlp_w2': False}
BIG = list(BIG_COLS)


def _block(ref2d, j, cols, size):
    if cols:
        return ref2d.at[:, pl.ds(pl.multiple_of(j * size, 128), size)]
    return ref2d.at[pl.ds(pl.multiple_of(j * size, 8), size), :]


def _shard_size(shape, cols):
    return shape[-1] if cols else shape[-2]


def _cast_into_full(shard, cols, my_j, *, name):
    _, r, c = shard.shape
    tr, tc = _tile(r, (256,)), _tile(c, (1024, 768, 512))

    def body(j_ref, x_ref, o_ref):
        o_ref[...] = x_ref[...].astype(BF16)

    if cols:
        out_spec = pl.BlockSpec((None, tr, tc), lambda l, i, j, j_ref: (l, i, j_ref[0] * (c // tc) + j))
    else:
        out_spec = pl.BlockSpec((None, tr, tc), lambda l, i, j, j_ref: (l, j_ref[0] * (r // tr) + i, j))
    return pl.pallas_call(
        body, name=name,
        grid_spec=pltpu.PrefetchScalarGridSpec(
            num_scalar_prefetch=1, grid=(DEPTH, r // tr, c // tc),
            in_specs=[pl.BlockSpec((None, tr, tc), lambda l, i, j, j_ref: (l, i, j))], out_specs=out_spec),
        out_shape=jax.ShapeDtypeStruct((DEPTH, r, 4 * c) if cols else (DEPTH, 4 * r, c), BF16),
        compiler_params=_params(3))(my_j, shard)


def _gather_big(fulls, cols, *, name):
    n = len(fulls)

    def body(*refs):
        outs = refs[n:2 * n]
        ici_send, ici_recv, d2d_send, d2d_recv = refs[2 * n:]
        x, y, c, peers = _xy_peers()
        me = 2 * x + y

        def blk(w, layer, j):
            shape = outs[w].shape
            return _block(outs[w].at[layer], j, cols[w], (shape[2] if cols[w] else shape[1]) // 4)

        def ici(w, k, px, py, j):
            return pltpu.make_async_remote_copy(src_ref=blk(w, c, j), dst_ref=blk(w, c, j),
                                                send_sem=ici_send.at[3 * w + k], recv_sem=ici_recv.at[3 * w + k],
                                                device_id=(px, py, c), device_id_type=MESH)

        def d2d(w, k, j, layer):
            return pltpu.make_async_remote_copy(src_ref=blk(w, layer, j), dst_ref=blk(w, layer, j),
                                                send_sem=d2d_send.at[3 * w + k], recv_sem=d2d_recv.at[3 * w + k],
                                                device_id=(x, y, 1 - c), device_id_type=MESH)

        started = [ici(w, k, px, py, me) for w in range(n) for k, (px, py) in enumerate(peers)]
        for cp in started:
            cp.start()
        passed = []
        for w in range(n):
            for k, (px, py) in enumerate(peers):
                ici(w, k, px, py, 2 * px + py).wait_recv()
                passed.append(d2d(w, k, 2 * px + py, c))
                passed[-1].start()
        for w in range(n):
            for k, (px, py) in enumerate(peers):
                d2d(w, k, 2 * px + py, 1 - c).wait_recv()
        for cp in started + passed:
            cp.wait_send()

    return pl.pallas_call(
        body, name=name, in_specs=[_ANY] * n, out_specs=[_ANY] * n,
        out_shape=[jax.ShapeDtypeStruct(f.shape, f.dtype) for f in fulls],
        input_output_aliases={w: w for w in range(n)},
        scratch_shapes=[pltpu.SemaphoreType.DMA((3 * n,))] * 4)(*fulls)


def _sibling_partials(gbufs, *, name):
    n = len(gbufs)

    def body(*refs):
        ins, outs, send, recv = refs[:n], refs[n:2 * n], refs[2 * n], refs[2 * n + 1]
        x, y, c = lax.axis_index("x"), lax.axis_index("y"), lax.axis_index("c")
        cps = [pltpu.make_async_remote_copy(src_ref=ins[w].at[1 - c], dst_ref=outs[w], send_sem=send.at[w],
                                            recv_sem=recv.at[w], device_id=(x, y, 1 - c), device_id_type=MESH)
               for w in range(n)]
        for cp in cps:
            cp.start()
        for cp in cps:
            cp.wait()

    return pl.pallas_call(body, name=name, in_specs=[_ANY] * n, out_specs=[_ANY] * n,
                          out_shape=[jax.ShapeDtypeStruct(g.shape[1:], g.dtype) for g in gbufs],
                          scratch_shapes=[pltpu.SemaphoreType.DMA((n,))] * 2)(*gbufs)


def _chip_sum(gbuf, other, my_c, *, name):
    _, k, n = gbuf.shape
    tr, tc = _tile(k, (512,)), _tile(n, (1024,))

    def body(c_ref, a_ref, b_ref, o_ref):
        o_ref[...] = (a_ref[...] + b_ref[...]).astype(BF16)

    spec = pl.BlockSpec((tr, tc), lambda i, j, c_ref: (i, j))
    return pl.pallas_call(
        body, name=name,
        grid_spec=pltpu.PrefetchScalarGridSpec(
            num_scalar_prefetch=1, grid=(k // tr, n // tc),
            in_specs=[pl.BlockSpec((None, tr, tc), lambda i, j, c_ref: (c_ref[0], i, j)), spec], out_specs=spec),
        out_shape=jax.ShapeDtypeStruct((k, n), BF16), compiler_params=_params(2))(my_c, gbuf, other)


def _scatter_big(sums, cols, *, name):
    n = len(sums)

    def shard(s, cf):
        return (s.shape[0], s.shape[1] // 4) if cf else (s.shape[0] // 4, s.shape[1])

    def body(*refs):
        ins, outs, send, recv = refs[:n], refs[n:2 * n], refs[2 * n], refs[2 * n + 1]
        x, y, c, peers = _xy_peers()
        cps = []
        for w in range(n):
            size = _shard_size(shard(ins[w], cols[w]), cols[w])
            for k, (px, py) in enumerate(peers):
                cps.append(pltpu.make_async_remote_copy(
                    src_ref=_block(ins[w], 2 * px + py, cols[w], size), dst_ref=outs[w].at[k],
                    send_sem=send.at[3 * w + k], recv_sem=recv.at[3 * w + k], device_id=(px, py, c),
                    device_id_type=MESH))
        for cp in cps:
            cp.start()
        for cp in cps:
            cp.wait()

    return pl.pallas_call(body, name=name, in_specs=[_ANY] * n, out_specs=[_ANY] * n,
                          out_shape=[jax.ShapeDtypeStruct((3,) + shard(s, cf), s.dtype) for s, cf in zip(sums, cols)],
                          scratch_shapes=[pltpu.SemaphoreType.DMA((3 * n,))] * 2)(*sums)


def _block_sum(own, got, cols, my_j, my_c, *, name):
    _, r, c = got.shape
    tr, tc = _tile(r, (256,)), _tile(c, (1024, 768, 512))

    def body(j_ref, c_ref, a_ref, g_ref, o_ref):
        o_ref[...] = ((a_ref[...].astype(F32) + g_ref[0].astype(F32)) + g_ref[1].astype(F32)) + g_ref[2].astype(F32)

    if cols:
        own_spec = pl.BlockSpec((tr, tc), lambda i, j, j_ref, c_ref: (i, j_ref[0] * (c // tc) + j))
    else:
        own_spec = pl.BlockSpec((tr, tc), lambda i, j, j_ref, c_ref: (j_ref[0] * (r // tr) + i, j))
    return pl.pallas_call(
        body, name=name,
        grid_spec=pltpu.PrefetchScalarGridSpec(
            num_scalar_prefetch=2, grid=(r // tr, c // tc),
            in_specs=[own_spec, pl.BlockSpec((3, tr, tc), lambda i, j, j_ref, c_ref: (0, i, j))],
            out_specs=pl.BlockSpec((None, tr, tc), lambda i, j, j_ref, c_ref: (c_ref[0], i, j))),
        out_shape=jax.ShapeDtypeStruct((DEPTH, r, c), F32), compiler_params=_params(2))(my_j, my_c, own, got)


def _share_final(bufs, *, name):
    n = len(bufs)

    def body(*refs):
        outs, send, recv = refs[n:2 * n], refs[2 * n], refs[2 * n + 1]
        x, y, c = lax.axis_index("x"), lax.axis_index("y"), lax.axis_index("c")

        def copy(w, slot):
            return pltpu.make_async_remote_copy(src_ref=outs[w].at[slot], dst_ref=outs[w].at[slot],
                                                send_sem=send.at[w], recv_sem=recv.at[w],
                                                device_id=(x, y, 1 - c), device_id_type=MESH)

        away = [copy(w, c) for w in range(n)]
        for cp in away:
            cp.start()
        for w in range(n):
            copy(w, 1 - c).wait_recv()
        for cp in away:
            cp.wait_send()

    return pl.pallas_call(body, name=name, in_specs=[_ANY] * n, out_specs=[_ANY] * n,
                          out_shape=[jax.ShapeDtypeStruct(b.shape, b.dtype) for b in bufs],
                          input_output_aliases={w: w for w in range(n)},
                          scratch_shapes=[pltpu.SemaphoreType.DMA((n,))] * 2)(*bufs)


def _adamw_native(w, g, m, v, *, name):
    r, c = w.shape
    tr = _tile(r, (256, 128, 64, 32, 16, 8))
    spec = pl.BlockSpec((tr, c), lambda i: (i, 0))
    c1 = 1.0 / (1.0 - ADAM_B1 ** ADAM_STEP)
    c2 = 1.0 / (1.0 - ADAM_B2 ** ADAM_STEP)

    def body(w_ref, g_ref, m_ref, v_ref, d_ref, nm_ref, nv_ref):
        g_t = g_ref[...]
        nm = ADAM_B1 * m_ref[...] + (1.0 - ADAM_B1) * g_t
        nv = ADAM_B2 * v_ref[...] + (1.0 - ADAM_B2) * (g_t * g_t)
        nm_ref[...] = nm
        nv_ref[...] = nv
        d_ref[...] = -ADAM_LR * ((nm * c1) / (jnp.sqrt(nv * c2) + ADAM_EPS) + ADAM_WD * w_ref[...])

    shp = jax.ShapeDtypeStruct((r, c), F32)
    return pl.pallas_call(body, name=name, grid=(r // tr,), in_specs=[spec] * 4, out_specs=(spec,) * 3,
                          out_shape=(shp,) * 3, compiler_params=_params(1))(w, g, m, v)


def _flat_tile(r):
    return _tile(r, (512, 256, 128, 64, 32, 16, 8))


def _sum4(parts, *, name):
    r = parts.shape[1]
    tr = _flat_tile(r)

    def body(p_ref, o_ref):
        o_ref[...] = ((p_ref[0] + p_ref[1]) + p_ref[2]) + p_ref[3]

    return pl.pallas_call(body, name=name, grid=(r // tr,),
                          in_specs=[pl.BlockSpec((4, tr, LANES), lambda i: (0, i, 0))],
                          out_specs=pl.BlockSpec((tr, LANES), lambda i: (i, 0)),
                          out_shape=jax.ShapeDtypeStruct((r, LANES), F32), compiler_params=_params(1))(parts)


def _add2(a, b, *, name):
    r = a.shape[0]
    tr = _flat_tile(r)
    spec = pl.BlockSpec((tr, LANES), lambda i: (i, 0))

    def body(a_ref, b_ref, o_ref):
        o_ref[...] = a_ref[...] + b_ref[...]

    return pl.pallas_call(body, name=name, grid=(r // tr,), in_specs=[spec, spec], out_specs=spec,
                          out_shape=jax.ShapeDtypeStruct((r, LANES), F32), compiler_params=_params(1))(a, b)


def _adamw(w, ga, gb, m, v, *, name):
    r = w.shape[0]
    tr = _flat_tile(r)
    spec = pl.BlockSpec((tr, LANES), lambda i: (i, 0))
    two = gb is not None
    c1 = 1.0 / (1.0 - ADAM_B1 ** ADAM_STEP)
    c2 = 1.0 / (1.0 - ADAM_B2 ** ADAM_STEP)

    def body(*refs):
        w_ref, ga_ref = refs[0], refs[1]
        m_ref, v_ref, g_ref, d_ref, nm_ref, nv_ref = refs[2 + two:]
        g = ga_ref[...] + refs[2][...] if two else ga_ref[...]
        nm = ADAM_B1 * m_ref[...] + (1.0 - ADAM_B1) * g
        nv = ADAM_B2 * v_ref[...] + (1.0 - ADAM_B2) * (g * g)
        g_ref[...] = g
        nm_ref[...] = nm
        nv_ref[...] = nv
        d_ref[...] = -ADAM_LR * ((nm * c1) / (jnp.sqrt(nv * c2) + ADAM_EPS) + ADAM_WD * w_ref[...])

    args = [w, ga] + ([gb] if two else []) + [m, v]
    shp = jax.ShapeDtypeStruct((r, LANES), F32)
    return pl.pallas_call(body, name=name, grid=(r // tr,), in_specs=[spec] * len(args), out_specs=(spec,) * 4,
                          out_shape=(shp,) * 4, compiler_params=_params(1))(*args)


def _pack(arrs, dtype=F32):
    flat = jnp.concatenate([a.astype(dtype).reshape(-1) for a in arrs])
    pad = (-flat.shape[0]) % (32 * LANES)
    return jnp.pad(flat, (0, pad)).reshape(-1, LANES)


def _unpack(buf, shapes):
    flat = buf.reshape(-1)
    out, off = [], 0
    for shp in shapes:
        sz = math.prod(shp)
        out.append(flat[off:off + sz].reshape(shp))
        off += sz
    return out


def _stack_shards(full, axis):
    shp = full.shape
    return jnp.moveaxis(full.reshape(shp[:axis] + (4, shp[axis] // 4) + shp[axis + 1:]), axis, 0)


def _unstack_shards(st, axis):
    v = jnp.moveaxis(st, 0, axis)
    shp = v.shape
    return v.reshape(shp[:axis] + (shp[axis] * shp[axis + 1],) + shp[axis + 2:])


def _layer_weights(w, l):
    lw = {n: (w[n], l) if n in BIG_COLS else w[n][l] for n in w}
    lw['wcat'] = jnp.concatenate([lw['rg_wa'][0], lw['rg_wi'][0], lw['rg_wa'][1], lw['rg_wi'][1]],
                                 axis=-1).astype(BF16)
    ba, bi = lw['rg_ba'].reshape(2, RG_HEADS, HEAD), lw['rg_bi'].reshape(2, RG_HEADS, HEAD)
    lw['bcat'] = jnp.concatenate([ba[0], bi[0], ba[1], bi[1]], axis=-1)[:, None, :]
    s5_names = ['s5_a_re', 's5_a_im', 's5_log_dt', 's5_b_re', 's5_b_im', 's5_c_re', 's5_c_im']
    (wcat, mout, a2), lw['s5_vjp'] = jax.vjp(_s5_mats, *[lw[n] for n in s5_names])

    def with_swapped(m, first):
        part = m[:, :, first:]
        swapped = part.reshape(part.shape[0], CW, -1, 2, S5_STATE)[:, :, :, ::-1].reshape(part.shape)
        return jnp.concatenate([m, swapped], axis=2).astype(BF16)

    lw['s5_wcat'], lw['s5_mout'], lw['s5_a2'] = with_swapped(wcat, CW), with_swapped(mout, 0), a2
    for n in ('conv_b', 's5_d', 's5_glu_b', 'b_out', 'mlp_b1', 'mlp_b2', 'ln1_g', 'ln1_b', 'ln2_g', 'ln2_b'):
        lw[n] = lw[n][None, :]
    return lw


def _layer_fwd(l, x0, modall, lw, dims):
    bl, s, lc, tps = dims
    ll = s - lc
    tag = f"l{l}_"
    sv = {'x0': x0}
    sv['u1'] = _modulate(x0, modall, 0, 1, tps, name=tag + "mod1")
    sv['p'] = p = _mm_nn(sv['u1'], lw['w_in'], name=tag + "w_in")
    rg, sv['hf'], sv['hb'] = _rg_fwd(p, lw['conv_w'], lw['conv_b'], lw['rg_lambda'], lw['wcat'], lw['bcat'],
                                     bl, s, lc, name=tag + "rg_fwd")
    sv['u_ch'] = _to_chunks(p, 2 * D_MODEL // HEAD, bl, s, lc, name=tag + "u_chunks")
    y_ch, sv['sf'], sv['sb'] = _s5_fwd(sv['u_ch'], lw['s5_wcat'], lw['s5_mout'], lw['s5_a2'], lc // T_CH,
                                       name=tag + "s5_fwd")
    sv['y'] = _from_chunks(y_ch, bl, s, lc, name=tag + "y_rows")
    s5 = _glu_fwd(sv['y'], p, lw['s5_d'], lw['s5_glu_w'], lw['s5_glu_b'], name=tag + "glu_fwd")
    sv['cat'] = jnp.concatenate([rg, s5], axis=1)
    sv['m'] = _mm_nn(sv['cat'], lw['w_out'], lw['b_out'], name=tag + "w_out")
    sv['x1'] = _resid_ln(x0, sv['m'], modall, 2, lw['ln1_g'], lw['ln1_b'], tps, name=tag + "ln1")
    sv['u2'] = _modulate(sv['x1'], modall, 3, 4, tps, name=tag + "mod2")
    sv['a'], sv['h'] = _mm_nn(sv['u2'], lw['mlp_w1'], lw['mlp_b1'], relu2=True, name=tag + "mlp1")
    sv['f'] = _mm_nn(sv['a'], lw['mlp_w2'], lw['mlp_b2'], name=tag + "mlp2")
    x2 = _resid_ln(sv['x1'], sv['f'], modall, 5, lw['ln2_g'], lw['ln2_b'], tps, name=tag + "ln2")
    return x2, sv


def _layer_bwd(l, dx2, modall, lw, sv, dims, gbufs):
    bl, s, lc, tps = dims
    ll = s - lc
    tag = f"l{l}_"
    g = {}

    def big_grad(n, a_mat, b_mat, label):
        gbufs[n] = _mm_tn(a_mat, b_mat, name=tag + label, layer=l, into=gbufs.get(n))
    dx1a, df, db2, g['ln2_g'], g['ln2_b'], dg2 = _resid_ln_bwd(sv['x1'], sv['f'], modall, 5, lw['ln2_g'], dx2, tps,
                                                              name=tag + "ln2_bwd")
    g['mlp_b2'] = db2
    big_grad('mlp_w2', sv['a'], df, "mlp2_dw")
    dh = _mm_nt(df, lw['mlp_w2'], sv['h'], name=tag + "mlp2_dx")
    g['mlp_b1'] = _colsum(dh, name=tag + "mlp1_db")
    big_grad('mlp_w1', sv['u2'], dh, "mlp1_dw")
    du2 = _mm_nt(dh, lw['mlp_w1'], name=tag + "mlp1_dx")
    dx1, dsc2, dsh2 = _modulate_bwd(du2, sv['x1'], modall, 4, dx1a, tps, name=tag + "mod2_bwd")
    dx0a, dm, g['b_out'], g['ln1_g'], g['ln1_b'], dg1 = _resid_ln_bwd(sv['x0'], sv['m'], modall, 2, lw['ln1_g'], dx1,
                                                                     tps, name=tag + "ln1_bwd")
    big_grad('w_out', sv['cat'], dm, "w_out_dw")
    dcat = _mm_nt(dm, lw['w_out'], name=tag + "w_out_dx")
    dy, dskip, g_bf, dz_bf, g['s5_d'], g['s5_glu_b'] = _glu_bwd(dcat, sv['y'], sv['p'], lw['s5_d'], lw['s5_glu_w'],
                                                                lw['s5_glu_b'], name=tag + "glu_bwd")
    big_grad('s5_glu_w', g_bf, dz_bf, "glu_dw")
    dy_ch = _to_chunks(dy, 0, bl, s, lc, name=tag + "dy_chunks")
    du_ch, dwcat, dmout, dacc = _s5_bwd(dy_ch, sv['u_ch'], sv['sf'], sv['sb'], lw['s5_wcat'],
                                        lw['s5_mout'], lw['s5_a2'], lc // T_CH, name=tag + "s5_bwd")
    s5g = lw['s5_vjp']((dwcat, dmout, dacc))
    for n, v in zip(['s5_a_re', 's5_a_im', 's5_log_dt', 's5_b_re', 's5_b_im', 's5_c_re', 's5_c_im'], s5g):
        g[n] = v
    ds5u = _add_cast(_from_chunks(du_ch, bl, s, lc, name=tag + "du_rows"), dskip, name=tag + "ds5u")
    drgx, dgate, dcw, dcb, dlam, dwc, dbc = _rg_bwd(sv['p'], dcat, sv['hf'], sv['hb'], lw['conv_w'], lw['conv_b'],
                                                    lw['rg_lambda'], lw['wcat'], lw['bcat'], bl, s, lc,
                                                    name=tag + "rg_bwd")
    g['conv_w'] = dcw.transpose(1, 0, 2).reshape(4, D_MODEL)
    g['conv_b'] = dcb.reshape(D_MODEL)
    g['rg_lambda'] = dlam.transpose(1, 0, 2).reshape(2, D_MODEL)
    g['rg_wa'] = jnp.stack([dwc[:, :, 0:HEAD], dwc[:, :, 2 * HEAD:3 * HEAD]])
    g['rg_wi'] = jnp.stack([dwc[:, :, HEAD:2 * HEAD], dwc[:, :, 3 * HEAD:]])
    dbc = dbc.reshape(RG_HEADS, 4, HEAD)
    g['rg_ba'] = jnp.stack([dbc[:, 0], dbc[:, 2]]).reshape(2, D_MODEL)
    g['rg_bi'] = jnp.stack([dbc[:, 1], dbc[:, 3]]).reshape(2, D_MODEL)
    dp = jnp.concatenate([drgx, dgate, ds5u], axis=1)
    big_grad('w_in', sv['u1'], dp, "w_in_dw")
    du1 = _mm_nt(dp, lw['w_in'], name=tag + "w_in_dx")
    dx0, dsc1, dsh1 = _modulate_bwd(du1, sv['x0'], modall, 1, dx0a, tps, name=tag + "mod1_bwd")
    dmod = jnp.concatenate([dsh1, dsc1, dg1, dsh2, dsc2, dg2], axis=1)
    return dx0, g, dmod


def _kernel_impl(*args):
    nin = len(IN_NAMES)
    a = dict(zip(IN_NAMES, args[:nin]))
    target = args[nin]
    nw = len(WEIGHTS)
    mom = dict(zip(WEIGHTS, args[nin + 1:nin + 1 + nw]))
    var = dict(zip(WEIGHTS, args[nin + 1 + nw:nin + 1 + 2 * nw]))
    bl, ll, d = a['x'].shape
    lc = a['ctx'].shape[1]
    assert d == D_MODEL and lc == TM and bl == 2 and ll % (GRID_W * T_CH) == 0
    s = lc + ll
    tps = s // TM
    dims = (bl, s, lc, tps)

    def gather(names, dtype, tag):
        shards = [a[n] for n in names]
        got = _all_gather_xy(_pack(shards, dtype), name="gather_" + tag)
        per = [_unpack(got[j], [w.shape for w in shards]) for j in range(4)]
        return {n: _unstack_shards(jnp.stack([per[j][i] for j in range(4)]), SHARD_AXIS[n])
                for i, n in enumerate(names)}

    big_cols = [BIG_COLS[n] for n in BIG]
    my_c = lax.axis_index("c").astype(jnp.int32).reshape(1)
    my_j = (2 * lax.axis_index("x") + lax.axis_index("y")).astype(jnp.int32).reshape(1)
    mine = [_cast_into_full(a[n], BIG_COLS[n], my_j, name=f"cast_{n}") for n in BIG]
    w = dict(zip(BIG, _gather_big(mine, big_cols, name="gather_big")))
    w.update(gather(GATHER_F32, F32, "f32"))
    for n in REPLICATED:
        w[n] = a[n]

    xs = jnp.concatenate([a['ctx'], a['x']], axis=1).reshape(bl * s, D_MODEL)
    c16 = jnp.zeros((16, D_MODEL), F32).at[0:2].set(a['c']).at[2].set(a['c_ctx'])
    s16, ds16 = _silu_rows(c16, name="silu")
    s16b = s16.astype(BF16)
    layers, saved, mods = [], [], []
    for l in range(DEPTH):
        lw = _layer_weights({n: w[n] for n in WEIGHTS if n not in ('c_ctx',)}, l)
        mod16 = _mm_nn(s16b, lw['ada_w'], lw['ada_b'][None, :], name=f"l{l}_ada").reshape(16, N_MOD, D_MODEL)
        modall = jnp.stack([mod16[2], mod16[0], mod16[2], mod16[1]])
        xs, sv = _layer_fwd(l, xs, modall, lw, dims)
        layers.append(lw)
        saved.append(sv)
        mods.append(modall)
    lossrow, dx = _loss_head(xs, target.reshape(bl * ll, D_MODEL), tps, name="loss_head")
    loss = lax.psum(0.5 / D_MODEL * jnp.sum(lossrow), ("x", "y", "c"))

    small = [n for n in WEIGHTS if n != 'c_ctx' and n not in BIG_COLS]
    grads = {n: [None] * DEPTH for n in small}
    gbufs = {}
    ds_rows = jnp.zeros((16, D_MODEL), F32)
    for l in reversed(range(DEPTH)):
        dx, g, dmod = _layer_bwd(l, dx, mods[l], layers[l], saved[l], dims, gbufs)
        dmod16 = jnp.zeros((16, N_MOD * D_MODEL), F32).at[0].set(dmod[1].reshape(-1)).at[1].set(
            dmod[3].reshape(-1)).at[2].set((dmod[0] + dmod[2]).reshape(-1))
        dmod16b = dmod16.astype(BF16)
        gbufs['ada_w'] = _mm_tn(s16b, dmod16b, name=f"l{l}_ada_dw", layer=l, into=gbufs.get('ada_w'))
        g['ada_b'] = _colsum(dmod16, name=f"l{l}_ada_db")
        ds_rows = ds_rows + _mm_nt(dmod16b, layers[l]['ada_w'], name=f"l{l}_ada_dx")
        for n, v in g.items():
            grads[n][l] = v.reshape(a[n].shape[1:] if n in REPLICATED else w[n].shape[1:])
    full = {n: jnp.stack(v) for n, v in grads.items()}
    full['c_ctx'] = _mul_rows(ds_rows, ds16, name="silu_bwd")[2]
    grad_x = dx.reshape(bl, s, D_MODEL)[:, lc:]

    from_sib = _sibling_partials([gbufs[n] for n in BIG], name="grad_big_sibling")
    sums = [_chip_sum(gbufs[n], o, my_c, name=f"grad_chip_sum_{n}") for n, o in zip(BIG, from_sib)]
    got = _scatter_big(sums, big_cols, name="grad_big_scatter")
    finals = [_block_sum(sm, gt, cf, my_j, my_c, name=f"grad_block_sum_{n}")
              for n, sm, gt, cf in zip(BIG, sums, got, big_cols)]
    res_big = {}
    for n, gfull in zip(BIG, _share_final(finals, name="grad_big_share")):
        flat = lambda t: t.reshape(-1, t.shape[-1])
        d_w, n_m, n_v = _adamw_native(flat(a[n]), flat(gfull), flat(mom[n]), flat(var[n]), name=f"adamw_{n}")
        res_big[n] = [gfull] + [t.reshape(a[n].shape) for t in (d_w, n_m, n_v)]

    rep_flat = _pack([full[n] for n in REPLICATED])
    rr = rep_flat.shape[0]
    sh_stacked = [_stack_shards(full[n], SHARD_AXIS[n] + 0).reshape(4, -1) for n in SHARDED]
    parts = jnp.concatenate(sh_stacked + [rep_flat.reshape(4, -1)], axis=1)
    pad = (-parts.shape[1]) % (32 * LANES)
    parts = jnp.pad(parts, ((0, 0), (0, pad))).reshape(4, -1, LANES)
    mine = _sum4(_scatter_xy(parts, name="grad_scatter"), name="grad_sum4")
    other = _swap_sibling(mine, name="grad_swap")
    n_sh = sum(math.prod(a[n].shape) for n in SHARDED)
    r_sh = n_sh // LANES
    assert n_sh % LANES == 0
    rq = rr // 4

    sh_shapes = [a[n].shape for n in SHARDED]
    pk = lambda dct: _pack([dct[n] for n in SHARDED])
    r_pk = pk(a).shape[0]
    take = lambda buf: jnp.pad(buf[:r_sh], ((0, r_pk - r_sh), (0, 0)))
    outs_sh = _adamw(pk(a), take(mine), take(other), pk(mom), pk(var), name="adamw_sharded")
    res_sh = [dict(zip(SHARDED, _unpack(o, sh_shapes))) for o in outs_sh]

    quarter = _add2(mine[r_sh:r_sh + rq], other[r_sh:r_sh + rq], name="grad_rep_sum")
    rep_g = _all_gather_xy(quarter, name="grad_rep_gather").reshape(rr, LANES)
    rep_shapes = [a[n].shape for n in REPLICATED]
    pr = lambda dct: _pack([dct[n] for n in REPLICATED])
    outs_rep = _adamw(pr(a), rep_g, None, pr(mom), pr(var), name="adamw_replicated")
    res_rep = [dict(zip(REPLICATED, _unpack(o, rep_shapes))) for o in outs_rep]

    out = [loss, grad_x]
    for k in range(4):
        out += [res_big[n][k] if n in BIG_COLS else res_sh[k][n] if n in SHARDED else res_rep[k][n] for n in WEIGHTS]
    return tuple(out)


def kernel(x, c, ctx, c_ctx, ada_w, ada_b, ln1_g, ln1_b, w_in, conv_w, conv_b, rg_lambda, rg_wa, rg_ba, rg_wi, rg_bi, s5_a_re, s5_a_im, s5_log_dt, s5_b_re, s5_b_im, s5_c_re, s5_c_im, s5_d, s5_glu_w, s5_glu_b, w_out, b_out, ln2_g, ln2_b, mlp_w1, mlp_b1, mlp_w2, mlp_b2, loss_target, m_c_ctx, m_ada_w, m_ada_b, m_ln1_g, m_ln1_b, m_w_in, m_conv_w, m_conv_b, m_rg_lambda, m_rg_wa, m_rg_ba, m_rg_wi, m_rg_bi, m_s5_a_re, m_s5_a_im, m_s5_log_dt, m_s5_b_re, m_s5_b_im, m_s5_c_re, m_s5_c_im, m_s5_d, m_s5_glu_w, m_s5_glu_b, m_w_out, m_b_out, m_ln2_g, m_ln2_b, m_mlp_w1, m_mlp_b1, m_mlp_w2, m_mlp_b2, v_c_ctx, v_ada_w, v_ada_b, v_ln1_g, v_ln1_b, v_w_in, v_conv_w, v_conv_b, v_rg_lambda, v_rg_wa, v_rg_ba, v_rg_wi, v_rg_bi, v_s5_a_re, v_s5_a_im, v_s5_log_dt, v_s5_b_re, v_s5_b_im, v_s5_c_re, v_s5_c_im, v_s5_d, v_s5_glu_w, v_s5_glu_b, v_w_out, v_b_out, v_ln2_g, v_ln2_b, v_mlp_w1, v_mlp_b1, v_mlp_w2, v_mlp_b2):
    return _kernel_impl(x, c, ctx, c_ctx, ada_w, ada_b, ln1_g, ln1_b, w_in, conv_w, conv_b, rg_lambda, rg_wa, rg_ba, rg_wi, rg_bi, s5_a_re, s5_a_im, s5_log_dt, s5_b_re, s5_b_im, s5_c_re, s5_c_im, s5_d, s5_glu_w, s5_glu_b, w_out, b_out, ln2_g, ln2_b, mlp_w1, mlp_b1, mlp_w2, mlp_b2, loss_target, m_c_ctx, m_ada_w, m_ada_b, m_ln1_g, m_ln1_b, m_w_in, m_conv_w, m_conv_b, m_rg_lambda, m_rg_wa, m_rg_ba, m_rg_wi, m_rg_bi, m_s5_a_re, m_s5_a_im, m_s5_log_dt, m_s5_b_re, m_s5_b_im, m_s5_c_re, m_s5_c_im, m_s5_d, m_s5_glu_w, m_s5_glu_b, m_w_out, m_b_out, m_ln2_g, m_ln2_b, m_mlp_w1, m_mlp_b1, m_mlp_w2, m_mlp_b2, v_c_ctx, v_ada_w, v_ada_b, v_ln1_g, v_ln1_b, v_w_in, v_conv_w, v_conv_b, v_rg_lambda, v_rg_wa, v_rg_ba, v_rg_wi, v_rg_bi, v_s5_a_re, v_s5_a_im, v_s5_log_dt, v_s5_b_re, v_s5_b_im, v_s5_c_re, v_s5_c_im, v_s5_d, v_s5_glu_w, v_s5_glu_b, v_w_out, v_b_out, v_ln2_g, v_ln2_b, v_mlp_w1, v_mlp_b1, v_mlp_w2, v_mlp_b2)
```

```python
import functools
import math

import jax
import jax.numpy as jnp
from jax import lax
from jax.experimental import pallas as pl
from jax.experimental.pallas import tpu as pltpu

F32 = jnp.float32
BF16 = jnp.bfloat16
MESH = pl.DeviceIdType.MESH

D_MODEL = 1024
N_MOD = 6
GRID_W = 64
RG_HEADS = 8
HEAD = 128
RG_C = 8.0
S5_GROUPS = 64
S5_GROUP = 16
S5_STATE = 64
T_CH = 16
GB = 8
CW = T_CH * S5_GROUP
SW = 2 * S5_STATE
DEPTH = 2
ALPHA = (2.0 * DEPTH) ** 0.25
LN_EPS = 1e-5
TM = 256
LANES = 1024
ADAM_LR, ADAM_B1, ADAM_B2, ADAM_EPS, ADAM_WD, ADAM_STEP = 0.001, 0.9, 0.999, 1e-08, 0.01, 10
MIB = 2 ** 20

IN_NAMES = ['x', 'c', 'ctx', 'c_ctx', 'ada_w', 'ada_b', 'ln1_g', 'ln1_b', 'w_in', 'conv_w', 'conv_b', 'rg_lambda',
            'rg_wa', 'rg_ba', 'rg_wi', 'rg_bi', 's5_a_re', 's5_a_im', 's5_log_dt', 's5_b_re', 's5_b_im', 's5_c_re',
            's5_c_im', 's5_d', 's5_glu_w', 's5_glu_b', 'w_out', 'b_out', 'ln2_g', 'ln2_b', 'mlp_w1', 'mlp_b1',
            'mlp_w2', 'mlp_b2']
WEIGHTS = IN_NAMES[3:]
SHARD_AXIS = {'ada_w': 2, 'w_in': 2, 'conv_w': 2, 'rg_lambda': 2, 'rg_ba': 2, 'rg_bi': 2, 's5_glu_w': 1, 'w_out': 1,
              'mlp_w1': 2, 'mlp_w2': 1}
SHARDED = ['conv_w', 'rg_lambda', 'rg_ba', 'rg_bi']
REPLICATED = [n for n in WEIGHTS if n not in SHARD_AXIS]
GATHER_BF16 = ['ada_w', 'w_in', 's5_glu_w', 'w_out', 'mlp_w1', 'mlp_w2']
GATHER_F32 = ['conv_w', 'rg_lambda', 'rg_ba', 'rg_bi']


def _params(n_axes, vmem_mb=40):
    return pltpu.CompilerParams(dimension_semantics=("arbitrary",) * n_axes, vmem_limit_bytes=vmem_mb * MIB)


def _tile(n, options):
    for t in options:
        if n % t == 0:
            return t
    return n


def _sigmoid(z):
    return 1.0 / (1.0 + jnp.exp(-z))


def _softplus(z):
    return jnp.maximum(z, 0.0) + jnp.log(1.0 + jnp.exp(-jnp.abs(z)))


def _neg_expm1_small(z):
    return -z * (1.0 + 0.5 * z * (1.0 + (1.0 / 3.0) * z * (1.0 + 0.25 * z)))


_G0 = math.sqrt(2.0 / math.pi)
_G1 = 0.044715


def _gelu(v):
    return 0.5 * v * (1.0 + jnp.tanh(_G0 * (v + _G1 * v * v * v)))


def _gelu_and_grad(v):
    t = jnp.tanh(_G0 * (v + _G1 * v * v * v))
    g = 0.5 * v * (1.0 + t)
    dg = 0.5 * (1.0 + t) + 0.5 * v * (1.0 - t * t) * _G0 * (1.0 + 3.0 * _G1 * v * v)
    return g, dg


def _seq_of_tile(i, tps):
    return 2 * (i // tps) + jnp.minimum(i % tps, 1)


def _dot(a, b, dims=(((1,), (0,)), ((), ()))):
    return lax.dot_general(a, b, dims, preferred_element_type=F32)


_NT = (((1,), (1,)), ((), ()))
_TN = (((0,), (0,)), ((), ()))


def _layer_spec(b, block, index):
    if isinstance(b, tuple):
        arr, layer = b
        return arr, pl.BlockSpec((None,) + block, lambda *g: (layer,) + index(*g))
    return b, pl.BlockSpec(block, index)


def _mm_nn(a, b, bias=None, *, relu2=False, name):
    m, k = a.shape
    n = (b[0] if isinstance(b, tuple) else b).shape[-1]
    tm, tn, tk = _tile(m, (512, 256)), _tile(n, (1024,)), _tile(k, (1024,))
    nk = k // tk
    has_bias = bias is not None

    def body(*refs):
        a_ref, b_ref = refs[0], refs[1]
        bias_ref = refs[2] if has_bias else None
        outs = refs[2 + has_bias:-1]
        acc = refs[-1]
        kk = pl.program_id(2)

        @pl.when(kk == 0)
        def _():
            acc[...] = jnp.zeros_like(acc)

        acc[...] += _dot(a_ref[...], b_ref[...])

        @pl.when(kk == nk - 1)
        def _():
            h = acc[...]
            if has_bias:
                h = h + bias_ref[...]
            if relu2:
                r = jnp.maximum(h, 0.0)
                outs[0][...] = (r * r).astype(BF16)
                outs[1][...] = h.astype(BF16)
            else:
                outs[0][...] = h

    b_arr, b_spec = _layer_spec(b, (tk, tn), lambda j, i, kk: (kk, j))
    in_specs = [pl.BlockSpec((tm, tk), lambda j, i, kk: (i, kk)), b_spec]
    args = [a, b_arr]
    if has_bias:
        in_specs.append(pl.BlockSpec((1, tn), lambda j, i, kk: (0, j)))
        args.append(bias)
    o_spec = pl.BlockSpec((tm, tn), lambda j, i, kk: (i, j))
    if relu2:
        out_shape = (jax.ShapeDtypeStruct((m, n), BF16), jax.ShapeDtypeStruct((m, n), BF16))
        out_specs = (o_spec, o_spec)
    else:
        out_shape, out_specs = jax.ShapeDtypeStruct((m, n), F32), o_spec
    return pl.pallas_call(body, name=name, grid=(n // tn, m // tm, nk), in_specs=in_specs, out_specs=out_specs,
                          out_shape=out_shape, scratch_shapes=[pltpu.VMEM((tm, tn), F32)],
                          compiler_params=_params(3))(*args)


def _mm_nt(a, b, hb=None, *, name):
    m, n = a.shape
    k = (b[0] if isinstance(b, tuple) else b).shape[-2]
    tm, tn, tk = _tile(m, (512, 256)), _tile(k, (1024,)), _tile(n, (1024,))
    nk = n // tk
    fused = hb is not None

    def body(*refs):
        a_ref, b_ref = refs[0], refs[1]
        hb_ref = refs[2] if fused else None
        o_ref, acc = refs[-2], refs[-1]
        kk = pl.program_id(2)

        @pl.when(kk == 0)
        def _():
            acc[...] = jnp.zeros_like(acc)

        acc[...] += _dot(a_ref[...], b_ref[...], _NT)

        @pl.when(kk == nk - 1)
        def _():
            if fused:
                o_ref[...] = (acc[...] * (2.0 * jnp.maximum(hb_ref[...].astype(F32), 0.0))).astype(BF16)
            else:
                o_ref[...] = acc[...]

    b_arr, b_spec = _layer_spec(b, (tn, tk), lambda j, i, kk: (j, kk))
    in_specs = [pl.BlockSpec((tm, tk), lambda j, i, kk: (i, kk)), b_spec]
    args = [a, b_arr]
    if fused:
        in_specs.append(pl.BlockSpec((tm, tn), lambda j, i, kk: (i, j)))
        args.append(hb)
    return pl.pallas_call(body, name=name, grid=(k // tn, m // tm, nk), in_specs=in_specs,
                          out_specs=pl.BlockSpec((tm, tn), lambda j, i, kk: (i, j)),
                          out_shape=jax.ShapeDtypeStruct((m, k), BF16 if fused else F32),
                          scratch_shapes=[pltpu.VMEM((tm, tn), F32)], compiler_params=_params(3))(*args)


def _mm_tn(a, b, *, name, layer=None, into=None):
    m, k = a.shape
    n = b.shape[1]
    tk, tn, tr = _tile(k, (1024,)), _tile(n, (1024,)), _tile(m, (512, 256))
    nr = m // tr

    def body(a_ref, b_ref, *rest):
        o_ref, acc = rest[-2], rest[-1]
        r = pl.program_id(2)

        @pl.when(r == 0)
        def _():
            acc[...] = jnp.zeros_like(acc)

        acc[...] += _dot(a_ref[...], b_ref[...], _TN)

        @pl.when(r == nr - 1)
        def _():
            o_ref[...] = acc[...]

    in_specs = [pl.BlockSpec((tr, tk), lambda i, j, r: (r, i)), pl.BlockSpec((tr, tn), lambda i, j, r: (r, j))]
    args, aliases = [a, b], {}
    if layer is None:
        out_spec, out_shape = pl.BlockSpec((tk, tn), lambda i, j, r: (i, j)), jax.ShapeDtypeStruct((k, n), F32)
    else:
        out_spec = pl.BlockSpec((None, tk, tn), lambda i, j, r: (layer, i, j))
        out_shape = jax.ShapeDtypeStruct((DEPTH, k, n), F32)
        if into is not None:
            in_specs.append(_ANY)
            args.append(into)
            aliases = {2: 0}
    return pl.pallas_call(body, name=name, grid=(k // tk, n // tn, nr), in_specs=in_specs, out_specs=out_spec,
                          out_shape=out_shape, input_output_aliases=aliases,
                          scratch_shapes=[pltpu.VMEM((tk, tn), F32)], compiler_params=_params(3))(*args)


def _colsum(v, *, name):
    m, n = v.shape
    tn, tr = _tile(n, (1024,)), _tile(m, (512, 256))

    def body(v_ref, o_ref):
        @pl.when(pl.program_id(1) == 0)
        def _():
            o_ref[...] = jnp.zeros_like(o_ref)

        o_ref[...] += jnp.sum(v_ref[...].astype(F32), axis=0, keepdims=True)

    return pl.pallas_call(body, name=name, grid=(n // tn, m // tr),
                          in_specs=[pl.BlockSpec((tr, tn), lambda j, r: (r, j))],
                          out_specs=pl.BlockSpec((1, tn), lambda j, r: (0, j)),
                          out_shape=jax.ShapeDtypeStruct((1, n), F32), compiler_params=_params(2))(v)


def _tok_spec(d=D_MODEL, col=0):
    return pl.BlockSpec((TM, d), lambda i: (i, col))


def _mod_spec(tps):
    return pl.BlockSpec((1, N_MOD, D_MODEL), lambda i: (_seq_of_tile(i, tps), 0, 0))


def _row_spec(d=D_MODEL):
    return pl.BlockSpec((1, d), lambda i: (0, 0))


def _seq_acc_spec(tps):
    return pl.BlockSpec((1, 1, D_MODEL), lambda i: (_seq_of_tile(i, tps), 0, 0))


def _modulate(xs, modall, k_shift, k_scale, tps, *, name):
    n = xs.shape[0]

    def body(x_ref, m_ref, o_ref):
        sh = m_ref[0, k_shift:k_shift + 1, :]
        sc = m_ref[0, k_scale:k_scale + 1, :]
        o_ref[...] = (x_ref[...] * (1.0 + sc) + sh).astype(BF16)

    return pl.pallas_call(body, name=name, grid=(n // TM,), in_specs=[_tok_spec(), _mod_spec(tps)],
                          out_specs=_tok_spec(), out_shape=jax.ShapeDtypeStruct((n, D_MODEL), BF16),
                          compiler_params=_params(1))(xs, modall)


def _resid_ln(xs, ms, modall, k_gate, g, b, tps, *, name):
    n = xs.shape[0]

    def body(x_ref, m_ref, mod_ref, g_ref, b_ref, o_ref):
        z = ALPHA * x_ref[...] + mod_ref[0, k_gate:k_gate + 1, :] * m_ref[...]
        mu = jnp.mean(z, axis=-1, keepdims=True)
        zc = z - mu
        var = jnp.mean(zc * zc, axis=-1, keepdims=True)
        o_ref[...] = zc * lax.rsqrt(var + LN_EPS) * g_ref[...] + b_ref[...]

    return pl.pallas_call(body, name=name, grid=(n // TM,),
                          in_specs=[_tok_spec(), _tok_spec(), _mod_spec(tps), _row_spec(), _row_spec()],
                          out_specs=_tok_spec(), out_shape=jax.ShapeDtypeStruct((n, D_MODEL), F32),
                          compiler_params=_params(1))(xs, ms, modall, g, b)


def _resid_ln_bwd(xs, ms, modall, k_gate, g, dout, tps, *, name):
    n = xs.shape[0]

    def body(x_ref, m_ref, mod_ref, g_ref, d_ref, dxa_ref, dm_ref, dbias_ref, dg_ref, db_ref, dgate_ref):
        i = pl.program_id(0)
        gate = mod_ref[0, k_gate:k_gate + 1, :]
        m = m_ref[...]
        z = ALPHA * x_ref[...] + gate * m
        mu = jnp.mean(z, axis=-1, keepdims=True)
        zc = z - mu
        var = jnp.mean(zc * zc, axis=-1, keepdims=True)
        rstd = lax.rsqrt(var + LN_EPS)
        xhat = zc * rstd
        d = d_ref[...]
        dxh = d * g_ref[...]
        dz = rstd * (dxh - jnp.mean(dxh, axis=-1, keepdims=True)
                     - xhat * jnp.mean(dxh * xhat, axis=-1, keepdims=True))
        dxa_ref[...] = ALPHA * dz
        dm = gate * dz
        dm_ref[...] = dm.astype(BF16)

        @pl.when(i == 0)
        def _():
            dbias_ref[...] = jnp.zeros_like(dbias_ref)
            dg_ref[...] = jnp.zeros_like(dg_ref)
            db_ref[...] = jnp.zeros_like(db_ref)

        dbias_ref[...] += jnp.sum(dm, axis=0, keepdims=True)
        dg_ref[...] += jnp.sum(d * xhat, axis=0, keepdims=True)
        db_ref[...] += jnp.sum(d, axis=0, keepdims=True)
        part = jnp.sum(dz * m, axis=0, keepdims=True)

        @pl.when(i % tps <= 1)
        def _():
            dgate_ref[0] = part

        @pl.when(i % tps > 1)
        def _():
            dgate_ref[0] += part

    row = jax.ShapeDtypeStruct((1, D_MODEL), F32)
    return pl.pallas_call(
        body, name=name, grid=(n // TM,),
        in_specs=[_tok_spec(), _tok_spec(), _mod_spec(tps), _row_spec(), _tok_spec()],
        out_specs=(_tok_spec(), _tok_spec(), _row_spec(), _row_spec(), _row_spec(), _seq_acc_spec(tps)),
        out_shape=(jax.ShapeDtypeStruct((n, D_MODEL), F32), jax.ShapeDtypeStruct((n, D_MODEL), BF16), row, row, row,
                   jax.ShapeDtypeStruct((n // TM // tps * 2, 1, D_MODEL), F32)),
        compiler_params=_params(1))(xs, ms, modall, g, dout)


def _modulate_bwd(du, xs, modall, k_scale, dxa, tps, *, name):
    n = xs.shape[0]

    def body(du_ref, x_ref, mod_ref, dxa_ref, dx_ref, dsc_ref, dsh_ref):
        i = pl.program_id(0)
        du_t = du_ref[...]
        dx_ref[...] = dxa_ref[...] + du_t * (1.0 + mod_ref[0, k_scale:k_scale + 1, :])
        psc = jnp.sum(du_t * x_ref[...], axis=0, keepdims=True)
        psh = jnp.sum(du_t, axis=0, keepdims=True)

        @pl.when(i % tps <= 1)
        def _():
            dsc_ref[0] = psc
            dsh_ref[0] = psh

        @pl.when(i % tps > 1)
        def _():
            dsc_ref[0] += psc
            dsh_ref[0] += psh

    acc = jax.ShapeDtypeStruct((n // TM // tps * 2, 1, D_MODEL), F32)
    return pl.pallas_call(body, name=name, grid=(n // TM,),
                          in_specs=[_tok_spec(), _tok_spec(), _mod_spec(tps), _tok_spec()],
                          out_specs=(_tok_spec(), _seq_acc_spec(tps), _seq_acc_spec(tps)),
                          out_shape=(jax.ShapeDtypeStruct((n, D_MODEL), F32), acc, acc),
                          compiler_params=_params(1))(du, xs, modall, dxa)


def _loss_head(ys, target, tps, *, name):
    n = ys.shape[0]
    lat_tiles = tps - 1

    def body(y_ref, t_ref, acc_ref, dy_ref):
        i = pl.program_id(0)

        @pl.when(i == 0)
        def _():
            acc_ref[...] = jnp.zeros_like(acc_ref)

        @pl.when(i % tps == 0)
        def _():
            dy_ref[...] = jnp.zeros_like(dy_ref)

        @pl.when(i % tps > 0)
        def _():
            e = y_ref[...] - t_ref[...]
            dy_ref[...] = e * (1.0 / D_MODEL)
            acc_ref[...] += jnp.sum(e * e, axis=0, keepdims=True)

    t_spec = pl.BlockSpec((TM, D_MODEL), lambda i: ((i // tps) * lat_tiles + jnp.maximum(i % tps - 1, 0), 0))
    return pl.pallas_call(body, name=name, grid=(n // TM,), in_specs=[_tok_spec(), t_spec],
                          out_specs=(_row_spec(), _tok_spec()),
                          out_shape=(jax.ShapeDtypeStruct((1, D_MODEL), F32), jax.ShapeDtypeStruct((n, D_MODEL), F32)),
                          compiler_params=_params(1))(ys, target)


def _glu_fwd(y_nat, p, d_skip, w, b, *, name):
    n = y_nat.shape[0]

    def body(y_ref, u_ref, d_ref, w_ref, b_ref, o_ref):
        g = _gelu(y_ref[...] + d_ref[...] * u_ref[...])
        z = _dot(g.astype(BF16), w_ref[...]) + b_ref[...]
        o_ref[...] = (g * _sigmoid(z)).astype(BF16)

    w_arr, w_spec = _layer_spec(w, (D_MODEL, D_MODEL), lambda i: (0, 0))
    return pl.pallas_call(body, name=name, grid=(n // TM,),
                          in_specs=[_tok_spec(), _tok_spec(col=2), _row_spec(), w_spec, _row_spec()],
                          out_specs=_tok_spec(), out_shape=jax.ShapeDtypeStruct((n, D_MODEL), BF16),
                          compiler_params=_params(1))(y_nat, p, d_skip, w_arr, b)


def _glu_bwd(dcat, y_nat, p, d_skip, w, b, *, name):
    n = y_nat.shape[0]

    def body(ds_ref, y_ref, u_ref, d_ref, w_ref, b_ref, dy_ref, dsk_ref, g_ref, dz_ref, dd_ref, dbz_ref):
        u = u_ref[...]
        g, gg = _gelu_and_grad(y_ref[...] + d_ref[...] * u)
        s = _sigmoid(_dot(g.astype(BF16), w_ref[...]) + b_ref[...])
        ds = ds_ref[...]
        dz = ds * g * s * (1.0 - s)
        dzb = dz.astype(BF16)
        dg = ds * s + _dot(dzb, w_ref[...], _NT)
        dyp = dg * gg
        dy_ref[...] = dyp
        dsk_ref[...] = dyp * d_ref[...]
        g_ref[...] = g.astype(BF16)
        dz_ref[...] = dzb

        @pl.when(pl.program_id(0) == 0)
        def _():
            dd_ref[...] = jnp.zeros_like(dd_ref)
            dbz_ref[...] = jnp.zeros_like(dbz_ref)

        dd_ref[...] += jnp.sum(dyp * u, axis=0, keepdims=True)
        dbz_ref[...] += jnp.sum(dz, axis=0, keepdims=True)

    tok_bf = jax.ShapeDtypeStruct((n, D_MODEL), BF16)
    tok_f32 = jax.ShapeDtypeStruct((n, D_MODEL), F32)
    row = jax.ShapeDtypeStruct((1, D_MODEL), F32)
    w_arr, w_spec = _layer_spec(w, (D_MODEL, D_MODEL), lambda i: (0, 0))
    return pl.pallas_call(
        body, name=name, grid=(n // TM,),
        in_specs=[_tok_spec(col=1), _tok_spec(), _tok_spec(col=2), _row_spec(), w_spec, _row_spec()],
        out_specs=(_tok_spec(), _tok_spec(), _tok_spec(), _tok_spec(), _row_spec(), _row_spec()),
        out_shape=(tok_f32, tok_f32, tok_bf, tok_bf, row, row),
        compiler_params=_params(1))(dcat, y_nat, p, d_skip, w_arr, b)


def _add_cast(a, b, *, name):
    n = a.shape[0]

    def body(a_ref, b_ref, o_ref):
        o_ref[...] = (a_ref[...] + b_ref[...]).astype(BF16)

    return pl.pallas_call(body, name=name, grid=(n // TM,), in_specs=[_tok_spec(), _tok_spec()],
                          out_specs=_tok_spec(), out_shape=jax.ShapeDtypeStruct((n, D_MODEL), BF16),
                          compiler_params=_params(1))(a, b)


def _silu_rows(c16, *, name):
    def body(c_ref, s_ref, ds_ref):
        v = c_ref[...]
        sg = _sigmoid(v)
        s_ref[...] = v * sg
        ds_ref[...] = sg * (1.0 + v * (1.0 - sg))

    shp = jax.ShapeDtypeStruct(c16.shape, F32)
    return pl.pallas_call(body, name=name, out_shape=(shp, shp))(c16)


def _mul_rows(a, b, *, name):
    def body(a_ref, b_ref, o_ref):
        o_ref[...] = a_ref[...] * b_ref[...]

    return pl.pallas_call(body, name=name, out_shape=jax.ShapeDtypeStruct(a.shape, F32))(a, b)


def _pad_off(c):
    return pl.multiple_of(c * TM + 8 + 8 * jnp.minimum(c, 1), 8)


def _rows8(k):
    return pl.ds(pl.multiple_of(k * 8, 8), 8)


def _windows(buf, c, shifts):
    n = TM + 16
    win = buf[pl.ds(pl.multiple_of(_pad_off(c) - 8, 8), n), :]
    return [win[8:8 + TM] if k == 0 else pltpu.roll(win, (-k) % n, 0)[8:8 + TM] for k in shifts]


def _conv_window(xpad, c):
    return _windows(xpad, c, (-1, 0, 1, 2))


def _rg_coeffs(z, d, spl, xc):
    r = 0.5 + 0.5 * jnp.tanh(0.5 * z[:, 256 * d:256 * d + HEAD])
    i = 0.5 + 0.5 * jnp.tanh(0.5 * z[:, 256 * d + HEAD:256 * d + 2 * HEAD])
    la = -RG_C * spl[d:d + 1, :] * r
    a = jnp.exp(la)
    one_minus_a = jnp.where(jnp.abs(la) < 1e-2, _neg_expm1_small(la), 1.0 - a)
    mult = jnp.sqrt(one_minus_a * (1.0 + a))
    return r, i, a, mult, a * a


def _chunk_scan(a, b, reverse):
    row = lax.broadcasted_iota(jnp.int32, (TM, HEAD), 0)
    sft = 1
    while sft < TM:
        keep = (row < TM - sft) if reverse else (row >= sft)
        amt = TM - sft if reverse else sft
        a_prev = jnp.where(keep, pltpu.roll(a, amt, 0), 1.0)
        b_prev = jnp.where(keep, pltpu.roll(b, amt, 0), 0.0)
        b = a * b_prev + b
        a = a * a_prev
        sft *= 2
    return a, b


def _zero_pads(buf, s, lc):
    z8 = jnp.zeros((8, HEAD), F32)
    buf[0:8, :] = z8
    buf[8 + lc:16 + lc, :] = z8
    buf[16 + s:24 + s, :] = z8


def _rg_fwd(p, conv_w, conv_b, lam, wcat, bcat, bl, s, lc, *, name):
    nch = s // TM

    def body(x_ref, gate_ref, cw_ref, cb_ref, lam_ref, w_ref, b_ref, rg_ref, hf_ref, hb_ref, xpad, af, bf, ab, bb):
        _zero_pads(xpad, s, lc)

        def copy_chunk(c, _):
            xpad[pl.ds(_pad_off(c), TM), :] = x_ref[pl.ds(pl.multiple_of(c * TM, TM), TM), :]
            return 0

        lax.fori_loop(0, nch, copy_chunk, 0)
        spl = _softplus(-lam_ref[...])
        cw = cw_ref[...]

        def coef_chunk(c, _):
            xm1, x0, xp1, xp2 = _conv_window(xpad, c)
            xc = cw[0:1] * xm1 + cw[1:2] * x0 + cw[2:3] * xp1 + cw[3:4] * xp2 + cb_ref[...]
            z = _dot(xc.astype(BF16), w_ref[0]) + b_ref[0]
            rows = pl.ds(pl.multiple_of(c * TM, TM), TM)
            for d, (a_s, b_s) in enumerate(((af, bf), (ab, bb))):
                _, i, a, mult, _ = _rg_coeffs(z, d, spl, xc)
                a_s[rows, :] = a
                b_s[rows, :] = mult * i * xc
            return 0

        lax.fori_loop(0, nch, coef_chunk, 0)

        def scan_pair(j, carry):
            cf, cb_ = carry
            rf = pl.ds(pl.multiple_of(j * TM, TM), TM)
            rb = pl.ds(pl.multiple_of(jnp.where(j == 0, 0, nch - j) * TM, TM), TM)
            a1, h1 = _chunk_scan(af[rf, :], bf[rf, :], False)
            h1 = h1 + a1 * cf
            hf_ref[rf, :] = h1
            a2, h2 = _chunk_scan(ab[rb, :], bb[rb, :], True)
            h2 = h2 + a2 * cb_
            hb_ref[rb, :] = h2
            return h1[TM - 1:TM], h2[0:1]

        zero = jnp.zeros((1, HEAD), F32)
        lax.fori_loop(0, nch, scan_pair, (zero, zero))

        def out_chunk(c, _):
            rows = pl.ds(pl.multiple_of(c * TM, TM), TM)
            rg_ref[rows, :] = ((hf_ref[rows, :] + hb_ref[rows, :]) * _gelu(gate_ref[rows, :])).astype(BF16)
            return 0

        lax.fori_loop(0, nch, out_chunk, 0)

    seq = lambda col0: pl.BlockSpec((s, HEAD), lambda b, h: (b, col0 + h))
    par = lambda r: pl.BlockSpec((r, HEAD), lambda b, h: (0, h))
    n = bl * s
    return pl.pallas_call(
        body, name=name, grid=(bl, RG_HEADS),
        in_specs=[seq(0), seq(RG_HEADS), par(4), par(1), par(2),
                  pl.BlockSpec((1, HEAD, 4 * HEAD), lambda b, h: (h, 0, 0)),
                  pl.BlockSpec((1, 1, 4 * HEAD), lambda b, h: (h, 0, 0))],
        out_specs=(seq(0), seq(0), seq(0)),
        out_shape=(jax.ShapeDtypeStruct((n, D_MODEL), BF16), jax.ShapeDtypeStruct((n, D_MODEL), F32),
                   jax.ShapeDtypeStruct((n, D_MODEL), F32)),
        scratch_shapes=[pltpu.VMEM((s + 24, HEAD), F32)] + [pltpu.VMEM((s, HEAD), F32)] * 4,
        compiler_params=_params(2, 48))(p, p, conv_w, conv_b, lam, wcat, bcat)


def _rg_bwd(p, dcat, hf, hb, conv_w, conv_b, lam, wcat, bcat, bl, s, lc, *, name):
    nch = s // TM
    ll = s - lc

    def body(p_hbm, dcat_hbm, hf_hbm, hb_hbm, gate_s, cw_ref, cb_ref, lam_ref, w_ref, b_ref,
             drgx_ref, dgate_ref, dcw_ref, dcb_ref, dlam_ref, dw_ref, db_ref,
             xpad, dxpad, hf_s, hb_s, dhs, a_f, a_b, lam_f, lam_b, sems):
        h = pl.program_id(0)
        b = pl.program_id(1)
        row0 = b * s
        col = pl.multiple_of(h * HEAD, HEAD)

        def rows_of(ref, r0, nr, c0):
            return ref.at[pl.ds(row0 + r0, nr), pl.ds(c0, HEAD)]

        copies = [
            pltpu.make_async_copy(rows_of(p_hbm, 0, lc, col), xpad.at[pl.ds(8, lc), :], sems.at[0]),
            pltpu.make_async_copy(rows_of(p_hbm, lc, ll, col), xpad.at[pl.ds(16 + lc, ll), :], sems.at[1]),
            pltpu.make_async_copy(rows_of(dcat_hbm, 0, s, col), dhs, sems.at[2]),
            pltpu.make_async_copy(rows_of(hf_hbm, 0, s, col), hf_s.at[pl.ds(8, s), :], sems.at[3]),
            pltpu.make_async_copy(rows_of(hb_hbm, 0, s, col), hb_s.at[pl.ds(8, s), :], sems.at[4]),
        ]
        for cp in copies:
            cp.start()
        _zero_pads(xpad, s, lc)
        _zero_pads(dxpad, s, lc)
        for buf in (hf_s, hb_s):
            buf[0:8, :] = jnp.zeros((8, HEAD), F32)
            buf[8 + s:16 + s, :] = jnp.zeros((8, HEAD), F32)

        @pl.when(b == 0)
        def _():
            dcw_ref[...] = jnp.zeros_like(dcw_ref)
            dcb_ref[...] = jnp.zeros_like(dcb_ref)
            dlam_ref[...] = jnp.zeros_like(dlam_ref)
            dw_ref[...] = jnp.zeros_like(dw_ref)
            db_ref[...] = jnp.zeros_like(db_ref)

        for cp in copies:
            cp.wait()
        lam_v = lam_ref[...]
        spl = _softplus(-lam_v)
        cw = cw_ref[...]

        def conv(c):
            xm1, x0, xp1, xp2 = _conv_window(xpad, c)
            return cw[0:1] * xm1 + cw[1:2] * x0 + cw[2:3] * xp1 + cw[3:4] * xp2 + cb_ref[...]

        def pass_a(c, _):
            rows = pl.ds(pl.multiple_of(c * TM, TM), TM)
            xc = conv(c)
            z = _dot(xc.astype(BF16), w_ref[0]) + b_ref[0]
            for d, a_s in enumerate((a_f, a_b)):
                a_s[rows, :] = _rg_coeffs(z, d, spl, xc)[2]
            g, gg = _gelu_and_grad(gate_s[rows, :])
            drg = dhs[rows, :]
            hrows = pl.ds(pl.multiple_of(c * TM + 8, 8), TM)
            dgate_ref[rows, :] = (drg * (hf_s[hrows, :] + hb_s[hrows, :]) * gg).astype(BF16)
            dhs[rows, :] = drg * g
            return 0

        lax.fori_loop(0, nch, pass_a, 0)

        row = lax.broadcasted_iota(jnp.int32, (TM, HEAD), 0)

        def adj_pair(j, carry):
            cf, cb_ = carry
            rf = pl.ds(pl.multiple_of((nch - 1 - j) * TM, TM), TM)
            rb = pl.ds(pl.multiple_of(jnp.where(j == nch - 1, 0, j + 1) * TM, TM), TM)
            d1, a1 = dhs[rf, :], a_f[rf, :]
            p1, m1 = _chunk_scan(a1, a1 * d1, True)
            m1 = m1 + p1 * cf
            lam_f[rf, :] = d1 + jnp.where(row == TM - 1, cf, pltpu.roll(m1, TM - 1, 0))
            d2, a2 = dhs[rb, :], a_b[rb, :]
            p2, m2 = _chunk_scan(a2, a2 * d2, False)
            m2 = m2 + p2 * cb_
            lam_b[rb, :] = d2 + jnp.where(row == 0, cb_, pltpu.roll(m2, 1, 0))
            return m1[0:1], m2[TM - 1:TM]

        zero = jnp.zeros((1, HEAD), F32)
        lax.fori_loop(0, nch, adj_pair, (zero, zero))

        sig_neg = _sigmoid(-lam_v)
        last_row = lax.broadcasted_iota(jnp.int32, (TM, HEAD), 0) == TM - 1
        hb_first = hb_s[8:9, :]

        def pass_b(c, _):
            rows = pl.ds(pl.multiple_of(c * TM, TM), TM)
            xc = conv(c)
            xcb = xc.astype(BF16)
            z = _dot(xcb, w_ref[0]) + b_ref[0]
            dxc = jnp.zeros((TM, HEAD), F32)
            dzs = []
            n = TM + 16
            hp_f = pltpu.roll(hf_s[pl.ds(pl.multiple_of(c * TM, TM), n), :], 1, 0)[8:8 + TM]
            hp_b = pltpu.roll(hb_s[pl.ds(pl.multiple_of(c * TM, TM), n), :], n - 1, 0)[8:8 + TM]
            hp_b = jnp.where(last_row & (c == 0), 0.0, hp_b)
            hp_b = jnp.where(last_row & (c == nch - 1), hb_first, hp_b)
            for d, (l_s, hp) in enumerate(((lam_f, hp_f), (lam_b, hp_b))):
                r, i, a, mult, e2 = _rg_coeffs(z, d, spl, xc)
                dbt = l_s[rows, :]
                dla = dbt * hp * a - dbt * i * xc * (e2 / mult)
                dlam_ref[0, d:d + 1, :] += jnp.sum(dla * r, axis=0, keepdims=True) * (RG_C * sig_neg[d:d + 1, :])
                dr = dla * (-RG_C * spl[d:d + 1, :])
                di = dbt * mult * xc
                dxc = dxc + dbt * mult * i
                dzs += [dr * r * (1.0 - r), di * i * (1.0 - i)]
            dz = jnp.concatenate(dzs, axis=1)
            dzb = dz.astype(BF16)
            dxc = dxc + _dot(dzb, w_ref[0], _NT)
            dw_ref[0] += _dot(xcb, dzb, _TN)
            db_ref[0] += jnp.sum(dz, axis=0, keepdims=True)
            dcb_ref[0] += jnp.sum(dxc, axis=0, keepdims=True)
            dxpad[pl.ds(_pad_off(c), TM), :] = dxc
            return 0

        lax.fori_loop(0, nch, pass_b, 0)

        def pass_c(c, _):
            rows = pl.ds(pl.multiple_of(c * TM, TM), TM)
            gp1, g0, gm1, gm2 = _windows(dxpad, c, (1, 0, -1, -2))
            drgx_ref[rows, :] = (cw[0:1] * gp1 + cw[1:2] * g0 + cw[2:3] * gm1 + cw[3:4] * gm2).astype(BF16)
            xm1, x0, xp1, xp2 = _conv_window(xpad, c)
            dcw_ref[0] += jnp.concatenate([jnp.sum(g0 * t, axis=0, keepdims=True) for t in (xm1, x0, xp1, xp2)],
                                          axis=0)
            return 0

        lax.fori_loop(0, nch, pass_c, 0)

    seq = pl.BlockSpec((s, HEAD), lambda h, b: (b, h))
    par = lambda r: pl.BlockSpec((r, HEAD), lambda h, b: (0, h))
    acc = lambda r, w: pl.BlockSpec((1, r, w), lambda h, b: (h, 0, 0))
    anyspec = pl.BlockSpec(memory_space=pl.ANY)
    n = bl * s
    big = pltpu.VMEM((s, HEAD), F32)
    return pl.pallas_call(
        body, name=name, grid=(RG_HEADS, bl),
        in_specs=[anyspec, anyspec, anyspec, anyspec, pl.BlockSpec((s, HEAD), lambda h, b: (b, RG_HEADS + h)),
                  par(4), par(1), par(2), acc(HEAD, 4 * HEAD), acc(1, 4 * HEAD)],
        out_specs=(seq, seq, acc(4, HEAD), acc(1, HEAD), acc(2, HEAD), acc(HEAD, 4 * HEAD), acc(1, 4 * HEAD)),
        out_shape=(jax.ShapeDtypeStruct((n, D_MODEL), BF16), jax.ShapeDtypeStruct((n, D_MODEL), BF16),
                   jax.ShapeDtypeStruct((RG_HEADS, 4, HEAD), F32), jax.ShapeDtypeStruct((RG_HEADS, 1, HEAD), F32),
                   jax.ShapeDtypeStruct((RG_HEADS, 2, HEAD), F32),
                   jax.ShapeDtypeStruct((RG_HEADS, HEAD, 4 * HEAD), F32),
                   jax.ShapeDtypeStruct((RG_HEADS, 1, 4 * HEAD), F32)),
        scratch_shapes=[pltpu.VMEM((s + 24, HEAD), F32)] * 2 + [pltpu.VMEM((s + 16, HEAD), F32)] * 2 + [big] * 5
        + [pltpu.SemaphoreType.DMA((5,))],
        compiler_params=_params(2, 52))(p, dcat, hf, hb, p, conv_w, conv_b, lam, wcat, bcat)


def _s5_mats(a_re, a_im, log_dt, b_re, b_im, c_re, c_im):
    t = T_CH
    g = a_re.shape[1]
    dt = jnp.exp(log_dt)[..., None]
    lr, li = a_re * dt, a_im * dt
    steps = jnp.arange(t + 1, dtype=F32)[:, None]
    mag = jnp.exp(lr[:, :, None, :] * steps)
    ang = li[:, :, None, :] * steps
    pr, pi = mag * jnp.cos(ang), mag * jnp.sin(ang)
    xr, xi = pr[:, :, 1] - 1.0, pi[:, :, 1]
    den = a_re * a_re + a_im * a_im
    qr, qi = (xr * a_re + xi * a_im) / den, (xi * a_re - xr * a_im) / den
    btr, bti = b_re.transpose(0, 1, 3, 2), b_im.transpose(0, 1, 3, 2)
    bbr = qr[:, :, None, :] * btr - qi[:, :, None, :] * bti
    bbi = qr[:, :, None, :] * bti + qi[:, :, None, :] * btr
    up, down = slice(0, t), slice(t - 1, None, -1)

    def pow_c(d, sl):
        wr, wi = pr[d][:, sl, None, :], pi[d][:, sl, None, :]
        cr, ci = c_re[d][:, None], c_im[d][:, None]
        return (wr * cr - wi * ci).reshape(g, CW, S5_STATE), (wr * ci + wi * cr).reshape(g, CW, S5_STATE)

    hp = lax.Precision.HIGHEST

    def lag_map(d, sl):
        re, im = pow_c(d, sl)
        return (jnp.einsum('gkp,gmp->gkm', bbr[d], re, precision=hp)
                - jnp.einsum('gkp,gmp->gkm', bbi[d], im, precision=hp))

    z_f, z_b = lag_map(0, up), lag_map(1, down)
    kf = jnp.stack([jnp.pad(z_f, ((0, 0), (0, 0), (S5_GROUP * s, 0)))[:, :, :CW] for s in range(t)], axis=1)
    kb = jnp.stack([jnp.pad(z_b, ((0, 0), (0, 0), (0, S5_GROUP * (t - 1 - s))))[:, :, S5_GROUP * (t - 1 - s):]
                    for s in range(t)], axis=1)
    kcat = (kf + kb).reshape(g, CW, CW)

    def state_in(d, sl):
        wr, wi = pr[d][:, sl, None, :], pi[d][:, sl, None, :]
        br, bi = bbr[d][:, None], bbi[d][:, None]
        return jnp.concatenate([wr * br - wi * bi, wr * bi + wi * br], axis=-1).reshape(g, CW, SW)

    wcat = jnp.concatenate([kcat, state_in(0, down), state_in(1, up)], axis=2)
    of_r, of_i = pow_c(0, slice(1, t + 1))
    ob_r, ob_i = pow_c(1, slice(t, 0, -1))
    mout_t = jnp.concatenate([of_r, -of_i, ob_r, -ob_i], axis=2)
    rows = []
    for d in range(2):
        art, ait = pr[d][:, t], pi[d][:, t]
        rows += [jnp.concatenate([art, art], axis=1).reshape(-1), jnp.concatenate([-ait, ait], axis=1).reshape(-1)]
    return wcat, mout_t, jnp.stack(rows)


def _lane_swap(v):
    return pltpu.roll(v, S5_STATE, 1)


def _grp(g, w):
    return slice(g * w, (g + 1) * w)


def _s5_fwd(u, wcat, mout, a2, ncc, *, name):
    bl, nc, _ = u.shape

    def body(u_ref, w_ref, mo_ref, a_ref, y_ref, sf_ref, sb_ref, vf, vb):
        for g in range(GB):
            zu = _dot(u_ref[:, _grp(g, CW)].astype(BF16), w_ref[g])
            y_ref[:, _grp(g, CW)] = zu[:, :CW]
            vf[:, _grp(2 * g, SW)] = zu[:, CW:CW + SW]
            vb[:, _grp(2 * g, SW)] = zu[:, CW + SW:]
            vf[:, _grp(2 * g + 1, SW)] = _lane_swap(zu[:, CW:CW + SW])
            vb[:, _grp(2 * g + 1, SW)] = _lane_swap(zu[:, CW + SW:])
        co = [[a_ref[r:r + 1, _grp(g, SW)] for g in range(GB)] for r in range(4)]

        rid = lax.broadcasted_iota(jnp.int32, (8, SW), 0)

        def step8(groups, kf, kb, carry):
            rf, rb = _rows8(kf), _rows8(kb)
            lanes = slice(groups[0] * 2 * SW, (groups[-1] + 1) * 2 * SW)
            vfb, vbb = vf[rf, lanes], vb[rb, lanes]
            st = list(carry)
            of = [jnp.zeros((8, SW), F32)] * len(groups)
            ob = list(of)
            for i in range(8):
                k = 7 - i
                for n, g in enumerate(groups):
                    sf, sfs, sb, sbs = st[4 * n:4 * n + 4]
                    of[n] = jnp.where(rid == i, sf, of[n])
                    ob[n] = jnp.where(rid == k, sb, ob[n])
                    st[4 * n] = co[0][g] * sf + co[1][g] * sfs + vfb[i:i + 1, _grp(2 * n, SW)]
                    st[4 * n + 1] = co[0][g] * sfs - co[1][g] * sf + vfb[i:i + 1, _grp(2 * n + 1, SW)]
                    st[4 * n + 2] = co[2][g] * sb + co[3][g] * sbs + vbb[k:k + 1, _grp(2 * n, SW)]
                    st[4 * n + 3] = co[2][g] * sbs - co[3][g] * sb + vbb[k:k + 1, _grp(2 * n + 1, SW)]
            for n, g in enumerate(groups):
                sf_ref[rf, _grp(g, SW)] = of[n]
                sb_ref[rb, _grp(g, SW)] = ob[n]
            return tuple(st)

        zero = jnp.zeros((1, SW), F32)
        nbc, nb = ncc // 8, nc // 8
        for groups in (tuple(range(0, GB // 2)), tuple(range(GB // 2, GB))):
            carry = lax.fori_loop(0, nbc, lambda j, cr, gs=groups: step8(gs, j, nbc - 1 - j, cr),
                                  (zero,) * (4 * len(groups)))
            lax.fori_loop(nbc, nb, lambda j, cr, gs=groups: step8(gs, j, nb + nbc - 1 - j, cr), carry)
        for g in range(GB):
            st = jnp.concatenate([sf_ref[:, _grp(g, SW)], sb_ref[:, _grp(g, SW)]], axis=1).astype(BF16)
            y_ref[:, _grp(g, CW)] += _dot(st, mo_ref[g], _NT)

    blk = lambda w: pl.BlockSpec((None, nc, GB * w), lambda b, gb: (b, 0, gb))
    return pl.pallas_call(
        body, name=name, grid=(bl, S5_GROUPS // GB),
        in_specs=[blk(CW), pl.BlockSpec((GB, CW, 2 * CW), lambda b, gb: (gb, 0, 0)),
                  pl.BlockSpec((GB, CW, CW), lambda b, gb: (gb, 0, 0)),
                  pl.BlockSpec((4, GB * SW), lambda b, gb: (0, gb))],
        out_specs=(blk(CW), blk(SW), blk(SW)),
        out_shape=(jax.ShapeDtypeStruct(u.shape, F32), jax.ShapeDtypeStruct((bl, nc, S5_GROUPS * SW), F32),
                   jax.ShapeDtypeStruct((bl, nc, S5_GROUPS * SW), F32)),
        scratch_shapes=[pltpu.VMEM((nc, GB * 2 * SW), F32)] * 2, compiler_params=_params(2))(u, wcat, mout, a2)


def _s5_bwd(dy, u, sf, sb, wcat, mout, a2, ncc, *, name):
    bl, nc, _ = u.shape

    def body(dy_ref, u_ref, sf_ref, sb_ref, w_ref, mo_ref, a_ref, du_ref, dw_ref, dmo_ref, dacc_ref, gsf, gsb, dvf, dvb):
        b = pl.program_id(1)

        @pl.when(b == 0)
        def _():
            dw_ref[...] = jnp.zeros_like(dw_ref)
            dmo_ref[...] = jnp.zeros_like(dmo_ref)
            dacc_ref[...] = jnp.zeros_like(dacc_ref)

        for g in range(GB):
            ds = _dot(dy_ref[:, _grp(g, CW)].astype(BF16), mo_ref[g])
            gsf[:, _grp(2 * g, SW)] = ds[:, :SW]
            gsb[:, _grp(2 * g, SW)] = ds[:, SW:]
            gsf[:, _grp(2 * g + 1, SW)] = _lane_swap(ds[:, :SW])
            gsb[:, _grp(2 * g + 1, SW)] = _lane_swap(ds[:, SW:])
        co = [[a_ref[r:r + 1, _grp(g, SW)] for g in range(GB)] for r in range(4)]

        rid = lax.broadcasted_iota(jnp.int32, (8, SW), 0)

        def step8(groups, kf, kb, carry):
            rf, rb = _rows8(kf), _rows8(kb)
            lanes = slice(groups[0] * 2 * SW, (groups[-1] + 1) * 2 * SW)
            gfb, gbb = gsf[rf, lanes], gsb[rb, lanes]
            st = list(carry)
            of = [jnp.zeros((8, SW), F32)] * len(groups)
            ob = list(of)
            for i in range(8):
                k = 7 - i
                for n, g in enumerate(groups):
                    gf, gfs, gb_, gbs = st[4 * n:4 * n + 4]
                    of[n] = jnp.where(rid == k, gf, of[n])
                    ob[n] = jnp.where(rid == i, gb_, ob[n])
                    st[4 * n] = gfb[k:k + 1, _grp(2 * n, SW)] + co[0][g] * gf - co[1][g] * gfs
                    st[4 * n + 1] = gfb[k:k + 1, _grp(2 * n + 1, SW)] + co[0][g] * gfs + co[1][g] * gf
                    st[4 * n + 2] = gbb[i:i + 1, _grp(2 * n, SW)] + co[2][g] * gb_ - co[3][g] * gbs
                    st[4 * n + 3] = gbb[i:i + 1, _grp(2 * n + 1, SW)] + co[2][g] * gbs + co[3][g] * gb_
            for n, g in enumerate(groups):
                dvf[rf, _grp(g, SW)] = of[n]
                dvb[rb, _grp(g, SW)] = ob[n]
            return tuple(st)

        zero = jnp.zeros((1, SW), F32)
        nbc, nb = ncc // 8, nc // 8
        for groups in (tuple(range(0, GB // 2)), tuple(range(GB // 2, GB))):
            carry = lax.fori_loop(0, nb - nbc, lambda j, cr, gs=groups: step8(gs, nb - 1 - j, nbc + j, cr),
                                  (zero,) * (4 * len(groups)))
            lax.fori_loop(0, nbc, lambda j, cr, gs=groups: step8(gs, nbc - 1 - j, j, cr), carry)
        for g in range(GB):
            dyg = dy_ref[:, _grp(g, CW)].astype(BF16)
            dvf_g, dvb_g = dvf[:, _grp(g, SW)], dvb[:, _grp(g, SW)]
            sf_g, sb_g = sf_ref[:, _grp(g, SW)], sb_ref[:, _grp(g, SW)]
            dz = jnp.concatenate([dyg, dvf_g.astype(BF16), dvb_g.astype(BF16)], axis=1)
            du_ref[:, _grp(g, CW)] = _dot(dz, w_ref[g], _NT)
            dw_ref[g] += _dot(u_ref[:, _grp(g, CW)].astype(BF16), dz, _TN)
            st = jnp.concatenate([sf_g, sb_g], axis=1).astype(BF16)
            dmo_ref[g] += _dot(dyg, st, _TN)
            dacc_ref[:, _grp(g, SW)] += jnp.concatenate(
                [jnp.sum(dvf_g * sf_g, axis=0, keepdims=True), jnp.sum(dvf_g * _lane_swap(sf_g), axis=0, keepdims=True),
                 jnp.sum(dvb_g * sb_g, axis=0, keepdims=True), jnp.sum(dvb_g * _lane_swap(sb_g), axis=0, keepdims=True)],
                axis=0)

    blk = lambda w: pl.BlockSpec((None, nc, GB * w), lambda gb, b: (b, 0, gb))
    wspec = lambda mult: pl.BlockSpec((GB, CW, mult * CW), lambda gb, b: (gb, 0, 0))
    aspec = pl.BlockSpec((4, GB * SW), lambda gb, b: (0, gb))
    return pl.pallas_call(
        body, name=name, grid=(S5_GROUPS // GB, bl),
        in_specs=[blk(CW), blk(CW), blk(SW), blk(SW), wspec(2), wspec(1), aspec],
        out_specs=(blk(CW), wspec(2), wspec(1), aspec),
        out_shape=(jax.ShapeDtypeStruct(u.shape, F32), jax.ShapeDtypeStruct((S5_GROUPS, CW, 2 * CW), F32),
                   jax.ShapeDtypeStruct((S5_GROUPS, CW, CW), F32), jax.ShapeDtypeStruct(a2.shape, F32)),
        scratch_shapes=[pltpu.VMEM((nc, GB * 2 * SW), F32)] * 2 + [pltpu.VMEM((nc, GB * SW), F32)] * 2,
        compiler_params=_params(2, 48))(dy, u, sf, sb, wcat, mout, a2)


def _lane_slot():
    return lax.broadcasted_iota(jnp.int32, (GRID_W, HEAD), 1) // S5_GROUP


def _slots_to_chunk(tiles, slot, q, j):
    acc = jnp.zeros(tiles[0].shape, F32)
    for m in range(HEAD // S5_GROUP):
        shift = ((m - q) * S5_GROUP) % HEAD
        v = tiles[8 * j + m]
        acc = jnp.where(slot == m, v if shift == 0 else pltpu.roll(v, shift, 1), acc)
    return acc


def _slots_to_rows(tiles, slot, m):
    acc = jnp.zeros(tiles[0].shape, F32)
    for q in range(HEAD // S5_GROUP):
        shift = ((q - m) * S5_GROUP) % HEAD
        acc = jnp.where(slot == q, tiles[q] if shift == 0 else pltpu.roll(tiles[q], shift, 1), acc)
    return acc


def _to_chunks(src, col0, bl, s, lc, *, name):
    ncc = lc // T_CH
    nrh = (s - lc) // GRID_W // T_CH
    nc = ncc + GRID_W * nrh

    def body(x_ref, o_ref, tmp):
        slot_c = lax.broadcasted_iota(jnp.int32, (ncc, HEAD), 1) // S5_GROUP
        slot = _lane_slot()
        for j in range(CW // HEAD):
            for q in range(HEAD // S5_GROUP):
                tiles = [x_ref[pl.ds(8 * j + m, ncc, stride=T_CH), :] for m in range(8)]
                tmp[0:ncc, :] = _slots_to_chunk(tiles, slot_c, q, 0)

                def one(rh, _):
                    tiles = [x_ref[pl.ds(pl.multiple_of(lc + (rh * T_CH + 8 * j + m) * GRID_W, GRID_W), GRID_W), :]
                             for m in range(8)]
                    tmp[pl.ds(ncc + rh, GRID_W, stride=nrh), :] = _slots_to_chunk(tiles, slot, q, 0)
                    return 0

                lax.fori_loop(0, nrh, one, 0)
                o_ref[:, q * CW + j * HEAD:q * CW + (j + 1) * HEAD] = tmp[...]

    return pl.pallas_call(
        body, name=name, grid=(bl, S5_GROUPS // GB),
        in_specs=[pl.BlockSpec((s, HEAD), lambda b, gb: (b, col0 + gb))],
        out_specs=pl.BlockSpec((None, nc, GB * CW), lambda b, gb: (b, 0, gb)),
        out_shape=jax.ShapeDtypeStruct((bl, nc, S5_GROUPS * CW), F32),
        scratch_shapes=[pltpu.VMEM((nc, HEAD), F32)], compiler_params=_params(2))(src)


def _from_chunks(v, bl, s, lc, *, name):
    ncc = lc // T_CH
    nrh = (s - lc) // GRID_W // T_CH
    nc = v.shape[1]

    def body(v_ref, o_ref, *tmp):
        slot_c = lax.broadcasted_iota(jnp.int32, (ncc, HEAD), 1) // S5_GROUP
        slot = _lane_slot()
        for j in range(CW // HEAD):
            for q in range(HEAD // S5_GROUP):
                tmp[q][...] = v_ref[:, q * CW + j * HEAD:q * CW + (j + 1) * HEAD]
            for m in range(8):
                tiles = [tmp[q][0:ncc, :] for q in range(HEAD // S5_GROUP)]
                o_ref[pl.ds(8 * j + m, ncc, stride=T_CH), :] = _slots_to_rows(tiles, slot_c, m)

            def one(rh, _):
                for m in range(8):
                    tiles = [tmp[q][pl.ds(ncc + rh, GRID_W, stride=nrh), :] for q in range(HEAD // S5_GROUP)]
                    rows = pl.ds(pl.multiple_of(lc + (rh * T_CH + 8 * j + m) * GRID_W, GRID_W), GRID_W)
                    o_ref[rows, :] = _slots_to_rows(tiles, slot, m)
                return 0

            lax.fori_loop(0, nrh, one, 0)

    return pl.pallas_call(
        body, name=name, grid=(bl, S5_GROUPS // GB),
        in_specs=[pl.BlockSpec((None, nc, GB * CW), lambda b, gb: (b, 0, gb))],
        out_specs=pl.BlockSpec((s, HEAD), lambda b, gb: (b, gb)),
        out_shape=jax.ShapeDtypeStruct((bl * s, D_MODEL), F32),
        scratch_shapes=[pltpu.VMEM((nc, HEAD), F32)] * (HEAD // S5_GROUP), compiler_params=_params(2))(v)


_ANY = pl.BlockSpec(memory_space=pl.ANY)


def _xy_peers():
    x, y, c = lax.axis_index("x"), lax.axis_index("y"), lax.axis_index("c")
    return x, y, c, [(1 - x, y), (x, 1 - y), (1 - x, 1 - y)]


def _all_gather_xy(shard, *, name):
    def body(x_ref, out_ref, send_sems, recv_sems, local_sem):
        x, y, c, peers = _xy_peers()
        me = 2 * x + y
        mine = pltpu.make_async_copy(x_ref, out_ref.at[me], local_sem)
        mine.start()

        def copy(k, px, py, slot):
            return pltpu.make_async_remote_copy(src_ref=x_ref, dst_ref=out_ref.at[slot], send_sem=send_sems.at[k],
                                                recv_sem=recv_sems.at[k], device_id=(px, py, c), device_id_type=MESH)

        sends = [copy(k, px, py, me) for k, (px, py) in enumerate(peers)]
        for cp in sends:
            cp.start()
        for k, (px, py) in enumerate(peers):
            copy(k, px, py, 2 * px + py).wait_recv()
        for cp in sends:
            cp.wait_send()
        mine.wait()

    return pl.pallas_call(body, name=name, in_specs=[_ANY], out_specs=_ANY,
                          out_shape=jax.ShapeDtypeStruct((4,) + shard.shape, shard.dtype),
                          scratch_shapes=[pltpu.SemaphoreType.DMA((3,)), pltpu.SemaphoreType.DMA((3,)),
                                          pltpu.SemaphoreType.DMA])(shard)


def _scatter_xy(parts, *, name):
    def body(p_ref, out_ref, send_sems, recv_sems, local_sem):
        x, y, c, peers = _xy_peers()
        mine = pltpu.make_async_copy(p_ref.at[2 * x + y], out_ref.at[0], local_sem)
        mine.start()

        def copy(k, px, py):
            return pltpu.make_async_remote_copy(src_ref=p_ref.at[2 * px + py], dst_ref=out_ref.at[1 + k],
                                                send_sem=send_sems.at[k], recv_sem=recv_sems.at[k],
                                                device_id=(px, py, c), device_id_type=MESH)

        sends = [copy(k, px, py) for k, (px, py) in enumerate(peers)]
        for cp in sends:
            cp.start()
        for cp in sends:
            cp.wait_recv()
        for cp in sends:
            cp.wait_send()
        mine.wait()

    return pl.pallas_call(body, name=name, in_specs=[_ANY], out_specs=_ANY,
                          out_shape=jax.ShapeDtypeStruct(parts.shape, parts.dtype),
                          scratch_shapes=[pltpu.SemaphoreType.DMA((3,)), pltpu.SemaphoreType.DMA((3,)),
                                          pltpu.SemaphoreType.DMA])(parts)


def _swap_sibling(v, *, name):
    def body(v_ref, out_ref, send_sem, recv_sem):
        x, y, c = lax.axis_index("x"), lax.axis_index("y"), lax.axis_index("c")
        cp = pltpu.make_async_remote_copy(src_ref=v_ref, dst_ref=out_ref, send_sem=send_sem, recv_sem=recv_sem,
                                          device_id=(x, y, 1 - c), device_id_type=MESH)
        cp.start()
        cp.wait()

    return pl.pallas_call(body, name=name, in_specs=[_ANY], out_specs=_ANY,
                          out_shape=jax.ShapeDtypeStruct(v.shape, v.dtype),
                          scratch_shapes=[pltpu.SemaphoreType.DMA, pltpu.SemaphoreType.DMA])(v)


BIG_COLS = {'ada_w': True, 'w_in': True, 'mlp_w1': True, 's5_glu_w': False, 'w_out': False, 'mlp_w2': False}
BIG = list(BIG_COLS)


def _block(ref2d, j, cols, size):
    if cols:
        return ref2d.at[:, pl.ds(pl.multiple_of(j * size, 128), size)]
    return ref2d.at[pl.ds(pl.multiple_of(j * size, 8), size), :]


def _shard_size(shape, cols):
    return shape[-1] if cols else shape[-2]


def _cast_into_full(shard, cols, my_j, *, name):
    _, r, c = shard.shape
    tr, tc = _tile(r, (256,)), _tile(c, (1024, 768, 512))

    def body(j_ref, x_ref, o_ref):
        o_ref[...] = x_ref[...].astype(BF16)

    if cols:
        out_spec = pl.BlockSpec((None, tr, tc), lambda l, i, j, j_ref: (l, i, j_ref[0] * (c // tc) + j))
    else:
        out_spec = pl.BlockSpec((None, tr, tc), lambda l, i, j, j_ref: (l, j_ref[0] * (r // tr) + i, j))
    return pl.pallas_call(
        body, name=name,
        grid_spec=pltpu.PrefetchScalarGridSpec(
            num_scalar_prefetch=1, grid=(DEPTH, r // tr, c // tc),
            in_specs=[pl.BlockSpec((None, tr, tc), lambda l, i, j, j_ref: (l, i, j))], out_specs=out_spec),
        out_shape=jax.ShapeDtypeStruct((DEPTH, r, 4 * c) if cols else (DEPTH, 4 * r, c), BF16),
        compiler_params=_params(3))(my_j, shard)


def _gather_big(fulls, cols, *, name):
    n = len(fulls)

    def body(*refs):
        outs = refs[n:2 * n]
        ici_send, ici_recv, d2d_send, d2d_recv = refs[2 * n:]
        x, y, c, peers = _xy_peers()
        me = 2 * x + y

        def blk(w, layer, j):
            shape = outs[w].shape
            return _block(outs[w].at[layer], j, cols[w], (shape[2] if cols[w] else shape[1]) // 4)

        def ici(w, k, px, py, j):
            return pltpu.make_async_remote_copy(src_ref=blk(w, c, j), dst_ref=blk(w, c, j),
                                                send_sem=ici_send.at[3 * w + k], recv_sem=ici_recv.at[3 * w + k],
                                                device_id=(px, py, c), device_id_type=MESH)

        def d2d(w, k, j, layer):
            return pltpu.make_async_remote_copy(src_ref=blk(w, layer, j), dst_ref=blk(w, layer, j),
                                                send_sem=d2d_send.at[3 * w + k], recv_sem=d2d_recv.at[3 * w + k],
                                                device_id=(x, y, 1 - c), device_id_type=MESH)

        started = [ici(w, k, px, py, me) for w in range(n) for k, (px, py) in enumerate(peers)]
        for cp in started:
            cp.start()
        passed = []
        for w in range(n):
            for k, (px, py) in enumerate(peers):
                ici(w, k, px, py, 2 * px + py).wait_recv()
                passed.append(d2d(w, k, 2 * px + py, c))
                passed[-1].start()
        for w in range(n):
            for k, (px, py) in enumerate(peers):
                d2d(w, k, 2 * px + py, 1 - c).wait_recv()
        for cp in started + passed:
            cp.wait_send()

    return pl.pallas_call(
        body, name=name, in_specs=[_ANY] * n, out_specs=[_ANY] * n,
        out_shape=[jax.ShapeDtypeStruct(f.shape, f.dtype) for f in fulls],
        input_output_aliases={w: w for w in range(n)},
        scratch_shapes=[pltpu.SemaphoreType.DMA((3 * n,))] * 4)(*fulls)


def _sibling_partials(gbufs, *, name):
    n = len(gbufs)

    def body(*refs):
        ins, outs, send, recv = refs[:n], refs[n:2 * n], refs[2 * n], refs[2 * n + 1]
        x, y, c = lax.axis_index("x"), lax.axis_index("y"), lax.axis_index("c")
        cps = [pltpu.make_async_remote_copy(src_ref=ins[w].at[1 - c], dst_ref=outs[w], send_sem=send.at[w],
                                            recv_sem=recv.at[w], device_id=(x, y, 1 - c), device_id_type=MESH)
               for w in range(n)]
        for cp in cps:
            cp.start()
        for cp in cps:
            cp.wait()

    return pl.pallas_call(body, name=name, in_specs=[_ANY] * n, out_specs=[_ANY] * n,
                          out_shape=[jax.ShapeDtypeStruct(g.shape[1:], g.dtype) for g in gbufs],
                          scratch_shapes=[pltpu.SemaphoreType.DMA((n,))] * 2)(*gbufs)


def _chip_sum(gbuf, other, my_c, *, name):
    _, k, n = gbuf.shape
    tr, tc = _tile(k, (512,)), _tile(n, (1024,))

    def body(c_ref, a_ref, b_ref, o_ref):
        o_ref[...] = (a_ref[...] + b_ref[...]).astype(BF16)

    spec = pl.BlockSpec((tr, tc), lambda i, j, c_ref: (i, j))
    return pl.pallas_call(
        body, name=name,
        grid_spec=pltpu.PrefetchScalarGridSpec(
            num_scalar_prefetch=1, grid=(k // tr, n // tc),
            in_specs=[pl.BlockSpec((None, tr, tc), lambda i, j, c_ref: (c_ref[0], i, j)), spec], out_specs=spec),
        out_shape=jax.ShapeDtypeStruct((k, n), BF16), compiler_params=_params(2))(my_c, gbuf, other)


def _scatter_big(sums, cols, *, name):
    n = len(sums)

    def shard(s, cf):
        return (s.shape[0], s.shape[1] // 4) if cf else (s.shape[0] // 4, s.shape[1])

    def body(*refs):
        ins, outs, send, recv = refs[:n], refs[n:2 * n], refs[2 * n], refs[2 * n + 1]
        x, y, c, peers = _xy_peers()
        cps = []
        for w in range(n):
            size = _shard_size(shard(ins[w], cols[w]), cols[w])
            for k, (px, py) in enumerate(peers):
                cps.append(pltpu.make_async_remote_copy(
                    src_ref=_block(ins[w], 2 * px + py, cols[w], size), dst_ref=outs[w].at[k],
                    send_sem=send.at[3 * w + k], recv_sem=recv.at[3 * w + k], device_id=(px, py, c),
                    device_id_type=MESH))
        for cp in cps:
            cp.start()
        for cp in cps:
            cp.wait()

    return pl.pallas_call(body, name=name, in_specs=[_ANY] * n, out_specs=[_ANY] * n,
                          out_shape=[jax.ShapeDtypeStruct((3,) + shard(s, cf), s.dtype) for s, cf in zip(sums, cols)],
                          scratch_shapes=[pltpu.SemaphoreType.DMA((3 * n,))] * 2)(*sums)


def _block_sum(own, got, cols, my_j, my_c, *, name):
    _, r, c = got.shape
    tr, tc = _tile(r, (256,)), _tile(c, (1024, 768, 512))

    def body(j_ref, c_ref, a_ref, g_ref, o_ref):
        o_ref[...] = ((a_ref[...].astype(F32) + g_ref[0].astype(F32)) + g_ref[1].astype(F32)) + g_ref[2].astype(F32)

    if cols:
        own_spec = pl.BlockSpec((tr, tc), lambda i, j, j_ref, c_ref: (i, j_ref[0] * (c // tc) + j))
    else:
        own_spec = pl.BlockSpec((tr, tc), lambda i, j, j_ref, c_ref: (j_ref[0] * (r // tr) + i, j))
    return pl.pallas_call(
        body, name=name,
        grid_spec=pltpu.PrefetchScalarGridSpec(
            num_scalar_prefetch=2, grid=(r // tr, c // tc),
            in_specs=[own_spec, pl.BlockSpec((3, tr, tc), lambda i, j, j_ref, c_ref: (0, i, j))],
            out_specs=pl.BlockSpec((None, tr, tc), lambda i, j, j_ref, c_ref: (c_ref[0], i, j))),
        out_shape=jax.ShapeDtypeStruct((DEPTH, r, c), F32), compiler_params=_params(2))(my_j, my_c, own, got)


def _share_final(bufs, *, name):
    n = len(bufs)

    def body(*refs):
        outs, send, recv = refs[n:2 * n], refs[2 * n], refs[2 * n + 1]
        x, y, c = lax.axis_index("x"), lax.axis_index("y"), lax.axis_index("c")

        def copy(w, slot):
            return pltpu.make_async_remote_copy(src_ref=outs[w].at[slot], dst_ref=outs[w].at[slot],
                                                send_sem=send.at[w], recv_sem=recv.at[w],
                                                device_id=(x, y, 1 - c), device_id_type=MESH)

        away = [copy(w, c) for w in range(n)]
        for cp in away:
            cp.start()
        for w in range(n):
            copy(w, 1 - c).wait_recv()
        for cp in away:
            cp.wait_send()

    return pl.pallas_call(body, name=name, in_specs=[_ANY] * n, out_specs=[_ANY] * n,
                          out_shape=[jax.ShapeDtypeStruct(b.shape, b.dtype) for b in bufs],
                          input_output_aliases={w: w for w in range(n)},
                          scratch_shapes=[pltpu.SemaphoreType.DMA((n,))] * 2)(*bufs)


def _adamw_native(w, g, m, v, *, name):
    r, c = w.shape
    tr = _tile(r, (256, 128, 64, 32, 16, 8))
    spec = pl.BlockSpec((tr, c), lambda i: (i, 0))
    c1 = 1.0 / (1.0 - ADAM_B1 ** ADAM_STEP)
    c2 = 1.0 / (1.0 - ADAM_B2 ** ADAM_STEP)

    def body(w_ref, g_ref, m_ref, v_ref, d_ref, nm_ref, nv_ref):
        g_t = g_ref[...]
        nm = ADAM_B1 * m_ref[...] + (1.0 - ADAM_B1) * g_t
        nv = ADAM_B2 * v_ref[...] + (1.0 - ADAM_B2) * (g_t * g_t)
        nm_ref[...] = nm
        nv_ref[...] = nv
        d_ref[...] = -ADAM_LR * ((nm * c1) / (jnp.sqrt(nv * c2) + ADAM_EPS) + ADAM_WD * w_ref[...])

    shp = jax.ShapeDtypeStruct((r, c), F32)
    return pl.pallas_call(body, name=name, grid=(r // tr,), in_specs=[spec] * 4, out_specs=(spec,) * 3,
                          out_shape=(shp,) * 3, compiler_params=_params(1))(w, g, m, v)


def _flat_tile(r):
    return _tile(r, (512, 256, 128, 64, 32, 16, 8))


def _sum4(parts, *, name):
    r = parts.shape[1]
    tr = _flat_tile(r)

    def body(p_ref, o_ref):
        o_ref[...] = ((p_ref[0] + p_ref[1]) + p_ref[2]) + p_ref[3]

    return pl.pallas_call(body, name=name, grid=(r // tr,),
                          in_specs=[pl.BlockSpec((4, tr, LANES), lambda i: (0, i, 0))],
                          out_specs=pl.BlockSpec((tr, LANES), lambda i: (i, 0)),
                          out_shape=jax.ShapeDtypeStruct((r, LANES), F32), compiler_params=_params(1))(parts)


def _add2(a, b, *, name):
    r = a.shape[0]
    tr = _flat_tile(r)
    spec = pl.BlockSpec((tr, LANES), lambda i: (i, 0))

    def body(a_ref, b_ref, o_ref):
        o_ref[...] = a_ref[...] + b_ref[...]

    return pl.pallas_call(body, name=name, grid=(r // tr,), in_specs=[spec, spec], out_specs=spec,
                          out_shape=jax.ShapeDtypeStruct((r, LANES), F32), compiler_params=_params(1))(a, b)


def _adamw(w, ga, gb, m, v, *, name):
    r = w.shape[0]
    tr = _flat_tile(r)
    spec = pl.BlockSpec((tr, LANES), lambda i: (i, 0))
    two = gb is not None
    c1 = 1.0 / (1.0 - ADAM_B1 ** ADAM_STEP)
    c2 = 1.0 / (1.0 - ADAM_B2 ** ADAM_STEP)

    def body(*refs):
        w_ref, ga_ref = refs[0], refs[1]
        m_ref, v_ref, g_ref, d_ref, nm_ref, nv_ref = refs[2 + two:]
        g = ga_ref[...] + refs[2][...] if two else ga_ref[...]
        nm = ADAM_B1 * m_ref[...] + (1.0 - ADAM_B1) * g
        nv = ADAM_B2 * v_ref[...] + (1.0 - ADAM_B2) * (g * g)
        g_ref[...] = g
        nm_ref[...] = nm
        nv_ref[...] = nv
        d_ref[...] = -ADAM_LR * ((nm * c1) / (jnp.sqrt(nv * c2) + ADAM_EPS) + ADAM_WD * w_ref[...])

    args = [w, ga] + ([gb] if two else []) + [m, v]
    shp = jax.ShapeDtypeStruct((r, LANES), F32)
    return pl.pallas_call(body, name=name, grid=(r // tr,), in_specs=[spec] * len(args), out_specs=(spec,) * 4,
                          out_shape=(shp,) * 4, compiler_params=_params(1))(*args)


def _pack(arrs, dtype=F32):
    rows = []
    for a in arrs:
        flat = a.astype(dtype).reshape(-1)
        rows.append(jnp.pad(flat, (0, (-flat.shape[0]) % LANES)).reshape(-1, LANES))
    buf = jnp.concatenate(rows)
    return jnp.pad(buf, ((0, (-buf.shape[0]) % 32), (0, 0)))


def _unpack(buf, shapes):
    out, row = [], 0
    for shp in shapes:
        sz = math.prod(shp)
        nr = -(-sz // LANES)
        out.append(buf[row:row + nr].reshape(-1)[:sz].reshape(shp))
        row += nr
    return out


def _stack_shards(full, axis):
    shp = full.shape
    return jnp.moveaxis(full.reshape(shp[:axis] + (4, shp[axis] // 4) + shp[axis + 1:]), axis, 0)


def _unstack_shards(st, axis):
    v = jnp.moveaxis(st, 0, axis)
    shp = v.shape
    return v.reshape(shp[:axis] + (shp[axis] * shp[axis + 1],) + shp[axis + 2:])


def _layer_weights(w, l):
    lw = {n: (w[n], l) if n in BIG_COLS else w[n][l] for n in w}
    lw['wcat'] = jnp.concatenate([lw['rg_wa'][0], lw['rg_wi'][0], lw['rg_wa'][1], lw['rg_wi'][1]],
                                 axis=-1).astype(BF16)
    ba, bi = lw['rg_ba'].reshape(2, RG_HEADS, HEAD), lw['rg_bi'].reshape(2, RG_HEADS, HEAD)
    lw['bcat'] = jnp.concatenate([ba[0], bi[0], ba[1], bi[1]], axis=-1)[:, None, :]
    s5_names = ['s5_a_re', 's5_a_im', 's5_log_dt', 's5_b_re', 's5_b_im', 's5_c_re', 's5_c_im']
    (wcat, mout, a2), lw['s5_vjp'] = jax.vjp(_s5_mats, *[lw[n] for n in s5_names])
    lw['s5_wcat'], lw['s5_mout'], lw['s5_a2'] = wcat.astype(BF16), mout.astype(BF16), a2
    for n in ('conv_b', 's5_d', 's5_glu_b', 'b_out', 'mlp_b1', 'mlp_b2', 'ln1_g', 'ln1_b', 'ln2_g', 'ln2_b'):
        lw[n] = lw[n][None, :]
    return lw


def _layer_fwd(l, x0, modall, lw, dims):
    bl, s, lc, tps = dims
    ll = s - lc
    tag = f"l{l}_"
    sv = {'x0': x0}
    sv['u1'] = _modulate(x0, modall, 0, 1, tps, name=tag + "mod1")
    sv['p'] = p = _mm_nn(sv['u1'], lw['w_in'], name=tag + "w_in")
    rg, sv['hf'], sv['hb'] = _rg_fwd(p, lw['conv_w'], lw['conv_b'], lw['rg_lambda'], lw['wcat'], lw['bcat'],
                                     bl, s, lc, name=tag + "rg_fwd")
    sv['u_ch'] = _to_chunks(p, 2 * D_MODEL // HEAD, bl, s, lc, name=tag + "u_chunks")
    y_ch, sv['sf'], sv['sb'] = _s5_fwd(sv['u_ch'], lw['s5_wcat'], lw['s5_mout'], lw['s5_a2'], lc // T_CH,
                                       name=tag + "s5_fwd")
    sv['y'] = _from_chunks(y_ch, bl, s, lc, name=tag + "y_rows")
    s5 = _glu_fwd(sv['y'], p, lw['s5_d'], lw['s5_glu_w'], lw['s5_glu_b'], name=tag + "glu_fwd")
    sv['cat'] = jnp.concatenate([rg, s5], axis=1)
    sv['m'] = _mm_nn(sv['cat'], lw['w_out'], lw['b_out'], name=tag + "w_out")
    sv['x1'] = _resid_ln(x0, sv['m'], modall, 2, lw['ln1_g'], lw['ln1_b'], tps, name=tag + "ln1")
    sv['u2'] = _modulate(sv['x1'], modall, 3, 4, tps, name=tag + "mod2")
    sv['a'], sv['h'] = _mm_nn(sv['u2'], lw['mlp_w1'], lw['mlp_b1'], relu2=True, name=tag + "mlp1")
    sv['f'] = _mm_nn(sv['a'], lw['mlp_w2'], lw['mlp_b2'], name=tag + "mlp2")
    x2 = _resid_ln(sv['x1'], sv['f'], modall, 5, lw['ln2_g'], lw['ln2_b'], tps, name=tag + "ln2")
    return x2, sv


def _layer_bwd(l, dx2, modall, lw, sv, dims, gbufs):
    bl, s, lc, tps = dims
    ll = s - lc
    tag = f"l{l}_"
    g = {}

    def big_grad(n, a_mat, b_mat, label):
        gbufs[n] = _mm_tn(a_mat, b_mat, name=tag + label, layer=l, into=gbufs.get(n))
    dx1a, df, db2, g['ln2_g'], g['ln2_b'], dg2 = _resid_ln_bwd(sv['x1'], sv['f'], modall, 5, lw['ln2_g'], dx2, tps,
                                                              name=tag + "ln2_bwd")
    g['mlp_b2'] = db2
    big_grad('mlp_w2', sv['a'], df, "mlp2_dw")
    dh = _mm_nt(df, lw['mlp_w2'], sv['h'], name=tag + "mlp2_dx")
    g['mlp_b1'] = _colsum(dh, name=tag + "mlp1_db")
    big_grad('mlp_w1', sv['u2'], dh, "mlp1_dw")
    du2 = _mm_nt(dh, lw['mlp_w1'], name=tag + "mlp1_dx")
    dx1, dsc2, dsh2 = _modulate_bwd(du2, sv['x1'], modall, 4, dx1a, tps, name=tag + "mod2_bwd")
    dx0a, dm, g['b_out'], g['ln1_g'], g['ln1_b'], dg1 = _resid_ln_bwd(sv['x0'], sv['m'], modall, 2, lw['ln1_g'], dx1,
                                                                     tps, name=tag + "ln1_bwd")
    big_grad('w_out', sv['cat'], dm, "w_out_dw")
    dcat = _mm_nt(dm, lw['w_out'], name=tag + "w_out_dx")
    dy, dskip, g_bf, dz_bf, g['s5_d'], g['s5_glu_b'] = _glu_bwd(dcat, sv['y'], sv['p'], lw['s5_d'], lw['s5_glu_w'],
                                                                lw['s5_glu_b'], name=tag + "glu_bwd")
    big_grad('s5_glu_w', g_bf, dz_bf, "glu_dw")
    dy_ch = _to_chunks(dy, 0, bl, s, lc, name=tag + "dy_chunks")
    du_ch, dwcat, dmout, dacc = _s5_bwd(dy_ch, sv['u_ch'], sv['sf'], sv['sb'], lw['s5_wcat'],
                                        lw['s5_mout'], lw['s5_a2'], lc // T_CH, name=tag + "s5_bwd")
    s5g = lw['s5_vjp']((dwcat, dmout, dacc))
    for n, v in zip(['s5_a_re', 's5_a_im', 's5_log_dt', 's5_b_re', 's5_b_im', 's5_c_re', 's5_c_im'], s5g):
        g[n] = v
    ds5u = _add_cast(_from_chunks(du_ch, bl, s, lc, name=tag + "du_rows"), dskip, name=tag + "ds5u")
    drgx, dgate, dcw, dcb, dlam, dwc, dbc = _rg_bwd(sv['p'], dcat, sv['hf'], sv['hb'], lw['conv_w'], lw['conv_b'],
                                                    lw['rg_lambda'], lw['wcat'], lw['bcat'], bl, s, lc,
                                                    name=tag + "rg_bwd")
    g['conv_w'] = dcw.transpose(1, 0, 2).reshape(4, D_MODEL)
    g['conv_b'] = dcb.reshape(D_MODEL)
    g['rg_lambda'] = dlam.transpose(1, 0, 2).reshape(2, D_MODEL)
    g['rg_wa'] = jnp.stack([dwc[:, :, 0:HEAD], dwc[:, :, 2 * HEAD:3 * HEAD]])
    g['rg_wi'] = jnp.stack([dwc[:, :, HEAD:2 * HEAD], dwc[:, :, 3 * HEAD:]])
    dbc = dbc.reshape(RG_HEADS, 4, HEAD)
    g['rg_ba'] = jnp.stack([dbc[:, 0], dbc[:, 2]]).reshape(2, D_MODEL)
    g['rg_bi'] = jnp.stack([dbc[:, 1], dbc[:, 3]]).reshape(2, D_MODEL)
    dp = jnp.concatenate([drgx, dgate, ds5u], axis=1)
    big_grad('w_in', sv['u1'], dp, "w_in_dw")
    du1 = _mm_nt(dp, lw['w_in'], name=tag + "w_in_dx")
    dx0, dsc1, dsh1 = _modulate_bwd(du1, sv['x0'], modall, 1, dx0a, tps, name=tag + "mod1_bwd")
    dmod = jnp.concatenate([dsh1, dsc1, dg1, dsh2, dsc2, dg2], axis=1)
    return dx0, g, dmod


def _kernel_impl(*args):
    nin = len(IN_NAMES)
    a = dict(zip(IN_NAMES, args[:nin]))
    target = args[nin]
    nw = len(WEIGHTS)
    mom = dict(zip(WEIGHTS, args[nin + 1:nin + 1 + nw]))
    var = dict(zip(WEIGHTS, args[nin + 1 + nw:nin + 1 + 2 * nw]))
    bl, ll, d = a['x'].shape
    lc = a['ctx'].shape[1]
    assert d == D_MODEL and lc == TM and bl == 2 and ll % (GRID_W * T_CH) == 0
    s = lc + ll
    tps = s // TM
    dims = (bl, s, lc, tps)

    def gather(names, dtype, tag):
        shards = [a[n] for n in names]
        got = _all_gather_xy(_pack(shards, dtype), name="gather_" + tag)
        per = [_unpack(got[j], [w.shape for w in shards]) for j in range(4)]
        return {n: _unstack_shards(jnp.stack([per[j][i] for j in range(4)]), SHARD_AXIS[n])
                for i, n in enumerate(names)}

    big_cols = [BIG_COLS[n] for n in BIG]
    my_c = lax.axis_index("c").astype(jnp.int32).reshape(1)
    my_j = (2 * lax.axis_index("x") + lax.axis_index("y")).astype(jnp.int32).reshape(1)
    mine = [_cast_into_full(a[n], BIG_COLS[n], my_j, name=f"cast_{n}") for n in BIG]
    w = dict(zip(BIG, _gather_big(mine, big_cols, name="gather_big")))
    w.update(gather(GATHER_F32, F32, "f32"))
    for n in REPLICATED:
        w[n] = a[n]

    xs = jnp.concatenate([a['ctx'], a['x']], axis=1).reshape(bl * s, D_MODEL)
    c16 = jnp.zeros((16, D_MODEL), F32).at[0:2].set(a['c']).at[2].set(a['c_ctx'])
    s16, ds16 = _silu_rows(c16, name="silu")
    s16b = s16.astype(BF16)
    layers, saved, mods = [], [], []
    for l in range(DEPTH):
        lw = _layer_weights({n: w[n] for n in WEIGHTS if n not in ('c_ctx',)}, l)
        mod16 = _mm_nn(s16b, lw['ada_w'], lw['ada_b'][None, :], name=f"l{l}_ada").reshape(16, N_MOD, D_MODEL)
        modall = jnp.stack([mod16[2], mod16[0], mod16[2], mod16[1]])
        xs, sv = _layer_fwd(l, xs, modall, lw, dims)
        layers.append(lw)
        saved.append(sv)
        mods.append(modall)
    lossrow, dx = _loss_head(xs, target.reshape(bl * ll, D_MODEL), tps, name="loss_head")
    loss = lax.psum(0.5 / D_MODEL * jnp.sum(lossrow), ("x", "y", "c"))

    small = [n for n in WEIGHTS if n != 'c_ctx' and n not in BIG_COLS]
    grads = {n: [None] * DEPTH for n in small}
    gbufs = {}
    ds_rows = jnp.zeros((16, D_MODEL), F32)
    for l in reversed(range(DEPTH)):
        dx, g, dmod = _layer_bwd(l, dx, mods[l], layers[l], saved[l], dims, gbufs)
        dmod16 = jnp.zeros((16, N_MOD * D_MODEL), F32).at[0].set(dmod[1].reshape(-1)).at[1].set(
            dmod[3].reshape(-1)).at[2].set((dmod[0] + dmod[2]).reshape(-1))
        dmod16b = dmod16.astype(BF16)
        gbufs['ada_w'] = _mm_tn(s16b, dmod16b, name=f"l{l}_ada_dw", layer=l, into=gbufs.get('ada_w'))
        g['ada_b'] = _colsum(dmod16, name=f"l{l}_ada_db")
        ds_rows = ds_rows + _mm_nt(dmod16b, layers[l]['ada_w'], name=f"l{l}_ada_dx")
        for n, v in g.items():
            grads[n][l] = v.reshape(a[n].shape[1:] if n in REPLICATED else w[n].shape[1:])
    full = {n: jnp.stack(v) for n, v in grads.items()}
    full['c_ctx'] = _mul_rows(ds_rows, ds16, name="silu_bwd")[2]
    grad_x = dx.reshape(bl, s, D_MODEL)[:, lc:]

    from_sib = _sibling_partials([gbufs[n] for n in BIG], name="grad_big_sibling")
    sums = [_chip_sum(gbufs[n], o, my_c, name=f"grad_chip_sum_{n}") for n, o in zip(BIG, from_sib)]
    got = _scatter_big(sums, big_cols, name="grad_big_scatter")
    finals = [_block_sum(sm, gt, cf, my_j, my_c, name=f"grad_block_sum_{n}")
              for n, sm, gt, cf in zip(BIG, sums, got, big_cols)]
    res_big = {}
    for n, gfull in zip(BIG, _share_final(finals, name="grad_big_share")):
        flat = lambda t: t.reshape(-1, t.shape[-1])
        d_w, n_m, n_v = _adamw_native(flat(a[n]), flat(gfull), flat(mom[n]), flat(var[n]), name=f"adamw_{n}")
        res_big[n] = [gfull] + [t.reshape(a[n].shape) for t in (d_w, n_m, n_v)]

    rep_flat = _pack([full[n] for n in REPLICATED])
    rr = rep_flat.shape[0]
    sh_stacked = [_stack_shards(full[n], SHARD_AXIS[n] + 0).reshape(4, -1) for n in SHARDED]
    parts = jnp.concatenate(sh_stacked + [rep_flat.reshape(4, -1)], axis=1)
    pad = (-parts.shape[1]) % (32 * LANES)
    parts = jnp.pad(parts, ((0, 0), (0, pad))).reshape(4, -1, LANES)
    mine = _sum4(_scatter_xy(parts, name="grad_scatter"), name="grad_sum4")
    other = _swap_sibling(mine, name="grad_swap")
    n_sh = sum(math.prod(a[n].shape) for n in SHARDED)
    r_sh = n_sh // LANES
    assert n_sh % LANES == 0
    rq = rr // 4

    sh_shapes = [a[n].shape for n in SHARDED]
    pk = lambda dct: _pack([dct[n] for n in SHARDED])
    r_pk = pk(a).shape[0]
    take = lambda buf: jnp.pad(buf[:r_sh], ((0, r_pk - r_sh), (0, 0)))
    outs_sh = _adamw(pk(a), take(mine), take(other), pk(mom), pk(var), name="adamw_sharded")
    res_sh = [dict(zip(SHARDED, _unpack(o, sh_shapes))) for o in outs_sh]

    quarter = _add2(mine[r_sh:r_sh + rq], other[r_sh:r_sh + rq], name="grad_rep_sum")
    rep_g = _all_gather_xy(quarter, name="grad_rep_gather").reshape(rr, LANES)
    rep_shapes = [a[n].shape for n in REPLICATED]
    pr = lambda dct: _pack([dct[n] for n in REPLICATED])
    outs_rep = _adamw(pr(a), rep_g, None, pr(mom), pr(var), name="adamw_replicated")
    res_rep = [dict(zip(REPLICATED, _unpack(o, rep_shapes))) for o in outs_rep]

    out = [loss, grad_x]
    for k in range(4):
        out += [res_big[n][k] if n in BIG_COLS else res_sh[k][n] if n in SHARDED else res_rep[k][n] for n in WEIGHTS]
    return tuple(out)


def kernel(x, c, ctx, c_ctx, ada_w, ada_b, ln1_g, ln1_b, w_in, conv_w, conv_b, rg_lambda, rg_wa, rg_ba, rg_wi, rg_bi, s5_a_re, s5_a_im, s5_log_dt, s5_b_re, s5_b_im, s5_c_re, s5_c_im, s5_d, s5_glu_w, s5_glu_b, w_out, b_out, ln2_g, ln2_b, mlp_w1, mlp_b1, mlp_w2, mlp_b2, loss_target, m_c_ctx, m_ada_w, m_ada_b, m_ln1_g, m_ln1_b, m_w_in, m_conv_w, m_conv_b, m_rg_lambda, m_rg_wa, m_rg_ba, m_rg_wi, m_rg_bi, m_s5_a_re, m_s5_a_im, m_s5_log_dt, m_s5_b_re, m_s5_b_im, m_s5_c_re, m_s5_c_im, m_s5_d, m_s5_glu_w, m_s5_glu_b, m_w_out, m_b_out, m_ln2_g, m_ln2_b, m_mlp_w1, m_mlp_b1, m_mlp_w2, m_mlp_b2, v_c_ctx, v_ada_w, v_ada_b, v_ln1_g, v_ln1_b, v_w_in, v_conv_w, v_conv_b, v_rg_lambda, v_rg_wa, v_rg_ba, v_rg_wi, v_rg_bi, v_s5_a_re, v_s5_a_im, v_s5_log_dt, v_s5_b_re, v_s5_b_im, v_s5_c_re, v_s5_c_im, v_s5_d, v_s5_glu_w, v_s5_glu_b, v_w_out, v_b_out, v_ln2_g, v_ln2_b, v_mlp_w1, v_mlp_b1, v_mlp_w2, v_mlp_b2):
    return _kernel_impl(x, c, ctx, c_ctx, ada_w, ada_b, ln1_g, ln1_b, w_in, conv_w, conv_b, rg_lambda, rg_wa, rg_ba, rg_wi, rg_bi, s5_a_re, s5_a_im, s5_log_dt, s5_b_re, s5_b_im, s5_c_re, s5_c_im, s5_d, s5_glu_w, s5_glu_b, w_out, b_out, ln2_g, ln2_b, mlp_w1, mlp_b1, mlp_w2, mlp_b2, loss_target, m_c_ctx, m_ada_w, m_ada_b, m_ln1_g, m_ln1_b, m_w_in, m_conv_w, m_conv_b, m_rg_lambda, m_rg_wa, m_rg_ba, m_rg_wi, m_rg_bi, m_s5_a_re, m_s5_a_im, m_s5_log_dt, m_s5_b_re, m_s5_b_im, m_s5_c_re, m_s5_c_im, m_s5_d, m_s5_glu_w, m_s5_glu_b, m_w_out, m_b_out, m_ln2_g, m_ln2_b, m_mlp_w1, m_mlp_b1, m_mlp_w2, m_mlp_b2, v_c_ctx, v_ada_w, v_ada_b, v_ln1_g, v_ln1_b, v_w_in, v_conv_w, v_conv_b, v_rg_lambda, v_rg_wa, v_rg_ba, v_rg_wi, v_rg_bi, v_s5_a_re, v_s5_a_im, v_s5_log_dt, v_s5_b_re, v_s5_b_im, v_s5_c_re, v_s5_c_im, v_s5_d, v_s5_glu_w, v_s5_glu_b, v_w_out, v_b_out, v_ln2_g, v_ln2_b, v_mlp_w1, v_mlp_b1, v_mlp_w2, v_mlp_b2)
```

```python
import functools
import math

import jax
import jax.numpy as jnp
from jax import lax
from jax.experimental import pallas as pl
from jax.experimental.pallas import tpu as pltpu

F32 = jnp.float32
BF16 = jnp.bfloat16
MESH = pl.DeviceIdType.MESH

D_MODEL = 1024
N_MOD = 6
GRID_W = 64
RG_HEADS = 8
HEAD = 128
RG_C = 8.0
S5_GROUPS = 64
S5_GROUP = 16
S5_STATE = 64
T_CH = 16
GB = 8
CW = T_CH * S5_GROUP
SW = 2 * S5_STATE
DEPTH = 2
ALPHA = (2.0 * DEPTH) ** 0.25
LN_EPS = 1e-5
TM = 256
LANES = 1024
ADAM_LR, ADAM_B1, ADAM_B2, ADAM_EPS, ADAM_WD, ADAM_STEP = 0.001, 0.9, 0.999, 1e-08, 0.01, 10
MIB = 2 ** 20

IN_NAMES = ['x', 'c', 'ctx', 'c_ctx', 'ada_w', 'ada_b', 'ln1_g', 'ln1_b', 'w_in', 'conv_w', 'conv_b', 'rg_lambda',
            'rg_wa', 'rg_ba', 'rg_wi', 'rg_bi', 's5_a_re', 's5_a_im', 's5_log_dt', 's5_b_re', 's5_b_im', 's5_c_re',
            's5_c_im', 's5_d', 's5_glu_w', 's5_glu_b', 'w_out', 'b_out', 'ln2_g', 'ln2_b', 'mlp_w1', 'mlp_b1',
            'mlp_w2', 'mlp_b2']
WEIGHTS = IN_NAMES[3:]
SHARD_AXIS = {'ada_w': 2, 'w_in': 2, 'conv_w': 2, 'rg_lambda': 2, 'rg_ba': 2, 'rg_bi': 2, 's5_glu_w': 1, 'w_out': 1,
              'mlp_w1': 2, 'mlp_w2': 1}
SHARDED = ['conv_w', 'rg_lambda', 'rg_ba', 'rg_bi']
REPLICATED = [n for n in WEIGHTS if n not in SHARD_AXIS]
GATHER_BF16 = ['ada_w', 'w_in', 's5_glu_w', 'w_out', 'mlp_w1', 'mlp_w2']
GATHER_F32 = ['conv_w', 'rg_lambda', 'rg_ba', 'rg_bi']


def _params(n_axes, vmem_mb=40):
    return pltpu.CompilerParams(dimension_semantics=("arbitrary",) * n_axes, vmem_limit_bytes=vmem_mb * MIB)


def _tile(n, options):
    for t in options:
        if n % t == 0:
            return t
    return n


def _sigmoid(z):
    return 1.0 / (1.0 + jnp.exp(-z))


def _softplus(z):
    return jnp.maximum(z, 0.0) + jnp.log(1.0 + jnp.exp(-jnp.abs(z)))


def _neg_expm1_small(z):
    return -z * (1.0 + 0.5 * z * (1.0 + (1.0 / 3.0) * z * (1.0 + 0.25 * z)))


_G0 = math.sqrt(2.0 / math.pi)
_G1 = 0.044715


def _gelu(v):
    return 0.5 * v * (1.0 + jnp.tanh(_G0 * (v + _G1 * v * v * v)))


def _gelu_and_grad(v):
    t = jnp.tanh(_G0 * (v + _G1 * v * v * v))
    g = 0.5 * v * (1.0 + t)
    dg = 0.5 * (1.0 + t) + 0.5 * v * (1.0 - t * t) * _G0 * (1.0 + 3.0 * _G1 * v * v)
    return g, dg


def _seq_of_tile(i, tps):
    return 2 * (i // tps) + jnp.minimum(i % tps, 1)


def _dot(a, b, dims=(((1,), (0,)), ((), ()))):
    return lax.dot_general(a, b, dims, preferred_element_type=F32)


_NT = (((1,), (1,)), ((), ()))
_TN = (((0,), (0,)), ((), ()))


def _layer_spec(b, block, index):
    if isinstance(b, tuple):
        arr, layer = b
        return arr, pl.BlockSpec((None,) + block, lambda *g: (layer,) + index(*g))
    return b, pl.BlockSpec(block, index)


def _mm_nn(a, b, bias=None, *, relu2=False, name):
    m, k = a.shape
    n = (b[0] if isinstance(b, tuple) else b).shape[-1]
    tm, tn, tk = _tile(m, (512, 256)), _tile(n, (1024,)), _tile(k, (1024,))
    nk = k // tk
    has_bias = bias is not None

    def body(*refs):
        a_ref, b_ref = refs[0], refs[1]
        bias_ref = refs[2] if has_bias else None
        outs = refs[2 + has_bias:-1]
        acc = refs[-1]
        kk = pl.program_id(2)

        @pl.when(kk == 0)
        def _():
            acc[...] = jnp.zeros_like(acc)

        acc[...] += _dot(a_ref[...], b_ref[...])

        @pl.when(kk == nk - 1)
        def _():
            h = acc[...]
            if has_bias:
                h = h + bias_ref[...]
            if relu2:
                r = jnp.maximum(h, 0.0)
                outs[0][...] = (r * r).astype(BF16)
                outs[1][...] = h.astype(BF16)
            else:
                outs[0][...] = h

    b_arr, b_spec = _layer_spec(b, (tk, tn), lambda j, i, kk: (kk, j))
    in_specs = [pl.BlockSpec((tm, tk), lambda j, i, kk: (i, kk)), b_spec]
    args = [a, b_arr]
    if has_bias:
        in_specs.append(pl.BlockSpec((1, tn), lambda j, i, kk: (0, j)))
        args.append(bias)
    o_spec = pl.BlockSpec((tm, tn), lambda j, i, kk: (i, j))
    if relu2:
        out_shape = (jax.ShapeDtypeStruct((m, n), BF16), jax.ShapeDtypeStruct((m, n), BF16))
        out_specs = (o_spec, o_spec)
    else:
        out_shape, out_specs = jax.ShapeDtypeStruct((m, n), F32), o_spec
    return pl.pallas_call(body, name=name, grid=(n // tn, m // tm, nk), in_specs=in_specs, out_specs=out_specs,
                          out_shape=out_shape, scratch_shapes=[pltpu.VMEM((tm, tn), F32)],
                          compiler_params=_params(3))(*args)


def _mm_nt(a, b, hb=None, *, name):
    m, n = a.shape
    k = (b[0] if isinstance(b, tuple) else b).shape[-2]
    tm, tn, tk = _tile(m, (512, 256)), _tile(k, (1024,)), _tile(n, (1024,))
    nk = n // tk
    fused = hb is not None

    def body(*refs):
        a_ref, b_ref = refs[0], refs[1]
        hb_ref = refs[2] if fused else None
        o_ref, acc = refs[-2], refs[-1]
        kk = pl.program_id(2)

        @pl.when(kk == 0)
        def _():
            acc[...] = jnp.zeros_like(acc)

        acc[...] += _dot(a_ref[...], b_ref[...], _NT)

        @pl.when(kk == nk - 1)
        def _():
            if fused:
                o_ref[...] = (acc[...] * (2.0 * jnp.maximum(hb_ref[...].astype(F32), 0.0))).astype(BF16)
            else:
                o_ref[...] = acc[...]

    b_arr, b_spec = _layer_spec(b, (tn, tk), lambda j, i, kk: (j, kk))
    in_specs = [pl.BlockSpec((tm, tk), lambda j, i, kk: (i, kk)), b_spec]
    args = [a, b_arr]
    if fused:
        in_specs.append(pl.BlockSpec((tm, tn), lambda j, i, kk: (i, j)))
        args.append(hb)
    return pl.pallas_call(body, name=name, grid=(k // tn, m // tm, nk), in_specs=in_specs,
                          out_specs=pl.BlockSpec((tm, tn), lambda j, i, kk: (i, j)),
                          out_shape=jax.ShapeDtypeStruct((m, k), BF16 if fused else F32),
                          scratch_shapes=[pltpu.VMEM((tm, tn), F32)], compiler_params=_params(3))(*args)


def _mm_tn(a, b, *, name, layer=None, into=None):
    m, k = a.shape
    n = b.shape[1]
    tk, tn, tr = _tile(k, (1024,)), _tile(n, (1024,)), _tile(m, (512, 256))
    nr = m // tr

    def body(a_ref, b_ref, *rest):
        o_ref, acc = rest[-2], rest[-1]
        r = pl.program_id(2)

        @pl.when(r == 0)
        def _():
            acc[...] = jnp.zeros_like(acc)

        acc[...] += _dot(a_ref[...], b_ref[...], _TN)

        @pl.when(r == nr - 1)
        def _():
            o_ref[...] = acc[...]

    in_specs = [pl.BlockSpec((tr, tk), lambda i, j, r: (r, i)), pl.BlockSpec((tr, tn), lambda i, j, r: (r, j))]
    args, aliases = [a, b], {}
    if layer is None:
        out_spec, out_shape = pl.BlockSpec((tk, tn), lambda i, j, r: (i, j)), jax.ShapeDtypeStruct((k, n), F32)
    else:
        out_spec = pl.BlockSpec((None, tk, tn), lambda i, j, r: (layer, i, j))
        out_shape = jax.ShapeDtypeStruct((DEPTH, k, n), F32)
        if into is not None:
            in_specs.append(_ANY)
            args.append(into)
            aliases = {2: 0}
    return pl.pallas_call(body, name=name, grid=(k // tk, n // tn, nr), in_specs=in_specs, out_specs=out_spec,
                          out_shape=out_shape, input_output_aliases=aliases,
                          scratch_shapes=[pltpu.VMEM((tk, tn), F32)], compiler_params=_params(3))(*args)


def _colsum(v, *, name):
    m, n = v.shape
    tn, tr = _tile(n, (1024,)), _tile(m, (512, 256))

    def body(v_ref, o_ref):
        @pl.when(pl.program_id(1) == 0)
        def _():
            o_ref[...] = jnp.zeros_like(o_ref)

        o_ref[...] += jnp.sum(v_ref[...].astype(F32), axis=0, keepdims=True)

    return pl.pallas_call(body, name=name, grid=(n // tn, m // tr),
                          in_specs=[pl.BlockSpec((tr, tn), lambda j, r: (r, j))],
                          out_specs=pl.BlockSpec((1, tn), lambda j, r: (0, j)),
                          out_shape=jax.ShapeDtypeStruct((1, n), F32), compiler_params=_params(2))(v)


def _tok_spec(d=D_MODEL, col=0):
    return pl.BlockSpec((TM, d), lambda i: (i, col))


def _mod_spec(tps):
    return pl.BlockSpec((1, N_MOD, D_MODEL), lambda i: (_seq_of_tile(i, tps), 0, 0))


def _row_spec(d=D_MODEL):
    return pl.BlockSpec((1, d), lambda i: (0, 0))


def _seq_acc_spec(tps):
    return pl.BlockSpec((1, 1, D_MODEL), lambda i: (_seq_of_tile(i, tps), 0, 0))


def _modulate(xs, modall, k_shift, k_scale, tps, *, name):
    n = xs.shape[0]

    def body(x_ref, m_ref, o_ref):
        sh = m_ref[0, k_shift:k_shift + 1, :]
        sc = m_ref[0, k_scale:k_scale + 1, :]
        o_ref[...] = (x_ref[...] * (1.0 + sc) + sh).astype(BF16)

    return pl.pallas_call(body, name=name, grid=(n // TM,), in_specs=[_tok_spec(), _mod_spec(tps)],
                          out_specs=_tok_spec(), out_shape=jax.ShapeDtypeStruct((n, D_MODEL), BF16),
                          compiler_params=_params(1))(xs, modall)


def _resid_ln(xs, ms, modall, k_gate, g, b, tps, *, name):
    n = xs.shape[0]

    def body(x_ref, m_ref, mod_ref, g_ref, b_ref, o_ref):
        z = ALPHA * x_ref[...] + mod_ref[0, k_gate:k_gate + 1, :] * m_ref[...]
        mu = jnp.mean(z, axis=-1, keepdims=True)
        zc = z - mu
        var = jnp.mean(zc * zc, axis=-1, keepdims=True)
        o_ref[...] = zc * lax.rsqrt(var + LN_EPS) * g_ref[...] + b_ref[...]

    return pl.pallas_call(body, name=name, grid=(n // TM,),
                          in_specs=[_tok_spec(), _tok_spec(), _mod_spec(tps), _row_spec(), _row_spec()],
                          out_specs=_tok_spec(), out_shape=jax.ShapeDtypeStruct((n, D_MODEL), F32),
                          compiler_params=_params(1))(xs, ms, modall, g, b)


def _resid_ln_bwd(xs, ms, modall, k_gate, g, dout, tps, *, name):
    n = xs.shape[0]

    def body(x_ref, m_ref, mod_ref, g_ref, d_ref, dxa_ref, dm_ref, dbias_ref, dg_ref, db_ref, dgate_ref):
        i = pl.program_id(0)
        gate = mod_ref[0, k_gate:k_gate + 1, :]
        m = m_ref[...]
        z = ALPHA * x_ref[...] + gate * m
        mu = jnp.mean(z, axis=-1, keepdims=True)
        zc = z - mu
        var = jnp.mean(zc * zc, axis=-1, keepdims=True)
        rstd = lax.rsqrt(var + LN_EPS)
        xhat = zc * rstd
        d = d_ref[...]
        dxh = d * g_ref[...]
        dz = rstd * (dxh - jnp.mean(dxh, axis=-1, keepdims=True)
                     - xhat * jnp.mean(dxh * xhat, axis=-1, keepdims=True))
        dxa_ref[...] = ALPHA * dz
        dm = gate * dz
        dm_ref[...] = dm.astype(BF16)

        @pl.when(i == 0)
        def _():
            dbias_ref[...] = jnp.zeros_like(dbias_ref)
            dg_ref[...] = jnp.zeros_like(dg_ref)
            db_ref[...] = jnp.zeros_like(db_ref)

        dbias_ref[...] += jnp.sum(dm, axis=0, keepdims=True)
        dg_ref[...] += jnp.sum(d * xhat, axis=0, keepdims=True)
        db_ref[...] += jnp.sum(d, axis=0, keepdims=True)
        part = jnp.sum(dz * m, axis=0, keepdims=True)

        @pl.when(i % tps <= 1)
        def _():
            dgate_ref[0] = part

        @pl.when(i % tps > 1)
        def _():
            dgate_ref[0] += part

    row = jax.ShapeDtypeStruct((1, D_MODEL), F32)
    return pl.pallas_call(
        body, name=name, grid=(n // TM,),
        in_specs=[_tok_spec(), _tok_spec(), _mod_spec(tps), _row_spec(), _tok_spec()],
        out_specs=(_tok_spec(), _tok_spec(), _row_spec(), _row_spec(), _row_spec(), _seq_acc_spec(tps)),
        out_shape=(jax.ShapeDtypeStruct((n, D_MODEL), F32), jax.ShapeDtypeStruct((n, D_MODEL), BF16), row, row, row,
                   jax.ShapeDtypeStruct((n // TM // tps * 2, 1, D_MODEL), F32)),
        compiler_params=_params(1))(xs, ms, modall, g, dout)


def _modulate_bwd(du, xs, modall, k_scale, dxa, tps, *, name):
    n = xs.shape[0]

    def body(du_ref, x_ref, mod_ref, dxa_ref, dx_ref, dsc_ref, dsh_ref):
        i = pl.program_id(0)
        du_t = du_ref[...]
        dx_ref[...] = dxa_ref[...] + du_t * (1.0 + mod_ref[0, k_scale:k_scale + 1, :])
        psc = jnp.sum(du_t * x_ref[...], axis=0, keepdims=True)
        psh = jnp.sum(du_t, axis=0, keepdims=True)

        @pl.when(i % tps <= 1)
        def _():
            dsc_ref[0] = psc
            dsh_ref[0] = psh

        @pl.when(i % tps > 1)
        def _():
            dsc_ref[0] += psc
            dsh_ref[0] += psh

    acc = jax.ShapeDtypeStruct((n // TM // tps * 2, 1, D_MODEL), F32)
    return pl.pallas_call(body, name=name, grid=(n // TM,),
                          in_specs=[_tok_spec(), _tok_spec(), _mod_spec(tps), _tok_spec()],
                          out_specs=(_tok_spec(), _seq_acc_spec(tps), _seq_acc_spec(tps)),
                          out_shape=(jax.ShapeDtypeStruct((n, D_MODEL), F32), acc, acc),
                          compiler_params=_params(1))(du, xs, modall, dxa)


def _loss_head(ys, target, tps, *, name):
    n = ys.shape[0]
    lat_tiles = tps - 1

    def body(y_ref, t_ref, acc_ref, dy_ref):
        i = pl.program_id(0)

        @pl.when(i == 0)
        def _():
            acc_ref[...] = jnp.zeros_like(acc_ref)

        @pl.when(i % tps == 0)
        def _():
            dy_ref[...] = jnp.zeros_like(dy_ref)

        @pl.when(i % tps > 0)
        def _():
            e = y_ref[...] - t_ref[...]
            dy_ref[...] = e * (1.0 / D_MODEL)
            acc_ref[...] += jnp.sum(e * e, axis=0, keepdims=True)

    t_spec = pl.BlockSpec((TM, D_MODEL), lambda i: ((i // tps) * lat_tiles + jnp.maximum(i % tps - 1, 0), 0))
    return pl.pallas_call(body, name=name, grid=(n // TM,), in_specs=[_tok_spec(), t_spec],
                          out_specs=(_row_spec(), _tok_spec()),
                          out_shape=(jax.ShapeDtypeStruct((1, D_MODEL), F32), jax.ShapeDtypeStruct((n, D_MODEL), F32)),
                          compiler_params=_params(1))(ys, target)


def _glu_fwd(y_nat, p, d_skip, w, b, cat, *, name):
    n = y_nat.shape[0]

    def body(y_ref, u_ref, d_ref, w_ref, b_ref, cat_ref, o_ref):
        g = _gelu(y_ref[...] + d_ref[...] * u_ref[...])
        z = _dot(g.astype(BF16), w_ref[...]) + b_ref[...]
        o_ref[...] = (g * _sigmoid(z)).astype(BF16)

    w_arr, w_spec = _layer_spec(w, (D_MODEL, D_MODEL), lambda i: (0, 0))
    return pl.pallas_call(body, name=name, grid=(n // TM,),
                          in_specs=[_tok_spec(), _tok_spec(col=2), _row_spec(), w_spec, _row_spec(), _ANY],
                          out_specs=_tok_spec(col=1), out_shape=jax.ShapeDtypeStruct(cat.shape, BF16),
                          input_output_aliases={5: 0},
                          compiler_params=_params(1))(y_nat, p, d_skip, w_arr, b, cat)


def _glu_bwd(dcat, y_nat, p, d_skip, w, b, *, name):
    n = y_nat.shape[0]

    def body(ds_ref, y_ref, u_ref, d_ref, w_ref, b_ref, dy_ref, dsk_ref, g_ref, dz_ref, dd_ref, dbz_ref):
        u = u_ref[...]
        g, gg = _gelu_and_grad(y_ref[...] + d_ref[...] * u)
        s = _sigmoid(_dot(g.astype(BF16), w_ref[...]) + b_ref[...])
        ds = ds_ref[...]
        dz = ds * g * s * (1.0 - s)
        dzb = dz.astype(BF16)
        dg = ds * s + _dot(dzb, w_ref[...], _NT)
        dyp = dg * gg
        dy_ref[...] = dyp
        dsk_ref[...] = dyp * d_ref[...]
        g_ref[...] = g.astype(BF16)
        dz_ref[...] = dzb

        @pl.when(pl.program_id(0) == 0)
        def _():
            dd_ref[...] = jnp.zeros_like(dd_ref)
            dbz_ref[...] = jnp.zeros_like(dbz_ref)

        dd_ref[...] += jnp.sum(dyp * u, axis=0, keepdims=True)
        dbz_ref[...] += jnp.sum(dz, axis=0, keepdims=True)

    tok_bf = jax.ShapeDtypeStruct((n, D_MODEL), BF16)
    tok_f32 = jax.ShapeDtypeStruct((n, D_MODEL), F32)
    row = jax.ShapeDtypeStruct((1, D_MODEL), F32)
    w_arr, w_spec = _layer_spec(w, (D_MODEL, D_MODEL), lambda i: (0, 0))
    return pl.pallas_call(
        body, name=name, grid=(n // TM,),
        in_specs=[_tok_spec(col=1), _tok_spec(), _tok_spec(col=2), _row_spec(), w_spec, _row_spec()],
        out_specs=(_tok_spec(), _tok_spec(), _tok_spec(), _tok_spec(), _row_spec(), _row_spec()),
        out_shape=(tok_f32, tok_f32, tok_bf, tok_bf, row, row),
        compiler_params=_params(1))(dcat, y_nat, p, d_skip, w_arr, b)


def _add_cast(a, b, *, name):
    n = a.shape[0]

    def body(a_ref, b_ref, o_ref):
        o_ref[...] = (a_ref[...] + b_ref[...]).astype(BF16)

    return pl.pallas_call(body, name=name, grid=(n // TM,), in_specs=[_tok_spec(), _tok_spec()],
                          out_specs=_tok_spec(), out_shape=jax.ShapeDtypeStruct((n, D_MODEL), BF16),
                          compiler_params=_params(1))(a, b)


def _silu_rows(c16, *, name):
    def body(c_ref, s_ref, ds_ref):
        v = c_ref[...]
        sg = _sigmoid(v)
        s_ref[...] = v * sg
        ds_ref[...] = sg * (1.0 + v * (1.0 - sg))

    shp = jax.ShapeDtypeStruct(c16.shape, F32)
    return pl.pallas_call(body, name=name, out_shape=(shp, shp))(c16)


def _mul_rows(a, b, *, name):
    def body(a_ref, b_ref, o_ref):
        o_ref[...] = a_ref[...] * b_ref[...]

    return pl.pallas_call(body, name=name, out_shape=jax.ShapeDtypeStruct(a.shape, F32))(a, b)


def _pad_off(c):
    return pl.multiple_of(c * TM + 8 + 8 * jnp.minimum(c, 1), 8)


def _rows8(k):
    return pl.ds(pl.multiple_of(k * 8, 8), 8)


def _windows(buf, c, shifts):
    n = TM + 16
    win = buf[pl.ds(pl.multiple_of(_pad_off(c) - 8, 8), n), :]
    return [win[8:8 + TM] if k == 0 else pltpu.roll(win, (-k) % n, 0)[8:8 + TM] for k in shifts]


def _conv_window(xpad, c):
    return _windows(xpad, c, (-1, 0, 1, 2))


def _rg_coeffs(z, d, spl, xc):
    r = 0.5 + 0.5 * jnp.tanh(0.5 * z[:, 256 * d:256 * d + HEAD])
    i = 0.5 + 0.5 * jnp.tanh(0.5 * z[:, 256 * d + HEAD:256 * d + 2 * HEAD])
    la = -RG_C * spl[d:d + 1, :] * r
    a = jnp.exp(la)
    one_minus_a = jnp.where(jnp.abs(la) < 1e-2, _neg_expm1_small(la), 1.0 - a)
    mult = jnp.sqrt(one_minus_a * (1.0 + a))
    return r, i, a, mult, a * a


def _chunk_scan(a, b, reverse):
    row = lax.broadcasted_iota(jnp.int32, (TM, HEAD), 0)
    sft = 1
    while sft < TM:
        keep = (row < TM - sft) if reverse else (row >= sft)
        amt = TM - sft if reverse else sft
        a_prev = jnp.where(keep, pltpu.roll(a, amt, 0), 1.0)
        b_prev = jnp.where(keep, pltpu.roll(b, amt, 0), 0.0)
        b = a * b_prev + b
        a = a * a_prev
        sft *= 2
    return a, b


def _zero_pads(buf, s, lc):
    z8 = jnp.zeros((8, HEAD), F32)
    buf[0:8, :] = z8
    buf[8 + lc:16 + lc, :] = z8
    buf[16 + s:24 + s, :] = z8


def _rg_fwd(p, conv_w, conv_b, lam, wcat, bcat, bl, s, lc, *, name):
    nch = s // TM

    def body(x_ref, gate_ref, cw_ref, cb_ref, lam_ref, w_ref, b_ref, rg_ref, hf_ref, hb_ref, xpad, af, bf, ab, bb):
        _zero_pads(xpad, s, lc)

        def copy_chunk(c, _):
            xpad[pl.ds(_pad_off(c), TM), :] = x_ref[pl.ds(pl.multiple_of(c * TM, TM), TM), :]
            return 0

        lax.fori_loop(0, nch, copy_chunk, 0)
        spl = _softplus(-lam_ref[...])
        cw = cw_ref[...]

        def coef_chunk(c, _):
            xm1, x0, xp1, xp2 = _conv_window(xpad, c)
            xc = cw[0:1] * xm1 + cw[1:2] * x0 + cw[2:3] * xp1 + cw[3:4] * xp2 + cb_ref[...]
            z = _dot(xc.astype(BF16), w_ref[0]) + b_ref[0]
            rows = pl.ds(pl.multiple_of(c * TM, TM), TM)
            for d, (a_s, b_s) in enumerate(((af, bf), (ab, bb))):
                _, i, a, mult, _ = _rg_coeffs(z, d, spl, xc)
                a_s[rows, :] = a
                b_s[rows, :] = mult * i * xc
            return 0

        lax.fori_loop(0, nch, coef_chunk, 0)

        def scan_pair(j, carry):
            cf, cb_ = carry
            rf = pl.ds(pl.multiple_of(j * TM, TM), TM)
            rb = pl.ds(pl.multiple_of(jnp.where(j == 0, 0, nch - j) * TM, TM), TM)
            a1, h1 = _chunk_scan(af[rf, :], bf[rf, :], False)
            h1 = h1 + a1 * cf
            hf_ref[rf, :] = h1
            a2, h2 = _chunk_scan(ab[rb, :], bb[rb, :], True)
            h2 = h2 + a2 * cb_
            hb_ref[rb, :] = h2
            return h1[TM - 1:TM], h2[0:1]

        zero = jnp.zeros((1, HEAD), F32)
        lax.fori_loop(0, nch, scan_pair, (zero, zero))

        def out_chunk(c, _):
            rows = pl.ds(pl.multiple_of(c * TM, TM), TM)
            rg_ref[rows, :] = ((hf_ref[rows, :] + hb_ref[rows, :]) * _gelu(gate_ref[rows, :])).astype(BF16)
            return 0

        lax.fori_loop(0, nch, out_chunk, 0)

    seq = lambda col0: pl.BlockSpec((s, HEAD), lambda b, h: (b, col0 + h))
    par = lambda r: pl.BlockSpec((r, HEAD), lambda b, h: (0, h))
    n = bl * s
    return pl.pallas_call(
        body, name=name, grid=(bl, RG_HEADS),
        in_specs=[seq(0), seq(RG_HEADS), par(4), par(1), par(2),
                  pl.BlockSpec((1, HEAD, 4 * HEAD), lambda b, h: (h, 0, 0)),
                  pl.BlockSpec((1, 1, 4 * HEAD), lambda b, h: (h, 0, 0))],
        out_specs=(seq(0), seq(0), seq(0)),
        out_shape=(jax.ShapeDtypeStruct((n, 2 * D_MODEL), BF16), jax.ShapeDtypeStruct((n, D_MODEL), F32),
                   jax.ShapeDtypeStruct((n, D_MODEL), F32)),
        scratch_shapes=[pltpu.VMEM((s + 24, HEAD), F32)] + [pltpu.VMEM((s, HEAD), F32)] * 4,
        compiler_params=_params(2, 48))(p, p, conv_w, conv_b, lam, wcat, bcat)


def _rg_bwd(p, dcat, hf, hb, conv_w, conv_b, lam, wcat, bcat, bl, s, lc, *, name):
    nch = s // TM
    ll = s - lc

    def body(p_hbm, dcat_hbm, hf_hbm, hb_hbm, gate_s, cw_ref, cb_ref, lam_ref, w_ref, b_ref,
             drgx_ref, dgate_ref, dcw_ref, dcb_ref, dlam_ref, dw_ref, db_ref,
             xpad2, dxpad, hf2, hb2, dhs2, a_f, a_b, lam_f, lam_b, sems):
        h = pl.program_id(0)
        b = pl.program_id(1)
        step = h * bl + b
        slot = step % 2

        def copies(hh, bb, sl):
            col = pl.multiple_of(hh * HEAD, HEAD)

            def rows_of(ref, r0, nr):
                return ref.at[pl.ds(bb * s + r0, nr), pl.ds(col, HEAD)]

            return [
                pltpu.make_async_copy(rows_of(p_hbm, 0, lc), xpad2.at[sl, pl.ds(8, lc), :], sems.at[sl, 0]),
                pltpu.make_async_copy(rows_of(p_hbm, lc, ll), xpad2.at[sl, pl.ds(16 + lc, ll), :], sems.at[sl, 1]),
                pltpu.make_async_copy(rows_of(dcat_hbm, 0, s), dhs2.at[sl], sems.at[sl, 2]),
                pltpu.make_async_copy(rows_of(hf_hbm, 0, s), hf2.at[sl, pl.ds(8, s), :], sems.at[sl, 3]),
                pltpu.make_async_copy(rows_of(hb_hbm, 0, s), hb2.at[sl, pl.ds(8, s), :], sems.at[sl, 4]),
            ]

        @pl.when(step == 0)
        def _():
            for cp in copies(h, b, slot):
                cp.start()

        @pl.when(step + 1 < RG_HEADS * bl)
        def _():
            for cp in copies((step + 1) // bl, (step + 1) % bl, 1 - slot):
                cp.start()

        xpad, hf_s, hb_s, dhs = xpad2.at[slot], hf2.at[slot], hb2.at[slot], dhs2.at[slot]
        _zero_pads(xpad, s, lc)
        _zero_pads(dxpad, s, lc)
        for buf in (hf_s, hb_s):
            buf[0:8, :] = jnp.zeros((8, HEAD), F32)
            buf[8 + s:16 + s, :] = jnp.zeros((8, HEAD), F32)

        @pl.when(b == 0)
        def _():
            dcw_ref[...] = jnp.zeros_like(dcw_ref)
            dcb_ref[...] = jnp.zeros_like(dcb_ref)
            dlam_ref[...] = jnp.zeros_like(dlam_ref)
            dw_ref[...] = jnp.zeros_like(dw_ref)
            db_ref[...] = jnp.zeros_like(db_ref)

        for cp in copies(h, b, slot):
            cp.wait()
        lam_v = lam_ref[...]
        spl = _softplus(-lam_v)
        cw = cw_ref[...]

        def conv(c):
            xm1, x0, xp1, xp2 = _conv_window(xpad, c)
            return cw[0:1] * xm1 + cw[1:2] * x0 + cw[2:3] * xp1 + cw[3:4] * xp2 + cb_ref[...]

        def pass_a(c, _):
            rows = pl.ds(pl.multiple_of(c * TM, TM), TM)
            xc = conv(c)
            z = _dot(xc.astype(BF16), w_ref[0]) + b_ref[0]
            for d, a_s in enumerate((a_f, a_b)):
                a_s[rows, :] = _rg_coeffs(z, d, spl, xc)[2]
            g, gg = _gelu_and_grad(gate_s[rows, :])
            drg = dhs[rows, :]
            hrows = pl.ds(pl.multiple_of(c * TM + 8, 8), TM)
            dgate_ref[rows, :] = (drg * (hf_s[hrows, :] + hb_s[hrows, :]) * gg).astype(BF16)
            dhs[rows, :] = drg * g
            return 0

        lax.fori_loop(0, nch, pass_a, 0)

        row = lax.broadcasted_iota(jnp.int32, (TM, HEAD), 0)

        def adj_pair(j, carry):
            cf, cb_ = carry
            rf = pl.ds(pl.multiple_of((nch - 1 - j) * TM, TM), TM)
            rb = pl.ds(pl.multiple_of(jnp.where(j == nch - 1, 0, j + 1) * TM, TM), TM)
            d1, a1 = dhs[rf, :], a_f[rf, :]
            p1, m1 = _chunk_scan(a1, a1 * d1, True)
            m1 = m1 + p1 * cf
            lam_f[rf, :] = d1 + jnp.where(row == TM - 1, cf, pltpu.roll(m1, TM - 1, 0))
            d2, a2 = dhs[rb, :], a_b[rb, :]
            p2, m2 = _chunk_scan(a2, a2 * d2, False)
            m2 = m2 + p2 * cb_
            lam_b[rb, :] = d2 + jnp.where(row == 0, cb_, pltpu.roll(m2, 1, 0))
            return m1[0:1], m2[TM - 1:TM]

        zero = jnp.zeros((1, HEAD), F32)
        lax.fori_loop(0, nch, adj_pair, (zero, zero))

        sig_neg = _sigmoid(-lam_v)
        last_row = lax.broadcasted_iota(jnp.int32, (TM, HEAD), 0) == TM - 1
        hb_first = hb_s[8:9, :]

        def pass_b(c, _):
            rows = pl.ds(pl.multiple_of(c * TM, TM), TM)
            xc = conv(c)
            xcb = xc.astype(BF16)
            z = _dot(xcb, w_ref[0]) + b_ref[0]
            dxc = jnp.zeros((TM, HEAD), F32)
            dzs = []
            n = TM + 16
            hp_f = pltpu.roll(hf_s[pl.ds(pl.multiple_of(c * TM, TM), n), :], 1, 0)[8:8 + TM]
            hp_b = pltpu.roll(hb_s[pl.ds(pl.multiple_of(c * TM, TM), n), :], n - 1, 0)[8:8 + TM]
            hp_b = jnp.where(last_row & (c == 0), 0.0, hp_b)
            hp_b = jnp.where(last_row & (c == nch - 1), hb_first, hp_b)
            for d, (l_s, hp) in enumerate(((lam_f, hp_f), (lam_b, hp_b))):
                r, i, a, mult, e2 = _rg_coeffs(z, d, spl, xc)
                dbt = l_s[rows, :]
                dla = dbt * hp * a - dbt * i * xc * (e2 / mult)
                dlam_ref[0, d:d + 1, :] += jnp.sum(dla * r, axis=0, keepdims=True) * (RG_C * sig_neg[d:d + 1, :])
                dr = dla * (-RG_C * spl[d:d + 1, :])
                di = dbt * mult * xc
                dxc = dxc + dbt * mult * i
                dzs += [dr * r * (1.0 - r), di * i * (1.0 - i)]
            dz = jnp.concatenate(dzs, axis=1)
            dzb = dz.astype(BF16)
            dxc = dxc + _dot(dzb, w_ref[0], _NT)
            dw_ref[0] += _dot(xcb, dzb, _TN)
            db_ref[0] += jnp.sum(dz, axis=0, keepdims=True)
            dcb_ref[0] += jnp.sum(dxc, axis=0, keepdims=True)
            dxpad[pl.ds(_pad_off(c), TM), :] = dxc
            return 0

        lax.fori_loop(0, nch, pass_b, 0)

        def pass_c(c, _):
            rows = pl.ds(pl.multiple_of(c * TM, TM), TM)
            gp1, g0, gm1, gm2 = _windows(dxpad, c, (1, 0, -1, -2))
            drgx_ref[rows, :] = (cw[0:1] * gp1 + cw[1:2] * g0 + cw[2:3] * gm1 + cw[3:4] * gm2).astype(BF16)
            xm1, x0, xp1, xp2 = _conv_window(xpad, c)
            dcw_ref[0] += jnp.concatenate([jnp.sum(g0 * t, axis=0, keepdims=True) for t in (xm1, x0, xp1, xp2)],
                                          axis=0)
            return 0

        lax.fori_loop(0, nch, pass_c, 0)

    seq = pl.BlockSpec((s, HEAD), lambda h, b: (b, h))
    par = lambda r: pl.BlockSpec((r, HEAD), lambda h, b: (0, h))
    acc = lambda r, w: pl.BlockSpec((1, r, w), lambda h, b: (h, 0, 0))
    anyspec = pl.BlockSpec(memory_space=pl.ANY)
    n = bl * s
    big = pltpu.VMEM((s, HEAD), F32)
    return pl.pallas_call(
        body, name=name, grid=(RG_HEADS, bl),
        in_specs=[anyspec, anyspec, anyspec, anyspec, pl.BlockSpec((s, HEAD), lambda h, b: (b, RG_HEADS + h)),
                  par(4), par(1), par(2), acc(HEAD, 4 * HEAD), acc(1, 4 * HEAD)],
        out_specs=(seq, seq, acc(4, HEAD), acc(1, HEAD), acc(2, HEAD), acc(HEAD, 4 * HEAD), acc(1, 4 * HEAD)),
        out_shape=(jax.ShapeDtypeStruct((n, D_MODEL), BF16), jax.ShapeDtypeStruct((n, D_MODEL), BF16),
                   jax.ShapeDtypeStruct((RG_HEADS, 4, HEAD), F32), jax.ShapeDtypeStruct((RG_HEADS, 1, HEAD), F32),
                   jax.ShapeDtypeStruct((RG_HEADS, 2, HEAD), F32),
                   jax.ShapeDtypeStruct((RG_HEADS, HEAD, 4 * HEAD), F32),
                   jax.ShapeDtypeStruct((RG_HEADS, 1, 4 * HEAD), F32)),
        scratch_shapes=[pltpu.VMEM((2, s + 24, HEAD), F32), pltpu.VMEM((s + 24, HEAD), F32),
                        pltpu.VMEM((2, s + 16, HEAD), F32), pltpu.VMEM((2, s + 16, HEAD), F32),
                        pltpu.VMEM((2, s, HEAD), F32)] + [big] * 4 + [pltpu.SemaphoreType.DMA((2, 5))],
        compiler_params=_params(2, 56))(p, dcat, hf, hb, p, conv_w, conv_b, lam, wcat, bcat)


def _s5_mats(a_re, a_im, log_dt, b_re, b_im, c_re, c_im):
    t = T_CH
    g = a_re.shape[1]
    dt = jnp.exp(log_dt)[..., None]
    lr, li = a_re * dt, a_im * dt
    steps = jnp.arange(t + 1, dtype=F32)[:, None]
    mag = jnp.exp(lr[:, :, None, :] * steps)
    ang = li[:, :, None, :] * steps
    pr, pi = mag * jnp.cos(ang), mag * jnp.sin(ang)
    xr, xi = pr[:, :, 1] - 1.0, pi[:, :, 1]
    den = a_re * a_re + a_im * a_im
    qr, qi = (xr * a_re + xi * a_im) / den, (xi * a_re - xr * a_im) / den
    btr, bti = b_re.transpose(0, 1, 3, 2), b_im.transpose(0, 1, 3, 2)
    bbr = qr[:, :, None, :] * btr - qi[:, :, None, :] * bti
    bbi = qr[:, :, None, :] * bti + qi[:, :, None, :] * btr
    up, down = slice(0, t), slice(t - 1, None, -1)

    def pow_c(d, sl):
        wr, wi = pr[d][:, sl, None, :], pi[d][:, sl, None, :]
        cr, ci = c_re[d][:, None], c_im[d][:, None]
        return (wr * cr - wi * ci).reshape(g, CW, S5_STATE), (wr * ci + wi * cr).reshape(g, CW, S5_STATE)

    hp = lax.Precision.HIGHEST

    def lag_map(d, sl):
        re, im = pow_c(d, sl)
        return (jnp.einsum('gkp,gmp->gkm', bbr[d], re, precision=hp)
                - jnp.einsum('gkp,gmp->gkm', bbi[d], im, precision=hp))

    z_f, z_b = lag_map(0, up), lag_map(1, down)
    kf = jnp.stack([jnp.pad(z_f, ((0, 0), (0, 0), (S5_GROUP * s, 0)))[:, :, :CW] for s in range(t)], axis=1)
    kb = jnp.stack([jnp.pad(z_b, ((0, 0), (0, 0), (0, S5_GROUP * (t - 1 - s))))[:, :, S5_GROUP * (t - 1 - s):]
                    for s in range(t)], axis=1)
    kcat = (kf + kb).reshape(g, CW, CW)

    def state_in(d, sl):
        wr, wi = pr[d][:, sl, None, :], pi[d][:, sl, None, :]
        br, bi = bbr[d][:, None], bbi[d][:, None]
        return jnp.concatenate([wr * br - wi * bi, wr * bi + wi * br], axis=-1).reshape(g, CW, SW)

    wcat = jnp.concatenate([kcat, state_in(0, down), state_in(1, up)], axis=2)
    of_r, of_i = pow_c(0, slice(1, t + 1))
    ob_r, ob_i = pow_c(1, slice(t, 0, -1))
    mout_t = jnp.concatenate([of_r, -of_i, ob_r, -ob_i], axis=2)
    rows = []
    for d in range(2):
        art, ait = pr[d][:, t], pi[d][:, t]
        rows += [jnp.concatenate([art, art], axis=1).reshape(-1), jnp.concatenate([-ait, ait], axis=1).reshape(-1)]
    return wcat, mout_t, jnp.stack(rows)


def _lane_swap(v):
    return pltpu.roll(v, S5_STATE, 1)


def _grp(g, w):
    return slice(g * w, (g + 1) * w)


def _s5_fwd(u, wcat, mout, a2, ncc, *, name):
    bl, nc, _ = u.shape

    def body(u_ref, w_ref, mo_ref, a_ref, y_ref, sf_ref, sb_ref, vf, vb):
        for g in range(GB):
            zu = _dot(u_ref[:, _grp(g, CW)].astype(BF16), w_ref[g])
            y_ref[:, _grp(g, CW)] = zu[:, :CW]
            vf[:, _grp(2 * g, SW)] = zu[:, CW:CW + SW]
            vb[:, _grp(2 * g, SW)] = zu[:, CW + SW:]
            vf[:, _grp(2 * g + 1, SW)] = _lane_swap(zu[:, CW:CW + SW])
            vb[:, _grp(2 * g + 1, SW)] = _lane_swap(zu[:, CW + SW:])
        co = [[a_ref[r:r + 1, _grp(g, SW)] for g in range(GB)] for r in range(4)]

        rid = lax.broadcasted_iota(jnp.int32, (8, SW), 0)

        def step8(groups, kf, kb, carry):
            rf, rb = _rows8(kf), _rows8(kb)
            lanes = slice(groups[0] * 2 * SW, (groups[-1] + 1) * 2 * SW)
            vfb, vbb = vf[rf, lanes], vb[rb, lanes]
            st = list(carry)
            of = [jnp.zeros((8, SW), F32)] * len(groups)
            ob = list(of)
            for i in range(8):
                k = 7 - i
                for n, g in enumerate(groups):
                    sf, sfs, sb, sbs = st[4 * n:4 * n + 4]
                    of[n] = jnp.where(rid == i, sf, of[n])
                    ob[n] = jnp.where(rid == k, sb, ob[n])
                    st[4 * n] = co[0][g] * sf + co[1][g] * sfs + vfb[i:i + 1, _grp(2 * n, SW)]
                    st[4 * n + 1] = co[0][g] * sfs - co[1][g] * sf + vfb[i:i + 1, _grp(2 * n + 1, SW)]
                    st[4 * n + 2] = co[2][g] * sb + co[3][g] * sbs + vbb[k:k + 1, _grp(2 * n, SW)]
                    st[4 * n + 3] = co[2][g] * sbs - co[3][g] * sb + vbb[k:k + 1, _grp(2 * n + 1, SW)]
            for n, g in enumerate(groups):
                sf_ref[rf, _grp(g, SW)] = of[n]
                sb_ref[rb, _grp(g, SW)] = ob[n]
            return tuple(st)

        zero = jnp.zeros((1, SW), F32)
        nbc, nb = ncc // 8, nc // 8
        for groups in (tuple(range(0, GB // 2)), tuple(range(GB // 2, GB))):
            carry = lax.fori_loop(0, nbc, lambda j, cr, gs=groups: step8(gs, j, nbc - 1 - j, cr),
                                  (zero,) * (4 * len(groups)))
            lax.fori_loop(nbc, nb, lambda j, cr, gs=groups: step8(gs, j, nb + nbc - 1 - j, cr), carry)
        for g in range(GB):
            st = jnp.concatenate([sf_ref[:, _grp(g, SW)], sb_ref[:, _grp(g, SW)]], axis=1).astype(BF16)
            y_ref[:, _grp(g, CW)] += _dot(st, mo_ref[g], _NT)

    blk = lambda w: pl.BlockSpec((None, nc, GB * w), lambda b, gb: (b, 0, gb))
    return pl.pallas_call(
        body, name=name, grid=(bl, S5_GROUPS // GB),
        in_specs=[blk(CW), pl.BlockSpec((GB, CW, 2 * CW), lambda b, gb: (gb, 0, 0)),
                  pl.BlockSpec((GB, CW, CW), lambda b, gb: (gb, 0, 0)),
                  pl.BlockSpec((4, GB * SW), lambda b, gb: (0, gb))],
        out_specs=(blk(CW), blk(SW), blk(SW)),
        out_shape=(jax.ShapeDtypeStruct(u.shape, F32), jax.ShapeDtypeStruct((bl, nc, S5_GROUPS * SW), F32),
                   jax.ShapeDtypeStruct((bl, nc, S5_GROUPS * SW), F32)),
        scratch_shapes=[pltpu.VMEM((nc, GB * 2 * SW), F32)] * 2, compiler_params=_params(2))(u, wcat, mout, a2)


def _s5_bwd(dy, u, sf, sb, wcat, mout, a2, ncc, *, name):
    bl, nc, _ = u.shape

    def body(dy_ref, u_ref, sf_ref, sb_ref, w_ref, mo_ref, a_ref, du_ref, dw_ref, dmo_ref, dacc_ref, gsf, gsb, dvf, dvb):
        b = pl.program_id(1)

        @pl.when(b == 0)
        def _():
            dw_ref[...] = jnp.zeros_like(dw_ref)
            dmo_ref[...] = jnp.zeros_like(dmo_ref)
            dacc_ref[...] = jnp.zeros_like(dacc_ref)

        for g in range(GB):
            ds = _dot(dy_ref[:, _grp(g, CW)].astype(BF16), mo_ref[g])
            gsf[:, _grp(2 * g, SW)] = ds[:, :SW]
            gsb[:, _grp(2 * g, SW)] = ds[:, SW:]
            gsf[:, _grp(2 * g + 1, SW)] = _lane_swap(ds[:, :SW])
            gsb[:, _grp(2 * g + 1, SW)] = _lane_swap(ds[:, SW:])
        co = [[a_ref[r:r + 1, _grp(g, SW)] for g in range(GB)] for r in range(4)]

        rid = lax.broadcasted_iota(jnp.int32, (8, SW), 0)

        def step8(groups, kf, kb, carry):
            rf, rb = _rows8(kf), _rows8(kb)
            lanes = slice(groups[0] * 2 * SW, (groups[-1] + 1) * 2 * SW)
            gfb, gbb = gsf[rf, lanes], gsb[rb, lanes]
            st = list(carry)
            of = [jnp.zeros((8, SW), F32)] * len(groups)
            ob = list(of)
            for i in range(8):
                k = 7 - i
                for n, g in enumerate(groups):
                    gf, gfs, gb_, gbs = st[4 * n:4 * n + 4]
                    of[n] = jnp.where(rid == k, gf, of[n])
                    ob[n] = jnp.where(rid == i, gb_, ob[n])
                    st[4 * n] = gfb[k:k + 1, _grp(2 * n, SW)] + co[0][g] * gf - co[1][g] * gfs
                    st[4 * n + 1] = gfb[k:k + 1, _grp(2 * n + 1, SW)] + co[0][g] * gfs + co[1][g] * gf
                    st[4 * n + 2] = gbb[i:i + 1, _grp(2 * n, SW)] + co[2][g] * gb_ - co[3][g] * gbs
                    st[4 * n + 3] = gbb[i:i + 1, _grp(2 * n + 1, SW)] + co[2][g] * gbs + co[3][g] * gb_
            for n, g in enumerate(groups):
                dvf[rf, _grp(g, SW)] = of[n]
                dvb[rb, _grp(g, SW)] = ob[n]
            return tuple(st)

        zero = jnp.zeros((1, SW), F32)
        nbc, nb = ncc // 8, nc // 8
        for groups in (tuple(range(0, GB // 2)), tuple(range(GB // 2, GB))):
            carry = lax.fori_loop(0, nb - nbc, lambda j, cr, gs=groups: step8(gs, nb - 1 - j, nbc + j, cr),
                                  (zero,) * (4 * len(groups)))
            lax.fori_loop(0, nbc, lambda j, cr, gs=groups: step8(gs, nbc - 1 - j, j, cr), carry)
        for g in range(GB):
            dyg = dy_ref[:, _grp(g, CW)].astype(BF16)
            dvf_g, dvb_g = dvf[:, _grp(g, SW)], dvb[:, _grp(g, SW)]
            sf_g, sb_g = sf_ref[:, _grp(g, SW)], sb_ref[:, _grp(g, SW)]
            dz = jnp.concatenate([dyg, dvf_g.astype(BF16), dvb_g.astype(BF16)], axis=1)
            du_ref[:, _grp(g, CW)] = _dot(dz, w_ref[g], _NT)
            dw_ref[g] += _dot(u_ref[:, _grp(g, CW)].astype(BF16), dz, _TN)
            st = jnp.concatenate([sf_g, sb_g], axis=1).astype(BF16)
            dmo_ref[g] += _dot(dyg, st, _TN)
            dacc_ref[:, _grp(g, SW)] += jnp.concatenate(
                [jnp.sum(dvf_g * sf_g, axis=0, keepdims=True), jnp.sum(dvf_g * _lane_swap(sf_g), axis=0, keepdims=True),
                 jnp.sum(dvb_g * sb_g, axis=0, keepdims=True), jnp.sum(dvb_g * _lane_swap(sb_g), axis=0, keepdims=True)],
                axis=0)

    blk = lambda w: pl.BlockSpec((None, nc, GB * w), lambda gb, b: (b, 0, gb))
    wspec = lambda mult: pl.BlockSpec((GB, CW, mult * CW), lambda gb, b: (gb, 0, 0))
    aspec = pl.BlockSpec((4, GB * SW), lambda gb, b: (0, gb))
    return pl.pallas_call(
        body, name=name, grid=(S5_GROUPS // GB, bl),
        in_specs=[blk(CW), blk(CW), blk(SW), blk(SW), wspec(2), wspec(1), aspec],
        out_specs=(blk(CW), wspec(2), wspec(1), aspec),
        out_shape=(jax.ShapeDtypeStruct(u.shape, F32), jax.ShapeDtypeStruct((S5_GROUPS, CW, 2 * CW), F32),
                   jax.ShapeDtypeStruct((S5_GROUPS, CW, CW), F32), jax.ShapeDtypeStruct(a2.shape, F32)),
        scratch_shapes=[pltpu.VMEM((nc, GB * 2 * SW), F32)] * 2 + [pltpu.VMEM((nc, GB * SW), F32)] * 2,
        compiler_params=_params(2, 48))(dy, u, sf, sb, wcat, mout, a2)


def _lane_slot():
    return lax.broadcasted_iota(jnp.int32, (GRID_W, HEAD), 1) // S5_GROUP


def _slots_to_chunk(tiles, slot, q, j):
    acc = jnp.zeros(tiles[0].shape, F32)
    for m in range(HEAD // S5_GROUP):
        shift = ((m - q) * S5_GROUP) % HEAD
        v = tiles[8 * j + m]
        acc = jnp.where(slot == m, v if shift == 0 else pltpu.roll(v, shift, 1), acc)
    return acc


def _slots_to_rows(tiles, slot, m):
    acc = jnp.zeros(tiles[0].shape, F32)
    for q in range(HEAD // S5_GROUP):
        shift = ((q - m) * S5_GROUP) % HEAD
        acc = jnp.where(slot == q, tiles[q] if shift == 0 else pltpu.roll(tiles[q], shift, 1), acc)
    return acc


def _to_chunks(src, col0, bl, s, lc, *, name):
    ncc = lc // T_CH
    nrh = (s - lc) // GRID_W // T_CH
    nc = ncc + GRID_W * nrh

    def body(x_ref, o_ref, tmp):
        slot_c = lax.broadcasted_iota(jnp.int32, (ncc, HEAD), 1) // S5_GROUP
        slot = _lane_slot()
        for j in range(CW // HEAD):
            for q in range(HEAD // S5_GROUP):
                tiles = [x_ref[pl.ds(8 * j + m, ncc, stride=T_CH), :] for m in range(8)]
                tmp[0:ncc, :] = _slots_to_chunk(tiles, slot_c, q, 0)

                def one(rh, _):
                    tiles = [x_ref[pl.ds(pl.multiple_of(lc + (rh * T_CH + 8 * j + m) * GRID_W, GRID_W), GRID_W), :]
                             for m in range(8)]
                    tmp[pl.ds(ncc + rh, GRID_W, stride=nrh), :] = _slots_to_chunk(tiles, slot, q, 0)
                    return 0

                lax.fori_loop(0, nrh, one, 0)
                o_ref[:, q * CW + j * HEAD:q * CW + (j + 1) * HEAD] = tmp[...]

    return pl.pallas_call(
        body, name=name, grid=(bl, S5_GROUPS // GB),
        in_specs=[pl.BlockSpec((s, HEAD), lambda b, gb: (b, col0 + gb))],
        out_specs=pl.BlockSpec((None, nc, GB * CW), lambda b, gb: (b, 0, gb)),
        out_shape=jax.ShapeDtypeStruct((bl, nc, S5_GROUPS * CW), F32),
        scratch_shapes=[pltpu.VMEM((nc, HEAD), F32)], compiler_params=_params(2))(src)


def _from_chunks(v, bl, s, lc, *, name):
    ncc = lc // T_CH
    nrh = (s - lc) // GRID_W // T_CH
    nc = v.shape[1]

    def body(v_ref, o_ref, *tmp):
        slot_c = lax.broadcasted_iota(jnp.int32, (ncc, HEAD), 1) // S5_GROUP
        slot = _lane_slot()
        for j in range(CW // HEAD):
            for q in range(HEAD // S5_GROUP):
                tmp[q][...] = v_ref[:, q * CW + j * HEAD:q * CW + (j + 1) * HEAD]
            for m in range(8):
                tiles = [tmp[q][0:ncc, :] for q in range(HEAD // S5_GROUP)]
                o_ref[pl.ds(8 * j + m, ncc, stride=T_CH), :] = _slots_to_rows(tiles, slot_c, m)

            def one(rh, _):
                for m in range(8):
                    tiles = [tmp[q][pl.ds(ncc + rh, GRID_W, stride=nrh), :] for q in range(HEAD // S5_GROUP)]
                    rows = pl.ds(pl.multiple_of(lc + (rh * T_CH + 8 * j + m) * GRID_W, GRID_W), GRID_W)
                    o_ref[rows, :] = _slots_to_rows(tiles, slot, m)
                return 0

            lax.fori_loop(0, nrh, one, 0)

    return pl.pallas_call(
        body, name=name, grid=(bl, S5_GROUPS // GB),
        in_specs=[pl.BlockSpec((None, nc, GB * CW), lambda b, gb: (b, 0, gb))],
        out_specs=pl.BlockSpec((s, HEAD), lambda b, gb: (b, gb)),
        out_shape=jax.ShapeDtypeStruct((bl * s, D_MODEL), F32),
        scratch_shapes=[pltpu.VMEM((nc, HEAD), F32)] * (HEAD // S5_GROUP), compiler_params=_params(2))(v)


_ANY = pl.BlockSpec(memory_space=pl.ANY)


def _xy_peers():
    x, y, c = lax.axis_index("x"), lax.axis_index("y"), lax.axis_index("c")
    return x, y, c, [(1 - x, y), (x, 1 - y), (1 - x, 1 - y)]


def _all_gather_xy(shard, *, name):
    def body(x_ref, out_ref, send_sems, recv_sems, local_sem):
        x, y, c, peers = _xy_peers()
        me = 2 * x + y
        mine = pltpu.make_async_copy(x_ref, out_ref.at[me], local_sem)
        mine.start()

        def copy(k, px, py, slot):
            return pltpu.make_async_remote_copy(src_ref=x_ref, dst_ref=out_ref.at[slot], send_sem=send_sems.at[k],
                                                recv_sem=recv_sems.at[k], device_id=(px, py, c), device_id_type=MESH)

        sends = [copy(k, px, py, me) for k, (px, py) in enumerate(peers)]
        for cp in sends:
            cp.start()
        for k, (px, py) in enumerate(peers):
            copy(k, px, py, 2 * px + py).wait_recv()
        for cp in sends:
            cp.wait_send()
        mine.wait()

    return pl.pallas_call(body, name=name, in_specs=[_ANY], out_specs=_ANY,
                          out_shape=jax.ShapeDtypeStruct((4,) + shard.shape, shard.dtype),
                          scratch_shapes=[pltpu.SemaphoreType.DMA((3,)), pltpu.SemaphoreType.DMA((3,)),
                                          pltpu.SemaphoreType.DMA])(shard)


def _scatter_xy(parts, *, name):
    def body(p_ref, out_ref, send_sems, recv_sems, local_sem):
        x, y, c, peers = _xy_peers()
        mine = pltpu.make_async_copy(p_ref.at[2 * x + y], out_ref.at[0], local_sem)
        mine.start()

        def copy(k, px, py):
            return pltpu.make_async_remote_copy(src_ref=p_ref.at[2 * px + py], dst_ref=out_ref.at[1 + k],
                                                send_sem=send_sems.at[k], recv_sem=recv_sems.at[k],
                                                device_id=(px, py, c), device_id_type=MESH)

        sends = [copy(k, px, py) for k, (px, py) in enumerate(peers)]
        for cp in sends:
            cp.start()
        for cp in sends:
            cp.wait_recv()
        for cp in sends:
            cp.wait_send()
        mine.wait()

    return pl.pallas_call(body, name=name, in_specs=[_ANY], out_specs=_ANY,
                          out_shape=jax.ShapeDtypeStruct(parts.shape, parts.dtype),
                          scratch_shapes=[pltpu.SemaphoreType.DMA((3,)), pltpu.SemaphoreType.DMA((3,)),
                                          pltpu.SemaphoreType.DMA])(parts)


def _swap_sibling(v, *, name):
    def body(v_ref, out_ref, send_sem, recv_sem):
        x, y, c = lax.axis_index("x"), lax.axis_index("y"), lax.axis_index("c")
        cp = pltpu.make_async_remote_copy(src_ref=v_ref, dst_ref=out_ref, send_sem=send_sem, recv_sem=recv_sem,
                                          device_id=(x, y, 1 - c), device_id_type=MESH)
        cp.start()
        cp.wait()

    return pl.pallas_call(body, name=name, in_specs=[_ANY], out_specs=_ANY,
                          out_shape=jax.ShapeDtypeStruct(v.shape, v.dtype),
                          scratch_shapes=[pltpu.SemaphoreType.DMA, pltpu.SemaphoreType.DMA])(v)


BIG_COLS = {'ada_w': True, 'w_in': True, 'mlp_w1': True, 's5_glu_w': False, 'w_out': False, 'mlp_w2': False}
BIG = list(BIG_COLS)


def _block(ref2d, j, cols, size):
    if cols:
        return ref2d.at[:, pl.ds(pl.multiple_of(j * size, 128), size)]
    return ref2d.at[pl.ds(pl.multiple_of(j * size, 8), size), :]


def _shard_size(shape, cols):
    return shape[-1] if cols else shape[-2]


def _cast_into_full(shard, cols, my_j, *, name):
    _, r, c = shard.shape
    tr, tc = _tile(r, (256,)), _tile(c, (1024, 768, 512))

    def body(j_ref, x_ref, o_ref):
        o_ref[...] = x_ref[...].astype(BF16)

    if cols:
        out_spec = pl.BlockSpec((None, tr, tc), lambda l, i, j, j_ref: (l, i, j_ref[0] * (c // tc) + j))
    else:
        out_spec = pl.BlockSpec((None, tr, tc), lambda l, i, j, j_ref: (l, j_ref[0] * (r // tr) + i, j))
    return pl.pallas_call(
        body, name=name,
        grid_spec=pltpu.PrefetchScalarGridSpec(
            num_scalar_prefetch=1, grid=(DEPTH, r // tr, c // tc),
            in_specs=[pl.BlockSpec((None, tr, tc), lambda l, i, j, j_ref: (l, i, j))], out_specs=out_spec),
        out_shape=jax.ShapeDtypeStruct((DEPTH, r, 4 * c) if cols else (DEPTH, 4 * r, c), BF16),
        compiler_params=_params(3))(my_j, shard)


def _gather_big(fulls, cols, *, name):
    n = len(fulls)

    def body(*refs):
        outs = refs[n:2 * n]
        ici_send, ici_recv, d2d_send, d2d_recv = refs[2 * n:]
        x, y, c, peers = _xy_peers()
        me = 2 * x + y

        def blk(w, layer, j):
            shape = outs[w].shape
            return _block(outs[w].at[layer], j, cols[w], (shape[2] if cols[w] else shape[1]) // 4)

        def ici(w, k, px, py, j):
            return pltpu.make_async_remote_copy(src_ref=blk(w, c, j), dst_ref=blk(w, c, j),
                                                send_sem=ici_send.at[3 * w + k], recv_sem=ici_recv.at[3 * w + k],
                                                device_id=(px, py, c), device_id_type=MESH)

        def d2d(w, k, j, layer):
            return pltpu.make_async_remote_copy(src_ref=blk(w, layer, j), dst_ref=blk(w, layer, j),
                                                send_sem=d2d_send.at[3 * w + k], recv_sem=d2d_recv.at[3 * w + k],
                                                device_id=(x, y, 1 - c), device_id_type=MESH)

        started = [ici(w, k, px, py, me) for w in range(n) for k, (px, py) in enumerate(peers)]
        for cp in started:
            cp.start()
        passed = []
        for w in range(n):
            for k, (px, py) in enumerate(peers):
                ici(w, k, px, py, 2 * px + py).wait_recv()
                passed.append(d2d(w, k, 2 * px + py, c))
                passed[-1].start()
        for w in range(n):
            for k, (px, py) in enumerate(peers):
                d2d(w, k, 2 * px + py, 1 - c).wait_recv()
        for cp in started + passed:
            cp.wait_send()

    return pl.pallas_call(
        body, name=name, in_specs=[_ANY] * n, out_specs=[_ANY] * n,
        out_shape=[jax.ShapeDtypeStruct(f.shape, f.dtype) for f in fulls],
        input_output_aliases={w: w for w in range(n)},
        scratch_shapes=[pltpu.SemaphoreType.DMA((3 * n,))] * 4)(*fulls)


def _sibling_partials(gbufs, *, name):
    n = len(gbufs)

    def body(*refs):
        ins, outs, send, recv = refs[:n], refs[n:2 * n], refs[2 * n], refs[2 * n + 1]
        x, y, c = lax.axis_index("x"), lax.axis_index("y"), lax.axis_index("c")
        cps = [pltpu.make_async_remote_copy(src_ref=ins[w].at[1 - c], dst_ref=outs[w], send_sem=send.at[w],
                                            recv_sem=recv.at[w], device_id=(x, y, 1 - c), device_id_type=MESH)
               for w in range(n)]
        for cp in cps:
            cp.start()
        for cp in cps:
            cp.wait()

    return pl.pallas_call(body, name=name, in_specs=[_ANY] * n, out_specs=[_ANY] * n,
                          out_shape=[jax.ShapeDtypeStruct(g.shape[1:], g.dtype) for g in gbufs],
                          scratch_shapes=[pltpu.SemaphoreType.DMA((n,))] * 2)(*gbufs)


def _chip_sum(gbuf, other, my_c, *, name):
    _, k, n = gbuf.shape
    tr, tc = _tile(k, (512,)), _tile(n, (1024,))

    def body(c_ref, a_ref, b_ref, o_ref):
        o_ref[...] = (a_ref[...] + b_ref[...]).astype(BF16)

    spec = pl.BlockSpec((tr, tc), lambda i, j, c_ref: (i, j))
    return pl.pallas_call(
        body, name=name,
        grid_spec=pltpu.PrefetchScalarGridSpec(
            num_scalar_prefetch=1, grid=(k // tr, n // tc),
            in_specs=[pl.BlockSpec((None, tr, tc), lambda i, j, c_ref: (c_ref[0], i, j)), spec], out_specs=spec),
        out_shape=jax.ShapeDtypeStruct((k, n), BF16), compiler_params=_params(2))(my_c, gbuf, other)


def _scatter_big(sums, cols, *, name):
    n = len(sums)

    def shard(s, cf):
        return (s.shape[0], s.shape[1] // 4) if cf else (s.shape[0] // 4, s.shape[1])

    def body(*refs):
        ins, outs, send, recv = refs[:n], refs[n:2 * n], refs[2 * n], refs[2 * n + 1]
        x, y, c, peers = _xy_peers()
        cps = []
        for w in range(n):
            size = _shard_size(shard(ins[w], cols[w]), cols[w])
            for k, (px, py) in enumerate(peers):
                cps.append(pltpu.make_async_remote_copy(
                    src_ref=_block(ins[w], 2 * px + py, cols[w], size), dst_ref=outs[w].at[k],
                    send_sem=send.at[3 * w + k], recv_sem=recv.at[3 * w + k], device_id=(px, py, c),
                    device_id_type=MESH))
        for cp in cps:
            cp.start()
        for cp in cps:
            cp.wait()

    return pl.pallas_call(body, name=name, in_specs=[_ANY] * n, out_specs=[_ANY] * n,
                          out_shape=[jax.ShapeDtypeStruct((3,) + shard(s, cf), s.dtype) for s, cf in zip(sums, cols)],
                          scratch_shapes=[pltpu.SemaphoreType.DMA((3 * n,))] * 2)(*sums)


def _block_sum(own, got, cols, my_j, my_c, *, name):
    _, r, c = got.shape
    tr, tc = _tile(r, (256,)), _tile(c, (1024, 768, 512))

    def body(j_ref, c_ref, a_ref, g_ref, o_ref):
        o_ref[...] = ((a_ref[...].astype(F32) + g_ref[0].astype(F32)) + g_ref[1].astype(F32)) + g_ref[2].astype(F32)

    if cols:
        own_spec = pl.BlockSpec((tr, tc), lambda i, j, j_ref, c_ref: (i, j_ref[0] * (c // tc) + j))
    else:
        own_spec = pl.BlockSpec((tr, tc), lambda i, j, j_ref, c_ref: (j_ref[0] * (r // tr) + i, j))
    return pl.pallas_call(
        body, name=name,
        grid_spec=pltpu.PrefetchScalarGridSpec(
            num_scalar_prefetch=2, grid=(r // tr, c // tc),
            in_specs=[own_spec, pl.BlockSpec((3, tr, tc), lambda i, j, j_ref, c_ref: (0, i, j))],
            out_specs=pl.BlockSpec((None, tr, tc), lambda i, j, j_ref, c_ref: (c_ref[0], i, j))),
        out_shape=jax.ShapeDtypeStruct((DEPTH, r, c), F32), compiler_params=_params(2))(my_j, my_c, own, got)


def _share_final(bufs, *, name):
    n = len(bufs)

    def body(*refs):
        outs, send, recv = refs[n:2 * n], refs[2 * n], refs[2 * n + 1]
        x, y, c = lax.axis_index("x"), lax.axis_index("y"), lax.axis_index("c")

        def copy(w, slot):
            return pltpu.make_async_remote_copy(src_ref=outs[w].at[slot], dst_ref=outs[w].at[slot],
                                                send_sem=send.at[w], recv_sem=recv.at[w],
                                                device_id=(x, y, 1 - c), device_id_type=MESH)

        away = [copy(w, c) for w in range(n)]
        for cp in away:
            cp.start()
        for w in range(n):
            copy(w, 1 - c).wait_recv()
        for cp in away:
            cp.wait_send()

    return pl.pallas_call(body, name=name, in_specs=[_ANY] * n, out_specs=[_ANY] * n,
                          out_shape=[jax.ShapeDtypeStruct(b.shape, b.dtype) for b in bufs],
                          input_output_aliases={w: w for w in range(n)},
                          scratch_shapes=[pltpu.SemaphoreType.DMA((n,))] * 2)(*bufs)


def _adamw_native(w, g, m, v, *, name):
    r, c = w.shape
    tr = _tile(r, (256, 128, 64, 32, 16, 8))
    spec = pl.BlockSpec((tr, c), lambda i: (i, 0))
    c1 = 1.0 / (1.0 - ADAM_B1 ** ADAM_STEP)
    c2 = 1.0 / (1.0 - ADAM_B2 ** ADAM_STEP)

    def body(w_ref, g_ref, m_ref, v_ref, d_ref, nm_ref, nv_ref):
        g_t = g_ref[...]
        nm = ADAM_B1 * m_ref[...] + (1.0 - ADAM_B1) * g_t
        nv = ADAM_B2 * v_ref[...] + (1.0 - ADAM_B2) * (g_t * g_t)
        nm_ref[...] = nm
        nv_ref[...] = nv
        d_ref[...] = -ADAM_LR * ((nm * c1) / (jnp.sqrt(nv * c2) + ADAM_EPS) + ADAM_WD * w_ref[...])

    shp = jax.ShapeDtypeStruct((r, c), F32)
    return pl.pallas_call(body, name=name, grid=(r // tr,), in_specs=[spec] * 4, out_specs=(spec,) * 3,
                          out_shape=(shp,) * 3, compiler_params=_params(1))(w, g, m, v)


def _flat_tile(r):
    return _tile(r, (512, 256, 128, 64, 32, 16, 8))


def _sum4(parts, *, name):
    r = parts.shape[1]
    tr = _flat_tile(r)

    def body(p_ref, o_ref):
        o_ref[...] = ((p_ref[0] + p_ref[1]) + p_ref[2]) + p_ref[3]

    return pl.pallas_call(body, name=name, grid=(r // tr,),
                          in_specs=[pl.BlockSpec((4, tr, LANES), lambda i: (0, i, 0))],
                          out_specs=pl.BlockSpec((tr, LANES), lambda i: (i, 0)),
                          out_shape=jax.ShapeDtypeStruct((r, LANES), F32), compiler_params=_params(1))(parts)


def _add2(a, b, *, name):
    r = a.shape[0]
    tr = _flat_tile(r)
    spec = pl.BlockSpec((tr, LANES), lambda i: (i, 0))

    def body(a_ref, b_ref, o_ref):
        o_ref[...] = a_ref[...] + b_ref[...]

    return pl.pallas_call(body, name=name, grid=(r // tr,), in_specs=[spec, spec], out_specs=spec,
                          out_shape=jax.ShapeDtypeStruct((r, LANES), F32), compiler_params=_params(1))(a, b)


def _adamw(w, ga, gb, m, v, *, name):
    r = w.shape[0]
    tr = _flat_tile(r)
    spec = pl.BlockSpec((tr, LANES), lambda i: (i, 0))
    two = gb is not None
    c1 = 1.0 / (1.0 - ADAM_B1 ** ADAM_STEP)
    c2 = 1.0 / (1.0 - ADAM_B2 ** ADAM_STEP)

    def body(*refs):
        w_ref, ga_ref = refs[0], refs[1]
        m_ref, v_ref, g_ref, d_ref, nm_ref, nv_ref = refs[2 + two:]
        g = ga_ref[...] + refs[2][...] if two else ga_ref[...]
        nm = ADAM_B1 * m_ref[...] + (1.0 - ADAM_B1) * g
        nv = ADAM_B2 * v_ref[...] + (1.0 - ADAM_B2) * (g * g)
        g_ref[...] = g
        nm_ref[...] = nm
        nv_ref[...] = nv
        d_ref[...] = -ADAM_LR * ((nm * c1) / (jnp.sqrt(nv * c2) + ADAM_EPS) + ADAM_WD * w_ref[...])

    args = [w, ga] + ([gb] if two else []) + [m, v]
    shp = jax.ShapeDtypeStruct((r, LANES), F32)
    return pl.pallas_call(body, name=name, grid=(r // tr,), in_specs=[spec] * len(args), out_specs=(spec,) * 4,
                          out_shape=(shp,) * 4, compiler_params=_params(1))(*args)


def _pack(arrs, dtype=F32):
    rows = []
    for a in arrs:
        flat = a.astype(dtype).reshape(-1)
        rows.append(jnp.pad(flat, (0, (-flat.shape[0]) % LANES)).reshape(-1, LANES))
    buf = jnp.concatenate(rows)
    return jnp.pad(buf, ((0, (-buf.shape[0]) % 32), (0, 0)))


def _unpack(buf, shapes):
    out, row = [], 0
    for shp in shapes:
        sz = math.prod(shp)
        nr = -(-sz // LANES)
        out.append(buf[row:row + nr].reshape(-1)[:sz].reshape(shp))
        row += nr
    return out


def _stack_shards(full, axis):
    shp = full.shape
    return jnp.moveaxis(full.reshape(shp[:axis] + (4, shp[axis] // 4) + shp[axis + 1:]), axis, 0)


def _unstack_shards(st, axis):
    v = jnp.moveaxis(st, 0, axis)
    shp = v.shape
    return v.reshape(shp[:axis] + (shp[axis] * shp[axis + 1],) + shp[axis + 2:])


def _layer_weights(w, l):
    lw = {n: (w[n], l) if n in BIG_COLS else w[n][l] for n in w}
    lw['wcat'] = jnp.concatenate([lw['rg_wa'][0], lw['rg_wi'][0], lw['rg_wa'][1], lw['rg_wi'][1]],
                                 axis=-1).astype(BF16)
    ba, bi = lw['rg_ba'].reshape(2, RG_HEADS, HEAD), lw['rg_bi'].reshape(2, RG_HEADS, HEAD)
    lw['bcat'] = jnp.concatenate([ba[0], bi[0], ba[1], bi[1]], axis=-1)[:, None, :]
    s5_names = ['s5_a_re', 's5_a_im', 's5_log_dt', 's5_b_re', 's5_b_im', 's5_c_re', 's5_c_im']
    (wcat, mout, a2), lw['s5_vjp'] = jax.vjp(_s5_mats, *[lw[n] for n in s5_names])
    lw['s5_wcat'], lw['s5_mout'], lw['s5_a2'] = wcat.astype(BF16), mout.astype(BF16), a2
    for n in ('conv_b', 's5_d', 's5_glu_b', 'b_out', 'mlp_b1', 'mlp_b2', 'ln1_g', 'ln1_b', 'ln2_g', 'ln2_b'):
        lw[n] = lw[n][None, :]
    return lw


def _layer_fwd(l, x0, modall, lw, dims):
    bl, s, lc, tps = dims
    ll = s - lc
    tag = f"l{l}_"
    sv = {'x0': x0}
    sv['u1'] = _modulate(x0, modall, 0, 1, tps, name=tag + "mod1")
    sv['p'] = p = _mm_nn(sv['u1'], lw['w_in'], name=tag + "w_in")
    rg, sv['hf'], sv['hb'] = _rg_fwd(p, lw['conv_w'], lw['conv_b'], lw['rg_lambda'], lw['wcat'], lw['bcat'],
                                     bl, s, lc, name=tag + "rg_fwd")
    sv['u_ch'] = _to_chunks(p, 2 * D_MODEL // HEAD, bl, s, lc, name=tag + "u_chunks")
    y_ch, sv['sf'], sv['sb'] = _s5_fwd(sv['u_ch'], lw['s5_wcat'], lw['s5_mout'], lw['s5_a2'], lc // T_CH,
                                       name=tag + "s5_fwd")
    sv['y'] = _from_chunks(y_ch, bl, s, lc, name=tag + "y_rows")
    sv['cat'] = _glu_fwd(sv['y'], p, lw['s5_d'], lw['s5_glu_w'], lw['s5_glu_b'], rg, name=tag + "glu_fwd")
    sv['m'] = _mm_nn(sv['cat'], lw['w_out'], lw['b_out'], name=tag + "w_out")
    sv['x1'] = _resid_ln(x0, sv['m'], modall, 2, lw['ln1_g'], lw['ln1_b'], tps, name=tag + "ln1")
    sv['u2'] = _modulate(sv['x1'], modall, 3, 4, tps, name=tag + "mod2")
    sv['a'], sv['h'] = _mm_nn(sv['u2'], lw['mlp_w1'], lw['mlp_b1'], relu2=True, name=tag + "mlp1")
    sv['f'] = _mm_nn(sv['a'], lw['mlp_w2'], lw['mlp_b2'], name=tag + "mlp2")
    x2 = _resid_ln(sv['x1'], sv['f'], modall, 5, lw['ln2_g'], lw['ln2_b'], tps, name=tag + "ln2")
    return x2, sv


def _layer_bwd(l, dx2, modall, lw, sv, dims, gbufs):
    bl, s, lc, tps = dims
    ll = s - lc
    tag = f"l{l}_"
    g = {}

    def big_grad(n, a_mat, b_mat, label):
        gbufs[n] = _mm_tn(a_mat, b_mat, name=tag + label, layer=l, into=gbufs.get(n))
    dx1a, df, db2, g['ln2_g'], g['ln2_b'], dg2 = _resid_ln_bwd(sv['x1'], sv['f'], modall, 5, lw['ln2_g'], dx2, tps,
                                                              name=tag + "ln2_bwd")
    g['mlp_b2'] = db2
    big_grad('mlp_w2', sv['a'], df, "mlp2_dw")
    dh = _mm_nt(df, lw['mlp_w2'], sv['h'], name=tag + "mlp2_dx")
    g['mlp_b1'] = _colsum(dh, name=tag + "mlp1_db")
    big_grad('mlp_w1', sv['u2'], dh, "mlp1_dw")
    du2 = _mm_nt(dh, lw['mlp_w1'], name=tag + "mlp1_dx")
    dx1, dsc2, dsh2 = _modulate_bwd(du2, sv['x1'], modall, 4, dx1a, tps, name=tag + "mod2_bwd")
    dx0a, dm, g['b_out'], g['ln1_g'], g['ln1_b'], dg1 = _resid_ln_bwd(sv['x0'], sv['m'], modall, 2, lw['ln1_g'], dx1,
                                                                     tps, name=tag + "ln1_bwd")
    big_grad('w_out', sv['cat'], dm, "w_out_dw")
    dcat = _mm_nt(dm, lw['w_out'], name=tag + "w_out_dx")
    dy, dskip, g_bf, dz_bf, g['s5_d'], g['s5_glu_b'] = _glu_bwd(dcat, sv['y'], sv['p'], lw['s5_d'], lw['s5_glu_w'],
                                                                lw['s5_glu_b'], name=tag + "glu_bwd")
    big_grad('s5_glu_w', g_bf, dz_bf, "glu_dw")
    dy_ch = _to_chunks(dy, 0, bl, s, lc, name=tag + "dy_chunks")
    du_ch, dwcat, dmout, dacc = _s5_bwd(dy_ch, sv['u_ch'], sv['sf'], sv['sb'], lw['s5_wcat'],
                                        lw['s5_mout'], lw['s5_a2'], lc // T_CH, name=tag + "s5_bwd")
    s5g = lw['s5_vjp']((dwcat, dmout, dacc))
    for n, v in zip(['s5_a_re', 's5_a_im', 's5_log_dt', 's5_b_re', 's5_b_im', 's5_c_re', 's5_c_im'], s5g):
        g[n] = v
    ds5u = _add_cast(_from_chunks(du_ch, bl, s, lc, name=tag + "du_rows"), dskip, name=tag + "ds5u")
    drgx, dgate, dcw, dcb, dlam, dwc, dbc = _rg_bwd(sv['p'], dcat, sv['hf'], sv['hb'], lw['conv_w'], lw['conv_b'],
                                                    lw['rg_lambda'], lw['wcat'], lw['bcat'], bl, s, lc,
                                                    name=tag + "rg_bwd")
    g['conv_w'] = dcw.transpose(1, 0, 2).reshape(4, D_MODEL)
    g['conv_b'] = dcb.reshape(D_MODEL)
    g['rg_lambda'] = dlam.transpose(1, 0, 2).reshape(2, D_MODEL)
    g['rg_wa'] = jnp.stack([dwc[:, :, 0:HEAD], dwc[:, :, 2 * HEAD:3 * HEAD]])
    g['rg_wi'] = jnp.stack([dwc[:, :, HEAD:2 * HEAD], dwc[:, :, 3 * HEAD:]])
    dbc = dbc.reshape(RG_HEADS, 4, HEAD)
    g['rg_ba'] = jnp.stack([dbc[:, 0], dbc[:, 2]]).reshape(2, D_MODEL)
    g['rg_bi'] = jnp.stack([dbc[:, 1], dbc[:, 3]]).reshape(2, D_MODEL)
    dp = jnp.concatenate([drgx, dgate, ds5u], axis=1)
    big_grad('w_in', sv['u1'], dp, "w_in_dw")
    du1 = _mm_nt(dp, lw['w_in'], name=tag + "w_in_dx")
    dx0, dsc1, dsh1 = _modulate_bwd(du1, sv['x0'], modall, 1, dx0a, tps, name=tag + "mod1_bwd")
    dmod = jnp.concatenate([dsh1, dsc1, dg1, dsh2, dsc2, dg2], axis=1)
    return dx0, g, dmod


def _kernel_impl(*args):
    nin = len(IN_NAMES)
    a = dict(zip(IN_NAMES, args[:nin]))
    target = args[nin]
    nw = len(WEIGHTS)
    mom = dict(zip(WEIGHTS, args[nin + 1:nin + 1 + nw]))
    var = dict(zip(WEIGHTS, args[nin + 1 + nw:nin + 1 + 2 * nw]))
    bl, ll, d = a['x'].shape
    lc = a['ctx'].shape[1]
    assert d == D_MODEL and lc == TM and bl == 2 and ll % (GRID_W * T_CH) == 0
    s = lc + ll
    tps = s // TM
    dims = (bl, s, lc, tps)

    def gather(names, dtype, tag):
        shards = [a[n] for n in names]
        got = _all_gather_xy(_pack(shards, dtype), name="gather_" + tag)
        per = [_unpack(got[j], [w.shape for w in shards]) for j in range(4)]
        return {n: _unstack_shards(jnp.stack([per[j][i] for j in range(4)]), SHARD_AXIS[n])
                for i, n in enumerate(names)}

    big_cols = [BIG_COLS[n] for n in BIG]
    my_c = lax.axis_index("c").astype(jnp.int32).reshape(1)
    my_j = (2 * lax.axis_index("x") + lax.axis_index("y")).astype(jnp.int32).reshape(1)
    mine = [_cast_into_full(a[n], BIG_COLS[n], my_j, name=f"cast_{n}") for n in BIG]
    w = dict(zip(BIG, _gather_big(mine, big_cols, name="gather_big")))
    w.update(gather(GATHER_F32, F32, "f32"))
    for n in REPLICATED:
        w[n] = a[n]

    xs = jnp.concatenate([a['ctx'], a['x']], axis=1).reshape(bl * s, D_MODEL)
    c16 = jnp.zeros((16, D_MODEL), F32).at[0:2].set(a['c']).at[2].set(a['c_ctx'])
    s16, ds16 = _silu_rows(c16, name="silu")
    s16b = s16.astype(BF16)
    layers, saved, mods = [], [], []
    for l in range(DEPTH):
        lw = _layer_weights({n: w[n] for n in WEIGHTS if n not in ('c_ctx',)}, l)
        mod16 = _mm_nn(s16b, lw['ada_w'], lw['ada_b'][None, :], name=f"l{l}_ada").reshape(16, N_MOD, D_MODEL)
        modall = jnp.stack([mod16[2], mod16[0], mod16[2], mod16[1]])
        xs, sv = _layer_fwd(l, xs, modall, lw, dims)
        layers.append(lw)
        saved.append(sv)
        mods.append(modall)
    lossrow, dx = _loss_head(xs, target.reshape(bl * ll, D_MODEL), tps, name="loss_head")
    loss = lax.psum(0.5 / D_MODEL * jnp.sum(lossrow), ("x", "y", "c"))

    small = [n for n in WEIGHTS if n != 'c_ctx' and n not in BIG_COLS]
    grads = {n: [None] * DEPTH for n in small}
    gbufs = {}
    ds_rows = jnp.zeros((16, D_MODEL), F32)
    for l in reversed(range(DEPTH)):
        dx, g, dmod = _layer_bwd(l, dx, mods[l], layers[l], saved[l], dims, gbufs)
        dmod16 = jnp.zeros((16, N_MOD * D_MODEL), F32).at[0].set(dmod[1].reshape(-1)).at[1].set(
            dmod[3].reshape(-1)).at[2].set((dmod[0] + dmod[2]).reshape(-1))
        dmod16b = dmod16.astype(BF16)
        gbufs['ada_w'] = _mm_tn(s16b, dmod16b, name=f"l{l}_ada_dw", layer=l, into=gbufs.get('ada_w'))
        g['ada_b'] = _colsum(dmod16, name=f"l{l}_ada_db")
        ds_rows = ds_rows + _mm_nt(dmod16b, layers[l]['ada_w'], name=f"l{l}_ada_dx")
        for n, v in g.items():
            grads[n][l] = v.reshape(a[n].shape[1:] if n in REPLICATED else w[n].shape[1:])
    full = {n: jnp.stack(v) for n, v in grads.items()}
    full['c_ctx'] = _mul_rows(ds_rows, ds16, name="silu_bwd")[2]
    grad_x = dx.reshape(bl, s, D_MODEL)[:, lc:]

    from_sib = _sibling_partials([gbufs[n] for n in BIG], name="grad_big_sibling")
    sums = [_chip_sum(gbufs[n], o, my_c, name=f"grad_chip_sum_{n}") for n, o in zip(BIG, from_sib)]
    got = _scatter_big(sums, big_cols, name="grad_big_scatter")
    finals = [_block_sum(sm, gt, cf, my_j, my_c, name=f"grad_block_sum_{n}")
              for n, sm, gt, cf in zip(BIG, sums, got, big_cols)]
    res_big = {}
    for n, gfull in zip(BIG, _share_final(finals, name="grad_big_share")):
        flat = lambda t: t.reshape(-1, t.shape[-1])
        d_w, n_m, n_v = _adamw_native(flat(a[n]), flat(gfull), flat(mom[n]), flat(var[n]), name=f"adamw_{n}")
        res_big[n] = [gfull] + [t.reshape(a[n].shape) for t in (d_w, n_m, n_v)]

    rep_flat = _pack([full[n] for n in REPLICATED])
    rr = rep_flat.shape[0]
    sh_stacked = [_stack_shards(full[n], SHARD_AXIS[n] + 0).reshape(4, -1) for n in SHARDED]
    parts = jnp.concatenate(sh_stacked + [rep_flat.reshape(4, -1)], axis=1)
    pad = (-parts.shape[1]) % (32 * LANES)
    parts = jnp.pad(parts, ((0, 0), (0, pad))).reshape(4, -1, LANES)
    mine = _sum4(_scatter_xy(parts, name="grad_scatter"), name="grad_sum4")
    other = _swap_sibling(mine, name="grad_swap")
    n_sh = sum(math.prod(a[n].shape) for n in SHARDED)
    r_sh = n_sh // LANES
    assert n_sh % LANES == 0
    rq = rr // 4

    sh_shapes = [a[n].shape for n in SHARDED]
    pk = lambda dct: _pack([dct[n] for n in SHARDED])
    r_pk = pk(a).shape[0]
    take = lambda buf: jnp.pad(buf[:r_sh], ((0, r_pk - r_sh), (0, 0)))
    outs_sh = _adamw(pk(a), take(mine), take(other), pk(mom), pk(var), name="adamw_sharded")
    res_sh = [dict(zip(SHARDED, _unpack(o, sh_shapes))) for o in outs_sh]

    quarter = _add2(mine[r_sh:r_sh + rq], other[r_sh:r_sh + rq], name="grad_rep_sum")
    rep_g = _all_gather_xy(quarter, name="grad_rep_gather").reshape(rr, LANES)
    rep_shapes = [a[n].shape for n in REPLICATED]
    pr = lambda dct: _pack([dct[n] for n in REPLICATED])
    outs_rep = _adamw(pr(a), rep_g, None, pr(mom), pr(var), name="adamw_replicated")
    res_rep = [dict(zip(REPLICATED, _unpack(o, rep_shapes))) for o in outs_rep]

    out = [loss, grad_x]
    for k in range(4):
        out += [res_big[n][k] if n in BIG_COLS else res_sh[k][n] if n in SHARDED else res_rep[k][n] for n in WEIGHTS]
    return tuple(out)


def kernel(x, c, ctx, c_ctx, ada_w, ada_b, ln1_g, ln1_b, w_in, conv_w, conv_b, rg_lambda, rg_wa, rg_ba, rg_wi, rg_bi, s5_a_re, s5_a_im, s5_log_dt, s5_b_re, s5_b_im, s5_c_re, s5_c_im, s5_d, s5_glu_w, s5_glu_b, w_out, b_out, ln2_g, ln2_b, mlp_w1, mlp_b1, mlp_w2, mlp_b2, loss_target, m_c_ctx, m_ada_w, m_ada_b, m_ln1_g, m_ln1_b, m_w_in, m_conv_w, m_conv_b, m_rg_lambda, m_rg_wa, m_rg_ba, m_rg_wi, m_rg_bi, m_s5_a_re, m_s5_a_im, m_s5_log_dt, m_s5_b_re, m_s5_b_im, m_s5_c_re, m_s5_c_im, m_s5_d, m_s5_glu_w, m_s5_glu_b, m_w_out, m_b_out, m_ln2_g, m_ln2_b, m_mlp_w1, m_mlp_b1, m_mlp_w2, m_mlp_b2, v_c_ctx, v_ada_w, v_ada_b, v_ln1_g, v_ln1_b, v_w_in, v_conv_w, v_conv_b, v_rg_lambda, v_rg_wa, v_rg_ba, v_rg_wi, v_rg_bi, v_s5_a_re, v_s5_a_im, v_s5_log_dt, v_s5_b_re, v_s5_b_im, v_s5_c_re, v_s5_c_im, v_s5_d, v_s5_glu_w, v_s5_glu_b, v_w_out, v_b_out, v_ln2_g, v_ln2_b, v_mlp_w1, v_mlp_b1, v_mlp_w2, v_mlp_b2):
    return _kernel_impl(x, c, ctx, c_ctx, ada_w, ada_b, ln1_g, ln1_b, w_in, conv_w, conv_b, rg_lambda, rg_wa, rg_ba, rg_wi, rg_bi, s5_a_re, s5_a_im, s5_log_dt, s5_b_re, s5_b_im, s5_c_re, s5_c_im, s5_d, s5_glu_w, s5_glu_b, w_out, b_out, ln2_g, ln2_b, mlp_w1, mlp_b1, mlp_w2, mlp_b2, loss_target, m_c_ctx, m_ada_w, m_ada_b, m_ln1_g, m_ln1_b, m_w_in, m_conv_w, m_conv_b, m_rg_lambda, m_rg_wa, m_rg_ba, m_rg_wi, m_rg_bi, m_s5_a_re, m_s5_a_im, m_s5_log_dt, m_s5_b_re, m_s5_b_im, m_s5_c_re, m_s5_c_im, m_s5_d, m_s5_glu_w, m_s5_glu_b, m_w_out, m_b_out, m_ln2_g, m_ln2_b, m_mlp_w1, m_mlp_b1, m_mlp_w2, m_mlp_b2, v_c_ctx, v_ada_w, v_ada_b, v_ln1_g, v_ln1_b, v_w_in, v_conv_w, v_conv_b, v_rg_lambda, v_rg_wa, v_rg_ba, v_rg_wi, v_rg_bi, v_s5_a_re, v_s5_a_im, v_s5_log_dt, v_s5_b_re, v_s5_b_im, v_s5_c_re, v_s5_c_im, v_s5_d, v_s5_glu_w, v_s5_glu_b, v_w_out, v_b_out, v_ln2_g, v_ln2_b, v_mlp_w1, v_mlp_b1, v_mlp_w2, v_mlp_b2)
```

```python
import functools
import math

import jax
import jax.numpy as jnp
from jax import lax
from jax.experimental import pallas as pl
from jax.experimental.pallas import tpu as pltpu

F32 = jnp.float32
BF16 = jnp.bfloat16
MESH = pl.DeviceIdType.MESH

D_MODEL = 1024
N_MOD = 6
GRID_W = 64
RG_HEADS = 8
HEAD = 128
RG_C = 8.0
S5_GROUPS = 64
S5_GROUP = 16
S5_STATE = 64
T_CH = 16
GB = 8
CW = T_CH * S5_GROUP
SW = 2 * S5_STATE
DEPTH = 2
ALPHA = (2.0 * DEPTH) ** 0.25
LN_EPS = 1e-5
TM = 256
LANES = 1024
ADAM_LR, ADAM_B1, ADAM_B2, ADAM_EPS, ADAM_WD, ADAM_STEP = 0.001, 0.9, 0.999, 1e-08, 0.01, 10
MIB = 2 ** 20

IN_NAMES = ['x', 'c', 'ctx', 'c_ctx', 'ada_w', 'ada_b', 'ln1_g', 'ln1_b', 'w_in', 'conv_w', 'conv_b', 'rg_lambda',
            'rg_wa', 'rg_ba', 'rg_wi', 'rg_bi', 's5_a_re', 's5_a_im', 's5_log_dt', 's5_b_re', 's5_b_im', 's5_c_re',
            's5_c_im', 's5_d', 's5_glu_w', 's5_glu_b', 'w_out', 'b_out', 'ln2_g', 'ln2_b', 'mlp_w1', 'mlp_b1',
            'mlp_w2', 'mlp_b2']
WEIGHTS = IN_NAMES[3:]
SHARD_AXIS = {'ada_w': 2, 'w_in': 2, 'conv_w': 2, 'rg_lambda': 2, 'rg_ba': 2, 'rg_bi': 2, 's5_glu_w': 1, 'w_out': 1,
              'mlp_w1': 2, 'mlp_w2': 1}
SHARDED = ['conv_w', 'rg_lambda', 'rg_ba', 'rg_bi']
REPLICATED = [n for n in WEIGHTS if n not in SHARD_AXIS]
GATHER_BF16 = ['ada_w', 'w_in', 's5_glu_w', 'w_out', 'mlp_w1', 'mlp_w2']
GATHER_F32 = ['conv_w', 'rg_lambda', 'rg_ba', 'rg_bi']


def _params(n_axes, vmem_mb=40):
    return pltpu.CompilerParams(dimension_semantics=("arbitrary",) * n_axes, vmem_limit_bytes=vmem_mb * MIB)


def _tile(n, options):
    for t in options:
        if n % t == 0:
            return t
    return n


def _sigmoid(z):
    return 1.0 / (1.0 + jnp.exp(-z))


def _softplus(z):
    return jnp.maximum(z, 0.0) + jnp.log(1.0 + jnp.exp(-jnp.abs(z)))


def _neg_expm1_small(z):
    return -z * (1.0 + 0.5 * z * (1.0 + (1.0 / 3.0) * z * (1.0 + 0.25 * z)))


_G0 = math.sqrt(2.0 / math.pi)
_G1 = 0.044715


def _gelu(v):
    return 0.5 * v * (1.0 + jnp.tanh(_G0 * (v + _G1 * v * v * v)))


def _gelu_and_grad(v):
    t = jnp.tanh(_G0 * (v + _G1 * v * v * v))
    g = 0.5 * v * (1.0 + t)
    dg = 0.5 * (1.0 + t) + 0.5 * v * (1.0 - t * t) * _G0 * (1.0 + 3.0 * _G1 * v * v)
    return g, dg


def _seq_of_tile(i, tps):
    return 2 * (i // tps) + jnp.minimum(i % tps, 1)


def _dot(a, b, dims=(((1,), (0,)), ((), ()))):
    return lax.dot_general(a, b, dims, preferred_element_type=F32)


_NT = (((1,), (1,)), ((), ()))
_TN = (((0,), (0,)), ((), ()))


def _layer_spec(b, block, index):
    if isinstance(b, tuple):
        arr, layer = b
        return arr, pl.BlockSpec((None,) + block, lambda *g: (layer,) + index(*g))
    return b, pl.BlockSpec(block, index)


def _mm_nn(a, b, bias=None, *, relu2=False, name):
    m, k = a.shape
    n = (b[0] if isinstance(b, tuple) else b).shape[-1]
    tm, tn, tk = _tile(m, (512, 256)), _tile(n, (1024,)), _tile(k, (1024,))
    nk = k // tk
    has_bias = bias is not None

    def body(*refs):
        a_ref, b_ref = refs[0], refs[1]
        bias_ref = refs[2] if has_bias else None
        outs = refs[2 + has_bias:-1]
        acc = refs[-1]
        kk = pl.program_id(2)

        @pl.when(kk == 0)
        def _():
            acc[...] = jnp.zeros_like(acc)

        acc[...] += _dot(a_ref[...], b_ref[...])

        @pl.when(kk == nk - 1)
        def _():
            h = acc[...]
            if has_bias:
                h = h + bias_ref[...]
            if relu2:
                r = jnp.maximum(h, 0.0)
                outs[0][...] = (r * r).astype(BF16)
                outs[1][...] = h.astype(BF16)
            else:
                outs[0][...] = h

    b_arr, b_spec = _layer_spec(b, (tk, tn), lambda j, i, kk: (kk, j))
    in_specs = [pl.BlockSpec((tm, tk), lambda j, i, kk: (i, kk)), b_spec]
    args = [a, b_arr]
    if has_bias:
        in_specs.append(pl.BlockSpec((1, tn), lambda j, i, kk: (0, j)))
        args.append(bias)
    o_spec = pl.BlockSpec((tm, tn), lambda j, i, kk: (i, j))
    if relu2:
        out_shape = (jax.ShapeDtypeStruct((m, n), BF16), jax.ShapeDtypeStruct((m, n), BF16))
        out_specs = (o_spec, o_spec)
    else:
        out_shape, out_specs = jax.ShapeDtypeStruct((m, n), F32), o_spec
    return pl.pallas_call(body, name=name, grid=(n // tn, m // tm, nk), in_specs=in_specs, out_specs=out_specs,
                          out_shape=out_shape, scratch_shapes=[pltpu.VMEM((tm, tn), F32)],
                          compiler_params=_params(3))(*args)


def _mm_nt(a, b, hb=None, *, name):
    m, n = a.shape
    k = (b[0] if isinstance(b, tuple) else b).shape[-2]
    tm, tn, tk = _tile(m, (512, 256)), _tile(k, (1024,)), _tile(n, (1024,))
    nk = n // tk
    fused = hb is not None

    def body(*refs):
        a_ref, b_ref = refs[0], refs[1]
        hb_ref = refs[2] if fused else None
        o_ref, acc = refs[-2], refs[-1]
        kk = pl.program_id(2)

        @pl.when(kk == 0)
        def _():
            acc[...] = jnp.zeros_like(acc)

        acc[...] += _dot(a_ref[...], b_ref[...], _NT)

        @pl.when(kk == nk - 1)
        def _():
            if fused:
                o_ref[...] = (acc[...] * (2.0 * jnp.maximum(hb_ref[...].astype(F32), 0.0))).astype(BF16)
            else:
                o_ref[...] = acc[...]

    b_arr, b_spec = _layer_spec(b, (tn, tk), lambda j, i, kk: (j, kk))
    in_specs = [pl.BlockSpec((tm, tk), lambda j, i, kk: (i, kk)), b_spec]
    args = [a, b_arr]
    if fused:
        in_specs.append(pl.BlockSpec((tm, tn), lambda j, i, kk: (i, j)))
        args.append(hb)
    return pl.pallas_call(body, name=name, grid=(k // tn, m // tm, nk), in_specs=in_specs,
                          out_specs=pl.BlockSpec((tm, tn), lambda j, i, kk: (i, j)),
                          out_shape=jax.ShapeDtypeStruct((m, k), BF16 if fused else F32),
                          scratch_shapes=[pltpu.VMEM((tm, tn), F32)], compiler_params=_params(3))(*args)


def _mm_tn(a, b, *, name, layer=None, into=None):
    m, k = a.shape
    n = b.shape[1]
    tk, tn, tr = _tile(k, (1024,)), _tile(n, (1024,)), _tile(m, (512, 256))
    nr = m // tr

    def body(a_ref, b_ref, *rest):
        o_ref, acc = rest[-2], rest[-1]
        r = pl.program_id(2)

        @pl.when(r == 0)
        def _():
            acc[...] = jnp.zeros_like(acc)

        acc[...] += _dot(a_ref[...], b_ref[...], _TN)

        @pl.when(r == nr - 1)
        def _():
            o_ref[...] = acc[...]

    in_specs = [pl.BlockSpec((tr, tk), lambda i, j, r: (r, i)), pl.BlockSpec((tr, tn), lambda i, j, r: (r, j))]
    args, aliases = [a, b], {}
    if layer is None:
        out_spec, out_shape = pl.BlockSpec((tk, tn), lambda i, j, r: (i, j)), jax.ShapeDtypeStruct((k, n), F32)
    else:
        out_spec = pl.BlockSpec((None, tk, tn), lambda i, j, r: (layer, i, j))
        out_shape = jax.ShapeDtypeStruct((DEPTH, k, n), F32)
        if into is not None:
            in_specs.append(_ANY)
            args.append(into)
            aliases = {2: 0}
    return pl.pallas_call(body, name=name, grid=(k // tk, n // tn, nr), in_specs=in_specs, out_specs=out_spec,
                          out_shape=out_shape, input_output_aliases=aliases,
                          scratch_shapes=[pltpu.VMEM((tk, tn), F32)], compiler_params=_params(3))(*args)


def _colsum(v, *, name):
    m, n = v.shape
    tn, tr = _tile(n, (1024,)), _tile(m, (512, 256))

    def body(v_ref, o_ref):
        @pl.when(pl.program_id(1) == 0)
        def _():
            o_ref[...] = jnp.zeros_like(o_ref)

        o_ref[...] += jnp.sum(v_ref[...].astype(F32), axis=0, keepdims=True)

    return pl.pallas_call(body, name=name, grid=(n // tn, m // tr),
                          in_specs=[pl.BlockSpec((tr, tn), lambda j, r: (r, j))],
                          out_specs=pl.BlockSpec((1, tn), lambda j, r: (0, j)),
                          out_shape=jax.ShapeDtypeStruct((1, n), F32), compiler_params=_params(2))(v)


def _tok_spec(d=D_MODEL, col=0):
    return pl.BlockSpec((TM, d), lambda i: (i, col))


def _mod_spec(tps):
    return pl.BlockSpec((1, N_MOD, D_MODEL), lambda i: (_seq_of_tile(i, tps), 0, 0))


def _row_spec(d=D_MODEL):
    return pl.BlockSpec((1, d), lambda i: (0, 0))


def _seq_acc_spec(tps):
    return pl.BlockSpec((1, 1, D_MODEL), lambda i: (_seq_of_tile(i, tps), 0, 0))


def _modulate(xs, modall, k_shift, k_scale, tps, *, name):
    n = xs.shape[0]

    def body(x_ref, m_ref, o_ref):
        sh = m_ref[0, k_shift:k_shift + 1, :]
        sc = m_ref[0, k_scale:k_scale + 1, :]
        o_ref[...] = (x_ref[...] * (1.0 + sc) + sh).astype(BF16)

    return pl.pallas_call(body, name=name, grid=(n // TM,), in_specs=[_tok_spec(), _mod_spec(tps)],
                          out_specs=_tok_spec(), out_shape=jax.ShapeDtypeStruct((n, D_MODEL), BF16),
                          compiler_params=_params(1))(xs, modall)


def _resid_ln(xs, ms, modall, k_gate, g, b, tps, *, name):
    n = xs.shape[0]

    def body(x_ref, m_ref, mod_ref, g_ref, b_ref, o_ref):
        z = ALPHA * x_ref[...] + mod_ref[0, k_gate:k_gate + 1, :] * m_ref[...]
        mu = jnp.mean(z, axis=-1, keepdims=True)
        zc = z - mu
        var = jnp.mean(zc * zc, axis=-1, keepdims=True)
        o_ref[...] = zc * lax.rsqrt(var + LN_EPS) * g_ref[...] + b_ref[...]

    return pl.pallas_call(body, name=name, grid=(n // TM,),
                          in_specs=[_tok_spec(), _tok_spec(), _mod_spec(tps), _row_spec(), _row_spec()],
                          out_specs=_tok_spec(), out_shape=jax.ShapeDtypeStruct((n, D_MODEL), F32),
                          compiler_params=_params(1))(xs, ms, modall, g, b)


def _resid_ln_bwd(xs, ms, modall, k_gate, g, dout, tps, *, name):
    n = xs.shape[0]

    def body(x_ref, m_ref, mod_ref, g_ref, d_ref, dxa_ref, dm_ref, dbias_ref, dg_ref, db_ref, dgate_ref):
        i = pl.program_id(0)
        gate = mod_ref[0, k_gate:k_gate + 1, :]
        m = m_ref[...]
        z = ALPHA * x_ref[...] + gate * m
        mu = jnp.mean(z, axis=-1, keepdims=True)
        zc = z - mu
        var = jnp.mean(zc * zc, axis=-1, keepdims=True)
        rstd = lax.rsqrt(var + LN_EPS)
        xhat = zc * rstd
        d = d_ref[...]
        dxh = d * g_ref[...]
        dz = rstd * (dxh - jnp.mean(dxh, axis=-1, keepdims=True)
                     - xhat * jnp.mean(dxh * xhat, axis=-1, keepdims=True))
        dxa_ref[...] = ALPHA * dz
        dm = gate * dz
        dm_ref[...] = dm.astype(BF16)

        @pl.when(i == 0)
        def _():
            dbias_ref[...] = jnp.zeros_like(dbias_ref)
            dg_ref[...] = jnp.zeros_like(dg_ref)
            db_ref[...] = jnp.zeros_like(db_ref)

        dbias_ref[...] += jnp.sum(dm, axis=0, keepdims=True)
        dg_ref[...] += jnp.sum(d * xhat, axis=0, keepdims=True)
        db_ref[...] += jnp.sum(d, axis=0, keepdims=True)
        part = jnp.sum(dz * m, axis=0, keepdims=True)

        @pl.when(i % tps <= 1)
        def _():
            dgate_ref[0] = part

        @pl.when(i % tps > 1)
        def _():
            dgate_ref[0] += part

    row = jax.ShapeDtypeStruct((1, D_MODEL), F32)
    return pl.pallas_call(
        body, name=name, grid=(n // TM,),
        in_specs=[_tok_spec(), _tok_spec(), _mod_spec(tps), _row_spec(), _tok_spec()],
        out_specs=(_tok_spec(), _tok_spec(), _row_spec(), _row_spec(), _row_spec(), _seq_acc_spec(tps)),
        out_shape=(jax.ShapeDtypeStruct((n, D_MODEL), F32), jax.ShapeDtypeStruct((n, D_MODEL), BF16), row, row, row,
                   jax.ShapeDtypeStruct((n // TM // tps * 2, 1, D_MODEL), F32)),
        compiler_params=_params(1))(xs, ms, modall, g, dout)


def _modulate_bwd(du, xs, modall, k_scale, dxa, tps, *, name):
    n = xs.shape[0]

    def body(du_ref, x_ref, mod_ref, dxa_ref, dx_ref, dsc_ref, dsh_ref):
        i = pl.program_id(0)
        du_t = du_ref[...]
        dx_ref[...] = dxa_ref[...] + du_t * (1.0 + mod_ref[0, k_scale:k_scale + 1, :])
        psc = jnp.sum(du_t * x_ref[...], axis=0, keepdims=True)
        psh = jnp.sum(du_t, axis=0, keepdims=True)

        @pl.when(i % tps <= 1)
        def _():
            dsc_ref[0] = psc
            dsh_ref[0] = psh

        @pl.when(i % tps > 1)
        def _():
            dsc_ref[0] += psc
            dsh_ref[0] += psh

    acc = jax.ShapeDtypeStruct((n // TM // tps * 2, 1, D_MODEL), F32)
    return pl.pallas_call(body, name=name, grid=(n // TM,),
                          in_specs=[_tok_spec(), _tok_spec(), _mod_spec(tps), _tok_spec()],
                          out_specs=(_tok_spec(), _seq_acc_spec(tps), _seq_acc_spec(tps)),
                          out_shape=(jax.ShapeDtypeStruct((n, D_MODEL), F32), acc, acc),
                          compiler_params=_params(1))(du, xs, modall, dxa)


def _loss_head(ys, target, tps, *, name):
    n = ys.shape[0]
    lat_tiles = tps - 1

    def body(y_ref, t_ref, acc_ref, dy_ref):
        i = pl.program_id(0)

        @pl.when(i == 0)
        def _():
            acc_ref[...] = jnp.zeros_like(acc_ref)

        @pl.when(i % tps == 0)
        def _():
            dy_ref[...] = jnp.zeros_like(dy_ref)

        @pl.when(i % tps > 0)
        def _():
            e = y_ref[...] - t_ref[...]
            dy_ref[...] = e * (1.0 / D_MODEL)
            acc_ref[...] += jnp.sum(e * e, axis=0, keepdims=True)

    t_spec = pl.BlockSpec((TM, D_MODEL), lambda i: ((i // tps) * lat_tiles + jnp.maximum(i % tps - 1, 0), 0))
    return pl.pallas_call(body, name=name, grid=(n // TM,), in_specs=[_tok_spec(), t_spec],
                          out_specs=(_row_spec(), _tok_spec()),
                          out_shape=(jax.ShapeDtypeStruct((1, D_MODEL), F32), jax.ShapeDtypeStruct((n, D_MODEL), F32)),
                          compiler_params=_params(1))(ys, target)


def _glu_fwd(y_nat, p, d_skip, w, b, cat, *, name):
    n = y_nat.shape[0]

    def body(y_ref, u_ref, d_ref, w_ref, b_ref, cat_ref, o_ref):
        g = _gelu(y_ref[...] + d_ref[...] * u_ref[...])
        z = _dot(g.astype(BF16), w_ref[...]) + b_ref[...]
        o_ref[...] = (g * _sigmoid(z)).astype(BF16)

    w_arr, w_spec = _layer_spec(w, (D_MODEL, D_MODEL), lambda i: (0, 0))
    return pl.pallas_call(body, name=name, grid=(n // TM,),
                          in_specs=[_tok_spec(), _tok_spec(col=2), _row_spec(), w_spec, _row_spec(), _ANY],
                          out_specs=_tok_spec(col=1), out_shape=jax.ShapeDtypeStruct(cat.shape, BF16),
                          input_output_aliases={5: 0},
                          compiler_params=_params(1))(y_nat, p, d_skip, w_arr, b, cat)


def _glu_bwd(dcat, y_nat, p, d_skip, w, b, *, name):
    n = y_nat.shape[0]

    def body(ds_ref, y_ref, u_ref, d_ref, w_ref, b_ref, dy_ref, dsk_ref, g_ref, dz_ref, dd_ref, dbz_ref):
        u = u_ref[...]
        g, gg = _gelu_and_grad(y_ref[...] + d_ref[...] * u)
        s = _sigmoid(_dot(g.astype(BF16), w_ref[...]) + b_ref[...])
        ds = ds_ref[...]
        dz = ds * g * s * (1.0 - s)
        dzb = dz.astype(BF16)
        dg = ds * s + _dot(dzb, w_ref[...], _NT)
        dyp = dg * gg
        dy_ref[...] = dyp
        dsk_ref[...] = dyp * d_ref[...]
        g_ref[...] = g.astype(BF16)
        dz_ref[...] = dzb

        @pl.when(pl.program_id(0) == 0)
        def _():
            dd_ref[...] = jnp.zeros_like(dd_ref)
            dbz_ref[...] = jnp.zeros_like(dbz_ref)

        dd_ref[...] += jnp.sum(dyp * u, axis=0, keepdims=True)
        dbz_ref[...] += jnp.sum(dz, axis=0, keepdims=True)

    tok_bf = jax.ShapeDtypeStruct((n, D_MODEL), BF16)
    tok_f32 = jax.ShapeDtypeStruct((n, D_MODEL), F32)
    row = jax.ShapeDtypeStruct((1, D_MODEL), F32)
    w_arr, w_spec = _layer_spec(w, (D_MODEL, D_MODEL), lambda i: (0, 0))
    return pl.pallas_call(
        body, name=name, grid=(n // TM,),
        in_specs=[_tok_spec(col=1), _tok_spec(), _tok_spec(col=2), _row_spec(), w_spec, _row_spec()],
        out_specs=(_tok_spec(), _tok_spec(), _tok_spec(), _tok_spec(), _row_spec(), _row_spec()),
        out_shape=(tok_f32, tok_f32, tok_bf, tok_bf, row, row),
        compiler_params=_params(1))(dcat, y_nat, p, d_skip, w_arr, b)


def _add_cast(a, b, *, name):
    n = a.shape[0]

    def body(a_ref, b_ref, o_ref):
        o_ref[...] = (a_ref[...] + b_ref[...]).astype(BF16)

    return pl.pallas_call(body, name=name, grid=(n // TM,), in_specs=[_tok_spec(), _tok_spec()],
                          out_specs=_tok_spec(), out_shape=jax.ShapeDtypeStruct((n, D_MODEL), BF16),
                          compiler_params=_params(1))(a, b)


def _silu_rows(c16, *, name):
    def body(c_ref, s_ref, ds_ref):
        v = c_ref[...]
        sg = _sigmoid(v)
        s_ref[...] = v * sg
        ds_ref[...] = sg * (1.0 + v * (1.0 - sg))

    shp = jax.ShapeDtypeStruct(c16.shape, F32)
    return pl.pallas_call(body, name=name, out_shape=(shp, shp))(c16)


def _mul_rows(a, b, *, name):
    def body(a_ref, b_ref, o_ref):
        o_ref[...] = a_ref[...] * b_ref[...]

    return pl.pallas_call(body, name=name, out_shape=jax.ShapeDtypeStruct(a.shape, F32))(a, b)


def _pad_off(c):
    return pl.multiple_of(c * TM + 8 + 8 * jnp.minimum(c, 1), 8)


def _rows8(k):
    return pl.ds(pl.multiple_of(k * 8, 8), 8)


def _windows(buf, c, shifts):
    n = TM + 16
    win = buf[pl.ds(pl.multiple_of(_pad_off(c) - 8, 8), n), :]
    return [win[8:8 + TM] if k == 0 else pltpu.roll(win, (-k) % n, 0)[8:8 + TM] for k in shifts]


def _conv_window(xpad, c):
    return _windows(xpad, c, (-1, 0, 1, 2))


def _rg_coeffs(z, d, spl, xc):
    r = 0.5 + 0.5 * jnp.tanh(0.5 * z[:, 256 * d:256 * d + HEAD])
    i = 0.5 + 0.5 * jnp.tanh(0.5 * z[:, 256 * d + HEAD:256 * d + 2 * HEAD])
    la = -RG_C * spl[d:d + 1, :] * r
    a = jnp.exp(la)
    one_minus_a = jnp.where(jnp.abs(la) < 1e-2, _neg_expm1_small(la), 1.0 - a)
    mult = jnp.sqrt(one_minus_a * (1.0 + a))
    return r, i, a, mult, a * a


def _chunk_scan(a, b, reverse):
    row = lax.broadcasted_iota(jnp.int32, (TM, HEAD), 0)
    sft = 1
    while sft < TM:
        keep = (row < TM - sft) if reverse else (row >= sft)
        amt = TM - sft if reverse else sft
        a_prev = jnp.where(keep, pltpu.roll(a, amt, 0), 1.0)
        b_prev = jnp.where(keep, pltpu.roll(b, amt, 0), 0.0)
        b = a * b_prev + b
        a = a * a_prev
        sft *= 2
    return a, b


def _zero_pads(buf, s, lc):
    z8 = jnp.zeros((8, HEAD), F32)
    buf[0:8, :] = z8
    buf[8 + lc:16 + lc, :] = z8
    buf[16 + s:24 + s, :] = z8


def _rg_fwd(p, conv_w, conv_b, lam, wcat, bcat, bl, s, lc, *, name):
    nch = s // TM

    def body(x_ref, gate_ref, cw_ref, cb_ref, lam_ref, w_ref, b_ref, rg_ref, hf_ref, hb_ref, xpad, af, bf, ab, bb):
        _zero_pads(xpad, s, lc)

        def copy_chunk(c, _):
            xpad[pl.ds(_pad_off(c), TM), :] = x_ref[pl.ds(pl.multiple_of(c * TM, TM), TM), :]
            return 0

        lax.fori_loop(0, nch, copy_chunk, 0)
        spl = _softplus(-lam_ref[...])
        cw = cw_ref[...]

        def coef_chunk(c, _):
            xm1, x0, xp1, xp2 = _conv_window(xpad, c)
            xc = cw[0:1] * xm1 + cw[1:2] * x0 + cw[2:3] * xp1 + cw[3:4] * xp2 + cb_ref[...]
            z = _dot(xc.astype(BF16), w_ref[0]) + b_ref[0]
            rows = pl.ds(pl.multiple_of(c * TM, TM), TM)
            for d, (a_s, b_s) in enumerate(((af, bf), (ab, bb))):
                _, i, a, mult, _ = _rg_coeffs(z, d, spl, xc)
                a_s[rows, :] = a
                b_s[rows, :] = mult * i * xc
            return 0

        lax.fori_loop(0, nch, coef_chunk, 0)

        def scan_pair(j, carry):
            cf, cb_ = carry
            rf = pl.ds(pl.multiple_of(j * TM, TM), TM)
            rb = pl.ds(pl.multiple_of(jnp.where(j == 0, 0, nch - j) * TM, TM), TM)
            a1, h1 = _chunk_scan(af[rf, :], bf[rf, :], False)
            h1 = h1 + a1 * cf
            hf_ref[rf, :] = h1
            a2, h2 = _chunk_scan(ab[rb, :], bb[rb, :], True)
            h2 = h2 + a2 * cb_
            hb_ref[rb, :] = h2
            return h1[TM - 1:TM], h2[0:1]

        zero = jnp.zeros((1, HEAD), F32)
        lax.fori_loop(0, nch, scan_pair, (zero, zero))

        def out_chunk(c, _):
            rows = pl.ds(pl.multiple_of(c * TM, TM), TM)
            rg_ref[rows, :] = ((hf_ref[rows, :] + hb_ref[rows, :]) * _gelu(gate_ref[rows, :])).astype(BF16)
            return 0

        lax.fori_loop(0, nch, out_chunk, 0)

    seq = lambda col0: pl.BlockSpec((s, HEAD), lambda b, h: (b, col0 + h))
    par = lambda r: pl.BlockSpec((r, HEAD), lambda b, h: (0, h))
    n = bl * s
    return pl.pallas_call(
        body, name=name, grid=(bl, RG_HEADS),
        in_specs=[seq(0), seq(RG_HEADS), par(4), par(1), par(2),
                  pl.BlockSpec((1, HEAD, 4 * HEAD), lambda b, h: (h, 0, 0)),
                  pl.BlockSpec((1, 1, 4 * HEAD), lambda b, h: (h, 0, 0))],
        out_specs=(seq(0), seq(0), seq(0)),
        out_shape=(jax.ShapeDtypeStruct((n, 2 * D_MODEL), BF16), jax.ShapeDtypeStruct((n, D_MODEL), F32),
                   jax.ShapeDtypeStruct((n, D_MODEL), F32)),
        scratch_shapes=[pltpu.VMEM((s + 24, HEAD), F32)] + [pltpu.VMEM((s, HEAD), F32)] * 4,
        compiler_params=_params(2, 48))(p, p, conv_w, conv_b, lam, wcat, bcat)


def _rg_bwd(p, dcat, hf, hb, conv_w, conv_b, lam, wcat, bcat, bl, s, lc, *, name):
    nch = s // TM
    ll = s - lc

    def body(p_hbm, dcat_hbm, hf_hbm, hb_hbm, gate_s, cw_ref, cb_ref, lam_ref, w_ref, b_ref,
             drgx_ref, dgate_ref, dcw_ref, dcb_ref, dlam_ref, dw_ref, db_ref,
             xpad2, dxpad, hf2, hb2, dhs2, a_f, a_b, lam_f, lam_b, sems):
        h = pl.program_id(0)
        b = pl.program_id(1)
        step = h * bl + b
        slot = step % 2

        def copies(hh, bb, sl):
            col = pl.multiple_of(hh * HEAD, HEAD)

            def rows_of(ref, r0, nr):
                return ref.at[pl.ds(bb * s + r0, nr), pl.ds(col, HEAD)]

            return [
                pltpu.make_async_copy(rows_of(p_hbm, 0, lc), xpad2.at[sl, pl.ds(8, lc), :], sems.at[sl, 0]),
                pltpu.make_async_copy(rows_of(p_hbm, lc, ll), xpad2.at[sl, pl.ds(16 + lc, ll), :], sems.at[sl, 1]),
                pltpu.make_async_copy(rows_of(dcat_hbm, 0, s), dhs2.at[sl], sems.at[sl, 2]),
                pltpu.make_async_copy(rows_of(hf_hbm, 0, s), hf2.at[sl, pl.ds(8, s), :], sems.at[sl, 3]),
                pltpu.make_async_copy(rows_of(hb_hbm, 0, s), hb2.at[sl, pl.ds(8, s), :], sems.at[sl, 4]),
            ]

        @pl.when(step == 0)
        def _():
            for cp in copies(h, b, slot):
                cp.start()

        @pl.when(step + 1 < RG_HEADS * bl)
        def _():
            for cp in copies((step + 1) // bl, (step + 1) % bl, 1 - slot):
                cp.start()

        xpad, hf_s, hb_s, dhs = xpad2.at[slot], hf2.at[slot], hb2.at[slot], dhs2.at[slot]
        _zero_pads(xpad, s, lc)
        _zero_pads(dxpad, s, lc)
        for buf in (hf_s, hb_s):
            buf[0:8, :] = jnp.zeros((8, HEAD), F32)
            buf[8 + s:16 + s, :] = jnp.zeros((8, HEAD), F32)

        @pl.when(b == 0)
        def _():
            dcw_ref[...] = jnp.zeros_like(dcw_ref)
            dcb_ref[...] = jnp.zeros_like(dcb_ref)
            dlam_ref[...] = jnp.zeros_like(dlam_ref)
            dw_ref[...] = jnp.zeros_like(dw_ref)
            db_ref[...] = jnp.zeros_like(db_ref)

        for cp in copies(h, b, slot):
            cp.wait()
        lam_v = lam_ref[...]
        spl = _softplus(-lam_v)
        cw = cw_ref[...]

        def conv(c):
            xm1, x0, xp1, xp2 = _conv_window(xpad, c)
            return cw[0:1] * xm1 + cw[1:2] * x0 + cw[2:3] * xp1 + cw[3:4] * xp2 + cb_ref[...]

        def pass_a(c, _):
            rows = pl.ds(pl.multiple_of(c * TM, TM), TM)
            xc = conv(c)
            z = _dot(xc.astype(BF16), w_ref[0]) + b_ref[0]
            for d, a_s in enumerate((a_f, a_b)):
                a_s[rows, :] = _rg_coeffs(z, d, spl, xc)[2]
            g, gg = _gelu_and_grad(gate_s[rows, :])
            drg = dhs[rows, :]
            hrows = pl.ds(pl.multiple_of(c * TM + 8, 8), TM)
            dgate_ref[rows, :] = (drg * (hf_s[hrows, :] + hb_s[hrows, :]) * gg).astype(BF16)
            dhs[rows, :] = drg * g
            return 0

        lax.fori_loop(0, nch, pass_a, 0)

        row = lax.broadcasted_iota(jnp.int32, (TM, HEAD), 0)

        def adj_pair(j, carry):
            cf, cb_ = carry
            rf = pl.ds(pl.multiple_of((nch - 1 - j) * TM, TM), TM)
            rb = pl.ds(pl.multiple_of(jnp.where(j == nch - 1, 0, j + 1) * TM, TM), TM)
            d1, a1 = dhs[rf, :], a_f[rf, :]
            p1, m1 = _chunk_scan(a1, a1 * d1, True)
            m1 = m1 + p1 * cf
            lam_f[rf, :] = d1 + jnp.where(row == TM - 1, cf, pltpu.roll(m1, TM - 1, 0))
            d2, a2 = dhs[rb, :], a_b[rb, :]
            p2, m2 = _chunk_scan(a2, a2 * d2, False)
            m2 = m2 + p2 * cb_
            lam_b[rb, :] = d2 + jnp.where(row == 0, cb_, pltpu.roll(m2, 1, 0))
            return m1[0:1], m2[TM - 1:TM]

        zero = jnp.zeros((1, HEAD), F32)
        lax.fori_loop(0, nch, adj_pair, (zero, zero))

        sig_neg = _sigmoid(-lam_v)
        last_row = lax.broadcasted_iota(jnp.int32, (TM, HEAD), 0) == TM - 1
        hb_first = hb_s[8:9, :]

        def pass_b(c, _):
            rows = pl.ds(pl.multiple_of(c * TM, TM), TM)
            xc = conv(c)
            xcb = xc.astype(BF16)
            z = _dot(xcb, w_ref[0]) + b_ref[0]
            dxc = jnp.zeros((TM, HEAD), F32)
            dzs = []
            n = TM + 16
            hp_f = pltpu.roll(hf_s[pl.ds(pl.multiple_of(c * TM, TM), n), :], 1, 0)[8:8 + TM]
            hp_b = pltpu.roll(hb_s[pl.ds(pl.multiple_of(c * TM, TM), n), :], n - 1, 0)[8:8 + TM]
            hp_b = jnp.where(last_row & (c == 0), 0.0, hp_b)
            hp_b = jnp.where(last_row & (c == nch - 1), hb_first, hp_b)
            for d, (l_s, hp) in enumerate(((lam_f, hp_f), (lam_b, hp_b))):
                r, i, a, mult, e2 = _rg_coeffs(z, d, spl, xc)
                dbt = l_s[rows, :]
                dla = dbt * hp * a - dbt * i * xc * (e2 / mult)
                dlam_ref[0, d:d + 1, :] += jnp.sum(dla * r, axis=0, keepdims=True) * (RG_C * sig_neg[d:d + 1, :])
                dr = dla * (-RG_C * spl[d:d + 1, :])
                di = dbt * mult * xc
                dxc = dxc + dbt * mult * i
                dzs += [dr * r * (1.0 - r), di * i * (1.0 - i)]
            dz = jnp.concatenate(dzs, axis=1)
            dzb = dz.astype(BF16)
            dxc = dxc + _dot(dzb, w_ref[0], _NT)
            dw_ref[0] += _dot(xcb, dzb, _TN)
            db_ref[0] += jnp.sum(dz, axis=0, keepdims=True)
            dcb_ref[0] += jnp.sum(dxc, axis=0, keepdims=True)
            dxpad[pl.ds(_pad_off(c), TM), :] = dxc
            return 0

        lax.fori_loop(0, nch, pass_b, 0)

        def pass_c(c, _):
            rows = pl.ds(pl.multiple_of(c * TM, TM), TM)
            gp1, g0, gm1, gm2 = _windows(dxpad, c, (1, 0, -1, -2))
            drgx_ref[rows, :] = (cw[0:1] * gp1 + cw[1:2] * g0 + cw[2:3] * gm1 + cw[3:4] * gm2).astype(BF16)
            xm1, x0, xp1, xp2 = _conv_window(xpad, c)
            dcw_ref[0] += jnp.concatenate([jnp.sum(g0 * t, axis=0, keepdims=True) for t in (xm1, x0, xp1, xp2)],
                                          axis=0)
            return 0

        lax.fori_loop(0, nch, pass_c, 0)

    seq = pl.BlockSpec((s, HEAD), lambda h, b: (b, h))
    par = lambda r: pl.BlockSpec((r, HEAD), lambda h, b: (0, h))
    acc = lambda r, w: pl.BlockSpec((1, r, w), lambda h, b: (h, 0, 0))
    anyspec = pl.BlockSpec(memory_space=pl.ANY)
    n = bl * s
    big = pltpu.VMEM((s, HEAD), F32)
    return pl.pallas_call(
        body, name=name, grid=(RG_HEADS, bl),
        in_specs=[anyspec, anyspec, anyspec, anyspec, pl.BlockSpec((s, HEAD), lambda h, b: (b, RG_HEADS + h)),
                  par(4), par(1), par(2), acc(HEAD, 4 * HEAD), acc(1, 4 * HEAD)],
        out_specs=(seq, seq, acc(4, HEAD), acc(1, HEAD), acc(2, HEAD), acc(HEAD, 4 * HEAD), acc(1, 4 * HEAD)),
        out_shape=(jax.ShapeDtypeStruct((n, D_MODEL), BF16), jax.ShapeDtypeStruct((n, D_MODEL), BF16),
                   jax.ShapeDtypeStruct((RG_HEADS, 4, HEAD), F32), jax.ShapeDtypeStruct((RG_HEADS, 1, HEAD), F32),
                   jax.ShapeDtypeStruct((RG_HEADS, 2, HEAD), F32),
                   jax.ShapeDtypeStruct((RG_HEADS, HEAD, 4 * HEAD), F32),
                   jax.ShapeDtypeStruct((RG_HEADS, 1, 4 * HEAD), F32)),
        scratch_shapes=[pltpu.VMEM((2, s + 24, HEAD), F32), pltpu.VMEM((s + 24, HEAD), F32),
                        pltpu.VMEM((2, s + 16, HEAD), F32), pltpu.VMEM((2, s + 16, HEAD), F32),
                        pltpu.VMEM((2, s, HEAD), F32)] + [big] * 4 + [pltpu.SemaphoreType.DMA((2, 5))],
        compiler_params=_params(2, 56))(p, dcat, hf, hb, p, conv_w, conv_b, lam, wcat, bcat)


def _s5_mats(a_re, a_im, log_dt, b_re, b_im, c_re, c_im):
    t = T_CH
    g = a_re.shape[1]
    dt = jnp.exp(log_dt)[..., None]
    lr, li = a_re * dt, a_im * dt
    steps = jnp.arange(t + 1, dtype=F32)[:, None]
    mag = jnp.exp(lr[:, :, None, :] * steps)
    ang = li[:, :, None, :] * steps
    pr, pi = mag * jnp.cos(ang), mag * jnp.sin(ang)
    xr, xi = pr[:, :, 1] - 1.0, pi[:, :, 1]
    den = a_re * a_re + a_im * a_im
    qr, qi = (xr * a_re + xi * a_im) / den, (xi * a_re - xr * a_im) / den
    btr, bti = b_re.transpose(0, 1, 3, 2), b_im.transpose(0, 1, 3, 2)
    bbr = qr[:, :, None, :] * btr - qi[:, :, None, :] * bti
    bbi = qr[:, :, None, :] * bti + qi[:, :, None, :] * btr
    up, down = slice(0, t), slice(t - 1, None, -1)

    def pow_c(d, sl):
        wr, wi = pr[d][:, sl, None, :], pi[d][:, sl, None, :]
        cr, ci = c_re[d][:, None], c_im[d][:, None]
        return (wr * cr - wi * ci).reshape(g, CW, S5_STATE), (wr * ci + wi * cr).reshape(g, CW, S5_STATE)

    hp = lax.Precision.HIGHEST

    def lag_map(d, sl):
        re, im = pow_c(d, sl)
        return (jnp.einsum('gkp,gmp->gkm', bbr[d], re, precision=hp)
                - jnp.einsum('gkp,gmp->gkm', bbi[d], im, precision=hp))

    z_f, z_b = lag_map(0, up), lag_map(1, down)
    kf = jnp.stack([jnp.pad(z_f, ((0, 0), (0, 0), (S5_GROUP * s, 0)))[:, :, :CW] for s in range(t)], axis=1)
    kb = jnp.stack([jnp.pad(z_b, ((0, 0), (0, 0), (0, S5_GROUP * (t - 1 - s))))[:, :, S5_GROUP * (t - 1 - s):]
                    for s in range(t)], axis=1)
    kcat = (kf + kb).reshape(g, CW, CW)

    def state_in(d, sl):
        wr, wi = pr[d][:, sl, None, :], pi[d][:, sl, None, :]
        br, bi = bbr[d][:, None], bbi[d][:, None]
        return jnp.concatenate([wr * br - wi * bi, wr * bi + wi * br], axis=-1).reshape(g, CW, SW)

    wcat = jnp.concatenate([kcat, state_in(0, down), state_in(1, up)], axis=2)
    of_r, of_i = pow_c(0, slice(1, t + 1))
    ob_r, ob_i = pow_c(1, slice(t, 0, -1))
    mout_t = jnp.concatenate([of_r, -of_i, ob_r, -ob_i], axis=2)
    rows = []
    for d in range(2):
        art, ait = pr[d][:, t], pi[d][:, t]
        rows += [jnp.concatenate([art, art], axis=1).reshape(-1), jnp.concatenate([-ait, ait], axis=1).reshape(-1)]
    return wcat, mout_t, jnp.stack(rows)


def _lane_swap(v):
    return pltpu.roll(v, S5_STATE, 1)


def _grp(g, w):
    return slice(g * w, (g + 1) * w)


def _s5_fwd(u, wcat, mout, a2, ncc, *, name):
    bl, nc, _ = u.shape

    def body(u_ref, w_ref, mo_ref, a_ref, y_ref, sf_ref, sb_ref, vf, vb):
        for g in range(GB):
            zu = _dot(u_ref[:, _grp(g, CW)].astype(BF16), w_ref[g])
            y_ref[:, _grp(g, CW)] = zu[:, :CW]
            vf[:, _grp(2 * g, SW)] = zu[:, CW:CW + SW]
            vb[:, _grp(2 * g, SW)] = zu[:, CW + SW:]
            vf[:, _grp(2 * g + 1, SW)] = _lane_swap(zu[:, CW:CW + SW])
            vb[:, _grp(2 * g + 1, SW)] = _lane_swap(zu[:, CW + SW:])
        co = [[a_ref[r:r + 1, _grp(g, SW)] for g in range(GB)] for r in range(4)]

        rid = lax.broadcasted_iota(jnp.int32, (8, SW), 0)

        def step8(groups, kf, kb, carry):
            rf, rb = _rows8(kf), _rows8(kb)
            lanes = slice(groups[0] * 2 * SW, (groups[-1] + 1) * 2 * SW)
            vfb, vbb = vf[rf, lanes], vb[rb, lanes]
            st = list(carry)
            of = [jnp.zeros((8, SW), F32)] * len(groups)
            ob = list(of)
            for i in range(8):
                k = 7 - i
                for n, g in enumerate(groups):
                    sf, sfs, sb, sbs = st[4 * n:4 * n + 4]
                    of[n] = jnp.where(rid == i, sf, of[n])
                    ob[n] = jnp.where(rid == k, sb, ob[n])
                    st[4 * n] = co[0][g] * sf + co[1][g] * sfs + vfb[i:i + 1, _grp(2 * n, SW)]
                    st[4 * n + 1] = co[0][g] * sfs - co[1][g] * sf + vfb[i:i + 1, _grp(2 * n + 1, SW)]
                    st[4 * n + 2] = co[2][g] * sb + co[3][g] * sbs + vbb[k:k + 1, _grp(2 * n, SW)]
                    st[4 * n + 3] = co[2][g] * sbs - co[3][g] * sb + vbb[k:k + 1, _grp(2 * n + 1, SW)]
            for n, g in enumerate(groups):
                sf_ref[rf, _grp(g, SW)] = of[n]
                sb_ref[rb, _grp(g, SW)] = ob[n]
            return tuple(st)

        zero = jnp.zeros((1, SW), F32)
        nbc, nb = ncc // 8, nc // 8
        for groups in (tuple(range(0, GB // 2)), tuple(range(GB // 2, GB))):
            carry = lax.fori_loop(0, nbc, lambda j, cr, gs=groups: step8(gs, j, nbc - 1 - j, cr),
                                  (zero,) * (4 * len(groups)))
            lax.fori_loop(nbc, nb, lambda j, cr, gs=groups: step8(gs, j, nb + nbc - 1 - j, cr), carry)
        for g in range(GB):
            st = jnp.concatenate([sf_ref[:, _grp(g, SW)], sb_ref[:, _grp(g, SW)]], axis=1).astype(BF16)
            y_ref[:, _grp(g, CW)] += _dot(st, mo_ref[g], _NT)

    blk = lambda w: pl.BlockSpec((None, nc, GB * w), lambda b, gb: (b, 0, gb))
    return pl.pallas_call(
        body, name=name, grid=(bl, S5_GROUPS // GB),
        in_specs=[blk(CW), pl.BlockSpec((GB, CW, 2 * CW), lambda b, gb: (gb, 0, 0)),
                  pl.BlockSpec((GB, CW, CW), lambda b, gb: (gb, 0, 0)),
                  pl.BlockSpec((4, GB * SW), lambda b, gb: (0, gb))],
        out_specs=(blk(CW), blk(SW), blk(SW)),
        out_shape=(jax.ShapeDtypeStruct(u.shape, F32), jax.ShapeDtypeStruct((bl, nc, S5_GROUPS * SW), F32),
                   jax.ShapeDtypeStruct((bl, nc, S5_GROUPS * SW), F32)),
        scratch_shapes=[pltpu.VMEM((nc, GB * 2 * SW), F32)] * 2, compiler_params=_params(2))(u, wcat, mout, a2)


def _s5_bwd(dy, u, sf, sb, wcat, mout, a2, ncc, *, name):
    bl, nc, _ = u.shape

    def body(dy_ref, u_ref, sf_ref, sb_ref, w_ref, mo_ref, a_ref, du_ref, dw_ref, dmo_ref, dacc_ref, gsf, gsb, dvf, dvb):
        b = pl.program_id(1)

        @pl.when(b == 0)
        def _():
            dw_ref[...] = jnp.zeros_like(dw_ref)
            dmo_ref[...] = jnp.zeros_like(dmo_ref)
            dacc_ref[...] = jnp.zeros_like(dacc_ref)

        for g in range(GB):
            ds = _dot(dy_ref[:, _grp(g, CW)].astype(BF16), mo_ref[g])
            gsf[:, _grp(2 * g, SW)] = ds[:, :SW]
            gsb[:, _grp(2 * g, SW)] = ds[:, SW:]
            gsf[:, _grp(2 * g + 1, SW)] = _lane_swap(ds[:, :SW])
            gsb[:, _grp(2 * g + 1, SW)] = _lane_swap(ds[:, SW:])
        co = [[a_ref[r:r + 1, _grp(g, SW)] for g in range(GB)] for r in range(4)]

        rid = lax.broadcasted_iota(jnp.int32, (8, SW), 0)

        def step8(groups, kf, kb, carry):
            rf, rb = _rows8(kf), _rows8(kb)
            lanes = slice(groups[0] * 2 * SW, (groups[-1] + 1) * 2 * SW)
            gfb, gbb = gsf[rf, lanes], gsb[rb, lanes]
            st = list(carry)
            of = [jnp.zeros((8, SW), F32)] * len(groups)
            ob = list(of)
            for i in range(8):
                k = 7 - i
                for n, g in enumerate(groups):
                    gf, gfs, gb_, gbs = st[4 * n:4 * n + 4]
                    of[n] = jnp.where(rid == k, gf, of[n])
                    ob[n] = jnp.where(rid == i, gb_, ob[n])
                    st[4 * n] = gfb[k:k + 1, _grp(2 * n, SW)] + co[0][g] * gf - co[1][g] * gfs
                    st[4 * n + 1] = gfb[k:k + 1, _grp(2 * n + 1, SW)] + co[0][g] * gfs + co[1][g] * gf
                    st[4 * n + 2] = gbb[i:i + 1, _grp(2 * n, SW)] + co[2][g] * gb_ - co[3][g] * gbs
                    st[4 * n + 3] = gbb[i:i + 1, _grp(2 * n + 1, SW)] + co[2][g] * gbs + co[3][g] * gb_
            for n, g in enumerate(groups):
                dvf[rf, _grp(g, SW)] = of[n]
                dvb[rb, _grp(g, SW)] = ob[n]
            return tuple(st)

        zero = jnp.zeros((1, SW), F32)
        nbc, nb = ncc // 8, nc // 8
        for groups in (tuple(range(0, GB // 2)), tuple(range(GB // 2, GB))):
            carry = lax.fori_loop(0, nb - nbc, lambda j, cr, gs=groups: step8(gs, nb - 1 - j, nbc + j, cr),
                                  (zero,) * (4 * len(groups)))
            lax.fori_loop(0, nbc, lambda j, cr, gs=groups: step8(gs, nbc - 1 - j, j, cr), carry)
        for g in range(GB):
            dyg = dy_ref[:, _grp(g, CW)].astype(BF16)
            dvf_g, dvb_g = dvf[:, _grp(g, SW)], dvb[:, _grp(g, SW)]
            sf_g, sb_g = sf_ref[:, _grp(g, SW)], sb_ref[:, _grp(g, SW)]
            dz = jnp.concatenate([dyg, dvf_g.astype(BF16), dvb_g.astype(BF16)], axis=1)
            du_ref[:, _grp(g, CW)] = _dot(dz, w_ref[g], _NT)
            dw_ref[g] += _dot(u_ref[:, _grp(g, CW)].astype(BF16), dz, _TN)
            st = jnp.concatenate([sf_g, sb_g], axis=1).astype(BF16)
            dmo_ref[g] += _dot(dyg, st, _TN)
            dacc_ref[:, _grp(g, SW)] += jnp.concatenate(
                [jnp.sum(dvf_g * sf_g, axis=0, keepdims=True), jnp.sum(dvf_g * _lane_swap(sf_g), axis=0, keepdims=True),
                 jnp.sum(dvb_g * sb_g, axis=0, keepdims=True), jnp.sum(dvb_g * _lane_swap(sb_g), axis=0, keepdims=True)],
                axis=0)

    blk = lambda w: pl.BlockSpec((None, nc, GB * w), lambda gb, b: (b, 0, gb))
    wspec = lambda mult: pl.BlockSpec((GB, CW, mult * CW), lambda gb, b: (gb, 0, 0))
    aspec = pl.BlockSpec((4, GB * SW), lambda gb, b: (0, gb))
    return pl.pallas_call(
        body, name=name, grid=(S5_GROUPS // GB, bl),
        in_specs=[blk(CW), blk(CW), blk(SW), blk(SW), wspec(2), wspec(1), aspec],
        out_specs=(blk(CW), wspec(2), wspec(1), aspec),
        out_shape=(jax.ShapeDtypeStruct(u.shape, F32), jax.ShapeDtypeStruct((S5_GROUPS, CW, 2 * CW), F32),
                   jax.ShapeDtypeStruct((S5_GROUPS, CW, CW), F32), jax.ShapeDtypeStruct(a2.shape, F32)),
        scratch_shapes=[pltpu.VMEM((nc, GB * 2 * SW), F32)] * 2 + [pltpu.VMEM((nc, GB * SW), F32)] * 2,
        compiler_params=_params(2, 48))(dy, u, sf, sb, wcat, mout, a2)


def _lane_slot():
    return lax.broadcasted_iota(jnp.int32, (GRID_W, HEAD), 1) // S5_GROUP


def _slots_to_chunk(tiles, slot, q, j):
    acc = jnp.zeros(tiles[0].shape, F32)
    for m in range(HEAD // S5_GROUP):
        shift = ((m - q) * S5_GROUP) % HEAD
        v = tiles[8 * j + m]
        acc = jnp.where(slot == m, v if shift == 0 else pltpu.roll(v, shift, 1), acc)
    return acc


def _slots_to_rows(tiles, slot, m):
    acc = jnp.zeros(tiles[0].shape, F32)
    for q in range(HEAD // S5_GROUP):
        shift = ((q - m) * S5_GROUP) % HEAD
        acc = jnp.where(slot == q, tiles[q] if shift == 0 else pltpu.roll(tiles[q], shift, 1), acc)
    return acc


def _to_chunks(src, col0, bl, s, lc, *, name):
    ncc = lc // T_CH
    nrh = (s - lc) // GRID_W // T_CH
    nc = ncc + GRID_W * nrh

    def body(x_ref, o_ref, tmp, dense):
        slot_c = lax.broadcasted_iota(jnp.int32, (ncc, HEAD), 1) // S5_GROUP
        slot = _lane_slot()
        for j in range(CW // HEAD):
            for q in range(HEAD // S5_GROUP):
                tiles = [x_ref[pl.ds(8 * j + m, ncc, stride=T_CH), :] for m in range(8)]
                tmp[0:ncc, :] = _slots_to_chunk(tiles, slot_c, q, 0)

                def one(rh, _):
                    tiles = [x_ref[pl.ds(pl.multiple_of(lc + (rh * T_CH + 8 * j + m) * GRID_W, GRID_W), GRID_W), :]
                             for m in range(8)]
                    dense[pl.ds(pl.multiple_of(rh * GRID_W, GRID_W), GRID_W), :] = _slots_to_chunk(tiles, slot, q, 0)
                    return 0

                lax.fori_loop(0, nrh, one, 0)
                for w in range(GRID_W):
                    tmp[ncc + nrh * w:ncc + nrh * (w + 1), :] = dense[pl.ds(w, nrh, stride=GRID_W), :]
                o_ref[:, q * CW + j * HEAD:q * CW + (j + 1) * HEAD] = tmp[...]

    return pl.pallas_call(
        body, name=name, grid=(bl, S5_GROUPS // GB),
        in_specs=[pl.BlockSpec((s, HEAD), lambda b, gb: (b, col0 + gb))],
        out_specs=pl.BlockSpec((None, nc, GB * CW), lambda b, gb: (b, 0, gb)),
        out_shape=jax.ShapeDtypeStruct((bl, nc, S5_GROUPS * CW), F32),
        scratch_shapes=[pltpu.VMEM((nc, HEAD), F32), pltpu.VMEM((nrh * GRID_W, HEAD), F32)],
        compiler_params=_params(2))(src)


def _from_chunks(v, bl, s, lc, *, name):
    ncc = lc // T_CH
    nrh = (s - lc) // GRID_W // T_CH
    nc = v.shape[1]

    def body(v_ref, o_ref, *tmp):
        slot_c = lax.broadcasted_iota(jnp.int32, (ncc, HEAD), 1) // S5_GROUP
        slot = _lane_slot()
        for j in range(CW // HEAD):
            for q in range(HEAD // S5_GROUP):
                tmp[q][...] = v_ref[:, q * CW + j * HEAD:q * CW + (j + 1) * HEAD]
            for m in range(8):
                tiles = [tmp[q][0:ncc, :] for q in range(HEAD // S5_GROUP)]
                o_ref[pl.ds(8 * j + m, ncc, stride=T_CH), :] = _slots_to_rows(tiles, slot_c, m)

            def one(rh, _):
                for m in range(8):
                    tiles = [tmp[q][pl.ds(ncc + rh, GRID_W, stride=nrh), :] for q in range(HEAD // S5_GROUP)]
                    rows = pl.ds(pl.multiple_of(lc + (rh * T_CH + 8 * j + m) * GRID_W, GRID_W), GRID_W)
                    o_ref[rows, :] = _slots_to_rows(tiles, slot, m)
                return 0

            lax.fori_loop(0, nrh, one, 0)

    return pl.pallas_call(
        body, name=name, grid=(bl, S5_GROUPS // GB),
        in_specs=[pl.BlockSpec((None, nc, GB * CW), lambda b, gb: (b, 0, gb))],
        out_specs=pl.BlockSpec((s, HEAD), lambda b, gb: (b, gb)),
        out_shape=jax.ShapeDtypeStruct((bl * s, D_MODEL), F32),
        scratch_shapes=[pltpu.VMEM((nc, HEAD), F32)] * (HEAD // S5_GROUP), compiler_params=_params(2))(v)


_ANY = pl.BlockSpec(memory_space=pl.ANY)


def _xy_peers():
    x, y, c = lax.axis_index("x"), lax.axis_index("y"), lax.axis_index("c")
    return x, y, c, [(1 - x, y), (x, 1 - y), (1 - x, 1 - y)]


def _all_gather_xy(shard, *, name):
    def body(x_ref, out_ref, send_sems, recv_sems, local_sem):
        x, y, c, peers = _xy_peers()
        me = 2 * x + y
        mine = pltpu.make_async_copy(x_ref, out_ref.at[me], local_sem)
        mine.start()

        def copy(k, px, py, slot):
            return pltpu.make_async_remote_copy(src_ref=x_ref, dst_ref=out_ref.at[slot], send_sem=send_sems.at[k],
                                                recv_sem=recv_sems.at[k], device_id=(px, py, c), device_id_type=MESH)

        sends = [copy(k, px, py, me) for k, (px, py) in enumerate(peers)]
        for cp in sends:
            cp.start()
        for k, (px, py) in enumerate(peers):
            copy(k, px, py, 2 * px + py).wait_recv()
        for cp in sends:
            cp.wait_send()
        mine.wait()

    return pl.pallas_call(body, name=name, in_specs=[_ANY], out_specs=_ANY,
                          out_shape=jax.ShapeDtypeStruct((4,) + shard.shape, shard.dtype),
                          scratch_shapes=[pltpu.SemaphoreType.DMA((3,)), pltpu.SemaphoreType.DMA((3,)),
                                          pltpu.SemaphoreType.DMA])(shard)


def _scatter_xy(parts, *, name):
    def body(p_ref, out_ref, send_sems, recv_sems, local_sem):
        x, y, c, peers = _xy_peers()
        mine = pltpu.make_async_copy(p_ref.at[2 * x + y], out_ref.at[0], local_sem)
        mine.start()

        def copy(k, px, py):
            return pltpu.make_async_remote_copy(src_ref=p_ref.at[2 * px + py], dst_ref=out_ref.at[1 + k],
                                                send_sem=send_sems.at[k], recv_sem=recv_sems.at[k],
                                                device_id=(px, py, c), device_id_type=MESH)

        sends = [copy(k, px, py) for k, (px, py) in enumerate(peers)]
        for cp in sends:
            cp.start()
        for cp in sends:
            cp.wait_recv()
        for cp in sends:
            cp.wait_send()
        mine.wait()

    return pl.pallas_call(body, name=name, in_specs=[_ANY], out_specs=_ANY,
                          out_shape=jax.ShapeDtypeStruct(parts.shape, parts.dtype),
                          scratch_shapes=[pltpu.SemaphoreType.DMA((3,)), pltpu.SemaphoreType.DMA((3,)),
                                          pltpu.SemaphoreType.DMA])(parts)


def _swap_sibling(v, *, name):
    def body(v_ref, out_ref, send_sem, recv_sem):
        x, y, c = lax.axis_index("x"), lax.axis_index("y"), lax.axis_index("c")
        cp = pltpu.make_async_remote_copy(src_ref=v_ref, dst_ref=out_ref, send_sem=send_sem, recv_sem=recv_sem,
                                          device_id=(x, y, 1 - c), device_id_type=MESH)
        cp.start()
        cp.wait()

    return pl.pallas_call(body, name=name, in_specs=[_ANY], out_specs=_ANY,
                          out_shape=jax.ShapeDtypeStruct(v.shape, v.dtype),
                          scratch_shapes=[pltpu.SemaphoreType.DMA, pltpu.SemaphoreType.DMA])(v)


BIG_COLS = {'ada_w': True, 'w_in': True, 'mlp_w1': True, 's5_glu_w': False, 'w_out': False, 'mlp_w2': False}
BIG = list(BIG_COLS)


def _block(ref2d, j, cols, size):
    if cols:
        return ref2d.at[:, pl.ds(pl.multiple_of(j * size, 128), size)]
    return ref2d.at[pl.ds(pl.multiple_of(j * size, 8), size), :]


def _shard_size(shape, cols):
    return shape[-1] if cols else shape[-2]


def _cast_into_full(shard, cols, my_j, *, name):
    _, r, c = shard.shape
    tr, tc = _tile(r, (256,)), _tile(c, (1024, 768, 512))

    def body(j_ref, x_ref, o_ref):
        o_ref[...] = x_ref[...].astype(BF16)

    if cols:
        out_spec = pl.BlockSpec((None, tr, tc), lambda l, i, j, j_ref: (l, i, j_ref[0] * (c // tc) + j))
    else:
        out_spec = pl.BlockSpec((None, tr, tc), lambda l, i, j, j_ref: (l, j_ref[0] * (r // tr) + i, j))
    return pl.pallas_call(
        body, name=name,
        grid_spec=pltpu.PrefetchScalarGridSpec(
            num_scalar_prefetch=1, grid=(DEPTH, r // tr, c // tc),
            in_specs=[pl.BlockSpec((None, tr, tc), lambda l, i, j, j_ref: (l, i, j))], out_specs=out_spec),
        out_shape=jax.ShapeDtypeStruct((DEPTH, r, 4 * c) if cols else (DEPTH, 4 * r, c), BF16),
        compiler_params=_params(3))(my_j, shard)


def _gather_big(fulls, cols, *, name):
    n = len(fulls)

    def body(*refs):
        outs = refs[n:2 * n]
        ici_send, ici_recv, d2d_send, d2d_recv = refs[2 * n:]
        x, y, c, peers = _xy_peers()
        me = 2 * x + y

        def blk(w, layer, j):
            shape = outs[w].shape
            return _block(outs[w].at[layer], j, cols[w], (shape[2] if cols[w] else shape[1]) // 4)

        def ici(w, k, px, py, j):
            return pltpu.make_async_remote_copy(src_ref=blk(w, c, j), dst_ref=blk(w, c, j),
                                                send_sem=ici_send.at[3 * w + k], recv_sem=ici_recv.at[3 * w + k],
                                                device_id=(px, py, c), device_id_type=MESH)

        def d2d(w, k, j, layer):
            return pltpu.make_async_remote_copy(src_ref=blk(w, layer, j), dst_ref=blk(w, layer, j),
                                                send_sem=d2d_send.at[3 * w + k], recv_sem=d2d_recv.at[3 * w + k],
                                                device_id=(x, y, 1 - c), device_id_type=MESH)

        started = [ici(w, k, px, py, me) for w in range(n) for k, (px, py) in enumerate(peers)]
        for cp in started:
            cp.start()
        passed = []
        for w in range(n):
            for k, (px, py) in enumerate(peers):
                ici(w, k, px, py, 2 * px + py).wait_recv()
                passed.append(d2d(w, k, 2 * px + py, c))
                passed[-1].start()
        for w in range(n):
            for k, (px, py) in enumerate(peers):
                d2d(w, k, 2 * px + py, 1 - c).wait_recv()
        for cp in started + passed:
            cp.wait_send()

    return pl.pallas_call(
        body, name=name, in_specs=[_ANY] * n, out_specs=[_ANY] * n,
        out_shape=[jax.ShapeDtypeStruct(f.shape, f.dtype) for f in fulls],
        input_output_aliases={w: w for w in range(n)},
        scratch_shapes=[pltpu.SemaphoreType.DMA((3 * n,))] * 4)(*fulls)


def _sibling_partials(gbufs, *, name):
    n = len(gbufs)

    def body(*refs):
        ins, outs, send, recv = refs[:n], refs[n:2 * n], refs[2 * n], refs[2 * n + 1]
        x, y, c = lax.axis_index("x"), lax.axis_index("y"), lax.axis_index("c")
        cps = [pltpu.make_async_remote_copy(src_ref=ins[w].at[1 - c], dst_ref=outs[w], send_sem=send.at[w],
                                            recv_sem=recv.at[w], device_id=(x, y, 1 - c), device_id_type=MESH)
               for w in range(n)]
        for cp in cps:
            cp.start()
        for cp in cps:
            cp.wait()

    return pl.pallas_call(body, name=name, in_specs=[_ANY] * n, out_specs=[_ANY] * n,
                          out_shape=[jax.ShapeDtypeStruct(g.shape[1:], g.dtype) for g in gbufs],
                          scratch_shapes=[pltpu.SemaphoreType.DMA((n,))] * 2)(*gbufs)


def _chip_sum(gbuf, other, my_c, *, name):
    _, k, n = gbuf.shape
    tr, tc = _tile(k, (512,)), _tile(n, (1024,))

    def body(c_ref, a_ref, b_ref, o_ref):
        o_ref[...] = (a_ref[...] + b_ref[...]).astype(BF16)

    spec = pl.BlockSpec((tr, tc), lambda i, j, c_ref: (i, j))
    return pl.pallas_call(
        body, name=name,
        grid_spec=pltpu.PrefetchScalarGridSpec(
            num_scalar_prefetch=1, grid=(k // tr, n // tc),
            in_specs=[pl.BlockSpec((None, tr, tc), lambda i, j, c_ref: (c_ref[0], i, j)), spec], out_specs=spec),
        out_shape=jax.ShapeDtypeStruct((k, n), BF16), compiler_params=_params(2))(my_c, gbuf, other)


def _scatter_big(sums, cols, *, name):
    n = len(sums)

    def shard(s, cf):
        return (s.shape[0], s.shape[1] // 4) if cf else (s.shape[0] // 4, s.shape[1])

    def body(*refs):
        ins, outs, send, recv = refs[:n], refs[n:2 * n], refs[2 * n], refs[2 * n + 1]
        x, y, c, peers = _xy_peers()
        cps = []
        for w in range(n):
            size = _shard_size(shard(ins[w], cols[w]), cols[w])
            for k, (px, py) in enumerate(peers):
                cps.append(pltpu.make_async_remote_copy(
                    src_ref=_block(ins[w], 2 * px + py, cols[w], size), dst_ref=outs[w].at[k],
                    send_sem=send.at[3 * w + k], recv_sem=recv.at[3 * w + k], device_id=(px, py, c),
                    device_id_type=MESH))
        for cp in cps:
            cp.start()
        for cp in cps:
            cp.wait()

    return pl.pallas_call(body, name=name, in_specs=[_ANY] * n, out_specs=[_ANY] * n,
                          out_shape=[jax.ShapeDtypeStruct((3,) + shard(s, cf), s.dtype) for s, cf in zip(sums, cols)],
                          scratch_shapes=[pltpu.SemaphoreType.DMA((3 * n,))] * 2)(*sums)


def _block_sum(own, got, cols, my_j, my_c, *, name):
    _, r, c = got.shape
    tr, tc = _tile(r, (256,)), _tile(c, (1024, 768, 512))

    def body(j_ref, c_ref, a_ref, g_ref, o_ref):
        o_ref[...] = ((a_ref[...].astype(F32) + g_ref[0].astype(F32)) + g_ref[1].astype(F32)) + g_ref[2].astype(F32)

    if cols:
        own_spec = pl.BlockSpec((tr, tc), lambda i, j, j_ref, c_ref: (i, j_ref[0] * (c // tc) + j))
    else:
        own_spec = pl.BlockSpec((tr, tc), lambda i, j, j_ref, c_ref: (j_ref[0] * (r // tr) + i, j))
    return pl.pallas_call(
        body, name=name,
        grid_spec=pltpu.PrefetchScalarGridSpec(
            num_scalar_prefetch=2, grid=(r // tr, c // tc),
            in_specs=[own_spec, pl.BlockSpec((3, tr, tc), lambda i, j, j_ref, c_ref: (0, i, j))],
            out_specs=pl.BlockSpec((None, tr, tc), lambda i, j, j_ref, c_ref: (c_ref[0], i, j))),
        out_shape=jax.ShapeDtypeStruct((DEPTH, r, c), F32), compiler_params=_params(2))(my_j, my_c, own, got)


def _share_final(bufs, *, name):
    n = len(bufs)

    def body(*refs):
        outs, send, recv = refs[n:2 * n], refs[2 * n], refs[2 * n + 1]
        x, y, c = lax.axis_index("x"), lax.axis_index("y"), lax.axis_index("c")

        def copy(w, slot):
            return pltpu.make_async_remote_copy(src_ref=outs[w].at[slot], dst_ref=outs[w].at[slot],
                                                send_sem=send.at[w], recv_sem=recv.at[w],
                                                device_id=(x, y, 1 - c), device_id_type=MESH)

        away = [copy(w, c) for w in range(n)]
        for cp in away:
            cp.start()
        for w in range(n):
            copy(w, 1 - c).wait_recv()
        for cp in away:
            cp.wait_send()

    return pl.pallas_call(body, name=name, in_specs=[_ANY] * n, out_specs=[_ANY] * n,
                          out_shape=[jax.ShapeDtypeStruct(b.shape, b.dtype) for b in bufs],
                          input_output_aliases={w: w for w in range(n)},
                          scratch_shapes=[pltpu.SemaphoreType.DMA((n,))] * 2)(*bufs)


def _adamw_native(w, g, m, v, *, name):
    r, c = w.shape
    tr = _tile(r, (256, 128, 64, 32, 16, 8))
    spec = pl.BlockSpec((tr, c), lambda i: (i, 0))
    c1 = 1.0 / (1.0 - ADAM_B1 ** ADAM_STEP)
    c2 = 1.0 / (1.0 - ADAM_B2 ** ADAM_STEP)

    def body(w_ref, g_ref, m_ref, v_ref, d_ref, nm_ref, nv_ref):
        g_t = g_ref[...]
        nm = ADAM_B1 * m_ref[...] + (1.0 - ADAM_B1) * g_t
        nv = ADAM_B2 * v_ref[...] + (1.0 - ADAM_B2) * (g_t * g_t)
        nm_ref[...] = nm
        nv_ref[...] = nv
        d_ref[...] = -ADAM_LR * ((nm * c1) / (jnp.sqrt(nv * c2) + ADAM_EPS) + ADAM_WD * w_ref[...])

    shp = jax.ShapeDtypeStruct((r, c), F32)
    return pl.pallas_call(body, name=name, grid=(r // tr,), in_specs=[spec] * 4, out_specs=(spec,) * 3,
                          out_shape=(shp,) * 3, compiler_params=_params(1))(w, g, m, v)


def _flat_tile(r):
    return _tile(r, (512, 256, 128, 64, 32, 16, 8))


def _sum4(parts, *, name):
    r = parts.shape[1]
    tr = _flat_tile(r)

    def body(p_ref, o_ref):
        o_ref[...] = ((p_ref[0] + p_ref[1]) + p_ref[2]) + p_ref[3]

    return pl.pallas_call(body, name=name, grid=(r // tr,),
                          in_specs=[pl.BlockSpec((4, tr, LANES), lambda i: (0, i, 0))],
                          out_specs=pl.BlockSpec((tr, LANES), lambda i: (i, 0)),
                          out_shape=jax.ShapeDtypeStruct((r, LANES), F32), compiler_params=_params(1))(parts)


def _add2(a, b, *, name):
    r = a.shape[0]
    tr = _flat_tile(r)
    spec = pl.BlockSpec((tr, LANES), lambda i: (i, 0))

    def body(a_ref, b_ref, o_ref):
        o_ref[...] = a_ref[...] + b_ref[...]

    return pl.pallas_call(body, name=name, grid=(r // tr,), in_specs=[spec, spec], out_specs=spec,
                          out_shape=jax.ShapeDtypeStruct((r, LANES), F32), compiler_params=_params(1))(a, b)


def _adamw(w, ga, gb, m, v, *, name):
    r = w.shape[0]
    tr = _flat_tile(r)
    spec = pl.BlockSpec((tr, LANES), lambda i: (i, 0))
    two = gb is not None
    c1 = 1.0 / (1.0 - ADAM_B1 ** ADAM_STEP)
    c2 = 1.0 / (1.0 - ADAM_B2 ** ADAM_STEP)

    def body(*refs):
        w_ref, ga_ref = refs[0], refs[1]
        m_ref, v_ref, g_ref, d_ref, nm_ref, nv_ref = refs[2 + two:]
        g = ga_ref[...] + refs[2][...] if two else ga_ref[...]
        nm = ADAM_B1 * m_ref[...] + (1.0 - ADAM_B1) * g
        nv = ADAM_B2 * v_ref[...] + (1.0 - ADAM_B2) * (g * g)
        g_ref[...] = g
        nm_ref[...] = nm
        nv_ref[...] = nv
        d_ref[...] = -ADAM_LR * ((nm * c1) / (jnp.sqrt(nv * c2) + ADAM_EPS) + ADAM_WD * w_ref[...])

    args = [w, ga] + ([gb] if two else []) + [m, v]
    shp = jax.ShapeDtypeStruct((r, LANES), F32)
    return pl.pallas_call(body, name=name, grid=(r // tr,), in_specs=[spec] * len(args), out_specs=(spec,) * 4,
                          out_shape=(shp,) * 4, compiler_params=_params(1))(*args)


def _pack(arrs, dtype=F32):
    rows = []
    for a in arrs:
        flat = a.astype(dtype).reshape(-1)
        rows.append(jnp.pad(flat, (0, (-flat.shape[0]) % LANES)).reshape(-1, LANES))
    buf = jnp.concatenate(rows)
    return jnp.pad(buf, ((0, (-buf.shape[0]) % 32), (0, 0)))


def _unpack(buf, shapes):
    out, row = [], 0
    for shp in shapes:
        sz = math.prod(shp)
        nr = -(-sz // LANES)
        out.append(buf[row:row + nr].reshape(-1)[:sz].reshape(shp))
        row += nr
    return out


def _stack_shards(full, axis):
    shp = full.shape
    return jnp.moveaxis(full.reshape(shp[:axis] + (4, shp[axis] // 4) + shp[axis + 1:]), axis, 0)


def _unstack_shards(st, axis):
    v = jnp.moveaxis(st, 0, axis)
    shp = v.shape
    return v.reshape(shp[:axis] + (shp[axis] * shp[axis + 1],) + shp[axis + 2:])


def _layer_weights(w, l):
    lw = {n: (w[n], l) if n in BIG_COLS else w[n][l] for n in w}
    lw['wcat'] = jnp.concatenate([lw['rg_wa'][0], lw['rg_wi'][0], lw['rg_wa'][1], lw['rg_wi'][1]],
                                 axis=-1).astype(BF16)
    ba, bi = lw['rg_ba'].reshape(2, RG_HEADS, HEAD), lw['rg_bi'].reshape(2, RG_HEADS, HEAD)
    lw['bcat'] = jnp.concatenate([ba[0], bi[0], ba[1], bi[1]], axis=-1)[:, None, :]
    s5_names = ['s5_a_re', 's5_a_im', 's5_log_dt', 's5_b_re', 's5_b_im', 's5_c_re', 's5_c_im']
    (wcat, mout, a2), lw['s5_vjp'] = jax.vjp(_s5_mats, *[lw[n] for n in s5_names])
    lw['s5_wcat'], lw['s5_mout'], lw['s5_a2'] = wcat.astype(BF16), mout.astype(BF16), a2
    for n in ('conv_b', 's5_d', 's5_glu_b', 'b_out', 'mlp_b1', 'mlp_b2', 'ln1_g', 'ln1_b', 'ln2_g', 'ln2_b'):
        lw[n] = lw[n][None, :]
    return lw


def _layer_fwd(l, x0, modall, lw, dims):
    bl, s, lc, tps = dims
    ll = s - lc
    tag = f"l{l}_"
    sv = {'x0': x0}
    sv['u1'] = _modulate(x0, modall, 0, 1, tps, name=tag + "mod1")
    sv['p'] = p = _mm_nn(sv['u1'], lw['w_in'], name=tag + "w_in")
    rg, sv['hf'], sv['hb'] = _rg_fwd(p, lw['conv_w'], lw['conv_b'], lw['rg_lambda'], lw['wcat'], lw['bcat'],
                                     bl, s, lc, name=tag + "rg_fwd")
    sv['u_ch'] = _to_chunks(p, 2 * D_MODEL // HEAD, bl, s, lc, name=tag + "u_chunks")
    y_ch, sv['sf'], sv['sb'] = _s5_fwd(sv['u_ch'], lw['s5_wcat'], lw['s5_mout'], lw['s5_a2'], lc // T_CH,
                                       name=tag + "s5_fwd")
    sv['y'] = _from_chunks(y_ch, bl, s, lc, name=tag + "y_rows")
    sv['cat'] = _glu_fwd(sv['y'], p, lw['s5_d'], lw['s5_glu_w'], lw['s5_glu_b'], rg, name=tag + "glu_fwd")
    sv['m'] = _mm_nn(sv['cat'], lw['w_out'], lw['b_out'], name=tag + "w_out")
    sv['x1'] = _resid_ln(x0, sv['m'], modall, 2, lw['ln1_g'], lw['ln1_b'], tps, name=tag + "ln1")
    sv['u2'] = _modulate(sv['x1'], modall, 3, 4, tps, name=tag + "mod2")
    sv['a'], sv['h'] = _mm_nn(sv['u2'], lw['mlp_w1'], lw['mlp_b1'], relu2=True, name=tag + "mlp1")
    sv['f'] = _mm_nn(sv['a'], lw['mlp_w2'], lw['mlp_b2'], name=tag + "mlp2")
    x2 = _resid_ln(sv['x1'], sv['f'], modall, 5, lw['ln2_g'], lw['ln2_b'], tps, name=tag + "ln2")
    return x2, sv


def _layer_bwd(l, dx2, modall, lw, sv, dims, gbufs):
    bl, s, lc, tps = dims
    ll = s - lc
    tag = f"l{l}_"
    g = {}

    def big_grad(n, a_mat, b_mat, label):
        gbufs[n] = _mm_tn(a_mat, b_mat, name=tag + label, layer=l, into=gbufs.get(n))
    dx1a, df, db2, g['ln2_g'], g['ln2_b'], dg2 = _resid_ln_bwd(sv['x1'], sv['f'], modall, 5, lw['ln2_g'], dx2, tps,
                                                              name=tag + "ln2_bwd")
    g['mlp_b2'] = db2
    big_grad('mlp_w2', sv['a'], df, "mlp2_dw")
    dh = _mm_nt(df, lw['mlp_w2'], sv['h'], name=tag + "mlp2_dx")
    g['mlp_b1'] = _colsum(dh, name=tag + "mlp1_db")
    big_grad('mlp_w1', sv['u2'], dh, "mlp1_dw")
    du2 = _mm_nt(dh, lw['mlp_w1'], name=tag + "mlp1_dx")
    dx1, dsc2, dsh2 = _modulate_bwd(du2, sv['x1'], modall, 4, dx1a, tps, name=tag + "mod2_bwd")
    dx0a, dm, g['b_out'], g['ln1_g'], g['ln1_b'], dg1 = _resid_ln_bwd(sv['x0'], sv['m'], modall, 2, lw['ln1_g'], dx1,
                                                                     tps, name=tag + "ln1_bwd")
    big_grad('w_out', sv['cat'], dm, "w_out_dw")
    dcat = _mm_nt(dm, lw['w_out'], name=tag + "w_out_dx")
    dy, dskip, g_bf, dz_bf, g['s5_d'], g['s5_glu_b'] = _glu_bwd(dcat, sv['y'], sv['p'], lw['s5_d'], lw['s5_glu_w'],
                                                                lw['s5_glu_b'], name=tag + "glu_bwd")
    big_grad('s5_glu_w', g_bf, dz_bf, "glu_dw")
    dy_ch = _to_chunks(dy, 0, bl, s, lc, name=tag + "dy_chunks")
    du_ch, dwcat, dmout, dacc = _s5_bwd(dy_ch, sv['u_ch'], sv['sf'], sv['sb'], lw['s5_wcat'],
                                        lw['s5_mout'], lw['s5_a2'], lc // T_CH, name=tag + "s5_bwd")
    s5g = lw['s5_vjp']((dwcat, dmout, dacc))
    for n, v in zip(['s5_a_re', 's5_a_im', 's5_log_dt', 's5_b_re', 's5_b_im', 's5_c_re', 's5_c_im'], s5g):
        g[n] = v
    ds5u = _add_cast(_from_chunks(du_ch, bl, s, lc, name=tag + "du_rows"), dskip, name=tag + "ds5u")
    drgx, dgate, dcw, dcb, dlam, dwc, dbc = _rg_bwd(sv['p'], dcat, sv['hf'], sv['hb'], lw['conv_w'], lw['conv_b'],
                                                    lw['rg_lambda'], lw['wcat'], lw['bcat'], bl, s, lc,
                                                    name=tag + "rg_bwd")
    g['conv_w'] = dcw.transpose(1, 0, 2).reshape(4, D_MODEL)
    g['conv_b'] = dcb.reshape(D_MODEL)
    g['rg_lambda'] = dlam.transpose(1, 0, 2).reshape(2, D_MODEL)
    g['rg_wa'] = jnp.stack([dwc[:, :, 0:HEAD], dwc[:, :, 2 * HEAD:3 * HEAD]])
    g['rg_wi'] = jnp.stack([dwc[:, :, HEAD:2 * HEAD], dwc[:, :, 3 * HEAD:]])
    dbc = dbc.reshape(RG_HEADS, 4, HEAD)
    g['rg_ba'] = jnp.stack([dbc[:, 0], dbc[:, 2]]).reshape(2, D_MODEL)
    g['rg_bi'] = jnp.stack([dbc[:, 1], dbc[:, 3]]).reshape(2, D_MODEL)
    dp = jnp.concatenate([drgx, dgate, ds5u], axis=1)
    big_grad('w_in', sv['u1'], dp, "w_in_dw")
    du1 = _mm_nt(dp, lw['w_in'], name=tag + "w_in_dx")
    dx0, dsc1, dsh1 = _modulate_bwd(du1, sv['x0'], modall, 1, dx0a, tps, name=tag + "mod1_bwd")
    dmod = jnp.concatenate([dsh1, dsc1, dg1, dsh2, dsc2, dg2], axis=1)
    return dx0, g, dmod


def _kernel_impl(*args):
    nin = len(IN_NAMES)
    a = dict(zip(IN_NAMES, args[:nin]))
    target = args[nin]
    nw = len(WEIGHTS)
    mom = dict(zip(WEIGHTS, args[nin + 1:nin + 1 + nw]))
    var = dict(zip(WEIGHTS, args[nin + 1 + nw:nin + 1 + 2 * nw]))
    bl, ll, d = a['x'].shape
    lc = a['ctx'].shape[1]
    assert d == D_MODEL and lc == TM and bl == 2 and ll % (GRID_W * T_CH) == 0
    s = lc + ll
    tps = s // TM
    dims = (bl, s, lc, tps)

    def gather(names, dtype, tag):
        shards = [a[n] for n in names]
        got = _all_gather_xy(_pack(shards, dtype), name="gather_" + tag)
        per = [_unpack(got[j], [w.shape for w in shards]) for j in range(4)]
        return {n: _unstack_shards(jnp.stack([per[j][i] for j in range(4)]), SHARD_AXIS[n])
                for i, n in enumerate(names)}

    big_cols = [BIG_COLS[n] for n in BIG]
    my_c = lax.axis_index("c").astype(jnp.int32).reshape(1)
    my_j = (2 * lax.axis_index("x") + lax.axis_index("y")).astype(jnp.int32).reshape(1)
    mine = [_cast_into_full(a[n], BIG_COLS[n], my_j, name=f"cast_{n}") for n in BIG]
    w = dict(zip(BIG, _gather_big(mine, big_cols, name="gather_big")))
    w.update(gather(GATHER_F32, F32, "f32"))
    for n in REPLICATED:
        w[n] = a[n]

    xs = jnp.concatenate([a['ctx'], a['x']], axis=1).reshape(bl * s, D_MODEL)
    c16 = jnp.zeros((16, D_MODEL), F32).at[0:2].set(a['c']).at[2].set(a['c_ctx'])
    s16, ds16 = _silu_rows(c16, name="silu")
    s16b = s16.astype(BF16)
    layers, saved, mods = [], [], []
    for l in range(DEPTH):
        lw = _layer_weights({n: w[n] for n in WEIGHTS if n not in ('c_ctx',)}, l)
        mod16 = _mm_nn(s16b, lw['ada_w'], lw['ada_b'][None, :], name=f"l{l}_ada").reshape(16, N_MOD, D_MODEL)
        modall = jnp.stack([mod16[2], mod16[0], mod16[2], mod16[1]])
        xs, sv = _layer_fwd(l, xs, modall, lw, dims)
        layers.append(lw)
        saved.append(sv)
        mods.append(modall)
    lossrow, dx = _loss_head(xs, target.reshape(bl * ll, D_MODEL), tps, name="loss_head")
    loss = lax.psum(0.5 / D_MODEL * jnp.sum(lossrow), ("x", "y", "c"))

    small = [n for n in WEIGHTS if n != 'c_ctx' and n not in BIG_COLS]
    grads = {n: [None] * DEPTH for n in small}
    gbufs = {}
    ds_rows = jnp.zeros((16, D_MODEL), F32)
    for l in reversed(range(DEPTH)):
        dx, g, dmod = _layer_bwd(l, dx, mods[l], layers[l], saved[l], dims, gbufs)
        dmod16 = jnp.zeros((16, N_MOD * D_MODEL), F32).at[0].set(dmod[1].reshape(-1)).at[1].set(
            dmod[3].reshape(-1)).at[2].set((dmod[0] + dmod[2]).reshape(-1))
        dmod16b = dmod16.astype(BF16)
        gbufs['ada_w'] = _mm_tn(s16b, dmod16b, name=f"l{l}_ada_dw", layer=l, into=gbufs.get('ada_w'))
        g['ada_b'] = _colsum(dmod16, name=f"l{l}_ada_db")
        ds_rows = ds_rows + _mm_nt(dmod16b, layers[l]['ada_w'], name=f"l{l}_ada_dx")
        for n, v in g.items():
            grads[n][l] = v.reshape(a[n].shape[1:] if n in REPLICATED else w[n].shape[1:])
    full = {n: jnp.stack(v) for n, v in grads.items()}
    full['c_ctx'] = _mul_rows(ds_rows, ds16, name="silu_bwd")[2]
    grad_x = dx.reshape(bl, s, D_MODEL)[:, lc:]

    from_sib = _sibling_partials([gbufs[n] for n in BIG], name="grad_big_sibling")
    sums = [_chip_sum(gbufs[n], o, my_c, name=f"grad_chip_sum_{n}") for n, o in zip(BIG, from_sib)]
    got = _scatter_big(sums, big_cols, name="grad_big_scatter")
    finals = [_block_sum(sm, gt, cf, my_j, my_c, name=f"grad_block_sum_{n}")
              for n, sm, gt, cf in zip(BIG, sums, got, big_cols)]
    res_big = {}
    for n, gfull in zip(BIG, _share_final(finals, name="grad_big_share")):
        flat = lambda t: t.reshape(-1, t.shape[-1])
        d_w, n_m, n_v = _adamw_native(flat(a[n]), flat(gfull), flat(mom[n]), flat(var[n]), name=f"adamw_{n}")
        res_big[n] = [gfull] + [t.reshape(a[n].shape) for t in (d_w, n_m, n_v)]

    rep_flat = _pack([full[n] for n in REPLICATED])
    rr = rep_flat.shape[0]
    sh_stacked = [_stack_shards(full[n], SHARD_AXIS[n] + 0).reshape(4, -1) for n in SHARDED]
    parts = jnp.concatenate(sh_stacked + [rep_flat.reshape(4, -1)], axis=1)
    pad = (-parts.shape[1]) % (32 * LANES)
    parts = jnp.pad(parts, ((0, 0), (0, pad))).reshape(4, -1, LANES)
    mine = _sum4(_scatter_xy(parts, name="grad_scatter"), name="grad_sum4")
    other = _swap_sibling(mine, name="grad_swap")
    n_sh = sum(math.prod(a[n].shape) for n in SHARDED)
    r_sh = n_sh // LANES
    assert n_sh % LANES == 0
    rq = rr // 4

    sh_shapes = [a[n].shape for n in SHARDED]
    pk = lambda dct: _pack([dct[n] for n in SHARDED])
    r_pk = pk(a).shape[0]
    take = lambda buf: jnp.pad(buf[:r_sh], ((0, r_pk - r_sh), (0, 0)))
    outs_sh = _adamw(pk(a), take(mine), take(other), pk(mom), pk(var), name="adamw_sharded")
    res_sh = [dict(zip(SHARDED, _unpack(o, sh_shapes))) for o in outs_sh]

    quarter = _add2(mine[r_sh:r_sh + rq], other[r_sh:r_sh + rq], name="grad_rep_sum")
    rep_g = _all_gather_xy(quarter, name="grad_rep_gather").reshape(rr, LANES)
    rep_shapes = [a[n].shape for n in REPLICATED]
    pr = lambda dct: _pack([dct[n] for n in REPLICATED])
    outs_rep = _adamw(pr(a), rep_g, None, pr(mom), pr(var), name="adamw_replicated")
    res_rep = [dict(zip(REPLICATED, _unpack(o, rep_shapes))) for o in outs_rep]

    out = [loss, grad_x]
    for k in range(4):
        out += [res_big[n][k] if n in BIG_COLS else res_sh[k][n] if n in SHARDED else res_rep[k][n] for n in WEIGHTS]
    return tuple(out)


def kernel(x, c, ctx, c_ctx, ada_w, ada_b, ln1_g, ln1_b, w_in, conv_w, conv_b, rg_lambda, rg_wa, rg_ba, rg_wi, rg_bi, s5_a_re, s5_a_im, s5_log_dt, s5_b_re, s5_b_im, s5_c_re, s5_c_im, s5_d, s5_glu_w, s5_glu_b, w_out, b_out, ln2_g, ln2_b, mlp_w1, mlp_b1, mlp_w2, mlp_b2, loss_target, m_c_ctx, m_ada_w, m_ada_b, m_ln1_g, m_ln1_b, m_w_in, m_conv_w, m_conv_b, m_rg_lambda, m_rg_wa, m_rg_ba, m_rg_wi, m_rg_bi, m_s5_a_re, m_s5_a_im, m_s5_log_dt, m_s5_b_re, m_s5_b_im, m_s5_c_re, m_s5_c_im, m_s5_d, m_s5_glu_w, m_s5_glu_b, m_w_out, m_b_out, m_ln2_g, m_ln2_b, m_mlp_w1, m_mlp_b1, m_mlp_w2, m_mlp_b2, v_c_ctx, v_ada_w, v_ada_b, v_ln1_g, v_ln1_b, v_w_in, v_conv_w, v_conv_b, v_rg_lambda, v_rg_wa, v_rg_ba, v_rg_wi, v_rg_bi, v_s5_a_re, v_s5_a_im, v_s5_log_dt, v_s5_b_re, v_s5_b_im, v_s5_c_re, v_s5_c_im, v_s5_d, v_s5_glu_w, v_s5_glu_b, v_w_out, v_b_out, v_ln2_g, v_ln2_b, v_mlp_w1, v_mlp_b1, v_mlp_w2, v_mlp_b2):
    return _kernel_impl(x, c, ctx, c_ctx, ada_w, ada_b, ln1_g, ln1_b, w_in, conv_w, conv_b, rg_lambda, rg_wa, rg_ba, rg_wi, rg_bi, s5_a_re, s5_a_im, s5_log_dt, s5_b_re, s5_b_im, s5_c_re, s5_c_im, s5_d, s5_glu_w, s5_glu_b, w_out, b_out, ln2_g, ln2_b, mlp_w1, mlp_b1, mlp_w2, mlp_b2, loss_target, m_c_ctx, m_ada_w, m_ada_b, m_ln1_g, m_ln1_b, m_w_in, m_conv_w, m_conv_b, m_rg_lambda, m_rg_wa, m_rg_ba, m_rg_wi, m_rg_bi, m_s5_a_re, m_s5_a_im, m_s5_log_dt, m_s5_b_re, m_s5_b_im, m_s5_c_re, m_s5_c_im, m_s5_d, m_s5_glu_w, m_s5_glu_b, m_w_out, m_b_out, m_ln2_g, m_ln2_b, m_mlp_w1, m_mlp_b1, m_mlp_w2, m_mlp_b2, v_c_ctx, v_ada_w, v_ada_b, v_ln1_g, v_ln1_b, v_w_in, v_conv_w, v_conv_b, v_rg_lambda, v_rg_wa, v_rg_ba, v_rg_wi, v_rg_bi, v_s5_a_re, v_s5_a_im, v_s5_log_dt, v_s5_b_re, v_s5_b_im, v_s5_c_re, v_s5_c_im, v_s5_d, v_s5_glu_w, v_s5_glu_b, v_w_out, v_b_out, v_ln2_g, v_ln2_b, v_mlp_w1, v_mlp_b1, v_mlp_w2, v_mlp_b2)
```

```python
import functools
import math

import jax
import jax.numpy as jnp
from jax import lax
from jax.experimental import pallas as pl
from jax.experimental.pallas import tpu as pltpu

F32 = jnp.float32
BF16 = jnp.bfloat16
MESH = pl.DeviceIdType.MESH

D_MODEL = 1024
N_MOD = 6
GRID_W = 64
RG_HEADS = 8
HEAD = 128
RG_C = 8.0
S5_GROUPS = 64
S5_GROUP = 16
S5_STATE = 64
T_CH = 16
GB = 8
CW = T_CH * S5_GROUP
SW = 2 * S5_STATE
DEPTH = 2
ALPHA = (2.0 * DEPTH) ** 0.25
LN_EPS = 1e-5
TM = 256
LANES = 1024
ADAM_LR, ADAM_B1, ADAM_B2, ADAM_EPS, ADAM_WD, ADAM_STEP = 0.001, 0.9, 0.999, 1e-08, 0.01, 10
MIB = 2 ** 20

IN_NAMES = ['x', 'c', 'ctx', 'c_ctx', 'ada_w', 'ada_b', 'ln1_g', 'ln1_b', 'w_in', 'conv_w', 'conv_b', 'rg_lambda',
            'rg_wa', 'rg_ba', 'rg_wi', 'rg_bi', 's5_a_re', 's5_a_im', 's5_log_dt', 's5_b_re', 's5_b_im', 's5_c_re',
            's5_c_im', 's5_d', 's5_glu_w', 's5_glu_b', 'w_out', 'b_out', 'ln2_g', 'ln2_b', 'mlp_w1', 'mlp_b1',
            'mlp_w2', 'mlp_b2']
WEIGHTS = IN_NAMES[3:]
SHARD_AXIS = {'ada_w': 2, 'w_in': 2, 'conv_w': 2, 'rg_lambda': 2, 'rg_ba': 2, 'rg_bi': 2, 's5_glu_w': 1, 'w_out': 1,
              'mlp_w1': 2, 'mlp_w2': 1}
SHARDED = ['conv_w', 'rg_lambda', 'rg_ba', 'rg_bi']
REPLICATED = [n for n in WEIGHTS if n not in SHARD_AXIS]
GATHER_BF16 = ['ada_w', 'w_in', 's5_glu_w', 'w_out', 'mlp_w1', 'mlp_w2']
GATHER_F32 = ['conv_w', 'rg_lambda', 'rg_ba', 'rg_bi']


def _params(n_axes, vmem_mb=40):
    return pltpu.CompilerParams(dimension_semantics=("arbitrary",) * n_axes, vmem_limit_bytes=vmem_mb * MIB)


def _tile(n, options):
    for t in options:
        if n % t == 0:
            return t
    return n


def _sigmoid(z):
    return 1.0 / (1.0 + jnp.exp(-z))


def _softplus(z):
    return jnp.maximum(z, 0.0) + jnp.log(1.0 + jnp.exp(-jnp.abs(z)))


def _neg_expm1_small(z):
    return -z * (1.0 + 0.5 * z * (1.0 + (1.0 / 3.0) * z * (1.0 + 0.25 * z)))


_G0 = math.sqrt(2.0 / math.pi)
_G1 = 0.044715


def _gelu(v):
    return 0.5 * v * (1.0 + jnp.tanh(_G0 * (v + _G1 * v * v * v)))


def _gelu_and_grad(v):
    t = jnp.tanh(_G0 * (v + _G1 * v * v * v))
    g = 0.5 * v * (1.0 + t)
    dg = 0.5 * (1.0 + t) + 0.5 * v * (1.0 - t * t) * _G0 * (1.0 + 3.0 * _G1 * v * v)
    return g, dg


def _seq_of_tile(i, tps):
    return 2 * (i // tps) + jnp.minimum(i % tps, 1)


def _dot(a, b, dims=(((1,), (0,)), ((), ()))):
    return lax.dot_general(a, b, dims, preferred_element_type=F32)


_NT = (((1,), (1,)), ((), ()))
_TN = (((0,), (0,)), ((), ()))


def _layer_spec(b, block, index):
    if isinstance(b, tuple):
        arr, layer = b
        return arr, pl.BlockSpec((None,) + block, lambda *g: (layer,) + index(*g))
    return b, pl.BlockSpec(block, index)


def _mm_nn(a, b, bias=None, *, relu2=False, name):
    m, k = a.shape
    n = (b[0] if isinstance(b, tuple) else b).shape[-1]
    tm, tn, tk = _tile(m, (512, 256)), _tile(n, (1024,)), _tile(k, (1024,))
    nk = k // tk
    has_bias = bias is not None

    def body(*refs):
        a_ref, b_ref = refs[0], refs[1]
        bias_ref = refs[2] if has_bias else None
        outs = refs[2 + has_bias:-1]
        acc = refs[-1]
        kk = pl.program_id(2)

        @pl.when(kk == 0)
        def _():
            acc[...] = jnp.zeros_like(acc)

        acc[...] += _dot(a_ref[...], b_ref[...])

        @pl.when(kk == nk - 1)
        def _():
            h = acc[...]
            if has_bias:
                h = h + bias_ref[...]
            if relu2:
                r = jnp.maximum(h, 0.0)
                outs[0][...] = (r * r).astype(BF16)
                outs[1][...] = h.astype(BF16)
            else:
                outs[0][...] = h

    b_arr, b_spec = _layer_spec(b, (tk, tn), lambda j, i, kk: (kk, j))
    in_specs = [pl.BlockSpec((tm, tk), lambda j, i, kk: (i, kk)), b_spec]
    args = [a, b_arr]
    if has_bias:
        in_specs.append(pl.BlockSpec((1, tn), lambda j, i, kk: (0, j)))
        args.append(bias)
    o_spec = pl.BlockSpec((tm, tn), lambda j, i, kk: (i, j))
    if relu2:
        out_shape = (jax.ShapeDtypeStruct((m, n), BF16), jax.ShapeDtypeStruct((m, n), BF16))
        out_specs = (o_spec, o_spec)
    else:
        out_shape, out_specs = jax.ShapeDtypeStruct((m, n), F32), o_spec
    return pl.pallas_call(body, name=name, grid=(n // tn, m // tm, nk), in_specs=in_specs, out_specs=out_specs,
                          out_shape=out_shape, scratch_shapes=[pltpu.VMEM((tm, tn), F32)],
                          compiler_params=_params(3))(*args)


def _mm_nt(a, b, hb=None, *, name):
    m, n = a.shape
    k = (b[0] if isinstance(b, tuple) else b).shape[-2]
    tm, tn, tk = _tile(m, (512, 256)), _tile(k, (1024,)), _tile(n, (1024,))
    nk = n // tk
    fused = hb is not None

    def body(*refs):
        a_ref, b_ref = refs[0], refs[1]
        hb_ref = refs[2] if fused else None
        o_ref, acc = refs[-2], refs[-1]
        kk = pl.program_id(2)

        @pl.when(kk == 0)
        def _():
            acc[...] = jnp.zeros_like(acc)

        acc[...] += _dot(a_ref[...], b_ref[...], _NT)

        @pl.when(kk == nk - 1)
        def _():
            if fused:
                o_ref[...] = (acc[...] * (2.0 * jnp.maximum(hb_ref[...].astype(F32), 0.0))).astype(BF16)
            else:
                o_ref[...] = acc[...]

    b_arr, b_spec = _layer_spec(b, (tn, tk), lambda j, i, kk: (j, kk))
    in_specs = [pl.BlockSpec((tm, tk), lambda j, i, kk: (i, kk)), b_spec]
    args = [a, b_arr]
    if fused:
        in_specs.append(pl.BlockSpec((tm, tn), lambda j, i, kk: (i, j)))
        args.append(hb)
    return pl.pallas_call(body, name=name, grid=(k // tn, m // tm, nk), in_specs=in_specs,
                          out_specs=pl.BlockSpec((tm, tn), lambda j, i, kk: (i, j)),
                          out_shape=jax.ShapeDtypeStruct((m, k), BF16 if fused else F32),
                          scratch_shapes=[pltpu.VMEM((tm, tn), F32)], compiler_params=_params(3))(*args)


def _mm_tn(a, b, *, name, layer=None, into=None):
    m, k = a.shape
    n = b.shape[1]
    tk, tn, tr = _tile(k, (1024,)), _tile(n, (1024,)), _tile(m, (512, 256))
    nr = m // tr

    def body(a_ref, b_ref, *rest):
        o_ref, acc = rest[-2], rest[-1]
        r = pl.program_id(2)

        @pl.when(r == 0)
        def _():
            acc[...] = jnp.zeros_like(acc)

        acc[...] += _dot(a_ref[...], b_ref[...], _TN)

        @pl.when(r == nr - 1)
        def _():
            o_ref[...] = acc[...]

    in_specs = [pl.BlockSpec((tr, tk), lambda i, j, r: (r, i)), pl.BlockSpec((tr, tn), lambda i, j, r: (r, j))]
    args, aliases = [a, b], {}
    if layer is None:
        out_spec, out_shape = pl.BlockSpec((tk, tn), lambda i, j, r: (i, j)), jax.ShapeDtypeStruct((k, n), F32)
    else:
        out_spec = pl.BlockSpec((None, tk, tn), lambda i, j, r: (layer, i, j))
        out_shape = jax.ShapeDtypeStruct((DEPTH, k, n), F32)
        if into is not None:
            in_specs.append(_ANY)
            args.append(into)
            aliases = {2: 0}
    return pl.pallas_call(body, name=name, grid=(k // tk, n // tn, nr), in_specs=in_specs, out_specs=out_spec,
                          out_shape=out_shape, input_output_aliases=aliases,
                          scratch_shapes=[pltpu.VMEM((tk, tn), F32)], compiler_params=_params(3))(*args)


def _colsum(v, *, name):
    m, n = v.shape
    tn, tr = _tile(n, (1024,)), _tile(m, (512, 256))

    def body(v_ref, o_ref):
        @pl.when(pl.program_id(1) == 0)
        def _():
            o_ref[...] = jnp.zeros_like(o_ref)

        o_ref[...] += jnp.sum(v_ref[...].astype(F32), axis=0, keepdims=True)

    return pl.pallas_call(body, name=name, grid=(n // tn, m // tr),
                          in_specs=[pl.BlockSpec((tr, tn), lambda j, r: (r, j))],
                          out_specs=pl.BlockSpec((1, tn), lambda j, r: (0, j)),
                          out_shape=jax.ShapeDtypeStruct((1, n), F32), compiler_params=_params(2))(v)


def _tok_spec(d=D_MODEL, col=0):
    return pl.BlockSpec((TM, d), lambda i: (i, col))


def _mod_spec(tps):
    return pl.BlockSpec((1, N_MOD, D_MODEL), lambda i: (_seq_of_tile(i, tps), 0, 0))


def _row_spec(d=D_MODEL):
    return pl.BlockSpec((1, d), lambda i: (0, 0))


def _seq_acc_spec(tps):
    return pl.BlockSpec((1, 1, D_MODEL), lambda i: (_seq_of_tile(i, tps), 0, 0))


def _modulate(xs, modall, k_shift, k_scale, tps, *, name):
    n = xs.shape[0]

    def body(x_ref, m_ref, o_ref):
        sh = m_ref[0, k_shift:k_shift + 1, :]
        sc = m_ref[0, k_scale:k_scale + 1, :]
        o_ref[...] = (x_ref[...] * (1.0 + sc) + sh).astype(BF16)

    return pl.pallas_call(body, name=name, grid=(n // TM,), in_specs=[_tok_spec(), _mod_spec(tps)],
                          out_specs=_tok_spec(), out_shape=jax.ShapeDtypeStruct((n, D_MODEL), BF16),
                          compiler_params=_params(1))(xs, modall)


def _resid_ln(xs, ms, modall, k_gate, g, b, tps, *, name):
    n = xs.shape[0]

    def body(x_ref, m_ref, mod_ref, g_ref, b_ref, o_ref):
        z = ALPHA * x_ref[...] + mod_ref[0, k_gate:k_gate + 1, :] * m_ref[...]
        mu = jnp.mean(z, axis=-1, keepdims=True)
        zc = z - mu
        var = jnp.mean(zc * zc, axis=-1, keepdims=True)
        o_ref[...] = zc * lax.rsqrt(var + LN_EPS) * g_ref[...] + b_ref[...]

    return pl.pallas_call(body, name=name, grid=(n // TM,),
                          in_specs=[_tok_spec(), _tok_spec(), _mod_spec(tps), _row_spec(), _row_spec()],
                          out_specs=_tok_spec(), out_shape=jax.ShapeDtypeStruct((n, D_MODEL), F32),
                          compiler_params=_params(1))(xs, ms, modall, g, b)


def _resid_ln_bwd(xs, ms, modall, k_gate, g, dout, tps, *, name):
    n = xs.shape[0]

    def body(x_ref, m_ref, mod_ref, g_ref, d_ref, dxa_ref, dm_ref, dbias_ref, dg_ref, db_ref, dgate_ref):
        i = pl.program_id(0)
        gate = mod_ref[0, k_gate:k_gate + 1, :]
        m = m_ref[...]
        z = ALPHA * x_ref[...] + gate * m
        mu = jnp.mean(z, axis=-1, keepdims=True)
        zc = z - mu
        var = jnp.mean(zc * zc, axis=-1, keepdims=True)
        rstd = lax.rsqrt(var + LN_EPS)
        xhat = zc * rstd
        d = d_ref[...]
        dxh = d * g_ref[...]
        dz = rstd * (dxh - jnp.mean(dxh, axis=-1, keepdims=True)
                     - xhat * jnp.mean(dxh * xhat, axis=-1, keepdims=True))
        dxa_ref[...] = ALPHA * dz
        dm = gate * dz
        dm_ref[...] = dm.astype(BF16)

        @pl.when(i == 0)
        def _():
            dbias_ref[...] = jnp.zeros_like(dbias_ref)
            dg_ref[...] = jnp.zeros_like(dg_ref)
            db_ref[...] = jnp.zeros_like(db_ref)

        dbias_ref[...] += jnp.sum(dm, axis=0, keepdims=True)
        dg_ref[...] += jnp.sum(d * xhat, axis=0, keepdims=True)
        db_ref[...] += jnp.sum(d, axis=0, keepdims=True)
        part = jnp.sum(dz * m, axis=0, keepdims=True)

        @pl.when(i % tps <= 1)
        def _():
            dgate_ref[0] = part

        @pl.when(i % tps > 1)
        def _():
            dgate_ref[0] += part

    row = jax.ShapeDtypeStruct((1, D_MODEL), F32)
    return pl.pallas_call(
        body, name=name, grid=(n // TM,),
        in_specs=[_tok_spec(), _tok_spec(), _mod_spec(tps), _row_spec(), _tok_spec()],
        out_specs=(_tok_spec(), _tok_spec(), _row_spec(), _row_spec(), _row_spec(), _seq_acc_spec(tps)),
        out_shape=(jax.ShapeDtypeStruct((n, D_MODEL), F32), jax.ShapeDtypeStruct((n, D_MODEL), BF16), row, row, row,
                   jax.ShapeDtypeStruct((n // TM // tps * 2, 1, D_MODEL), F32)),
        compiler_params=_params(1))(xs, ms, modall, g, dout)


def _modulate_bwd(du, xs, modall, k_scale, dxa, tps, *, name):
    n = xs.shape[0]

    def body(du_ref, x_ref, mod_ref, dxa_ref, dx_ref, dsc_ref, dsh_ref):
        i = pl.program_id(0)
        du_t = du_ref[...]
        dx_ref[...] = dxa_ref[...] + du_t * (1.0 + mod_ref[0, k_scale:k_scale + 1, :])
        psc = jnp.sum(du_t * x_ref[...], axis=0, keepdims=True)
        psh = jnp.sum(du_t, axis=0, keepdims=True)

        @pl.when(i % tps <= 1)
        def _():
            dsc_ref[0] = psc
            dsh_ref[0] = psh

        @pl.when(i % tps > 1)
        def _():
            dsc_ref[0] += psc
            dsh_ref[0] += psh

    acc = jax.ShapeDtypeStruct((n // TM // tps * 2, 1, D_MODEL), F32)
    return pl.pallas_call(body, name=name, grid=(n // TM,),
                          in_specs=[_tok_spec(), _tok_spec(), _mod_spec(tps), _tok_spec()],
                          out_specs=(_tok_spec(), _seq_acc_spec(tps), _seq_acc_spec(tps)),
                          out_shape=(jax.ShapeDtypeStruct((n, D_MODEL), F32), acc, acc),
                          compiler_params=_params(1))(du, xs, modall, dxa)


def _loss_head(ys, target, tps, *, name):
    n = ys.shape[0]
    lat_tiles = tps - 1

    def body(y_ref, t_ref, acc_ref, dy_ref):
        i = pl.program_id(0)

        @pl.when(i == 0)
        def _():
            acc_ref[...] = jnp.zeros_like(acc_ref)

        @pl.when(i % tps == 0)
        def _():
            dy_ref[...] = jnp.zeros_like(dy_ref)

        @pl.when(i % tps > 0)
        def _():
            e = y_ref[...] - t_ref[...]
            dy_ref[...] = e * (1.0 / D_MODEL)
            acc_ref[...] += jnp.sum(e * e, axis=0, keepdims=True)

    t_spec = pl.BlockSpec((TM, D_MODEL), lambda i: ((i // tps) * lat_tiles + jnp.maximum(i % tps - 1, 0), 0))
    return pl.pallas_call(body, name=name, grid=(n // TM,), in_specs=[_tok_spec(), t_spec],
                          out_specs=(_row_spec(), _tok_spec()),
                          out_shape=(jax.ShapeDtypeStruct((1, D_MODEL), F32), jax.ShapeDtypeStruct((n, D_MODEL), F32)),
                          compiler_params=_params(1))(ys, target)


def _glu_fwd(y_nat, p, d_skip, w, b, cat, *, name):
    n = y_nat.shape[0]

    def body(y_ref, u_ref, d_ref, w_ref, b_ref, cat_ref, o_ref):
        g = _gelu(y_ref[...] + d_ref[...] * u_ref[...])
        z = _dot(g.astype(BF16), w_ref[...]) + b_ref[...]
        o_ref[...] = (g * _sigmoid(z)).astype(BF16)

    w_arr, w_spec = _layer_spec(w, (D_MODEL, D_MODEL), lambda i: (0, 0))
    return pl.pallas_call(body, name=name, grid=(n // TM,),
                          in_specs=[_tok_spec(), _tok_spec(col=2), _row_spec(), w_spec, _row_spec(), _ANY],
                          out_specs=_tok_spec(col=1), out_shape=jax.ShapeDtypeStruct(cat.shape, BF16),
                          input_output_aliases={5: 0},
                          compiler_params=_params(1))(y_nat, p, d_skip, w_arr, b, cat)


def _glu_bwd(dcat, y_nat, p, d_skip, w, b, *, name):
    n = y_nat.shape[0]

    def body(ds_ref, y_ref, u_ref, d_ref, w_ref, b_ref, dy_ref, dsk_ref, g_ref, dz_ref, dd_ref, dbz_ref):
        u = u_ref[...]
        g, gg = _gelu_and_grad(y_ref[...] + d_ref[...] * u)
        s = _sigmoid(_dot(g.astype(BF16), w_ref[...]) + b_ref[...])
        ds = ds_ref[...]
        dz = ds * g * s * (1.0 - s)
        dzb = dz.astype(BF16)
        dg = ds * s + _dot(dzb, w_ref[...], _NT)
        dyp = dg * gg
        dy_ref[...] = dyp
        dsk_ref[...] = dyp * d_ref[...]
        g_ref[...] = g.astype(BF16)
        dz_ref[...] = dzb

        @pl.when(pl.program_id(0) == 0)
        def _():
            dd_ref[...] = jnp.zeros_like(dd_ref)
            dbz_ref[...] = jnp.zeros_like(dbz_ref)

        dd_ref[...] += jnp.sum(dyp * u, axis=0, keepdims=True)
        dbz_ref[...] += jnp.sum(dz, axis=0, keepdims=True)

    tok_bf = jax.ShapeDtypeStruct((n, D_MODEL), BF16)
    tok_f32 = jax.ShapeDtypeStruct((n, D_MODEL), F32)
    row = jax.ShapeDtypeStruct((1, D_MODEL), F32)
    w_arr, w_spec = _layer_spec(w, (D_MODEL, D_MODEL), lambda i: (0, 0))
    return pl.pallas_call(
        body, name=name, grid=(n // TM,),
        in_specs=[_tok_spec(col=1), _tok_spec(), _tok_spec(col=2), _row_spec(), w_spec, _row_spec()],
        out_specs=(_tok_spec(), _tok_spec(), _tok_spec(), _tok_spec(), _row_spec(), _row_spec()),
        out_shape=(tok_f32, tok_f32, tok_bf, tok_bf, row, row),
        compiler_params=_params(1))(dcat, y_nat, p, d_skip, w_arr, b)


def _add_cast(a, b, *, name):
    n = a.shape[0]

    def body(a_ref, b_ref, o_ref):
        o_ref[...] = (a_ref[...] + b_ref[...]).astype(BF16)

    return pl.pallas_call(body, name=name, grid=(n // TM,), in_specs=[_tok_spec(), _tok_spec()],
                          out_specs=_tok_spec(), out_shape=jax.ShapeDtypeStruct((n, D_MODEL), BF16),
                          compiler_params=_params(1))(a, b)


def _silu_rows(c16, *, name):
    def body(c_ref, s_ref, ds_ref):
        v = c_ref[...]
        sg = _sigmoid(v)
        s_ref[...] = v * sg
        ds_ref[...] = sg * (1.0 + v * (1.0 - sg))

    shp = jax.ShapeDtypeStruct(c16.shape, F32)
    return pl.pallas_call(body, name=name, out_shape=(shp, shp))(c16)


def _mul_rows(a, b, *, name):
    def body(a_ref, b_ref, o_ref):
        o_ref[...] = a_ref[...] * b_ref[...]

    return pl.pallas_call(body, name=name, out_shape=jax.ShapeDtypeStruct(a.shape, F32))(a, b)


def _pad_off(c):
    return pl.multiple_of(c * TM + 8 + 8 * jnp.minimum(c, 1), 8)


def _rows8(k):
    return pl.ds(pl.multiple_of(k * 8, 8), 8)


def _windows(buf, c, shifts):
    n = TM + 16
    win = buf[pl.ds(pl.multiple_of(_pad_off(c) - 8, 8), n), :]
    return [win[8:8 + TM] if k == 0 else pltpu.roll(win, (-k) % n, 0)[8:8 + TM] for k in shifts]


def _conv_window(xpad, c):
    return _windows(xpad, c, (-1, 0, 1, 2))


def _rg_coeffs(z, d, spl, xc):
    r = 0.5 + 0.5 * jnp.tanh(0.5 * z[:, 256 * d:256 * d + HEAD])
    i = 0.5 + 0.5 * jnp.tanh(0.5 * z[:, 256 * d + HEAD:256 * d + 2 * HEAD])
    la = -RG_C * spl[d:d + 1, :] * r
    a = jnp.exp(la)
    one_minus_a = jnp.where(jnp.abs(la) < 1e-2, _neg_expm1_small(la), 1.0 - a)
    mult = jnp.sqrt(one_minus_a * (1.0 + a))
    return r, i, a, mult, a * a


def _chunk_scan(a, b, reverse):
    row = lax.broadcasted_iota(jnp.int32, (TM, HEAD), 0)
    sft = 1
    while sft < TM:
        keep = (row < TM - sft) if reverse else (row >= sft)
        amt = TM - sft if reverse else sft
        a_prev = jnp.where(keep, pltpu.roll(a, amt, 0), 1.0)
        b_prev = jnp.where(keep, pltpu.roll(b, amt, 0), 0.0)
        b = a * b_prev + b
        a = a * a_prev
        sft *= 2
    return a, b


def _zero_pads(buf, s, lc):
    z8 = jnp.zeros((8, HEAD), F32)
    buf[0:8, :] = z8
    buf[8 + lc:16 + lc, :] = z8
    buf[16 + s:24 + s, :] = z8


def _rg_fwd(p, conv_w, conv_b, lam, wcat, bcat, bl, s, lc, *, name):
    nch = s // TM

    def body(x_ref, gate_ref, cw_ref, cb_ref, lam_ref, w_ref, b_ref, rg_ref, hf_ref, hb_ref, xpad, af, bf, ab, bb):
        _zero_pads(xpad, s, lc)

        def copy_chunk(c, _):
            xpad[pl.ds(_pad_off(c), TM), :] = x_ref[pl.ds(pl.multiple_of(c * TM, TM), TM), :]
            return 0

        lax.fori_loop(0, nch, copy_chunk, 0)
        spl = _softplus(-lam_ref[...])
        cw = cw_ref[...]

        def coef_chunk(c, _):
            xm1, x0, xp1, xp2 = _conv_window(xpad, c)
            xc = cw[0:1] * xm1 + cw[1:2] * x0 + cw[2:3] * xp1 + cw[3:4] * xp2 + cb_ref[...]
            z = _dot(xc.astype(BF16), w_ref[0]) + b_ref[0]
            rows = pl.ds(pl.multiple_of(c * TM, TM), TM)
            for d, (a_s, b_s) in enumerate(((af, bf), (ab, bb))):
                _, i, a, mult, _ = _rg_coeffs(z, d, spl, xc)
                a_s[rows, :] = a
                b_s[rows, :] = mult * i * xc
            return 0

        lax.fori_loop(0, nch, coef_chunk, 0)

        def scan_pair(j, carry):
            cf, cb_ = carry
            rf = pl.ds(pl.multiple_of(j * TM, TM), TM)
            rb = pl.ds(pl.multiple_of(jnp.where(j == 0, 0, nch - j) * TM, TM), TM)
            a1, h1 = _chunk_scan(af[rf, :], bf[rf, :], False)
            h1 = h1 + a1 * cf
            hf_ref[rf, :] = h1
            a2, h2 = _chunk_scan(ab[rb, :], bb[rb, :], True)
            h2 = h2 + a2 * cb_
            hb_ref[rb, :] = h2
            return h1[TM - 1:TM], h2[0:1]

        zero = jnp.zeros((1, HEAD), F32)
        lax.fori_loop(0, nch, scan_pair, (zero, zero))

        def out_chunk(c, _):
            rows = pl.ds(pl.multiple_of(c * TM, TM), TM)
            rg_ref[rows, :] = ((hf_ref[rows, :] + hb_ref[rows, :]) * _gelu(gate_ref[rows, :])).astype(BF16)
            return 0

        lax.fori_loop(0, nch, out_chunk, 0)

    seq = lambda col0: pl.BlockSpec((s, HEAD), lambda b, h: (b, col0 + h))
    par = lambda r: pl.BlockSpec((r, HEAD), lambda b, h: (0, h))
    n = bl * s
    return pl.pallas_call(
        body, name=name, grid=(bl, RG_HEADS),
        in_specs=[seq(0), seq(RG_HEADS), par(4), par(1), par(2),
                  pl.BlockSpec((1, HEAD, 4 * HEAD), lambda b, h: (h, 0, 0)),
                  pl.BlockSpec((1, 1, 4 * HEAD), lambda b, h: (h, 0, 0))],
        out_specs=(seq(0), seq(0), seq(0)),
        out_shape=(jax.ShapeDtypeStruct((n, 2 * D_MODEL), BF16), jax.ShapeDtypeStruct((n, D_MODEL), F32),
                   jax.ShapeDtypeStruct((n, D_MODEL), F32)),
        scratch_shapes=[pltpu.VMEM((s + 24, HEAD), F32)] + [pltpu.VMEM((s, HEAD), F32)] * 4,
        compiler_params=_params(2, 48))(p, p, conv_w, conv_b, lam, wcat, bcat)


def _rg_bwd(p, dcat, hf, hb, conv_w, conv_b, lam, wcat, bcat, bl, s, lc, *, name):
    nch = s // TM
    ll = s - lc

    def body(p_hbm, dcat_hbm, hf_hbm, hb_hbm, gate_s, cw_ref, cb_ref, lam_ref, w_ref, b_ref,
             drgx_ref, dgate_ref, dcw_ref, dcb_ref, dlam_ref, dw_ref, db_ref,
             xpad2, dxpad, hf2, hb2, dhs2, a_f, a_b, lam_f, lam_b, sems):
        h = pl.program_id(0)
        b = pl.program_id(1)
        step = h * bl + b
        slot = step % 2

        def copies(hh, bb, sl):
            col = pl.multiple_of(hh * HEAD, HEAD)

            def rows_of(ref, r0, nr):
                return ref.at[pl.ds(bb * s + r0, nr), pl.ds(col, HEAD)]

            return [
                pltpu.make_async_copy(rows_of(p_hbm, 0, lc), xpad2.at[sl, pl.ds(8, lc), :], sems.at[sl, 0]),
                pltpu.make_async_copy(rows_of(p_hbm, lc, ll), xpad2.at[sl, pl.ds(16 + lc, ll), :], sems.at[sl, 1]),
                pltpu.make_async_copy(rows_of(dcat_hbm, 0, s), dhs2.at[sl], sems.at[sl, 2]),
                pltpu.make_async_copy(rows_of(hf_hbm, 0, s), hf2.at[sl, pl.ds(8, s), :], sems.at[sl, 3]),
                pltpu.make_async_copy(rows_of(hb_hbm, 0, s), hb2.at[sl, pl.ds(8, s), :], sems.at[sl, 4]),
            ]

        @pl.when(step == 0)
        def _():
            for cp in copies(h, b, slot):
                cp.start()

        @pl.when(step + 1 < RG_HEADS * bl)
        def _():
            for cp in copies((step + 1) // bl, (step + 1) % bl, 1 - slot):
                cp.start()

        xpad, hf_s, hb_s, dhs = xpad2.at[slot], hf2.at[slot], hb2.at[slot], dhs2.at[slot]
        _zero_pads(xpad, s, lc)
        _zero_pads(dxpad, s, lc)
        for buf in (hf_s, hb_s):
            buf[0:8, :] = jnp.zeros((8, HEAD), F32)
            buf[8 + s:16 + s, :] = jnp.zeros((8, HEAD), F32)

        @pl.when(b == 0)
        def _():
            dcw_ref[...] = jnp.zeros_like(dcw_ref)
            dcb_ref[...] = jnp.zeros_like(dcb_ref)
            dlam_ref[...] = jnp.zeros_like(dlam_ref)
            dw_ref[...] = jnp.zeros_like(dw_ref)
            db_ref[...] = jnp.zeros_like(db_ref)

        for cp in copies(h, b, slot):
            cp.wait()
        lam_v = lam_ref[...]
        spl = _softplus(-lam_v)
        cw = cw_ref[...]

        def conv(c):
            xm1, x0, xp1, xp2 = _conv_window(xpad, c)
            return cw[0:1] * xm1 + cw[1:2] * x0 + cw[2:3] * xp1 + cw[3:4] * xp2 + cb_ref[...]

        def pass_a(c, _):
            rows = pl.ds(pl.multiple_of(c * TM, TM), TM)
            xc = conv(c)
            z = _dot(xc.astype(BF16), w_ref[0]) + b_ref[0]
            for d, a_s in enumerate((a_f, a_b)):
                a_s[rows, :] = _rg_coeffs(z, d, spl, xc)[2]
            g, gg = _gelu_and_grad(gate_s[rows, :])
            drg = dhs[rows, :]
            hrows = pl.ds(pl.multiple_of(c * TM + 8, 8), TM)
            dgate_ref[rows, :] = (drg * (hf_s[hrows, :] + hb_s[hrows, :]) * gg).astype(BF16)
            dhs[rows, :] = drg * g
            return 0

        lax.fori_loop(0, nch, pass_a, 0)

        row = lax.broadcasted_iota(jnp.int32, (TM, HEAD), 0)

        def adj_pair(j, carry):
            cf, cb_ = carry
            rf = pl.ds(pl.multiple_of((nch - 1 - j) * TM, TM), TM)
            rb = pl.ds(pl.multiple_of(jnp.where(j == nch - 1, 0, j + 1) * TM, TM), TM)
            d1, a1 = dhs[rf, :], a_f[rf, :]
            p1, m1 = _chunk_scan(a1, a1 * d1, True)
            m1 = m1 + p1 * cf
            lam_f[rf, :] = d1 + jnp.where(row == TM - 1, cf, pltpu.roll(m1, TM - 1, 0))
            d2, a2 = dhs[rb, :], a_b[rb, :]
            p2, m2 = _chunk_scan(a2, a2 * d2, False)
            m2 = m2 + p2 * cb_
            lam_b[rb, :] = d2 + jnp.where(row == 0, cb_, pltpu.roll(m2, 1, 0))
            return m1[0:1], m2[TM - 1:TM]

        zero = jnp.zeros((1, HEAD), F32)
        lax.fori_loop(0, nch, adj_pair, (zero, zero))

        sig_neg = _sigmoid(-lam_v)
        last_row = lax.broadcasted_iota(jnp.int32, (TM, HEAD), 0) == TM - 1
        hb_first = hb_s[8:9, :]

        def pass_b(c, _):
            rows = pl.ds(pl.multiple_of(c * TM, TM), TM)
            xc = conv(c)
            xcb = xc.astype(BF16)
            z = _dot(xcb, w_ref[0]) + b_ref[0]
            dxc = jnp.zeros((TM, HEAD), F32)
            dzs = []
            n = TM + 16
            hp_f = pltpu.roll(hf_s[pl.ds(pl.multiple_of(c * TM, TM), n), :], 1, 0)[8:8 + TM]
            hp_b = pltpu.roll(hb_s[pl.ds(pl.multiple_of(c * TM, TM), n), :], n - 1, 0)[8:8 + TM]
            hp_b = jnp.where(last_row & (c == 0), 0.0, hp_b)
            hp_b = jnp.where(last_row & (c == nch - 1), hb_first, hp_b)
            for d, (l_s, hp) in enumerate(((lam_f, hp_f), (lam_b, hp_b))):
                r, i, a, mult, e2 = _rg_coeffs(z, d, spl, xc)
                dbt = l_s[rows, :]
                dla = dbt * hp * a - dbt * i * xc * (e2 / mult)
                dlam_ref[0, d:d + 1, :] += jnp.sum(dla * r, axis=0, keepdims=True) * (RG_C * sig_neg[d:d + 1, :])
                dr = dla * (-RG_C * spl[d:d + 1, :])
                di = dbt * mult * xc
                dxc = dxc + dbt * mult * i
                dzs += [dr * r * (1.0 - r), di * i * (1.0 - i)]
            dz = jnp.concatenate(dzs, axis=1)
            dzb = dz.astype(BF16)
            dxc = dxc + _dot(dzb, w_ref[0], _NT)
            dw_ref[0] += _dot(xcb, dzb, _TN)
            db_ref[0] += jnp.sum(dz, axis=0, keepdims=True)
            dcb_ref[0] += jnp.sum(dxc, axis=0, keepdims=True)
            dxpad[pl.ds(_pad_off(c), TM), :] = dxc
            return 0

        lax.fori_loop(0, nch, pass_b, 0)

        def pass_c(c, _):
            rows = pl.ds(pl.multiple_of(c * TM, TM), TM)
            gp1, g0, gm1, gm2 = _windows(dxpad, c, (1, 0, -1, -2))
            drgx_ref[rows, :] = (cw[0:1] * gp1 + cw[1:2] * g0 + cw[2:3] * gm1 + cw[3:4] * gm2).astype(BF16)
            xm1, x0, xp1, xp2 = _conv_window(xpad, c)
            dcw_ref[0] += jnp.concatenate([jnp.sum(g0 * t, axis=0, keepdims=True) for t in (xm1, x0, xp1, xp2)],
                                          axis=0)
            return 0

        lax.fori_loop(0, nch, pass_c, 0)

    seq = pl.BlockSpec((s, HEAD), lambda h, b: (b, h))
    par = lambda r: pl.BlockSpec((r, HEAD), lambda h, b: (0, h))
    acc = lambda r, w: pl.BlockSpec((1, r, w), lambda h, b: (h, 0, 0))
    anyspec = pl.BlockSpec(memory_space=pl.ANY)
    n = bl * s
    big = pltpu.VMEM((s, HEAD), F32)
    return pl.pallas_call(
        body, name=name, grid=(RG_HEADS, bl),
        in_specs=[anyspec, anyspec, anyspec, anyspec, pl.BlockSpec((s, HEAD), lambda h, b: (b, RG_HEADS + h)),
                  par(4), par(1), par(2), acc(HEAD, 4 * HEAD), acc(1, 4 * HEAD)],
        out_specs=(seq, seq, acc(4, HEAD), acc(1, HEAD), acc(2, HEAD), acc(HEAD, 4 * HEAD), acc(1, 4 * HEAD)),
        out_shape=(jax.ShapeDtypeStruct((n, D_MODEL), BF16), jax.ShapeDtypeStruct((n, D_MODEL), BF16),
                   jax.ShapeDtypeStruct((RG_HEADS, 4, HEAD), F32), jax.ShapeDtypeStruct((RG_HEADS, 1, HEAD), F32),
                   jax.ShapeDtypeStruct((RG_HEADS, 2, HEAD), F32),
                   jax.ShapeDtypeStruct((RG_HEADS, HEAD, 4 * HEAD), F32),
                   jax.ShapeDtypeStruct((RG_HEADS, 1, 4 * HEAD), F32)),
        scratch_shapes=[pltpu.VMEM((2, s + 24, HEAD), F32), pltpu.VMEM((s + 24, HEAD), F32),
                        pltpu.VMEM((2, s + 16, HEAD), F32), pltpu.VMEM((2, s + 16, HEAD), F32),
                        pltpu.VMEM((2, s, HEAD), F32)] + [big] * 4 + [pltpu.SemaphoreType.DMA((2, 5))],
        compiler_params=_params(2, 56))(p, dcat, hf, hb, p, conv_w, conv_b, lam, wcat, bcat)


def _s5_mats(a_re, a_im, log_dt, b_re, b_im, c_re, c_im):
    t = T_CH
    g = a_re.shape[1]
    dt = jnp.exp(log_dt)[..., None]
    lr, li = a_re * dt, a_im * dt
    steps = jnp.arange(t + 1, dtype=F32)[:, None]
    mag = jnp.exp(lr[:, :, None, :] * steps)
    ang = li[:, :, None, :] * steps
    pr, pi = mag * jnp.cos(ang), mag * jnp.sin(ang)
    xr, xi = pr[:, :, 1] - 1.0, pi[:, :, 1]
    den = a_re * a_re + a_im * a_im
    qr, qi = (xr * a_re + xi * a_im) / den, (xi * a_re - xr * a_im) / den
    btr, bti = b_re.transpose(0, 1, 3, 2), b_im.transpose(0, 1, 3, 2)
    bbr = qr[:, :, None, :] * btr - qi[:, :, None, :] * bti
    bbi = qr[:, :, None, :] * bti + qi[:, :, None, :] * btr
    up, down = slice(0, t), slice(t - 1, None, -1)

    def pow_c(d, sl):
        wr, wi = pr[d][:, sl, None, :], pi[d][:, sl, None, :]
        cr, ci = c_re[d][:, None], c_im[d][:, None]
        return (wr * cr - wi * ci).reshape(g, CW, S5_STATE), (wr * ci + wi * cr).reshape(g, CW, S5_STATE)

    hp = lax.Precision.HIGHEST

    def lag_map(d, sl):
        re, im = pow_c(d, sl)
        return (jnp.einsum('gkp,gmp->gkm', bbr[d], re, precision=hp)
                - jnp.einsum('gkp,gmp->gkm', bbi[d], im, precision=hp))

    z_f, z_b = lag_map(0, up), lag_map(1, down)
    kf = jnp.stack([jnp.pad(z_f, ((0, 0), (0, 0), (S5_GROUP * s, 0)))[:, :, :CW] for s in range(t)], axis=1)
    kb = jnp.stack([jnp.pad(z_b, ((0, 0), (0, 0), (0, S5_GROUP * (t - 1 - s))))[:, :, S5_GROUP * (t - 1 - s):]
                    for s in range(t)], axis=1)
    kcat = (kf + kb).reshape(g, CW, CW)

    def state_in(d, sl):
        wr, wi = pr[d][:, sl, None, :], pi[d][:, sl, None, :]
        br, bi = bbr[d][:, None], bbi[d][:, None]
        return jnp.concatenate([wr * br - wi * bi, wr * bi + wi * br], axis=-1).reshape(g, CW, SW)

    wcat = jnp.concatenate([kcat, state_in(0, down), state_in(1, up)], axis=2)
    of_r, of_i = pow_c(0, slice(1, t + 1))
    ob_r, ob_i = pow_c(1, slice(t, 0, -1))
    mout_t = jnp.concatenate([of_r, -of_i, ob_r, -ob_i], axis=2)
    rows = []
    for d in range(2):
        art, ait = pr[d][:, t], pi[d][:, t]
        rows += [jnp.concatenate([art, art], axis=1).reshape(-1), jnp.concatenate([-ait, ait], axis=1).reshape(-1)]
    return wcat, mout_t, jnp.stack(rows)


def _lane_swap(v):
    return pltpu.roll(v, S5_STATE, 1)


def _grp(g, w):
    return slice(g * w, (g + 1) * w)


def _s5_fwd(u, wcat, mout, a2, ncc, *, name):
    bl, nc, _ = u.shape

    def body(u_ref, w_ref, mo_ref, a_ref, y_ref, sf_ref, sb_ref, vf, vb):
        for g in range(GB):
            zu = _dot(u_ref[:, _grp(g, CW)].astype(BF16), w_ref[g])
            y_ref[:, _grp(g, CW)] = zu[:, :CW]
            vf[:, _grp(2 * g, SW)] = zu[:, CW:CW + SW]
            vb[:, _grp(2 * g, SW)] = zu[:, CW + SW:]
            vf[:, _grp(2 * g + 1, SW)] = _lane_swap(zu[:, CW:CW + SW])
            vb[:, _grp(2 * g + 1, SW)] = _lane_swap(zu[:, CW + SW:])
        co = [[a_ref[r:r + 1, _grp(g, SW)] for g in range(GB)] for r in range(4)]

        rid = lax.broadcasted_iota(jnp.int32, (8, SW), 0)

        def step8(groups, kf, kb, carry):
            rf, rb = _rows8(kf), _rows8(kb)
            lanes = slice(groups[0] * 2 * SW, (groups[-1] + 1) * 2 * SW)
            vfb, vbb = vf[rf, lanes], vb[rb, lanes]
            st = list(carry)
            of = [jnp.zeros((8, SW), F32)] * len(groups)
            ob = list(of)
            for i in range(8):
                k = 7 - i
                for n, g in enumerate(groups):
                    sf, sfs, sb, sbs = st[4 * n:4 * n + 4]
                    of[n] = jnp.where(rid == i, sf, of[n])
                    ob[n] = jnp.where(rid == k, sb, ob[n])
                    st[4 * n] = co[0][g] * sf + co[1][g] * sfs + vfb[i:i + 1, _grp(2 * n, SW)]
                    st[4 * n + 1] = co[0][g] * sfs - co[1][g] * sf + vfb[i:i + 1, _grp(2 * n + 1, SW)]
                    st[4 * n + 2] = co[2][g] * sb + co[3][g] * sbs + vbb[k:k + 1, _grp(2 * n, SW)]
                    st[4 * n + 3] = co[2][g] * sbs - co[3][g] * sb + vbb[k:k + 1, _grp(2 * n + 1, SW)]
            for n, g in enumerate(groups):
                sf_ref[rf, _grp(g, SW)] = of[n]
                sb_ref[rb, _grp(g, SW)] = ob[n]
            return tuple(st)

        zero = jnp.zeros((1, SW), F32)
        nbc, nb = ncc // 8, nc // 8
        for groups in (tuple(range(0, GB // 2)), tuple(range(GB // 2, GB))):
            carry = lax.fori_loop(0, nbc, lambda j, cr, gs=groups: step8(gs, j, nbc - 1 - j, cr),
                                  (zero,) * (4 * len(groups)))
            lax.fori_loop(nbc, nb, lambda j, cr, gs=groups: step8(gs, j, nb + nbc - 1 - j, cr), carry)
        for g in range(GB):
            st = jnp.concatenate([sf_ref[:, _grp(g, SW)], sb_ref[:, _grp(g, SW)]], axis=1).astype(BF16)
            y_ref[:, _grp(g, CW)] += _dot(st, mo_ref[g], _NT)

    blk = lambda w: pl.BlockSpec((None, nc, GB * w), lambda b, gb: (b, 0, gb))
    return pl.pallas_call(
        body, name=name, grid=(bl, S5_GROUPS // GB),
        in_specs=[blk(CW), pl.BlockSpec((GB, CW, 2 * CW), lambda b, gb: (gb, 0, 0)),
                  pl.BlockSpec((GB, CW, CW), lambda b, gb: (gb, 0, 0)),
                  pl.BlockSpec((4, GB * SW), lambda b, gb: (0, gb))],
        out_specs=(blk(CW), blk(SW), blk(SW)),
        out_shape=(jax.ShapeDtypeStruct(u.shape, F32), jax.ShapeDtypeStruct((bl, nc, S5_GROUPS * SW), F32),
                   jax.ShapeDtypeStruct((bl, nc, S5_GROUPS * SW), F32)),
        scratch_shapes=[pltpu.VMEM((nc, GB * 2 * SW), F32)] * 2, compiler_params=_params(2))(u, wcat, mout, a2)


def _s5_bwd(dy, u, sf, sb, wcat, mout, a2, ncc, *, name):
    bl, nc, _ = u.shape

    def body(dy_ref, u_ref, sf_ref, sb_ref, w_ref, mo_ref, a_ref, du_ref, dw_ref, dmo_ref, dacc_ref, gsf, gsb, dvf, dvb):
        b = pl.program_id(1)

        @pl.when(b == 0)
        def _():
            dw_ref[...] = jnp.zeros_like(dw_ref)
            dmo_ref[...] = jnp.zeros_like(dmo_ref)
            dacc_ref[...] = jnp.zeros_like(dacc_ref)

        for g in range(GB):
            ds = _dot(dy_ref[:, _grp(g, CW)].astype(BF16), mo_ref[g])
            gsf[:, _grp(2 * g, SW)] = ds[:, :SW]
            gsb[:, _grp(2 * g, SW)] = ds[:, SW:]
            gsf[:, _grp(2 * g + 1, SW)] = _lane_swap(ds[:, :SW])
            gsb[:, _grp(2 * g + 1, SW)] = _lane_swap(ds[:, SW:])
        co = [[a_ref[r:r + 1, _grp(g, SW)] for g in range(GB)] for r in range(4)]

        rid = lax.broadcasted_iota(jnp.int32, (8, SW), 0)

        def step8(groups, kf, kb, carry):
            rf, rb = _rows8(kf), _rows8(kb)
            lanes = slice(groups[0] * 2 * SW, (groups[-1] + 1) * 2 * SW)
            gfb, gbb = gsf[rf, lanes], gsb[rb, lanes]
            st = list(carry)
            of = [jnp.zeros((8, SW), F32)] * len(groups)
            ob = list(of)
            for i in range(8):
                k = 7 - i
                for n, g in enumerate(groups):
                    gf, gfs, gb_, gbs = st[4 * n:4 * n + 4]
                    of[n] = jnp.where(rid == k, gf, of[n])
                    ob[n] = jnp.where(rid == i, gb_, ob[n])
                    st[4 * n] = gfb[k:k + 1, _grp(2 * n, SW)] + co[0][g] * gf - co[1][g] * gfs
                    st[4 * n + 1] = gfb[k:k + 1, _grp(2 * n + 1, SW)] + co[0][g] * gfs + co[1][g] * gf
                    st[4 * n + 2] = gbb[i:i + 1, _grp(2 * n, SW)] + co[2][g] * gb_ - co[3][g] * gbs
                    st[4 * n + 3] = gbb[i:i + 1, _grp(2 * n + 1, SW)] + co[2][g] * gbs + co[3][g] * gb_
            for n, g in enumerate(groups):
                dvf[rf, _grp(g, SW)] = of[n]
                dvb[rb, _grp(g, SW)] = ob[n]
            return tuple(st)

        zero = jnp.zeros((1, SW), F32)
        nbc, nb = ncc // 8, nc // 8
        for groups in (tuple(range(0, GB // 2)), tuple(range(GB // 2, GB))):
            carry = lax.fori_loop(0, nb - nbc, lambda j, cr, gs=groups: step8(gs, nb - 1 - j, nbc + j, cr),
                                  (zero,) * (4 * len(groups)))
            lax.fori_loop(0, nbc, lambda j, cr, gs=groups: step8(gs, nbc - 1 - j, j, cr), carry)
        for g in range(GB):
            dyg = dy_ref[:, _grp(g, CW)].astype(BF16)
            dvf_g, dvb_g = dvf[:, _grp(g, SW)], dvb[:, _grp(g, SW)]
            sf_g, sb_g = sf_ref[:, _grp(g, SW)], sb_ref[:, _grp(g, SW)]
            dz = jnp.concatenate([dyg, dvf_g.astype(BF16), dvb_g.astype(BF16)], axis=1)
            du_ref[:, _grp(g, CW)] = _dot(dz, w_ref[g], _NT)
            dw_ref[g] += _dot(u_ref[:, _grp(g, CW)].astype(BF16), dz, _TN)
            st = jnp.concatenate([sf_g, sb_g], axis=1).astype(BF16)
            dmo_ref[g] += _dot(dyg, st, _TN)
            dacc_ref[:, _grp(g, SW)] += jnp.concatenate(
                [jnp.sum(dvf_g * sf_g, axis=0, keepdims=True), jnp.sum(dvf_g * _lane_swap(sf_g), axis=0, keepdims=True),
                 jnp.sum(dvb_g * sb_g, axis=0, keepdims=True), jnp.sum(dvb_g * _lane_swap(sb_g), axis=0, keepdims=True)],
                axis=0)

    blk = lambda w: pl.BlockSpec((None, nc, GB * w), lambda gb, b: (b, 0, gb))
    wspec = lambda mult: pl.BlockSpec((GB, CW, mult * CW), lambda gb, b: (gb, 0, 0))
    aspec = pl.BlockSpec((4, GB * SW), lambda gb, b: (0, gb))
    return pl.pallas_call(
        body, name=name, grid=(S5_GROUPS // GB, bl),
        in_specs=[blk(CW), blk(CW), blk(SW), blk(SW), wspec(2), wspec(1), aspec],
        out_specs=(blk(CW), wspec(2), wspec(1), aspec),
        out_shape=(jax.ShapeDtypeStruct(u.shape, F32), jax.ShapeDtypeStruct((S5_GROUPS, CW, 2 * CW), F32),
                   jax.ShapeDtypeStruct((S5_GROUPS, CW, CW), F32), jax.ShapeDtypeStruct(a2.shape, F32)),
        scratch_shapes=[pltpu.VMEM((nc, GB * 2 * SW), F32)] * 2 + [pltpu.VMEM((nc, GB * SW), F32)] * 2,
        compiler_params=_params(2, 48))(dy, u, sf, sb, wcat, mout, a2)


def _lane_slot():
    return lax.broadcasted_iota(jnp.int32, (GRID_W, HEAD), 1) // S5_GROUP


def _slots_to_chunk(tiles, slot, q, j):
    acc = jnp.zeros(tiles[0].shape, F32)
    for m in range(HEAD // S5_GROUP):
        shift = ((m - q) * S5_GROUP) % HEAD
        v = tiles[8 * j + m]
        acc = jnp.where(slot == m, v if shift == 0 else pltpu.roll(v, shift, 1), acc)
    return acc


def _slots_to_rows(tiles, slot, m):
    acc = jnp.zeros(tiles[0].shape, F32)
    for q in range(HEAD // S5_GROUP):
        shift = ((q - m) * S5_GROUP) % HEAD
        acc = jnp.where(slot == q, tiles[q] if shift == 0 else pltpu.roll(tiles[q], shift, 1), acc)
    return acc


def _to_chunks(src, col0, bl, s, lc, *, name):
    ncc = lc // T_CH
    nrh = (s - lc) // GRID_W // T_CH
    nc = ncc + GRID_W * nrh

    ntile = (CW // HEAD) * (HEAD // S5_GROUP)

    def body(x_ref, o_ref, tmp, *dense):
        slot_c = lax.broadcasted_iota(jnp.int32, (ncc, HEAD), 1) // S5_GROUP
        slot = _lane_slot()

        def one(rh, _):
            tiles = [x_ref[pl.ds(pl.multiple_of(lc + (rh * T_CH + t) * GRID_W, GRID_W), GRID_W), :]
                     for t in range(T_CH)]
            for j in range(CW // HEAD):
                for q in range(HEAD // S5_GROUP):
                    dense[j * 8 + q][pl.ds(pl.multiple_of(rh * GRID_W, GRID_W), GRID_W), :] = (
                        _slots_to_chunk(tiles, slot, q, j))
            return 0

        lax.fori_loop(0, nrh, one, 0)
        ctx_tiles = [x_ref[pl.ds(t, ncc, stride=T_CH), :] for t in range(T_CH)]
        for j in range(CW // HEAD):
            for q in range(HEAD // S5_GROUP):
                tmp[0:ncc, :] = _slots_to_chunk(ctx_tiles, slot_c, q, j)
                for w in range(GRID_W):
                    tmp[ncc + nrh * w:ncc + nrh * (w + 1), :] = dense[j * 8 + q][pl.ds(w, nrh, stride=GRID_W), :]
                o_ref[:, q * CW + j * HEAD:q * CW + (j + 1) * HEAD] = tmp[...]

    return pl.pallas_call(
        body, name=name, grid=(bl, S5_GROUPS // GB),
        in_specs=[pl.BlockSpec((s, HEAD), lambda b, gb: (b, col0 + gb))],
        out_specs=pl.BlockSpec((None, nc, GB * CW), lambda b, gb: (b, 0, gb)),
        out_shape=jax.ShapeDtypeStruct((bl, nc, S5_GROUPS * CW), F32),
        scratch_shapes=[pltpu.VMEM((nc, HEAD), F32)] + [pltpu.VMEM((nrh * GRID_W, HEAD), F32)] * ntile,
        compiler_params=_params(2))(src)


def _from_chunks(v, bl, s, lc, *, name):
    ncc = lc // T_CH
    nrh = (s - lc) // GRID_W // T_CH
    nc = v.shape[1]

    def body(v_ref, o_ref, *tmp):
        slot_c = lax.broadcasted_iota(jnp.int32, (ncc, HEAD), 1) // S5_GROUP
        slot = _lane_slot()
        for j in range(CW // HEAD):
            for q in range(HEAD // S5_GROUP):
                tmp[q][...] = v_ref[:, q * CW + j * HEAD:q * CW + (j + 1) * HEAD]
            for m in range(8):
                tiles = [tmp[q][0:ncc, :] for q in range(HEAD // S5_GROUP)]
                o_ref[pl.ds(8 * j + m, ncc, stride=T_CH), :] = _slots_to_rows(tiles, slot_c, m)

            def one(rh, _):
                for m in range(8):
                    tiles = [tmp[q][pl.ds(ncc + rh, GRID_W, stride=nrh), :] for q in range(HEAD // S5_GROUP)]
                    rows = pl.ds(pl.multiple_of(lc + (rh * T_CH + 8 * j + m) * GRID_W, GRID_W), GRID_W)
                    o_ref[rows, :] = _slots_to_rows(tiles, slot, m)
                return 0

            lax.fori_loop(0, nrh, one, 0)

    return pl.pallas_call(
        body, name=name, grid=(bl, S5_GROUPS // GB),
        in_specs=[pl.BlockSpec((None, nc, GB * CW), lambda b, gb: (b, 0, gb))],
        out_specs=pl.BlockSpec((s, HEAD), lambda b, gb: (b, gb)),
        out_shape=jax.ShapeDtypeStruct((bl * s, D_MODEL), F32),
        scratch_shapes=[pltpu.VMEM((nc, HEAD), F32)] * (HEAD // S5_GROUP), compiler_params=_params(2))(v)


_ANY = pl.BlockSpec(memory_space=pl.ANY)


def _xy_peers():
    x, y, c = lax.axis_index("x"), lax.axis_index("y"), lax.axis_index("c")
    return x, y, c, [(1 - x, y), (x, 1 - y), (1 - x, 1 - y)]


def _all_gather_xy(shard, *, name):
    def body(x_ref, out_ref, send_sems, recv_sems, local_sem):
        x, y, c, peers = _xy_peers()
        me = 2 * x + y
        mine = pltpu.make_async_copy(x_ref, out_ref.at[me], local_sem)
        mine.start()

        def copy(k, px, py, slot):
            return pltpu.make_async_remote_copy(src_ref=x_ref, dst_ref=out_ref.at[slot], send_sem=send_sems.at[k],
                                                recv_sem=recv_sems.at[k], device_id=(px, py, c), device_id_type=MESH)

        sends = [copy(k, px, py, me) for k, (px, py) in enumerate(peers)]
        for cp in sends:
            cp.start()
        for k, (px, py) in enumerate(peers):
            copy(k, px, py, 2 * px + py).wait_recv()
        for cp in sends:
            cp.wait_send()
        mine.wait()

    return pl.pallas_call(body, name=name, in_specs=[_ANY], out_specs=_ANY,
                          out_shape=jax.ShapeDtypeStruct((4,) + shard.shape, shard.dtype),
                          scratch_shapes=[pltpu.SemaphoreType.DMA((3,)), pltpu.SemaphoreType.DMA((3,)),
                                          pltpu.SemaphoreType.DMA])(shard)


def _scatter_xy(parts, *, name):
    def body(p_ref, out_ref, send_sems, recv_sems, local_sem):
        x, y, c, peers = _xy_peers()
        mine = pltpu.make_async_copy(p_ref.at[2 * x + y], out_ref.at[0], local_sem)
        mine.start()

        def copy(k, px, py):
            return pltpu.make_async_remote_copy(src_ref=p_ref.at[2 * px + py], dst_ref=out_ref.at[1 + k],
                                                send_sem=send_sems.at[k], recv_sem=recv_sems.at[k],
                                                device_id=(px, py, c), device_id_type=MESH)

        sends = [copy(k, px, py) for k, (px, py) in enumerate(peers)]
        for cp in sends:
            cp.start()
        for cp in sends:
            cp.wait_recv()
        for cp in sends:
            cp.wait_send()
        mine.wait()

    return pl.pallas_call(body, name=name, in_specs=[_ANY], out_specs=_ANY,
                          out_shape=jax.ShapeDtypeStruct(parts.shape, parts.dtype),
                          scratch_shapes=[pltpu.SemaphoreType.DMA((3,)), pltpu.SemaphoreType.DMA((3,)),
                                          pltpu.SemaphoreType.DMA])(parts)


def _swap_sibling(v, *, name):
    def body(v_ref, out_ref, send_sem, recv_sem):
        x, y, c = lax.axis_index("x"), lax.axis_index("y"), lax.axis_index("c")
        cp = pltpu.make_async_remote_copy(src_ref=v_ref, dst_ref=out_ref, send_sem=send_sem, recv_sem=recv_sem,
                                          device_id=(x, y, 1 - c), device_id_type=MESH)
        cp.start()
        cp.wait()

    return pl.pallas_call(body, name=name, in_specs=[_ANY], out_specs=_ANY,
                          out_shape=jax.ShapeDtypeStruct(v.shape, v.dtype),
                          scratch_shapes=[pltpu.SemaphoreType.DMA, pltpu.SemaphoreType.DMA])(v)


BIG_COLS = {'ada_w': True, 'w_in': True, 'mlp_w1': True, 's5_glu_w': False, 'w_out': False, 'mlp_w2': False}
BIG = list(BIG_COLS)


def _block(ref2d, j, cols, size):
    if cols:
        return ref2d.at[:, pl.ds(pl.multiple_of(j * size, 128), size)]
    return ref2d.at[pl.ds(pl.multiple_of(j * size, 8), size), :]


def _shard_size(shape, cols):
    return shape[-1] if cols else shape[-2]


def _cast_into_full(shard, cols, my_j, *, name):
    _, r, c = shard.shape
    tr, tc = _tile(r, (256,)), _tile(c, (1024, 768, 512))

    def body(j_ref, x_ref, o_ref):
        o_ref[...] = x_ref[...].astype(BF16)

    if cols:
        out_spec = pl.BlockSpec((None, tr, tc), lambda l, i, j, j_ref: (l, i, j_ref[0] * (c // tc) + j))
    else:
        out_spec = pl.BlockSpec((None, tr, tc), lambda l, i, j, j_ref: (l, j_ref[0] * (r // tr) + i, j))
    return pl.pallas_call(
        body, name=name,
        grid_spec=pltpu.PrefetchScalarGridSpec(
            num_scalar_prefetch=1, grid=(DEPTH, r // tr, c // tc),
            in_specs=[pl.BlockSpec((None, tr, tc), lambda l, i, j, j_ref: (l, i, j))], out_specs=out_spec),
        out_shape=jax.ShapeDtypeStruct((DEPTH, r, 4 * c) if cols else (DEPTH, 4 * r, c), BF16),
        compiler_params=_params(3))(my_j, shard)


def _gather_big(fulls, cols, *, name):
    n = len(fulls)

    def body(*refs):
        outs = refs[n:2 * n]
        ici_send, ici_recv, d2d_send, d2d_recv = refs[2 * n:]
        x, y, c, peers = _xy_peers()
        me = 2 * x + y

        def blk(w, layer, j):
            shape = outs[w].shape
            return _block(outs[w].at[layer], j, cols[w], (shape[2] if cols[w] else shape[1]) // 4)

        def ici(w, k, px, py, j):
            return pltpu.make_async_remote_copy(src_ref=blk(w, c, j), dst_ref=blk(w, c, j),
                                                send_sem=ici_send.at[3 * w + k], recv_sem=ici_recv.at[3 * w + k],
                                                device_id=(px, py, c), device_id_type=MESH)

        def d2d(w, k, j, layer):
            return pltpu.make_async_remote_copy(src_ref=blk(w, layer, j), dst_ref=blk(w, layer, j),
                                                send_sem=d2d_send.at[3 * w + k], recv_sem=d2d_recv.at[3 * w + k],
                                                device_id=(x, y, 1 - c), device_id_type=MESH)

        started = [ici(w, k, px, py, me) for w in range(n) for k, (px, py) in enumerate(peers)]
        for cp in started:
            cp.start()
        passed = []
        for w in range(n):
            for k, (px, py) in enumerate(peers):
                ici(w, k, px, py, 2 * px + py).wait_recv()
                passed.append(d2d(w, k, 2 * px + py, c))
                passed[-1].start()
        for w in range(n):
            for k, (px, py) in enumerate(peers):
                d2d(w, k, 2 * px + py, 1 - c).wait_recv()
        for cp in started + passed:
            cp.wait_send()

    return pl.pallas_call(
        body, name=name, in_specs=[_ANY] * n, out_specs=[_ANY] * n,
        out_shape=[jax.ShapeDtypeStruct(f.shape, f.dtype) for f in fulls],
        input_output_aliases={w: w for w in range(n)},
        scratch_shapes=[pltpu.SemaphoreType.DMA((3 * n,))] * 4)(*fulls)


def _sibling_partials(gbufs, *, name):
    n = len(gbufs)

    def body(*refs):
        ins, outs, send, recv = refs[:n], refs[n:2 * n], refs[2 * n], refs[2 * n + 1]
        x, y, c = lax.axis_index("x"), lax.axis_index("y"), lax.axis_index("c")
        cps = [pltpu.make_async_remote_copy(src_ref=ins[w].at[1 - c], dst_ref=outs[w], send_sem=send.at[w],
                                            recv_sem=recv.at[w], device_id=(x, y, 1 - c), device_id_type=MESH)
               for w in range(n)]
        for cp in cps:
            cp.start()
        for cp in cps:
            cp.wait()

    return pl.pallas_call(body, name=name, in_specs=[_ANY] * n, out_specs=[_ANY] * n,
                          out_shape=[jax.ShapeDtypeStruct(g.shape[1:], g.dtype) for g in gbufs],
                          scratch_shapes=[pltpu.SemaphoreType.DMA((n,))] * 2)(*gbufs)


def _chip_sum(gbuf, other, my_c, *, name):
    _, k, n = gbuf.shape
    tr, tc = _tile(k, (512,)), _tile(n, (1024,))

    def body(c_ref, a_ref, b_ref, o_ref):
        o_ref[...] = (a_ref[...] + b_ref[...]).astype(BF16)

    spec = pl.BlockSpec((tr, tc), lambda i, j, c_ref: (i, j))
    return pl.pallas_call(
        body, name=name,
        grid_spec=pltpu.PrefetchScalarGridSpec(
            num_scalar_prefetch=1, grid=(k // tr, n // tc),
            in_specs=[pl.BlockSpec((None, tr, tc), lambda i, j, c_ref: (c_ref[0], i, j)), spec], out_specs=spec),
        out_shape=jax.ShapeDtypeStruct((k, n), BF16), compiler_params=_params(2))(my_c, gbuf, other)


def _scatter_big(sums, cols, *, name):
    n = len(sums)

    def shard(s, cf):
        return (s.shape[0], s.shape[1] // 4) if cf else (s.shape[0] // 4, s.shape[1])

    def body(*refs):
        ins, outs, send, recv = refs[:n], refs[n:2 * n], refs[2 * n], refs[2 * n + 1]
        x, y, c, peers = _xy_peers()
        cps = []
        for w in range(n):
            size = _shard_size(shard(ins[w], cols[w]), cols[w])
            for k, (px, py) in enumerate(peers):
                cps.append(pltpu.make_async_remote_copy(
                    src_ref=_block(ins[w], 2 * px + py, cols[w], size), dst_ref=outs[w].at[k],
                    send_sem=send.at[3 * w + k], recv_sem=recv.at[3 * w + k], device_id=(px, py, c),
                    device_id_type=MESH))
        for cp in cps:
            cp.start()
        for cp in cps:
            cp.wait()

    return pl.pallas_call(body, name=name, in_specs=[_ANY] * n, out_specs=[_ANY] * n,
                          out_shape=[jax.ShapeDtypeStruct((3,) + shard(s, cf), s.dtype) for s, cf in zip(sums, cols)],
                          scratch_shapes=[pltpu.SemaphoreType.DMA((3 * n,))] * 2)(*sums)


def _block_sum(own, got, cols, my_j, my_c, *, name):
    _, r, c = got.shape
    tr, tc = _tile(r, (256,)), _tile(c, (1024, 768, 512))

    def body(j_ref, c_ref, a_ref, g_ref, o_ref):
        o_ref[...] = ((a_ref[...].astype(F32) + g_ref[0].astype(F32)) + g_ref[1].astype(F32)) + g_ref[2].astype(F32)

    if cols:
        own_spec = pl.BlockSpec((tr, tc), lambda i, j, j_ref, c_ref: (i, j_ref[0] * (c // tc) + j))
    else:
        own_spec = pl.BlockSpec((tr, tc), lambda i, j, j_ref, c_ref: (j_ref[0] * (r // tr) + i, j))
    return pl.pallas_call(
        body, name=name,
        grid_spec=pltpu.PrefetchScalarGridSpec(
            num_scalar_prefetch=2, grid=(r // tr, c // tc),
            in_specs=[own_spec, pl.BlockSpec((3, tr, tc), lambda i, j, j_ref, c_ref: (0, i, j))],
            out_specs=pl.BlockSpec((None, tr, tc), lambda i, j, j_ref, c_ref: (c_ref[0], i, j))),
        out_shape=jax.ShapeDtypeStruct((DEPTH, r, c), F32), compiler_params=_params(2))(my_j, my_c, own, got)


def _share_final(bufs, *, name):
    n = len(bufs)

    def body(*refs):
        outs, send, recv = refs[n:2 * n], refs[2 * n], refs[2 * n + 1]
        x, y, c = lax.axis_index("x"), lax.axis_index("y"), lax.axis_index("c")

        def copy(w, slot):
            return pltpu.make_async_remote_copy(src_ref=outs[w].at[slot], dst_ref=outs[w].at[slot],
                                                send_sem=send.at[w], recv_sem=recv.at[w],
                                                device_id=(x, y, 1 - c), device_id_type=MESH)

        away = [copy(w, c) for w in range(n)]
        for cp in away:
            cp.start()
        for w in range(n):
            copy(w, 1 - c).wait_recv()
        for cp in away:
            cp.wait_send()

    return pl.pallas_call(body, name=name, in_specs=[_ANY] * n, out_specs=[_ANY] * n,
                          out_shape=[jax.ShapeDtypeStruct(b.shape, b.dtype) for b in bufs],
                          input_output_aliases={w: w for w in range(n)},
                          scratch_shapes=[pltpu.SemaphoreType.DMA((n,))] * 2)(*bufs)


def _adamw_native(w, g, m, v, *, name):
    r, c = w.shape
    tr = _tile(r, (256, 128, 64, 32, 16, 8))
    spec = pl.BlockSpec((tr, c), lambda i: (i, 0))
    c1 = 1.0 / (1.0 - ADAM_B1 ** ADAM_STEP)
    c2 = 1.0 / (1.0 - ADAM_B2 ** ADAM_STEP)

    def body(w_ref, g_ref, m_ref, v_ref, d_ref, nm_ref, nv_ref):
        g_t = g_ref[...]
        nm = ADAM_B1 * m_ref[...] + (1.0 - ADAM_B1) * g_t
        nv = ADAM_B2 * v_ref[...] + (1.0 - ADAM_B2) * (g_t * g_t)
        nm_ref[...] = nm
        nv_ref[...] = nv
        d_ref[...] = -ADAM_LR * ((nm * c1) / (jnp.sqrt(nv * c2) + ADAM_EPS) + ADAM_WD * w_ref[...])

    shp = jax.ShapeDtypeStruct((r, c), F32)
    return pl.pallas_call(body, name=name, grid=(r // tr,), in_specs=[spec] * 4, out_specs=(spec,) * 3,
                          out_shape=(shp,) * 3, compiler_params=_params(1))(w, g, m, v)


def _flat_tile(r):
    return _tile(r, (512, 256, 128, 64, 32, 16, 8))


def _sum4(parts, *, name):
    r = parts.shape[1]
    tr = _flat_tile(r)

    def body(p_ref, o_ref):
        o_ref[...] = ((p_ref[0] + p_ref[1]) + p_ref[2]) + p_ref[3]

    return pl.pallas_call(body, name=name, grid=(r // tr,),
                          in_specs=[pl.BlockSpec((4, tr, LANES), lambda i: (0, i, 0))],
                          out_specs=pl.BlockSpec((tr, LANES), lambda i: (i, 0)),
                          out_shape=jax.ShapeDtypeStruct((r, LANES), F32), compiler_params=_params(1))(parts)


def _add2(a, b, *, name):
    r = a.shape[0]
    tr = _flat_tile(r)
    spec = pl.BlockSpec((tr, LANES), lambda i: (i, 0))

    def body(a_ref, b_ref, o_ref):
        o_ref[...] = a_ref[...] + b_ref[...]

    return pl.pallas_call(body, name=name, grid=(r // tr,), in_specs=[spec, spec], out_specs=spec,
                          out_shape=jax.ShapeDtypeStruct((r, LANES), F32), compiler_params=_params(1))(a, b)


def _adamw(w, ga, gb, m, v, *, name):
    r = w.shape[0]
    tr = _flat_tile(r)
    spec = pl.BlockSpec((tr, LANES), lambda i: (i, 0))
    two = gb is not None
    c1 = 1.0 / (1.0 - ADAM_B1 ** ADAM_STEP)
    c2 = 1.0 / (1.0 - ADAM_B2 ** ADAM_STEP)

    def body(*refs):
        w_ref, ga_ref = refs[0], refs[1]
        m_ref, v_ref, g_ref, d_ref, nm_ref, nv_ref = refs[2 + two:]
        g = ga_ref[...] + refs[2][...] if two else ga_ref[...]
        nm = ADAM_B1 * m_ref[...] + (1.0 - ADAM_B1) * g
        nv = ADAM_B2 * v_ref[...] + (1.0 - ADAM_B2) * (g * g)
        g_ref[...] = g
        nm_ref[...] = nm
        nv_ref[...] = nv
        d_ref[...] = -ADAM_LR * ((nm * c1) / (jnp.sqrt(nv * c2) + ADAM_EPS) + ADAM_WD * w_ref[...])

    args = [w, ga] + ([gb] if two else []) + [m, v]
    shp = jax.ShapeDtypeStruct((r, LANES), F32)
    return pl.pallas_call(body, name=name, grid=(r // tr,), in_specs=[spec] * len(args), out_specs=(spec,) * 4,
                          out_shape=(shp,) * 4, compiler_params=_params(1))(*args)


def _pack(arrs, dtype=F32):
    rows = []
    for a in arrs:
        flat = a.astype(dtype).reshape(-1)
        rows.append(jnp.pad(flat, (0, (-flat.shape[0]) % LANES)).reshape(-1, LANES))
    buf = jnp.concatenate(rows)
    return jnp.pad(buf, ((0, (-buf.shape[0]) % 32), (0, 0)))


def _unpack(buf, shapes):
    out, row = [], 0
    for shp in shapes:
        sz = math.prod(shp)
        nr = -(-sz // LANES)
        out.append(buf[row:row + nr].reshape(-1)[:sz].reshape(shp))
        row += nr
    return out


def _stack_shards(full, axis):
    shp = full.shape
    return jnp.moveaxis(full.reshape(shp[:axis] + (4, shp[axis] // 4) + shp[axis + 1:]), axis, 0)


def _unstack_shards(st, axis):
    v = jnp.moveaxis(st, 0, axis)
    shp = v.shape
    return v.reshape(shp[:axis] + (shp[axis] * shp[axis + 1],) + shp[axis + 2:])


def _layer_weights(w, l):
    lw = {n: (w[n], l) if n in BIG_COLS else w[n][l] for n in w}
    lw['wcat'] = jnp.concatenate([lw['rg_wa'][0], lw['rg_wi'][0], lw['rg_wa'][1], lw['rg_wi'][1]],
                                 axis=-1).astype(BF16)
    ba, bi = lw['rg_ba'].reshape(2, RG_HEADS, HEAD), lw['rg_bi'].reshape(2, RG_HEADS, HEAD)
    lw['bcat'] = jnp.concatenate([ba[0], bi[0], ba[1], bi[1]], axis=-1)[:, None, :]
    s5_names = ['s5_a_re', 's5_a_im', 's5_log_dt', 's5_b_re', 's5_b_im', 's5_c_re', 's5_c_im']
    (wcat, mout, a2), lw['s5_vjp'] = jax.vjp(_s5_mats, *[lw[n] for n in s5_names])
    lw['s5_wcat'], lw['s5_mout'], lw['s5_a2'] = wcat.astype(BF16), mout.astype(BF16), a2
    for n in ('conv_b', 's5_d', 's5_glu_b', 'b_out', 'mlp_b1', 'mlp_b2', 'ln1_g', 'ln1_b', 'ln2_g', 'ln2_b'):
        lw[n] = lw[n][None, :]
    return lw


def _layer_fwd(l, x0, modall, lw, dims):
    bl, s, lc, tps = dims
    ll = s - lc
    tag = f"l{l}_"
    sv = {'x0': x0}
    sv['u1'] = _modulate(x0, modall, 0, 1, tps, name=tag + "mod1")
    sv['p'] = p = _mm_nn(sv['u1'], lw['w_in'], name=tag + "w_in")
    rg, sv['hf'], sv['hb'] = _rg_fwd(p, lw['conv_w'], lw['conv_b'], lw['rg_lambda'], lw['wcat'], lw['bcat'],
                                     bl, s, lc, name=tag + "rg_fwd")
    sv['u_ch'] = _to_chunks(p, 2 * D_MODEL // HEAD, bl, s, lc, name=tag + "u_chunks")
    y_ch, sv['sf'], sv['sb'] = _s5_fwd(sv['u_ch'], lw['s5_wcat'], lw['s5_mout'], lw['s5_a2'], lc // T_CH,
                                       name=tag + "s5_fwd")
    sv['y'] = _from_chunks(y_ch, bl, s, lc, name=tag + "y_rows")
    sv['cat'] = _glu_fwd(sv['y'], p, lw['s5_d'], lw['s5_glu_w'], lw['s5_glu_b'], rg, name=tag + "glu_fwd")
    sv['m'] = _mm_nn(sv['cat'], lw['w_out'], lw['b_out'], name=tag + "w_out")
    sv['x1'] = _resid_ln(x0, sv['m'], modall, 2, lw['ln1_g'], lw['ln1_b'], tps, name=tag + "ln1")
    sv['u2'] = _modulate(sv['x1'], modall, 3, 4, tps, name=tag + "mod2")
    sv['a'], sv['h'] = _mm_nn(sv['u2'], lw['mlp_w1'], lw['mlp_b1'], relu2=True, name=tag + "mlp1")
    sv['f'] = _mm_nn(sv['a'], lw['mlp_w2'], lw['mlp_b2'], name=tag + "mlp2")
    x2 = _resid_ln(sv['x1'], sv['f'], modall, 5, lw['ln2_g'], lw['ln2_b'], tps, name=tag + "ln2")
    return x2, sv


def _layer_bwd(l, dx2, modall, lw, sv, dims, gbufs):
    bl, s, lc, tps = dims
    ll = s - lc
    tag = f"l{l}_"
    g = {}

    def big_grad(n, a_mat, b_mat, label):
        gbufs[n] = _mm_tn(a_mat, b_mat, name=tag + label, layer=l, into=gbufs.get(n))
    dx1a, df, db2, g['ln2_g'], g['ln2_b'], dg2 = _resid_ln_bwd(sv['x1'], sv['f'], modall, 5, lw['ln2_g'], dx2, tps,
                                                              name=tag + "ln2_bwd")
    g['mlp_b2'] = db2
    big_grad('mlp_w2', sv['a'], df, "mlp2_dw")
    dh = _mm_nt(df, lw['mlp_w2'], sv['h'], name=tag + "mlp2_dx")
    g['mlp_b1'] = _colsum(dh, name=tag + "mlp1_db")
    big_grad('mlp_w1', sv['u2'], dh, "mlp1_dw")
    du2 = _mm_nt(dh, lw['mlp_w1'], name=tag + "mlp1_dx")
    dx1, dsc2, dsh2 = _modulate_bwd(du2, sv['x1'], modall, 4, dx1a, tps, name=tag + "mod2_bwd")
    dx0a, dm, g['b_out'], g['ln1_g'], g['ln1_b'], dg1 = _resid_ln_bwd(sv['x0'], sv['m'], modall, 2, lw['ln1_g'], dx1,
                                                                     tps, name=tag + "ln1_bwd")
    big_grad('w_out', sv['cat'], dm, "w_out_dw")
    dcat = _mm_nt(dm, lw['w_out'], name=tag + "w_out_dx")
    dy, dskip, g_bf, dz_bf, g['s5_d'], g['s5_glu_b'] = _glu_bwd(dcat, sv['y'], sv['p'], lw['s5_d'], lw['s5_glu_w'],
                                                                lw['s5_glu_b'], name=tag + "glu_bwd")
    big_grad('s5_glu_w', g_bf, dz_bf, "glu_dw")
    dy_ch = _to_chunks(dy, 0, bl, s, lc, name=tag + "dy_chunks")
    du_ch, dwcat, dmout, dacc = _s5_bwd(dy_ch, sv['u_ch'], sv['sf'], sv['sb'], lw['s5_wcat'],
                                        lw['s5_mout'], lw['s5_a2'], lc // T_CH, name=tag + "s5_bwd")
    s5g = lw['s5_vjp']((dwcat, dmout, dacc))
    for n, v in zip(['s5_a_re', 's5_a_im', 's5_log_dt', 's5_b_re', 's5_b_im', 's5_c_re', 's5_c_im'], s5g):
        g[n] = v
    ds5u = _add_cast(_from_chunks(du_ch, bl, s, lc, name=tag + "du_rows"), dskip, name=tag + "ds5u")
    drgx, dgate, dcw, dcb, dlam, dwc, dbc = _rg_bwd(sv['p'], dcat, sv['hf'], sv['hb'], lw['conv_w'], lw['conv_b'],
                                                    lw['rg_lambda'], lw['wcat'], lw['bcat'], bl, s, lc,
                                                    name=tag + "rg_bwd")
    g['conv_w'] = dcw.transpose(1, 0, 2).reshape(4, D_MODEL)
    g['conv_b'] = dcb.reshape(D_MODEL)
    g['rg_lambda'] = dlam.transpose(1, 0, 2).reshape(2, D_MODEL)
    g['rg_wa'] = jnp.stack([dwc[:, :, 0:HEAD], dwc[:, :, 2 * HEAD:3 * HEAD]])
    g['rg_wi'] = jnp.stack([dwc[:, :, HEAD:2 * HEAD], dwc[:, :, 3 * HEAD:]])
    dbc = dbc.reshape(RG_HEADS, 4, HEAD)
    g['rg_ba'] = jnp.stack([dbc[:, 0], dbc[:, 2]]).reshape(2, D_MODEL)
    g['rg_bi'] = jnp.stack([dbc[:, 1], dbc[:, 3]]).reshape(2, D_MODEL)
    dp = jnp.concatenate([drgx, dgate, ds5u], axis=1)
    big_grad('w_in', sv['u1'], dp, "w_in_dw")
    du1 = _mm_nt(dp, lw['w_in'], name=tag + "w_in_dx")
    dx0, dsc1, dsh1 = _modulate_bwd(du1, sv['x0'], modall, 1, dx0a, tps, name=tag + "mod1_bwd")
    dmod = jnp.concatenate([dsh1, dsc1, dg1, dsh2, dsc2, dg2], axis=1)
    return dx0, g, dmod


def _kernel_impl(*args):
    nin = len(IN_NAMES)
    a = dict(zip(IN_NAMES, args[:nin]))
    target = args[nin]
    nw = len(WEIGHTS)
    mom = dict(zip(WEIGHTS, args[nin + 1:nin + 1 + nw]))
    var = dict(zip(WEIGHTS, args[nin + 1 + nw:nin + 1 + 2 * nw]))
    bl, ll, d = a['x'].shape
    lc = a['ctx'].shape[1]
    assert d == D_MODEL and lc == TM and bl == 2 and ll % (GRID_W * T_CH) == 0
    s = lc + ll
    tps = s // TM
    dims = (bl, s, lc, tps)

    def gather(names, dtype, tag):
        shards = [a[n] for n in names]
        got = _all_gather_xy(_pack(shards, dtype), name="gather_" + tag)
        per = [_unpack(got[j], [w.shape for w in shards]) for j in range(4)]
        return {n: _unstack_shards(jnp.stack([per[j][i] for j in range(4)]), SHARD_AXIS[n])
                for i, n in enumerate(names)}

    big_cols = [BIG_COLS[n] for n in BIG]
    my_c = lax.axis_index("c").astype(jnp.int32).reshape(1)
    my_j = (2 * lax.axis_index("x") + lax.axis_index("y")).astype(jnp.int32).reshape(1)
    mine = [_cast_into_full(a[n], BIG_COLS[n], my_j, name=f"cast_{n}") for n in BIG]
    w = dict(zip(BIG, _gather_big(mine, big_cols, name="gather_big")))
    w.update(gather(GATHER_F32, F32, "f32"))
    for n in REPLICATED:
        w[n] = a[n]

    xs = jnp.concatenate([a['ctx'], a['x']], axis=1).reshape(bl * s, D_MODEL)
    c16 = jnp.zeros((16, D_MODEL), F32).at[0:2].set(a['c']).at[2].set(a['c_ctx'])
    s16, ds16 = _silu_rows(c16, name="silu")
    s16b = s16.astype(BF16)
    layers, saved, mods = [], [], []
    for l in range(DEPTH):
        lw = _layer_weights({n: w[n] for n in WEIGHTS if n not in ('c_ctx',)}, l)
        mod16 = _mm_nn(s16b, lw['ada_w'], lw['ada_b'][None, :], name=f"l{l}_ada").reshape(16, N_MOD, D_MODEL)
        modall = jnp.stack([mod16[2], mod16[0], mod16[2], mod16[1]])
        xs, sv = _layer_fwd(l, xs, modall, lw, dims)
        layers.append(lw)
        saved.append(sv)
        mods.append(modall)
    lossrow, dx = _loss_head(xs, target.reshape(bl * ll, D_MODEL), tps, name="loss_head")
    loss = lax.psum(0.5 / D_MODEL * jnp.sum(lossrow), ("x", "y", "c"))

    small = [n for n in WEIGHTS if n != 'c_ctx' and n not in BIG_COLS]
    grads = {n: [None] * DEPTH for n in small}
    gbufs = {}
    ds_rows = jnp.zeros((16, D_MODEL), F32)
    for l in reversed(range(DEPTH)):
        dx, g, dmod = _layer_bwd(l, dx, mods[l], layers[l], saved[l], dims, gbufs)
        dmod16 = jnp.zeros((16, N_MOD * D_MODEL), F32).at[0].set(dmod[1].reshape(-1)).at[1].set(
            dmod[3].reshape(-1)).at[2].set((dmod[0] + dmod[2]).reshape(-1))
        dmod16b = dmod16.astype(BF16)
        gbufs['ada_w'] = _mm_tn(s16b, dmod16b, name=f"l{l}_ada_dw", layer=l, into=gbufs.get('ada_w'))
        g['ada_b'] = _colsum(dmod16, name=f"l{l}_ada_db")
        ds_rows = ds_rows + _mm_nt(dmod16b, layers[l]['ada_w'], name=f"l{l}_ada_dx")
        for n, v in g.items():
            grads[n][l] = v.reshape(a[n].shape[1:] if n in REPLICATED else w[n].shape[1:])
    full = {n: jnp.stack(v) for n, v in grads.items()}
    full['c_ctx'] = _mul_rows(ds_rows, ds16, name="silu_bwd")[2]
    grad_x = dx.reshape(bl, s, D_MODEL)[:, lc:]

    from_sib = _sibling_partials([gbufs[n] for n in BIG], name="grad_big_sibling")
    sums = [_chip_sum(gbufs[n], o, my_c, name=f"grad_chip_sum_{n}") for n, o in zip(BIG, from_sib)]
    got = _scatter_big(sums, big_cols, name="grad_big_scatter")
    finals = [_block_sum(sm, gt, cf, my_j, my_c, name=f"grad_block_sum_{n}")
              for n, sm, gt, cf in zip(BIG, sums, got, big_cols)]
    res_big = {}
    for n, gfull in zip(BIG, _share_final(finals, name="grad_big_share")):
        flat = lambda t: t.reshape(-1, t.shape[-1])
        d_w, n_m, n_v = _adamw_native(flat(a[n]), flat(gfull), flat(mom[n]), flat(var[n]), name=f"adamw_{n}")
        res_big[n] = [gfull] + [t.reshape(a[n].shape) for t in (d_w, n_m, n_v)]

    rep_flat = _pack([full[n] for n in REPLICATED])
    rr = rep_flat.shape[0]
    sh_stacked = [_stack_shards(full[n], SHARD_AXIS[n] + 0).reshape(4, -1) for n in SHARDED]
    parts = jnp.concatenate(sh_stacked + [rep_flat.reshape(4, -1)], axis=1)
    pad = (-parts.shape[1]) % (32 * LANES)
    parts = jnp.pad(parts, ((0, 0), (0, pad))).reshape(4, -1, LANES)
    mine = _sum4(_scatter_xy(parts, name="grad_scatter"), name="grad_sum4")
    other = _swap_sibling(mine, name="grad_swap")
    n_sh = sum(math.prod(a[n].shape) for n in SHARDED)
    r_sh = n_sh // LANES
    assert n_sh % LANES == 0
    rq = rr // 4

    sh_shapes = [a[n].shape for n in SHARDED]
    pk = lambda dct: _pack([dct[n] for n in SHARDED])
    r_pk = pk(a).shape[0]
    take = lambda buf: jnp.pad(buf[:r_sh], ((0, r_pk - r_sh), (0, 0)))
    outs_sh = _adamw(pk(a), take(mine), take(other), pk(mom), pk(var), name="adamw_sharded")
    res_sh = [dict(zip(SHARDED, _unpack(o, sh_shapes))) for o in outs_sh]

    quarter = _add2(mine[r_sh:r_sh + rq], other[r_sh:r_sh + rq], name="grad_rep_sum")
    rep_g = _all_gather_xy(quarter, name="grad_rep_gather").reshape(rr, LANES)
    rep_shapes = [a[n].shape for n in REPLICATED]
    pr = lambda dct: _pack([dct[n] for n in REPLICATED])
    outs_rep = _adamw(pr(a), rep_g, None, pr(mom), pr(var), name="adamw_replicated")
    res_rep = [dict(zip(REPLICATED, _unpack(o, rep_shapes))) for o in outs_rep]

    out = [loss, grad_x]
    for k in range(4):
        out += [res_big[n][k] if n in BIG_COLS else res_sh[k][n] if n in SHARDED else res_rep[k][n] for n in WEIGHTS]
    return tuple(out)


def kernel(x, c, ctx, c_ctx, ada_w, ada_b, ln1_g, ln1_b, w_in, conv_w, conv_b, rg_lambda, rg_wa, rg_ba, rg_wi, rg_bi, s5_a_re, s5_a_im, s5_log_dt, s5_b_re, s5_b_im, s5_c_re, s5_c_im, s5_d, s5_glu_w, s5_glu_b, w_out, b_out, ln2_g, ln2_b, mlp_w1, mlp_b1, mlp_w2, mlp_b2, loss_target, m_c_ctx, m_ada_w, m_ada_b, m_ln1_g, m_ln1_b, m_w_in, m_conv_w, m_conv_b, m_rg_lambda, m_rg_wa, m_rg_ba, m_rg_wi, m_rg_bi, m_s5_a_re, m_s5_a_im, m_s5_log_dt, m_s5_b_re, m_s5_b_im, m_s5_c_re, m_s5_c_im, m_s5_d, m_s5_glu_w, m_s5_glu_b, m_w_out, m_b_out, m_ln2_g, m_ln2_b, m_mlp_w1, m_mlp_b1, m_mlp_w2, m_mlp_b2, v_c_ctx, v_ada_w, v_ada_b, v_ln1_g, v_ln1_b, v_w_in, v_conv_w, v_conv_b, v_rg_lambda, v_rg_wa, v_rg_ba, v_rg_wi, v_rg_bi, v_s5_a_re, v_s5_a_im, v_s5_log_dt, v_s5_b_re, v_s5_b_im, v_s5_c_re, v_s5_c_im, v_s5_d, v_s5_glu_w, v_s5_glu_b, v_w_out, v_b_out, v_ln2_g, v_ln2_b, v_mlp_w1, v_mlp_b1, v_mlp_w2, v_mlp_b2):
    return _kernel_impl(x, c, ctx, c_ctx, ada_w, ada_b, ln1_g, ln1_b, w_in, conv_w, conv_b, rg_lambda, rg_wa, rg_ba, rg_wi, rg_bi, s5_a_re, s5_a_im, s5_log_dt, s5_b_re, s5_b_im, s5_c_re, s5_c_im, s5_d, s5_glu_w, s5_glu_b, w_out, b_out, ln2_g, ln2_b, mlp_w1, mlp_b1, mlp_w2, mlp_b2, loss_target, m_c_ctx, m_ada_w, m_ada_b, m_ln1_g, m_ln1_b, m_w_in, m_conv_w, m_conv_b, m_rg_lambda, m_rg_wa, m_rg_ba, m_rg_wi, m_rg_bi, m_s5_a_re, m_s5_a_im, m_s5_log_dt, m_s5_b_re, m_s5_b_im, m_s5_c_re, m_s5_c_im, m_s5_d, m_s5_glu_w, m_s5_glu_b, m_w_out, m_b_out, m_ln2_g, m_ln2_b, m_mlp_w1, m_mlp_b1, m_mlp_w2, m_mlp_b2, v_c_ctx, v_ada_w, v_ada_b, v_ln1_g, v_ln1_b, v_w_in, v_conv_w, v_conv_b, v_rg_lambda, v_rg_wa, v_rg_ba, v_rg_wi, v_rg_bi, v_s5_a_re, v_s5_a_im, v_s5_log_dt, v_s5_b_re, v_s5_b_im, v_s5_c_re, v_s5_c_im, v_s5_d, v_s5_glu_w, v_s5_glu_b, v_w_out, v_b_out, v_ln2_g, v_ln2_b, v_mlp_w1, v_mlp_b1, v_mlp_w2, v_mlp_b2)
```
